```python
import jax, jax.numpy as jnp
from jax import lax
import numpy as np

D_MODEL = 1024
BATCH = 2
SEQ = 8192
DEPTH = 1
DEC_BATCH = 32
DEC_SEQ = 4
PAST_LEN = 16384
PAGE_SIZE = 128

CONV_CH = 512
CONV_W = 31
N_HEADS = 8
HEAD_DIM = 64
N_KV = 2
Q_PER_KV = N_HEADS // N_KV
ROPE_DIM = HEAD_DIM // 4
ROPE_THETA = 500000.0
CMP_BLOCK = 64
SEL_BLOCK = CMP_BLOCK
N_SEL = 16
WINDOW = 512
WIN_BLOCK = 128
CMP_HID = 2 * HEAD_DIM
D_MIX = CONV_CH + N_HEADS * HEAD_DIM
COL_Q = N_HEADS * HEAD_DIM
COL_KV = 2 * N_KV * HEAD_DIM
COL_GATE = 3 * N_HEADS
D_IN = 2 * CONV_CH + COL_Q + 3 * COL_KV + COL_GATE
N_GROUPS = 4
EXPERTS_PER_GROUP = 8
N_EXPERTS = N_GROUPS * EXPERTS_PER_GROUP
TOP_K = 2
D_EXPERT = 512
MOE_CHUNK = 128

RMS_EPS = 1e-6
LN_EPS = 1e-5
NEG_INF = -1e30
ATT_SCALE = HEAD_DIM ** -0.5

kernel_name = 'hymba_conformer_nsa_hmoe_step'


def rmsnorm(x, g):
    xf = x.astype(jnp.float32)
    y = xf * lax.rsqrt(jnp.mean(xf * xf, axis=-1, keepdims=True) + RMS_EPS)
    return (y * g.astype(jnp.float32)).astype(x.dtype)


def layernorm(x, g, b):
    xf = x.astype(jnp.float32)
    mu = jnp.mean(xf, axis=-1, keepdims=True)
    var = jnp.mean(jnp.square(xf - mu), axis=-1, keepdims=True)
    y = (xf - mu) * lax.rsqrt(var + LN_EPS) * g.astype(jnp.float32) + b.astype(jnp.float32)
    return y.astype(x.dtype)


def masked_softmax(s, valid):
    s = jnp.where(valid, s, NEG_INF)
    return jnp.where(valid, jax.nn.softmax(s, axis=-1), 0.0)


def apply_rope(x, pos):
    half = ROPE_DIM // 2
    inv = ROPE_THETA ** (-jnp.arange(half, dtype=jnp.float32) / half)
    ang = pos.astype(jnp.float32)[:, None] * inv
    ang = ang.reshape((ang.shape[0],) + (1,) * (x.ndim - 3) + (half,))
    cos, sin = jnp.cos(ang), jnp.sin(ang)
    xr = x[..., :ROPE_DIM].astype(jnp.float32)
    x1, x2 = xr[..., :half], xr[..., half:]
    rot = jnp.concatenate([x1 * cos - x2 * sin, x1 * sin + x2 * cos], axis=-1).astype(x.dtype)
    return jnp.concatenate([rot, x[..., ROPE_DIM:]], axis=-1)


def mix_inputs(x, pos, ln1, w_in):
    b, t, _ = x.shape
    p = rmsnorm(x, ln1) @ w_in
    conv_in = p[..., :CONV_CH] * jax.nn.sigmoid(p[..., CONV_CH:2 * CONV_CH])
    o = 2 * CONV_CH
    q = apply_rope(p[..., o:o + COL_Q].reshape(b, t, N_KV, Q_PER_KV, HEAD_DIM), pos)
    o += COL_Q
    kvs = []
    for _ in range(3):
        kv = p[..., o:o + COL_KV].reshape(b, t, N_KV, 2, HEAD_DIM)
        kvs.append(jnp.stack([apply_rope(kv[..., 0, :], pos), kv[..., 1, :]], axis=-2))
        o += COL_KV
    gates = jax.nn.sigmoid(p[..., o:o + COL_GATE].reshape(b, t, N_KV, Q_PER_KV, 3))
    return conv_in, q, kvs[0], kvs[1], kvs[2], gates


def conv_module(a_ext, dw_w, dw_b, ln_g, ln_b):
    y = lax.conv_general_dilated(a_ext, dw_w[:, None, :], window_strides=(1,), padding='VALID',
                                 dimension_numbers=('NWC', 'WIO', 'NWC'),
                                 feature_group_count=CONV_CH) + dw_b
    return jax.nn.silu(layernorm(y, ln_g, ln_b))


def compress_blocks(rows, pos_emb, w_k1, w_k2, w_v1, w_v2):
    b, g = rows.shape[0], rows.shape[2]
    n = rows.shape[1] // CMP_BLOCK
    blocks = rows.reshape(b, n, CMP_BLOCK, g, 2, HEAD_DIM)

    def phi(x, pe, w1, w2):
        h = jnp.einsum('bnlgd,ldh->bngh', x, w1) + jnp.einsum('ld,ldh->h', pe, w1)
        return jax.nn.silu(h) @ w2

    return (phi(blocks[..., 0, :], pos_emb[:, 0], w_k1, w_k2),
            phi(blocks[..., 1, :], pos_emb[:, 1], w_v1, w_v2))


def compressed_attention(q, k_c, v_c, q_pos):
    nb = k_c.shape[1]
    s = jnp.einsum('btgrd,bngd->btgrn', q, k_c).astype(jnp.float32) * ATT_SCALE
    valid = ((jnp.arange(nb) + 1) * CMP_BLOCK - 1)[None, :] <= q_pos[:, None]
    p = masked_softmax(s, valid[None, :, None, None, :])
    o = jnp.einsum('btgrn,bngd->btgrd', p.astype(v_c.dtype), v_c)
    return o, p.sum(axis=3)


def select_blocks(importance, q_pos):
    nb = importance.shape[-1]
    own = (q_pos // SEL_BLOCK)[None, :, None, None]
    cand = jnp.arange(nb)[None, None, None, :] < own
    score = jnp.where(cand, importance, -1.0)
    _, idx = lax.top_k(score, min(N_SEL, nb))
    return idx, idx < own


def block_rows(idx, valid):
    rows = (idx[..., None] * SEL_BLOCK + jnp.arange(SEL_BLOCK)).reshape(idx.shape[:-1] + (-1,))
    return rows, jnp.repeat(valid, SEL_BLOCK, axis=-1)


def selected_attention(q, kv_sel, valid_sel, kv_own, valid_own):
    s_sel = jnp.einsum('btgrd,btgsd->btgrs', q, kv_sel[..., 0, :])
    s_own = jnp.einsum('btgrd,bsgd->btgrs', q, kv_own[..., 0, :])
    s = jnp.concatenate([s_sel, s_own], axis=-1).astype(jnp.float32) * ATT_SCALE
    valid = jnp.concatenate([jnp.broadcast_to(valid_sel[:, :, :, None, :], s_sel.shape),
                             jnp.broadcast_to(valid_own[None, :, None, None, :], s_own.shape)], axis=-1)
    p = masked_softmax(s, valid).astype(kv_sel.dtype)
    n_sel = s_sel.shape[-1]
    return (jnp.einsum('btgrs,btgsd->btgrd', p[..., :n_sel], kv_sel[..., 1, :])
            + jnp.einsum('btgrs,bsgd->btgrd', p[..., n_sel:], kv_own[..., 1, :]))


def window_attention(q, kv, q_pos, k_pos):
    s = jnp.einsum('...tgrd,...sgd->...tgrs', q, kv[..., 0, :]).astype(jnp.float32) * ATT_SCALE
    diff = q_pos[..., :, None] - k_pos[..., None, :]
    valid = (diff >= 0) & (diff < WINDOW) & (k_pos[..., None, :] >= 0)
    p = masked_softmax(s, valid[..., :, None, None, :])
    return jnp.einsum('...tgrs,...sgd->...tgrd', p.astype(kv.dtype), kv[..., 1, :])


def combine_branches(gates, o_cmp, o_sel, o_win):
    o = gates[..., 0:1] * o_cmp + gates[..., 1:2] * o_sel + gates[..., 2:3] * o_win
    return o.reshape(o.shape[:2] + (N_HEADS * HEAD_DIM,))


def nsa_prompt(q, cmp_kv, sel_kv, win_kv, gates, pos, cmp_w):
    b, t = q.shape[:2]
    k_c, v_c = compress_blocks(cmp_kv, *cmp_w)
    o_cmp, imp = compressed_attention(q, k_c, v_c, pos)
    idx, valid = select_blocks(imp, pos)
    nqb = t // SEL_BLOCK

    def to_blocks(a):
        return jnp.moveaxis(a.reshape((b, nqb, SEL_BLOCK) + a.shape[2:]), 1, 0)

    own_mask = jnp.tril(jnp.ones((SEL_BLOCK, SEL_BLOCK), dtype=bool))
    bi = jnp.arange(b)[:, None, None, None]
    gi = jnp.arange(N_KV)[None, None, :, None]

    def sel_block(args):
        qb, ib, vb, kvo = args
        rows, vrows = block_rows(ib, vb)
        return selected_attention(qb, sel_kv[bi, rows, gi], vrows, kvo, own_mask)

    o_sel = lax.map(sel_block, (to_blocks(q), to_blocks(idx), to_blocks(valid), to_blocks(sel_kv)))
    o_sel = jnp.moveaxis(o_sel, 0, 1).reshape(q.shape)
    nwb = t // WIN_BLOCK
    nback = WINDOW // WIN_BLOCK
    padded = jnp.pad(win_kv, ((0, 0), (nback * WIN_BLOCK, 0), (0, 0), (0, 0), (0, 0)))
    kv_band = jnp.concatenate(
        [padded[:, j * WIN_BLOCK:j * WIN_BLOCK + t].reshape(b, nwb, WIN_BLOCK, N_KV, 2, HEAD_DIM)
         for j in range(nback + 1)], axis=2)
    q_pos = pos.reshape(nwb, WIN_BLOCK)
    k_pos = (jnp.arange(nwb)[:, None] - nback) * WIN_BLOCK + jnp.arange((nback + 1) * WIN_BLOCK)[None, :]
    o_win = window_attention(q.reshape(b, nwb, WIN_BLOCK, N_KV, Q_PER_KV, HEAD_DIM),
                             kv_band, q_pos, k_pos).reshape(q.shape)
    return combine_branches(gates, o_cmp, o_sel, o_win)


def nsa_sample(q, cmp_kv, sel_kv, win_kv, gates, pos, cache_cmp, cache_sel, win_state, page_table, cmp_w):
    db, ds = q.shape[:2]
    past = PAST_LEN
    past_cmp = cache_cmp[page_table].reshape(db, past, N_KV, 2, HEAD_DIM)
    k_p, v_p = compress_blocks(past_cmp, *cmp_w)
    k_n, v_n = compress_blocks(cmp_kv[:, :(ds // CMP_BLOCK) * CMP_BLOCK], *cmp_w)
    k_c = jnp.concatenate([k_p, k_n], axis=1)
    v_c = jnp.concatenate([v_p, v_n], axis=1)
    o_cmp, imp = compressed_attention(q, k_c, v_c, pos)
    idx, valid = select_blocks(imp, pos)
    rows, vrows = block_rows(idx, valid)
    bi = jnp.arange(db)[:, None, None, None]
    gi = jnp.arange(N_KV)[None, None, :, None]
    r_past = jnp.minimum(rows, past - 1)
    phys = page_table[bi, r_past // PAGE_SIZE]
    kv_past = cache_sel[phys, r_past % PAGE_SIZE, gi]
    kv_new = sel_kv[bi, jnp.clip(rows - past, 0, ds - 1), gi]
    kv_g = jnp.where((rows < past)[..., None, None], kv_past, kv_new)
    own_mask = (pos[:, None] >= pos[None, :]) & (pos[:, None] // SEL_BLOCK == pos[None, :] // SEL_BLOCK)
    o_sel = selected_attention(q, kv_g, vrows, sel_kv, own_mask)
    kv_w = jnp.concatenate([win_state, win_kv], axis=1)
    win_rows = win_state.shape[1]
    k_pos = past - win_rows + jnp.arange(win_rows + ds)
    o_win = window_attention(q, kv_w, pos, k_pos)
    return combine_branches(gates, o_cmp, o_sel, o_win), kv_w[:, ds:]


def hier_moe(h, w_rg, w_re, w_g, w_u, w_d):
    n, d = h.shape
    lg = (h @ w_rg).astype(jnp.float32)
    p_grp = jax.nn.softmax(lg, axis=-1)
    grp = jnp.argmax(lg, axis=-1)
    p_top = jnp.take_along_axis(p_grp, grp[:, None], axis=1)[:, 0]
    le = (h @ w_re).astype(jnp.float32).reshape(n, N_GROUPS, EXPERTS_PER_GROUP)
    le_g = jnp.take_along_axis(le, grp[:, None, None], axis=1)[:, 0]
    top_w, top_i = lax.top_k(jax.nn.softmax(le_g, axis=-1), TOP_K)
    top_w = top_w / jnp.sum(top_w, axis=-1, keepdims=True) * p_top[:, None]
    expert = grp[:, None] * EXPERTS_PER_GROUP + top_i
    a = n * TOP_K
    flat_e = expert.reshape(a)
    flat_t = jnp.repeat(jnp.arange(n), TOP_K)
    flat_w = top_w.reshape(a)
    order = jnp.argsort(flat_e)
    se, st, sw = flat_e[order], flat_t[order], flat_w[order]
    counts = jnp.zeros((N_EXPERTS,), jnp.int32).at[flat_e].add(1)
    starts = jnp.cumsum(counts) - counts
    padded = (counts + MOE_CHUNK - 1) // MOE_CHUNK * MOE_CHUNK
    pend = jnp.cumsum(padded)
    dest = (pend - padded)[se] + (jnp.arange(a) - starts[se])
    n_chunks = -(-a // MOE_CHUNK) + N_EXPERTS
    slot_tok = jnp.full((n_chunks * MOE_CHUNK,), n, jnp.int32).at[dest].set(st)
    slot_w = jnp.zeros((n_chunks * MOE_CHUNK,), jnp.float32).at[dest].set(sw)
    chunk_e = jnp.clip(jnp.searchsorted(pend, jnp.arange(n_chunks) * MOE_CHUNK, side='right'), 0, N_EXPERTS - 1)
    xs = jnp.concatenate([h, jnp.zeros((1, d), h.dtype)], axis=0)[slot_tok].reshape(n_chunks, MOE_CHUNK, d)

    def expert_chunk(args):
        xc, e = args
        return (jax.nn.silu(xc @ w_g[e]) * (xc @ w_u[e])) @ w_d[e]

    ys = lax.map(expert_chunk, (xs, chunk_e)).reshape(n_chunks * MOE_CHUNK, d)
    out = jax.ops.segment_sum(ys.astype(jnp.float32) * slot_w[:, None], slot_tok, num_segments=n + 1)[:n]
    return out.astype(h.dtype)


def merge_and_ffn(x, conv_out, att_out, out_norm_conv, out_norm_att, w_out, ln2, w_rg, w_re, w_g, w_u, w_d):
    mix = jnp.concatenate([rmsnorm(conv_out, out_norm_conv), rmsnorm(att_out, out_norm_att)], axis=-1) @ w_out
    x = x + mix
    b, t, d = x.shape
    return x + hier_moe(rmsnorm(x, ln2).reshape(b * t, d), w_rg, w_re, w_g, w_u, w_d).reshape(b, t, d)


def setup_inputs(seed: int = 0) -> dict:
    key = jax.random.key(seed)
    ks = jax.random.split(key, 32)
    f32 = jnp.float32
    n_pages = PAST_LEN // PAGE_SIZE
    n_used = DEC_BATCH * n_pages
    n_phys = (5 * n_used + 3) // 4
    win_rows = min(WINDOW, PAST_LEN)

    def nrm(k, shape, scale):
        return jax.random.normal(k, shape, f32) * scale

    def gain(k, shape):
        return 1.0 + 0.01 * jax.random.normal(k, shape, f32)

    page_table = jax.random.permutation(ks[6], n_phys)[:n_used].reshape(DEC_BATCH, n_pages).astype(jnp.int32)
    return {
        'x_prompt': nrm(ks[0], (BATCH, SEQ, D_MODEL), 1.0),
        'x_sample': nrm(ks[1], (DEC_BATCH, DEC_SEQ, D_MODEL), 1.0),
        'cache_cmp_kv': nrm(ks[2], (DEPTH, n_phys, PAGE_SIZE, N_KV, 2, HEAD_DIM), 1.0),
        'cache_sel_kv': nrm(ks[3], (DEPTH, n_phys, PAGE_SIZE, N_KV, 2, HEAD_DIM), 1.0),
        'state_win_kv': nrm(ks[4], (DEPTH, DEC_BATCH, win_rows, N_KV, 2, HEAD_DIM), 1.0),
        'state_conv': nrm(ks[5], (DEPTH, DEC_BATCH, CONV_W - 1, CONV_CH), 0.5),
        'page_table': page_table,
        'ln1': gain(ks[7], (DEPTH, D_MODEL)),
        'w_in': nrm(ks[8], (DEPTH, D_MODEL, D_IN), D_MODEL ** -0.5),
        'conv_dw_w': nrm(ks[9], (DEPTH, CONV_W, CONV_CH), CONV_W ** -0.5),
        'conv_dw_b': nrm(ks[10], (DEPTH, CONV_CH), 0.01),
        'conv_ln_g': gain(ks[11], (DEPTH, CONV_CH)),
        'conv_ln_b': nrm(ks[12], (DEPTH, CONV_CH), 0.01),
        'cmp_pos_emb': nrm(ks[13], (DEPTH, CMP_BLOCK, 2, HEAD_DIM), 0.1),
        'w_cmp_k1': nrm(ks[14], (DEPTH, CMP_BLOCK, HEAD_DIM, CMP_HID), (CMP_BLOCK * HEAD_DIM) ** -0.5),
        'w_cmp_k2': nrm(ks[15], (DEPTH, CMP_HID, HEAD_DIM), CMP_HID ** -0.5),
        'w_cmp_v1': nrm(ks[16], (DEPTH, CMP_BLOCK, HEAD_DIM, CMP_HID), (CMP_BLOCK * HEAD_DIM) ** -0.5),
        'w_cmp_v2': nrm(ks[17], (DEPTH, CMP_HID, HEAD_DIM), CMP_HID ** -0.5),
        'out_norm_conv': gain(ks[18], (DEPTH, CONV_CH)),
        'out_norm_att': gain(ks[19], (DEPTH, N_HEADS * HEAD_DIM)),
        'w_out': nrm(ks[20], (DEPTH, D_MIX, D_MODEL), D_MIX ** -0.5),
        'ln2': gain(ks[21], (DEPTH, D_MODEL)),
        'w_router_group': nrm(ks[22], (DEPTH, D_MODEL, N_GROUPS), D_MODEL ** -0.5),
        'w_router_expert': nrm(ks[23], (DEPTH, D_MODEL, N_EXPERTS), D_MODEL ** -0.5),
        'w_exp_gate': nrm(ks[24], (DEPTH, N_EXPERTS, D_MODEL, D_EXPERT), D_MODEL ** -0.5),
        'w_exp_up': nrm(ks[25], (DEPTH, N_EXPERTS, D_MODEL, D_EXPERT), D_MODEL ** -0.5),
        'w_exp_down': nrm(ks[26], (DEPTH, N_EXPERTS, D_EXPERT, D_MODEL), D_EXPERT ** -0.5),
        'ln_final': gain(ks[27], (D_MODEL,)),
    }


def reference(x_prompt, x_sample, cache_cmp_kv, cache_sel_kv, state_win_kv, state_conv, page_table,
              ln1, w_in, conv_dw_w, conv_dw_b, conv_ln_g, conv_ln_b, cmp_pos_emb,
              w_cmp_k1, w_cmp_k2, w_cmp_v1, w_cmp_v2, out_norm_conv, out_norm_att, w_out, ln2,
              w_router_group, w_router_expert, w_exp_gate, w_exp_up, w_exp_down, ln_final):
    xp, xs = x_prompt, x_sample
    t_p = xp.shape[1]
    pos_p = jnp.arange(t_p)
    pos_s = PAST_LEN + jnp.arange(xs.shape[1])
    p_cmp, p_sel, p_win, p_conv = [], [], [], []
    s_cmp, s_sel, s_win, s_conv = [], [], [], []
    for l in range(DEPTH):
        cmp_w = (cmp_pos_emb[l], w_cmp_k1[l], w_cmp_k2[l], w_cmp_v1[l], w_cmp_v2[l])
        conv_w = (conv_dw_w[l], conv_dw_b[l], conv_ln_g[l], conv_ln_b[l])
        ffn_w = (out_norm_conv[l], out_norm_att[l], w_out[l], ln2[l], w_router_group[l],
                 w_router_expert[l], w_exp_gate[l], w_exp_up[l], w_exp_down[l])
        a, q, ckv, skv, wkv, g = mix_inputs(xp, pos_p, ln1[l], w_in[l])
        conv_out = conv_module(jnp.pad(a, ((0, 0), (CONV_W - 1, 0), (0, 0))), *conv_w)
        att = nsa_prompt(q, ckv, skv, wkv, g, pos_p, cmp_w)
        xp = merge_and_ffn(xp, conv_out, att, *ffn_w)
        p_cmp.append(ckv)
        p_sel.append(skv)
        p_win.append(wkv[:, t_p - min(WINDOW, t_p):])
        p_conv.append(a[:, t_p - (CONV_W - 1):])
        a, q, ckv, skv, wkv, g = mix_inputs(xs, pos_s, ln1[l], w_in[l])
        a_ext = jnp.concatenate([state_conv[l], a], axis=1)
        conv_out = conv_module(a_ext, *conv_w)
        att, new_win = nsa_sample(q, ckv, skv, wkv, g, pos_s, cache_cmp_kv[l], cache_sel_kv[l],
                                  state_win_kv[l], page_table, cmp_w)
        xs = merge_and_ffn(xs, conv_out, att, *ffn_w)
        s_cmp.append(ckv)
        s_sel.append(skv)
        s_win.append(new_win)
        s_conv.append(a_ext[:, a.shape[1]:])
    y_prompt = rmsnorm(xp, ln_final)
    y_sample = rmsnorm(xs, ln_final)
    return (y_prompt, y_sample,
            jnp.stack(p_cmp), jnp.stack(s_cmp),
            jnp.stack(p_sel), jnp.stack(s_sel),
            jnp.stack(p_win), jnp.stack(s_win),
            jnp.stack(p_conv), jnp.stack(s_conv))
```

```python
import functools

import jax
import jax.numpy as jnp
from jax import lax
from jax.experimental import pallas as pl
from jax.experimental.pallas import tpu as pltpu

D_MODEL = 1024
CONV_CH = 512
CONV_W = 31
N_HEADS = 8
HEAD_DIM = 64
N_KV = 2
Q_PER_KV = N_HEADS // N_KV
ROPE_DIM = HEAD_DIM // 4
ROPE_THETA = 500000.0
CMP_BLOCK = 64
SEL_BLOCK = CMP_BLOCK
N_SEL = 16
WINDOW = 512
CMP_HID = 2 * HEAD_DIM
COL_Q = N_HEADS * HEAD_DIM
COL_KV = 2 * N_KV * HEAD_DIM
COL_GATE = 3 * N_HEADS
D_IN = 2 * CONV_CH + COL_Q + 3 * COL_KV + COL_GATE
N_GROUPS = 4
EXPERTS_PER_GROUP = 8
N_EXPERTS = N_GROUPS * EXPERTS_PER_GROUP
TOP_K = 2
D_EXPERT = 512
PAGE_SIZE = 128
RMS_EPS = 1e-6
LN_EPS = 1e-5
NEG_INF = -1e30
ATT_SCALE = HEAD_DIM ** -0.5

LANES = 128
SUBLANES = 8
VMEM_LIMIT_BYTES = 56 * 1024 * 1024

D_IN_PAD = ((D_IN + LANES - 1) // LANES) * LANES
COL_GATE_OFF = 2 * CONV_CH + COL_Q + 3 * COL_KV
HIST_ROWS = 32
MOE_ROWS = 256
ROW_TILES = D_MODEL // LANES

BF16 = jnp.bfloat16
F32 = jnp.float32


def _cparams(*sem):
    return pltpu.CompilerParams(dimension_semantics=sem, vmem_limit_bytes=VMEM_LIMIT_BYTES)


def _rms(x, g):
    return x * lax.rsqrt(jnp.mean(x * x, axis=-1, keepdims=True) + RMS_EPS) * g


def _rope_tables(pos):
    half = ROPE_DIM // 2
    inv = ROPE_THETA ** (-jnp.arange(half, dtype=F32) / half)
    ang = pos.astype(F32)[:, None] * inv
    cos, sin = jnp.cos(ang), jnp.sin(ang)
    m = jnp.arange(LANES) % HEAD_DIM
    idx = m % half
    c = jnp.where(m < ROPE_DIM, cos[:, idx], 1.0)
    s = jnp.where(m < half, -sin[:, idx], jnp.where(m < ROPE_DIM, sin[:, idx], 0.0))
    return c.astype(F32), s.astype(F32)


def _rope(v, c, s, first_half):
    w = v.shape[1]
    half = ROPE_DIM // 2
    partner = jnp.where(first_half, pltpu.roll(v, w - half, axis=1), pltpu.roll(v, half, axis=1))
    return v * c + partner * s


def _proj_kernel(x_ref, ln_ref, w_ref, c_ref, s_ref, a_ref, q_ref, ckv_ref, skv_ref, wkv_ref, gate_ref):
    x = x_ref[...]
    xn = _rms(x, ln_ref[...])
    p = jnp.dot(xn.astype(BF16), w_ref[...], preferred_element_type=F32)
    a_ref[...] = p[:, :CONV_CH] * jax.nn.sigmoid(p[:, CONV_CH:2 * CONV_CH])

    c128, s128 = c_ref[...], s_ref[...]
    tm = x.shape[0]
    lane_q = lax.broadcasted_iota(jnp.int32, (tm, COL_Q), 1)
    cq = jnp.concatenate([c128] * (COL_Q // LANES), axis=1)
    sq = jnp.concatenate([s128] * (COL_Q // LANES), axis=1)
    o = 2 * CONV_CH
    q = _rope(p[:, o:o + COL_Q], cq, sq, (lane_q % HEAD_DIM) < ROPE_DIM // 2)
    q_ref[...] = q.astype(q_ref.dtype)
    o += COL_Q

    lane_kv = lax.broadcasted_iota(jnp.int32, (tm, COL_KV), 1)
    is_k = (lane_kv % (2 * HEAD_DIM)) < HEAD_DIM
    ckv = jnp.where(is_k, jnp.concatenate([c128] * (COL_KV // LANES), axis=1), 1.0)
    skv = jnp.where(is_k, jnp.concatenate([s128] * (COL_KV // LANES), axis=1), 0.0)
    first_kv = (lane_kv % HEAD_DIM) < ROPE_DIM // 2
    for ref in (ckv_ref, skv_ref, wkv_ref):
        ref[...] = _rope(p[:, o:o + COL_KV], ckv, skv, first_kv)
        o += COL_KV
    gate_ref[...] = jax.nn.sigmoid(p[:, o:o + LANES])


def _project(x2d, ln, w_pad, c_tab, s_tab, tm):
    n = x2d.shape[0]
    t_tiles = c_tab.shape[0] // tm
    row = lambda i: (i, 0)
    tab = lambda i: (i % t_tiles, 0)
    const = lambda i: (0, 0)
    out_shape = (
        jax.ShapeDtypeStruct((n, CONV_CH), F32),
        jax.ShapeDtypeStruct((n, COL_Q), BF16),
        jax.ShapeDtypeStruct((n, COL_KV), F32),
        jax.ShapeDtypeStruct((n, COL_KV), F32),
        jax.ShapeDtypeStruct((n, COL_KV), F32),
        jax.ShapeDtypeStruct((n, LANES), F32),
    )
    return pl.pallas_call(
        _proj_kernel,
        grid=(n // tm,),
        in_specs=[
            pl.BlockSpec((tm, D_MODEL), row),
            pl.BlockSpec((1, D_MODEL), const),
            pl.BlockSpec((D_MODEL, D_IN_PAD), const),
            pl.BlockSpec((tm, LANES), tab),
            pl.BlockSpec((tm, LANES), tab),
        ],
        out_specs=(
            pl.BlockSpec((tm, CONV_CH), row),
            pl.BlockSpec((tm, COL_Q), row),
            pl.BlockSpec((tm, COL_KV), row),
            pl.BlockSpec((tm, COL_KV), row),
            pl.BlockSpec((tm, COL_KV), row),
            pl.BlockSpec((tm, LANES), row),
        ),
        out_shape=out_shape,
        compiler_params=_cparams("arbitrary"),
        name="in_proj",
    )(x2d, ln.reshape(1, D_MODEL), w_pad, c_tab, s_tab)


def _conv_kernel(a_ref, hist_ref, w_ref, b_ref, g_ref, beta_ref, o_ref, ext_ref):
    tt = a_ref.shape[1]

    @pl.when(pl.program_id(1) == 0)
    def _():
        ext_ref[0:HIST_ROWS, :] = hist_ref[0]

    ext_ref[HIST_ROWS:HIST_ROWS + tt, :] = a_ref[0]
    acc = jnp.broadcast_to(b_ref[...], (tt, CONV_CH))
    lead = HIST_ROWS - (CONV_W - 1)
    for k in range(CONV_W):
        acc = acc + w_ref[k:k + 1, :] * ext_ref[lead + k:lead + k + tt, :]
    mu = jnp.mean(acc, axis=-1, keepdims=True)
    var = jnp.mean(jnp.square(acc - mu), axis=-1, keepdims=True)
    y = (acc - mu) * lax.rsqrt(var + LN_EPS) * g_ref[...] + beta_ref[...]
    o_ref[0] = y * jax.nn.sigmoid(y)
    carry = ext_ref[tt:tt + HIST_ROWS, :]
    ext_ref[0:HIST_ROWS, :] = carry


def _conv_module(a3d, hist, dw_w, dw_b, ln_g, ln_b, tt):
    b, t, _ = a3d.shape
    w_pad = jnp.pad(dw_w, ((0, HIST_ROWS - CONV_W), (0, 0)))
    vec = lambda i, j: (0, 0)
    return pl.pallas_call(
        _conv_kernel,
        grid=(b, t // tt),
        in_specs=[
            pl.BlockSpec((1, tt, CONV_CH), lambda i, j: (i, j, 0)),
            pl.BlockSpec((1, HIST_ROWS, CONV_CH), lambda i, j: (i, 0, 0)),
            pl.BlockSpec((HIST_ROWS, CONV_CH), vec),
            pl.BlockSpec((1, CONV_CH), vec),
            pl.BlockSpec((1, CONV_CH), vec),
            pl.BlockSpec((1, CONV_CH), vec),
        ],
        out_specs=pl.BlockSpec((1, tt, CONV_CH), lambda i, j: (i, j, 0)),
        out_shape=jax.ShapeDtypeStruct((b, t, CONV_CH), F32),
        scratch_shapes=[pltpu.VMEM((HIST_ROWS + tt, CONV_CH), F32)],
        compiler_params=_cparams("arbitrary", "arbitrary"),
        name="conv_module",
    )(a3d, hist, w_pad, dw_b.reshape(1, -1), ln_g.reshape(1, -1), ln_b.reshape(1, -1))


L_GROUP = 2 * LANES // HEAD_DIM


def _compress_rows(x_refs, pe_ref, wk1_ref, wk2_ref, wv1_ref, wv2_ref, nb):
    hk = jnp.zeros((N_KV * nb, CMP_HID), F32)
    hv = jnp.zeros((N_KV * nb, CMP_HID), F32)
    for j in range(CMP_BLOCK // L_GROUP):
        parts_k, parts_v = [], []
        for x_ref in x_refs:
            xs = [x_ref[pl.ds(j * L_GROUP + i, nb, stride=CMP_BLOCK), :] + pe_ref[j * L_GROUP + i:j * L_GROUP + i + 1, :]
                  for i in range(L_GROUP)]
            parts_k.append(jnp.concatenate([x[:, :HEAD_DIM] for x in xs], axis=1))
            parts_v.append(jnp.concatenate([x[:, HEAD_DIM:] for x in xs], axis=1))
        xk = jnp.concatenate(parts_k, axis=0).astype(BF16)
        xv = jnp.concatenate(parts_v, axis=0).astype(BF16)
        rows = slice(j * L_GROUP * HEAD_DIM, (j + 1) * L_GROUP * HEAD_DIM)
        hk = hk + jnp.dot(xk, wk1_ref[rows, :], preferred_element_type=F32)
        hv = hv + jnp.dot(xv, wv1_ref[rows, :], preferred_element_type=F32)
    kc = jnp.dot((hk * jax.nn.sigmoid(hk)).astype(BF16), wk2_ref[...], preferred_element_type=F32)
    vc = jnp.dot((hv * jax.nn.sigmoid(hv)).astype(BF16), wv2_ref[...], preferred_element_type=F32)
    return kc, vc


def _compress_dense_kernel(*refs):
    x_refs, (pe_ref, wk1_ref, wk2_ref, wv1_ref, wv2_ref, kc_ref, vc_ref) = refs[:N_KV], refs[N_KV:]
    nb = x_refs[0].shape[1] // CMP_BLOCK
    kc, vc = _compress_rows([x.at[0] for x in x_refs], pe_ref, wk1_ref, wk2_ref, wv1_ref, wv2_ref, nb)
    for g in range(N_KV):
        kc_ref[0, g] = kc[g * nb:(g + 1) * nb]
        vc_ref[0, g] = vc[g * nb:(g + 1) * nb]


def _cmp_weight_specs():
    const = lambda *_: (0, 0)
    return [
        pl.BlockSpec((CMP_BLOCK, 2 * HEAD_DIM), const),
        pl.BlockSpec((CMP_BLOCK * HEAD_DIM, CMP_HID), const),
        pl.BlockSpec((CMP_HID, HEAD_DIM), const),
        pl.BlockSpec((CMP_BLOCK * HEAD_DIM, CMP_HID), const),
        pl.BlockSpec((CMP_HID, HEAD_DIM), const),
    ]


def _cmp_weights(pos_emb, w_k1, w_k2, w_v1, w_v2):
    pe = pos_emb.reshape(CMP_BLOCK, 2 * HEAD_DIM)
    return (pe, w_k1.reshape(-1, CMP_HID).astype(BF16), w_k2.astype(BF16),
            w_v1.reshape(-1, CMP_HID).astype(BF16), w_v2.astype(BF16))


def _compress_dense(kv3d, cmp_w):
    b, t, _ = kv3d.shape
    nb = t // CMP_BLOCK
    out = jax.ShapeDtypeStruct((b, N_KV, nb, HEAD_DIM), F32)
    ospec = pl.BlockSpec((1, N_KV, nb, HEAD_DIM), lambda i: (i, 0, 0, 0))
    return pl.pallas_call(
        _compress_dense_kernel,
        grid=(b,),
        in_specs=[pl.BlockSpec((1, t, 2 * HEAD_DIM), functools.partial(lambda g, i: (i, 0, g), g))
                  for g in range(N_KV)] + _cmp_weight_specs(),
        out_specs=(ospec, ospec),
        out_shape=(out, out),
        compiler_params=_cparams("arbitrary"),
        name="compress_prompt",
    )(*([kv3d] * N_KV), *cmp_w)


PAGES_PER_STEP = 64


def _compress_paged_kernel(pt_ref, cache_ref, pe_ref, wk1_ref, wk2_ref, wv1_ref, wv2_ref, kc_ref, vc_ref,
                           buf_ref, sem_ref):
    step = pl.program_id(0)
    n_steps = pl.num_programs(0)
    nb = PAGES_PER_STEP * PAGE_SIZE // CMP_BLOCK

    kvw = 2 * HEAD_DIM

    def page_copy(s, slot, p, g):
        page = pt_ref[s * PAGES_PER_STEP + p]
        return pltpu.make_async_copy(cache_ref.at[page, :, pl.ds(g * kvw, kvw)],
                                     buf_ref.at[slot, g, pl.ds(p * PAGE_SIZE, PAGE_SIZE), :], sem_ref.at[slot])

    def issue(s, slot):
        for p in range(PAGES_PER_STEP):
            for g in range(N_KV):
                page_copy(s, slot, p, g).start()

    slot = step % 2

    @pl.when(step == 0)
    def _():
        issue(step, slot)

    @pl.when(step + 1 < n_steps)
    def _():
        issue(step + 1, 1 - slot)

    for p in range(PAGES_PER_STEP):
        for g in range(N_KV):
            page_copy(step, slot, p, g).wait()

    kc, vc = _compress_rows([buf_ref.at[slot, g] for g in range(N_KV)], pe_ref, wk1_ref, wk2_ref, wv1_ref, wv2_ref, nb)
    for g in range(N_KV):
        kc_ref[0, g] = kc[g * nb:(g + 1) * nb]
        vc_ref[0, g] = vc[g * nb:(g + 1) * nb]


def _compress_paged(cache3d, page_table, cmp_w):
    db, n_pages = page_table.shape
    steps_per_row = n_pages // PAGES_PER_STEP
    nb = PAGES_PER_STEP * PAGE_SIZE // CMP_BLOCK
    out = jax.ShapeDtypeStruct((db, N_KV, steps_per_row * nb, HEAD_DIM), F32)
    ospec = pl.BlockSpec((1, N_KV, nb, HEAD_DIM), lambda i, pt: (i // steps_per_row, 0, i % steps_per_row, 0))
    grid_spec = pltpu.PrefetchScalarGridSpec(
        num_scalar_prefetch=1,
        grid=(db * steps_per_row,),
        in_specs=[pl.BlockSpec(memory_space=pl.ANY)] + _cmp_weight_specs(),
        out_specs=(ospec, ospec),
        scratch_shapes=[pltpu.VMEM((2, N_KV, PAGES_PER_STEP * PAGE_SIZE, 2 * HEAD_DIM), F32),
                        pltpu.SemaphoreType.DMA((2,))],
    )
    return pl.pallas_call(
        _compress_paged_kernel,
        grid_spec=grid_spec,
        out_shape=(out, out),
        compiler_params=_cparams("arbitrary"),
        name="compress_paged",
    )(page_table.reshape(-1), cache3d, *cmp_w)


TQ = LANES
KC = 512
WIN_KEYS = WINDOW + TQ
KAUG = 2 * LANES


def _top_blocks(imp, cand, n_blocks):
    blk = lax.broadcasted_iota(jnp.int32, imp.shape, 0)
    score = jnp.where(cand, imp, -1.0)
    sel = jnp.zeros(imp.shape, F32)
    for _ in range(N_SEL):
        mx = jnp.max(score, axis=0, keepdims=True)
        idx = jnp.min(jnp.where(score == mx, blk, n_blocks), axis=0, keepdims=True)
        pick = blk == idx
        sel = jnp.where(pick, 1.0, sel)
        score = jnp.where(pick, -2.0, score)
    return sel


def _prompt_attn_kernel(q_ref, kc_ref, vct_ref, ksel_ref, vselt_ref, kwin_ref, vwint_ref, gate_ref, o_ref):
    qt = pl.program_id(2)
    t0 = qt * TQ
    nb = kc_ref.shape[2]
    width = Q_PER_KV * TQ

    q = q_ref[0].astype(F32) * ATT_SCALE
    q_t = q.T
    q4 = jnp.concatenate([q_t[r * HEAD_DIM:(r + 1) * HEAD_DIM] for r in range(Q_PER_KV)], axis=1)
    q4b = q4.astype(BF16)
    tok = t0 + lax.broadcasted_iota(jnp.int32, (1, width), 1) % TQ

    sc = jnp.dot(kc_ref[0, 0].astype(BF16), q4b, preferred_element_type=F32)
    blk = lax.broadcasted_iota(jnp.int32, (nb, width), 0)
    valid_c = (blk + 1) * CMP_BLOCK - 1 <= tok
    sc = jnp.where(valid_c, sc, NEG_INF)
    e = jnp.where(valid_c, jnp.exp(sc - jnp.max(sc, axis=0, keepdims=True)), 0.0)
    den = jnp.sum(e, axis=0, keepdims=True)
    p = e / jnp.where(den > 0.0, den, 1.0)
    o_cmp = jnp.dot(vct_ref[0, 0].astype(BF16), p.astype(BF16), preferred_element_type=F32)
    imp = p[:, 0:TQ]
    for r in range(1, Q_PER_KV):
        imp = imp + p[:, r * TQ:(r + 1) * TQ]

    tok1 = t0 + lax.broadcasted_iota(jnp.int32, (1, TQ), 1)
    own = tok1 // SEL_BLOCK
    blk1 = lax.broadcasted_iota(jnp.int32, (nb, TQ), 0)
    cand = blk1 < own
    sel = _top_blocks(imp, cand, nb)
    bias = jnp.where(cand, jnp.where(sel > 0.0, 0.0, NEG_INF), jnp.where(blk1 == own, 0.0, NEG_INF))
    bias4 = jnp.concatenate([bias] * Q_PER_KV, axis=1).astype(BF16)
    q_aug = jnp.concatenate([q4b, bias4, jnp.zeros((KAUG - HEAD_DIM - nb, width), BF16)], axis=0)

    def sel_step(c, carry, causal):
        m, l, acc = carry
        k0 = pl.multiple_of(c * KC, KC)
        s = jnp.dot(ksel_ref[0, 0, pl.ds(k0, KC), :], q_aug, preferred_element_type=F32)
        if causal:
            key = k0 + lax.broadcasted_iota(jnp.int32, (KC, width), 0)
            s = jnp.where(key <= tok, s, NEG_INF)
        m_new = jnp.maximum(m, jnp.max(s, axis=0, keepdims=True))
        alpha = jnp.exp(m - m_new)
        pr = jnp.exp(s - m_new)
        l = alpha * l + jnp.sum(pr, axis=0, keepdims=True)
        pv = jnp.dot(vselt_ref[0, 0, :, pl.ds(k0, KC)], pr.astype(BF16), preferred_element_type=F32)
        return m_new, l, alpha * acc + pv

    init = (jnp.full((1, width), NEG_INF, F32), jnp.zeros((1, width), F32), jnp.zeros((HEAD_DIM, width), F32))
    last = t0 // KC
    carry = lax.fori_loop(0, last, lambda c, cr: sel_step(c, cr, False), init)
    _, l_sel, acc_sel = sel_step(last, carry, True)
    o_sel = acc_sel / l_sel

    w0 = pl.multiple_of(jnp.maximum(t0 - WINDOW, 0), TQ)
    sw = jnp.dot(kwin_ref[0, 0, pl.ds(w0, WIN_KEYS), :], q4b, preferred_element_type=F32)
    dist = tok - (w0 + lax.broadcasted_iota(jnp.int32, (WIN_KEYS, width), 0))
    sw = jnp.where((dist >= 0) & (dist < WINDOW), sw, NEG_INF)
    pw = jnp.exp(sw - jnp.max(sw, axis=0, keepdims=True))
    l_win = jnp.sum(pw, axis=0, keepdims=True)
    o_win = jnp.dot(vwint_ref[0, 0, :, pl.ds(w0, WIN_KEYS)], pw.astype(BF16), preferred_element_type=F32) / l_win

    outs = []
    for r in range(Q_PER_KV):
        sl = slice(r * TQ, (r + 1) * TQ)
        g = [gate_ref[0, 0, j * Q_PER_KV + r:j * Q_PER_KV + r + 1, :] for j in range(3)]
        outs.append(g[0] * o_cmp[:, sl] + g[1] * o_sel[:, sl] + g[2] * o_win[:, sl])
    o_ref[0] = jnp.concatenate(outs, axis=0).T


def _prompt_attention(q3, k_c, v_ct, ksel_aug, vsel_t, kwin, vwin_t, gates_t):
    b, t, _ = q3.shape
    nb = k_c.shape[2]
    width = Q_PER_KV * HEAD_DIM
    per_bg = lambda i, g, j: (i, g, 0, 0)
    return pl.pallas_call(
        _prompt_attn_kernel,
        grid=(b, N_KV, t // TQ),
        in_specs=[
            pl.BlockSpec((1, TQ, width), lambda i, g, j: (i, j, g)),
            pl.BlockSpec((1, 1, nb, HEAD_DIM), per_bg),
            pl.BlockSpec((1, 1, HEAD_DIM, nb), per_bg),
            pl.BlockSpec((1, 1, t, KAUG), per_bg),
            pl.BlockSpec((1, 1, HEAD_DIM, t), per_bg),
            pl.BlockSpec((1, 1, t, HEAD_DIM), per_bg),
            pl.BlockSpec((1, 1, HEAD_DIM, t), per_bg),
            pl.BlockSpec((1, 1, 3 * Q_PER_KV, TQ), lambda i, g, j: (i, g, 0, j)),
        ],
        out_specs=pl.BlockSpec((1, TQ, width), lambda i, g, j: (i, j, g)),
        out_shape=jax.ShapeDtypeStruct((b, t, COL_Q), F32),
        compiler_params=_cparams("arbitrary", "arbitrary", "arbitrary"),
        name="prompt_attention",
    )(q3, k_c, v_ct, ksel_aug, vsel_t, kwin, vwin_t, gates_t)


def _split_kv(kv3d):
    b, t, _ = kv3d.shape
    kv = kv3d.reshape(b, t, N_KV, 2, HEAD_DIM).astype(BF16)
    return jnp.transpose(kv[:, :, :, 0], (0, 2, 1, 3)), jnp.transpose(kv[:, :, :, 1], (0, 2, 3, 1))


def _augment_keys(k):
    b, g, t, _ = k.shape
    nb = t // SEL_BLOCK
    onehot = (jnp.arange(t)[:, None] // SEL_BLOCK == jnp.arange(nb)[None, :]).astype(BF16)
    onehot = jnp.broadcast_to(onehot, (b, g, t, nb))
    pad = jnp.zeros((b, g, t, KAUG - HEAD_DIM - nb), BF16)
    return jnp.concatenate([k, onehot, pad], axis=-1)


def _sample_cmp_kernel(q_ref, kc_ref, vc_ref, ocmp_ref, idx_ref, imp_ref, *, past_len, dec_seq):
    b = pl.program_id(0)
    nb = kc_ref.shape[2]
    rows = Q_PER_KV * SUBLANES
    t_row = lax.broadcasted_iota(jnp.int32, (rows, nb), 0) % SUBLANES
    blk = lax.broadcasted_iota(jnp.int32, (rows, nb), 1)
    valid = (blk + 1) * CMP_BLOCK - 1 <= past_len + t_row
    for g in range(N_KV):
        s = lax.dot_general(q_ref[0, g], kc_ref[0, g].astype(BF16), (((1,), (1,)), ((), ())),
                            preferred_element_type=F32) * ATT_SCALE
        s = jnp.where(valid, s, NEG_INF)
        e = jnp.where(valid, jnp.exp(s - jnp.max(s, axis=1, keepdims=True)), 0.0)
        den = jnp.sum(e, axis=1, keepdims=True)
        p = e / jnp.where(den > 0.0, den, 1.0)
        ocmp_ref[0, g] = jnp.dot(p.astype(BF16), vc_ref[0, g].astype(BF16), preferred_element_type=F32)
        imp = p[0:SUBLANES]
        for r in range(1, Q_PER_KV):
            imp = imp + p[r * SUBLANES:(r + 1) * SUBLANES]
        row0 = pl.multiple_of((b * N_KV + g) * SUBLANES, SUBLANES)
        imp_ref[pl.ds(row0, SUBLANES), :] = imp

    @pl.when(b == pl.num_programs(0) - 1)
    def _():
        n_rows = imp_ref.shape[0]
        lane = lax.broadcasted_iota(jnp.int32, (n_rows, nb), 1)
        own = (past_len + lax.broadcasted_iota(jnp.int32, (n_rows, nb), 0) % SUBLANES) // SEL_BLOCK
        score = jnp.where(lane < own, imp_ref[...], -1.0)
        col = lax.broadcasted_iota(jnp.int32, (n_rows, LANES), 1)
        picks = jnp.zeros((n_rows, LANES), jnp.int32)
        for i in range(N_SEL):
            mx = jnp.max(score, axis=1, keepdims=True)
            idx = jnp.min(jnp.where(score == mx, lane, nb), axis=1, keepdims=True)
            score = jnp.where(lane == idx, -2.0, score)
            picks = jnp.where(col == i, idx, picks)
        idx_ref[...] = picks


def _sample_cmp_select(q_rt, k_c, v_c, past_len, dec_seq):
    db, _, rows, _ = q_rt.shape
    nb = k_c.shape[2]
    spec4 = lambda r, c: pl.BlockSpec((1, N_KV, r, c), lambda i: (i, 0, 0, 0))
    n_rows = db * N_KV * SUBLANES
    return pl.pallas_call(
        functools.partial(_sample_cmp_kernel, past_len=past_len, dec_seq=dec_seq),
        grid=(db,),
        in_specs=[spec4(rows, HEAD_DIM), spec4(nb, HEAD_DIM), spec4(nb, HEAD_DIM)],
        out_specs=(spec4(rows, HEAD_DIM), pl.BlockSpec((n_rows, LANES), lambda i: (0, 0))),
        out_shape=(jax.ShapeDtypeStruct((db, N_KV, rows, HEAD_DIM), F32),
                   jax.ShapeDtypeStruct((n_rows, LANES), jnp.int32)),
        scratch_shapes=[pltpu.VMEM((n_rows, nb), F32)],
        compiler_params=_cparams("arbitrary"),
        name="sample_cmp_select",
    )(q_rt, k_c, v_c)


SEL_KEYS = N_SEL * SEL_BLOCK
OWN_ROWS = LANES


def _sample_attn_kernel(idx_ref, pt_ref, cache_ref, q_ref, snew_ref, wnew_ref, wstate_ref, ocmp_ref, gate_ref,
                        o_ref, kv_ref, sem_ref, *, dec_seq, n_pages):
    b = pl.program_id(0)
    kvw = 2 * HEAD_DIM

    def block_copy(g, t, i):
        blk = idx_ref[((b * N_KV + g) * dec_seq + t) * N_SEL + i]
        page = pt_ref[b * n_pages + blk // (PAGE_SIZE // SEL_BLOCK)]
        off = pl.multiple_of((blk % (PAGE_SIZE // SEL_BLOCK)) * SEL_BLOCK, SEL_BLOCK)
        return pltpu.make_async_copy(cache_ref.at[page, pl.ds(off, SEL_BLOCK), pl.ds(g * kvw, kvw)],
                                     kv_ref.at[g, t, pl.ds(i * SEL_BLOCK, SEL_BLOCK), :], sem_ref.at[0])

    for g in range(N_KV):
        for t in range(dec_seq):
            for i in range(N_SEL):
                block_copy(g, t, i).start()

    rows = dec_seq * SUBLANES
    tok = lax.broadcasted_iota(jnp.int32, (rows, 1), 0) // SUBLANES
    zeros_tail = jnp.zeros((OWN_ROWS - SUBLANES, kvw), F32)

    o_win = []
    for g in range(N_KV):
        kvwin = jnp.concatenate([wstate_ref[0, :, g * kvw:(g + 1) * kvw], wnew_ref[0, :, g * kvw:(g + 1) * kvw],
                                 zeros_tail], axis=0).astype(BF16)
        n_state = wstate_ref.shape[1]
        s = lax.dot_general(q_ref[0, g], kvwin, (((1,), (1,)), ((), ())), preferred_element_type=F32) * ATT_SCALE
        lane = lax.broadcasted_iota(jnp.int32, s.shape, 1)
        new_i = lane - n_state
        valid = ((lane < n_state) & (lane > tok)) | ((new_i >= 0) & (new_i <= tok) & (new_i < dec_seq))
        s = jnp.where(valid, s, NEG_INF)
        p = jnp.exp(s - jnp.max(s, axis=1, keepdims=True))
        den = jnp.sum(p, axis=1, keepdims=True)
        o_win.append(jnp.dot(p.astype(BF16), kvwin, preferred_element_type=F32) / den)

    for g in range(N_KV):
        for t in range(dec_seq):
            kv_ref[g, t, SEL_KEYS:SEL_KEYS + OWN_ROWS, :] = jnp.concatenate(
                [snew_ref[0, :, g * kvw:(g + 1) * kvw], zeros_tail], axis=0)

    for g in range(N_KV):
        for t in range(dec_seq):
            for i in range(N_SEL):
                block_copy(g, t, i).wait()

    for g in range(N_KV):
        o_sel = []
        for t in range(dec_seq):
            kv = kv_ref[g, t].astype(BF16)
            q = q_ref[0, g, t * SUBLANES:(t + 1) * SUBLANES, :]
            s = lax.dot_general(q, kv, (((1,), (1,)), ((), ())), preferred_element_type=F32) * ATT_SCALE
            lane = lax.broadcasted_iota(jnp.int32, s.shape, 1)
            new_i = lane - SEL_KEYS
            s = jnp.where((lane < SEL_KEYS) | ((new_i <= t) & (new_i < dec_seq)), s, NEG_INF)
            p = jnp.exp(s - jnp.max(s, axis=1, keepdims=True))
            den = jnp.sum(p, axis=1, keepdims=True)
            o_sel.append(jnp.dot(p.astype(BF16), kv, preferred_element_type=F32) / den)
        o_sel = jnp.concatenate(o_sel, axis=0)
        o_ref[0, g] = (gate_ref[0, g, 0] * ocmp_ref[0, g] + gate_ref[0, g, 1] * o_sel
                       + gate_ref[0, g, 2] * o_win[g])


def _sample_attention(sel_idx, page_table, cache3d, q_tr, snew, wnew, wstate, ocmp_tr, gates_tr, dec_seq):
    db, n_pages = page_table.shape
    rows = dec_seq * SUBLANES
    kvw = 2 * HEAD_DIM
    n_state = wstate.shape[1]
    per_b4 = lambda i, *_: (i, 0, 0, 0)
    per_b3 = lambda i, *_: (i, 0, 0)
    grid_spec = pltpu.PrefetchScalarGridSpec(
        num_scalar_prefetch=2,
        grid=(db,),
        in_specs=[
            pl.BlockSpec(memory_space=pl.ANY),
            pl.BlockSpec((1, N_KV, rows, kvw), per_b4),
            pl.BlockSpec((1, SUBLANES, COL_KV), per_b3),
            pl.BlockSpec((1, SUBLANES, COL_KV), per_b3),
            pl.BlockSpec((1, n_state, COL_KV), per_b3),
            pl.BlockSpec((1, N_KV, rows, kvw), per_b4),
            pl.BlockSpec((1, N_KV, 3, rows, kvw), lambda i, *_: (i, 0, 0, 0, 0)),
        ],
        out_specs=pl.BlockSpec((1, N_KV, rows, kvw), per_b4),
        scratch_shapes=[pltpu.VMEM((N_KV, dec_seq, SEL_KEYS + OWN_ROWS, kvw), F32), pltpu.SemaphoreType.DMA((1,))],
    )
    return pl.pallas_call(
        functools.partial(_sample_attn_kernel, dec_seq=dec_seq, n_pages=n_pages),
        grid_spec=grid_spec,
        out_shape=jax.ShapeDtypeStruct((db, N_KV, rows, kvw), F32),
        compiler_params=_cparams("arbitrary"),
        name="sample_attention",
    )(sel_idx, page_table.reshape(-1), cache3d, q_tr, snew, wnew, wstate, ocmp_tr, gates_tr)


ROUTE_E1, ROUTE_E2, ROUTE_W1, ROUTE_W2 = 0, 1, 2, 3


def _merge_kernel(conv_ref, att_ref, x_ref, gc_ref, ga_ref, wo_ref, ln2_ref, wr_ref, x1_ref, h3_ref, route_ref):
    mix = jnp.concatenate([_rms(conv_ref[...], gc_ref[...]), _rms(att_ref[...], ga_ref[...])], axis=1)
    x1 = x_ref[...] + jnp.dot(mix.astype(BF16), wo_ref[...], preferred_element_type=F32)
    x1_ref[...] = x1
    h = _rms(x1, ln2_ref[...])
    for j in range(ROW_TILES):
        h3_ref[:, j, :] = h[:, j * LANES:(j + 1) * LANES]

    logits = jnp.dot(h.astype(BF16), wr_ref[...], preferred_element_type=F32)
    lane = lax.broadcasted_iota(jnp.int32, logits.shape, 1)
    is_g = lane < N_GROUPS
    lg = jnp.where(is_g, logits, NEG_INF)
    mg = jnp.max(lg, axis=1, keepdims=True)
    sg = jnp.sum(jnp.where(is_g, jnp.exp(lg - mg), 0.0), axis=1, keepdims=True)
    grp = jnp.min(jnp.where(lg == mg, lane, LANES), axis=1, keepdims=True)
    p_top = 1.0 / sg
    in_grp = ((lane + (EXPERTS_PER_GROUP - N_GROUPS)) // EXPERTS_PER_GROUP) == grp + 1
    le = jnp.where(in_grp, logits, NEG_INF)
    ee = jnp.where(in_grp, jnp.exp(le - jnp.max(le, axis=1, keepdims=True)), 0.0)
    pe = jnp.where(in_grp, ee / jnp.sum(ee, axis=1, keepdims=True), -1.0)
    p1 = jnp.max(pe, axis=1, keepdims=True)
    i1 = jnp.min(jnp.where(pe == p1, lane, LANES), axis=1, keepdims=True)
    pe2 = jnp.where(lane == i1, -1.0, pe)
    p2 = jnp.max(pe2, axis=1, keepdims=True)
    i2 = jnp.min(jnp.where(pe2 == p2, lane, LANES), axis=1, keepdims=True)
    den = p1 + p2
    rec = jnp.where(lane == ROUTE_E1, (i1 - N_GROUPS).astype(F32), 0.0)
    rec = jnp.where(lane == ROUTE_E2, (i2 - N_GROUPS).astype(F32), rec)
    rec = jnp.where(lane == ROUTE_W1, p1 / den * p_top, rec)
    rec = jnp.where(lane == ROUTE_W2, p2 / den * p_top, rec)
    route_ref[...] = rec


def _merge(conv2d, att2d, x2d, g_conv, g_att, w_out_b, ln2, w_route_b, tm):
    n = x2d.shape[0]
    row = lambda i: (i, 0)
    const = lambda i: (0, 0)
    return pl.pallas_call(
        _merge_kernel,
        grid=(n // tm,),
        in_specs=[
            pl.BlockSpec((tm, CONV_CH), row),
            pl.BlockSpec((tm, COL_Q), row),
            pl.BlockSpec((tm, D_MODEL), row),
            pl.BlockSpec((1, CONV_CH), const),
            pl.BlockSpec((1, COL_Q), const),
            pl.BlockSpec((CONV_CH + COL_Q, D_MODEL), const),
            pl.BlockSpec((1, D_MODEL), const),
            pl.BlockSpec((D_MODEL, LANES), const),
        ],
        out_specs=(
            pl.BlockSpec((tm, D_MODEL), row),
            pl.BlockSpec((tm, ROW_TILES, LANES), lambda i: (i, 0, 0)),
            pl.BlockSpec((tm, LANES), row),
        ),
        out_shape=(
            jax.ShapeDtypeStruct((n, D_MODEL), F32),
            jax.ShapeDtypeStruct((n, ROW_TILES, LANES), F32),
            jax.ShapeDtypeStruct((n, LANES), F32),
        ),
        compiler_params=_cparams("arbitrary"),
        name="merge_route",
    )(conv2d, att2d, x2d, g_conv.reshape(1, -1), g_att.reshape(1, -1), w_out_b, ln2.reshape(1, -1), w_route_b)


def _dispatch_tables(route, n_tok):
    n_asg = n_tok * TOP_K
    e_flat = route[:, ROUTE_E1:ROUTE_E2 + 1].astype(jnp.int32).reshape(n_asg)
    onehot = (e_flat[:, None] == jnp.arange(N_EXPERTS)[None, :]).astype(jnp.int32)
    counts = jnp.sum(onehot, axis=0)
    rank = jnp.take_along_axis(jnp.cumsum(onehot, axis=0) - onehot, e_flat[:, None], axis=1)[:, 0]
    padded = (counts + MOE_ROWS - 1) // MOE_ROWS * MOE_ROWS
    pend = jnp.cumsum(padded)
    pstart = pend - padded
    dest = pstart[e_flat] + rank
    n_chunks = n_asg // MOE_ROWS + N_EXPERTS
    asg = jnp.arange(n_asg)
    slot_src = jnp.zeros((n_chunks * MOE_ROWS,), jnp.int32).at[dest].set(asg // TOP_K)
    slot_dst = jnp.zeros((n_chunks * MOE_ROWS,), jnp.int32).at[dest].set((asg % TOP_K) * n_tok + asg // TOP_K)
    chunk_start = jnp.arange(n_chunks) * MOE_ROWS
    chunk_e = jnp.clip(jnp.searchsorted(pend, chunk_start, side='right'), 0, N_EXPERTS - 1).astype(jnp.int32)
    n_valid = jnp.clip(counts[chunk_e] - (chunk_start - pstart[chunk_e]), 0, MOE_ROWS).astype(jnp.int32)
    shape3 = (n_chunks, 1, MOE_ROWS)
    return chunk_e, n_valid, slot_src.reshape(shape3), slot_dst.reshape(shape3)


def _moe_kernel(ce_ref, nv_ref, src_ref, dst_ref, h3_ref, wg_ref, wu_ref, wd_ref, y_ref, xbuf_ref, ybuf_ref, sem_ref):
    c = pl.program_id(0)
    nv = nv_ref[c]

    @pl.when(c == 0)
    def _():
        xbuf_ref[...] = jnp.zeros(xbuf_ref.shape, F32)

    @pl.when(nv > 0)
    def _():
        def row_in(i):
            return pltpu.make_async_copy(h3_ref.at[src_ref[0, 0, i]], xbuf_ref.at[i], sem_ref.at[0])

        def row_out(i):
            return pltpu.make_async_copy(ybuf_ref.at[i], y_ref.at[dst_ref[0, 0, i]], sem_ref.at[1])

        def start_in(i, carry):
            row_in(i).start()
            return carry

        def wait_in(i, carry):
            row_in(i).wait()
            return carry

        lax.fori_loop(0, nv, start_in, 0)
        lax.fori_loop(0, nv, wait_in, 0)
        x = jnp.concatenate([xbuf_ref[:, j, :] for j in range(ROW_TILES)], axis=1).astype(BF16)
        gate = jnp.dot(x, wg_ref[0].astype(BF16), preferred_element_type=F32)
        up = jnp.dot(x, wu_ref[0].astype(BF16), preferred_element_type=F32)
        act = (gate * jax.nn.sigmoid(gate) * up).astype(BF16)
        y = jnp.dot(act, wd_ref[0].astype(BF16), preferred_element_type=F32)
        for j in range(ROW_TILES):
            ybuf_ref[:, j, :] = y[:, j * LANES:(j + 1) * LANES]

        def start_out(i, carry):
            row_out(i).start()
            return carry

        def wait_out(i, carry):
            row_out(i).wait()
            return carry

        lax.fori_loop(0, nv, start_out, 0)
        lax.fori_loop(0, nv, wait_out, 0)


def _moe(h3, tables, w_g, w_u, w_d):
    n_tok = h3.shape[0]
    chunk_e, n_valid, slot_src, slot_dst = tables
    n_chunks = chunk_e.shape[0]
    slot_spec = pl.BlockSpec((1, 1, MOE_ROWS), lambda c, ce, nv: (c, 0, 0), memory_space=pltpu.SMEM)
    expert = lambda c, ce, nv: (ce[c], 0, 0)
    grid_spec = pltpu.PrefetchScalarGridSpec(
        num_scalar_prefetch=2,
        grid=(n_chunks,),
        in_specs=[
            slot_spec, slot_spec,
            pl.BlockSpec(memory_space=pl.ANY),
            pl.BlockSpec((1, D_MODEL, D_EXPERT), expert),
            pl.BlockSpec((1, D_MODEL, D_EXPERT), expert),
            pl.BlockSpec((1, D_EXPERT, D_MODEL), expert),
        ],
        out_specs=pl.BlockSpec(memory_space=pl.ANY),
        scratch_shapes=[pltpu.VMEM((MOE_ROWS, ROW_TILES, LANES), F32), pltpu.VMEM((MOE_ROWS, ROW_TILES, LANES), F32),
                        pltpu.SemaphoreType.DMA((2,))],
    )
    return pl.pallas_call(
        _moe_kernel,
        grid_spec=grid_spec,
        out_shape=jax.ShapeDtypeStruct((TOP_K * n_tok, ROW_TILES, LANES), F32),
        compiler_params=_cparams("arbitrary"),
        name="expert_mlp",
    )(chunk_e, n_valid, slot_src, slot_dst, h3, w_g, w_u, w_d)


def _final_kernel(x1_ref, ya_ref, yb_ref, route_ref, ln_ref, o_ref):
    ya = jnp.concatenate([ya_ref[:, j, :] for j in range(ROW_TILES)], axis=1)
    yb = jnp.concatenate([yb_ref[:, j, :] for j in range(ROW_TILES)], axis=1)
    route = route_ref[...]
    moe = ya * route[:, ROUTE_W1:ROUTE_W1 + 1] + yb * route[:, ROUTE_W2:ROUTE_W2 + 1]
    o_ref[...] = _rms(x1_ref[...] + moe, ln_ref[...])


def _final(x1, y3, route, ln_final, tm):
    n = x1.shape[0]
    tiles = n // tm
    row = lambda i: (i, 0)
    return pl.pallas_call(
        _final_kernel,
        grid=(tiles,),
        in_specs=[
            pl.BlockSpec((tm, D_MODEL), row),
            pl.BlockSpec((tm, ROW_TILES, LANES), lambda i: (i, 0, 0)),
            pl.BlockSpec((tm, ROW_TILES, LANES), lambda i: (i + tiles, 0, 0)),
            pl.BlockSpec((tm, LANES), row),
            pl.BlockSpec((1, D_MODEL), lambda i: (0, 0)),
        ],
        out_specs=pl.BlockSpec((tm, D_MODEL), row),
        out_shape=jax.ShapeDtypeStruct((n, D_MODEL), F32),
        compiler_params=_cparams("arbitrary"),
        name="moe_combine_norm",
    )(x1, y3, y3, route, ln_final.reshape(1, -1))


def _ffn(conv2d, att2d, x2d, g_conv, g_att, w_out_b, ln2, w_route_b, w_g, w_u, w_d, ln_final, tm):
    x1, h3, route = _merge(conv2d, att2d, x2d, g_conv, g_att, w_out_b, ln2, w_route_b, tm)
    y3 = _moe(h3, _dispatch_tables(route, x2d.shape[0]), w_g, w_u, w_d)
    return _final(x1, y3, route, ln_final, tm)


def kernel(x_prompt, x_sample, cache_cmp_kv, cache_sel_kv, state_win_kv, state_conv, page_table, ln1, w_in, conv_dw_w, conv_dw_b, conv_ln_g, conv_ln_b, cmp_pos_emb, w_cmp_k1, w_cmp_k2, w_cmp_v1, w_cmp_v2, out_norm_conv, out_norm_att, w_out, ln2, w_router_group, w_router_expert, w_exp_gate, w_exp_up, w_exp_down, ln_final):
    depth = ln1.shape[0]
    assert depth == 1, "single-layer step"
    b, t, _ = x_prompt.shape
    db, ds, _ = x_sample.shape
    n_phys = cache_cmp_kv.shape[1]
    n_pages = page_table.shape[1]
    past = n_pages * PAGE_SIZE
    win_rows = state_win_kv.shape[2]
    assert ds < CMP_BLOCK and ds <= SUBLANES and past % SEL_BLOCK == 0 and past // SEL_BLOCK >= N_SEL
    assert win_rows == WINDOW and past >= WINDOW and t % KC == 0 and t >= WIN_KEYS

    w_in_b = jnp.pad(w_in[0], ((0, 0), (0, D_IN_PAD - D_IN))).astype(BF16)
    w_out_b = w_out[0].astype(BF16)
    w_route_b = jnp.pad(jnp.concatenate([w_router_group[0], w_router_expert[0]], axis=1),
                        ((0, 0), (0, LANES - N_GROUPS - N_EXPERTS))).astype(BF16)
    cmp_w = _cmp_weights(cmp_pos_emb[0], w_cmp_k1[0], w_cmp_k2[0], w_cmp_v1[0], w_cmp_v2[0])
    conv_w = (conv_dw_w[0], conv_dw_b[0], conv_ln_g[0], conv_ln_b[0])
    ffn_w = (out_norm_conv[0], out_norm_att[0], w_out_b, ln2[0], w_route_b, w_exp_gate[0], w_exp_up[0],
             w_exp_down[0], ln_final)

    xp2 = x_prompt.reshape(b * t, D_MODEL)
    c_p, s_p = _rope_tables(jnp.arange(t))
    a_p, q_p, ckv_p, skv_p, wkv_p, gate_p = _project(xp2, ln1[0], w_in_b, c_p, s_p, 512)
    a_p3 = a_p.reshape(b, t, CONV_CH)
    conv_p = _conv_module(a_p3, jnp.zeros((b, HIST_ROWS, CONV_CH), F32), *conv_w, 512)
    kc_p, vc_p = _compress_dense(ckv_p.reshape(b, t, COL_KV), cmp_w)
    ksel_p, vsel_t = _split_kv(skv_p.reshape(b, t, COL_KV))
    kwin_p, vwin_t = _split_kv(wkv_p.reshape(b, t, COL_KV))
    gates_t = jnp.transpose(gate_p[:, :COL_GATE].reshape(b, t, N_KV, Q_PER_KV, 3), (0, 2, 4, 3, 1))
    att_p = _prompt_attention(q_p.reshape(b, t, COL_Q), kc_p, jnp.swapaxes(vc_p, 2, 3), _augment_keys(ksel_p),
                              vsel_t, kwin_p, vwin_t, gates_t.reshape(b, N_KV, 3 * Q_PER_KV, t))
    y_p = _ffn(conv_p.reshape(b * t, CONV_CH), att_p.reshape(b * t, COL_Q), xp2, *ffn_w, 256)

    n_s = db * ds
    xs2 = x_sample.reshape(n_s, D_MODEL)
    c_s, s_s = _rope_tables(jnp.tile(past + jnp.arange(ds), db))
    a_s, q_s, ckv_s, skv_s, wkv_s, gate_s = _project(xs2, ln1[0], w_in_b, c_s, s_s, n_s)
    a_s3 = a_s.reshape(db, ds, CONV_CH)
    hist_s = jnp.pad(state_conv[0], ((0, 0), (HIST_ROWS - (CONV_W - 1), 0), (0, 0)))
    conv_s = _conv_module(a_s3, hist_s, *conv_w, ds)
    kc_s, vc_s = _compress_paged(cache_cmp_kv[0].reshape(n_phys, PAGE_SIZE, COL_KV), page_table, cmp_w)

    q5 = q_s.reshape(db, ds, N_KV, Q_PER_KV, HEAD_DIM)
    pad_tok = SUBLANES - ds
    pad_head = SUBLANES - Q_PER_KV
    q_rt = jnp.pad(jnp.transpose(q5, (0, 2, 3, 1, 4)), ((0, 0), (0, 0), (0, 0), (0, pad_tok), (0, 0)))
    q_rt = q_rt.reshape(db, N_KV, Q_PER_KV * SUBLANES, HEAD_DIM)
    q_tr = jnp.pad(jnp.transpose(q5, (0, 2, 1, 3, 4)), ((0, 0), (0, 0), (0, 0), (0, pad_head), (0, HEAD_DIM)))
    q_tr = q_tr.reshape(db, N_KV, ds * SUBLANES, 2 * HEAD_DIM)
    g5 = gate_s[:, :COL_GATE].reshape(db, ds, N_KV, Q_PER_KV, 3)
    g_tr = jnp.pad(jnp.transpose(g5, (0, 2, 4, 1, 3)), ((0, 0),) * 4 + ((0, pad_head),))
    g_tr = jnp.broadcast_to(g_tr.reshape(db, N_KV, 3, ds * SUBLANES)[..., None], (db, N_KV, 3, ds * SUBLANES, 2 * HEAD_DIM))
    ocmp, picks = _sample_cmp_select(q_rt, kc_s, vc_s, past, ds)
    sel_idx = picks.reshape(db, N_KV, SUBLANES, LANES)[:, :, :ds, :N_SEL].reshape(-1)
    ocmp_tr = jnp.transpose(ocmp.reshape(db, N_KV, Q_PER_KV, SUBLANES, HEAD_DIM)[:, :, :, :ds], (0, 1, 3, 2, 4))
    ocmp_tr = jnp.pad(ocmp_tr, ((0, 0), (0, 0), (0, 0), (0, pad_head), (HEAD_DIM, 0)))
    ocmp_tr = ocmp_tr.reshape(db, N_KV, ds * SUBLANES, 2 * HEAD_DIM)
    pad_rows = lambda kv: jnp.pad(kv.reshape(db, ds, COL_KV), ((0, 0), (0, pad_tok), (0, 0)))
    o_s = _sample_attention(sel_idx, page_table, cache_sel_kv[0].reshape(n_phys, PAGE_SIZE, COL_KV), q_tr,
                            pad_rows(skv_s), pad_rows(wkv_s), state_win_kv[0].reshape(db, win_rows, COL_KV),
                            ocmp_tr, g_tr, ds)
    att_s = o_s[..., HEAD_DIM:].reshape(db, N_KV, ds, SUBLANES, HEAD_DIM)[:, :, :, :Q_PER_KV]
    att_s = jnp.transpose(att_s, (0, 2, 1, 3, 4)).reshape(n_s, COL_Q)
    y_s = _ffn(conv_s.reshape(n_s, CONV_CH), att_s, xs2, *ffn_w, n_s)

    kv6 = lambda kv, bb, tt: kv.reshape(1, bb, tt, N_KV, 2, HEAD_DIM)
    wkv_p6 = kv6(wkv_p, b, t)
    new_win_s = jnp.concatenate([state_win_kv, kv6(wkv_s, db, ds)], axis=2)[:, :, ds:]
    new_conv_s = jnp.concatenate([state_conv[0], a_s3], axis=1)[None, :, ds:]
    return (y_p.reshape(b, t, D_MODEL), y_s.reshape(db, ds, D_MODEL),
            kv6(ckv_p, b, t), kv6(ckv_s, db, ds), kv6(skv_p, b, t), kv6(skv_s, db, ds),
            wkv_p6[:, :, t - min(WINDOW, t):], new_win_s,
            a_p3[None, :, t - (CONV_W - 1):], new_conv_s)
```

```python
import functools

import jax
import jax.numpy as jnp
from jax import lax
from jax.experimental import pallas as pl
from jax.experimental.pallas import tpu as pltpu

D_MODEL = 1024
CONV_CH = 512
CONV_W = 31
N_HEADS = 8
HEAD_DIM = 64
N_KV = 2
Q_PER_KV = N_HEADS // N_KV
ROPE_DIM = HEAD_DIM // 4
ROPE_THETA = 500000.0
CMP_BLOCK = 64
SEL_BLOCK = CMP_BLOCK
N_SEL = 16
WINDOW = 512
CMP_HID = 2 * HEAD_DIM
COL_Q = N_HEADS * HEAD_DIM
COL_KV = 2 * N_KV * HEAD_DIM
COL_GATE = 3 * N_HEADS
D_IN = 2 * CONV_CH + COL_Q + 3 * COL_KV + COL_GATE
N_GROUPS = 4
EXPERTS_PER_GROUP = 8
N_EXPERTS = N_GROUPS * EXPERTS_PER_GROUP
TOP_K = 2
D_EXPERT = 512
PAGE_SIZE = 128
RMS_EPS = 1e-6
LN_EPS = 1e-5
NEG_INF = -1e30
ATT_SCALE = HEAD_DIM ** -0.5

LANES = 128
SUBLANES = 8
VMEM_LIMIT_BYTES = 56 * 1024 * 1024

D_IN_PAD = ((D_IN + LANES - 1) // LANES) * LANES
COL_GATE_OFF = 2 * CONV_CH + COL_Q + 3 * COL_KV
HIST_ROWS = 32
MOE_ROWS = 256
ROW_TILES = D_MODEL // LANES

BF16 = jnp.bfloat16
F32 = jnp.float32


def _cparams(*sem):
    return pltpu.CompilerParams(dimension_semantics=sem, vmem_limit_bytes=VMEM_LIMIT_BYTES)


def _rms(x, g):
    return x * lax.rsqrt(jnp.mean(x * x, axis=-1, keepdims=True) + RMS_EPS) * g


def _rope_tables(pos):
    half = ROPE_DIM // 2
    inv = ROPE_THETA ** (-jnp.arange(half, dtype=F32) / half)
    ang = pos.astype(F32)[:, None] * inv
    cos, sin = jnp.cos(ang), jnp.sin(ang)
    m = jnp.arange(LANES) % HEAD_DIM
    idx = m % half
    c = jnp.where(m < ROPE_DIM, cos[:, idx], 1.0)
    s = jnp.where(m < half, -sin[:, idx], jnp.where(m < ROPE_DIM, sin[:, idx], 0.0))
    return c.astype(F32), s.astype(F32)


def _rope(v, c, s, first_half):
    w = v.shape[1]
    half = ROPE_DIM // 2
    partner = jnp.where(first_half, pltpu.roll(v, w - half, axis=1), pltpu.roll(v, half, axis=1))
    return v * c + partner * s


def _proj_kernel(x_ref, ln_ref, w_ref, c_ref, s_ref, a_ref, q_ref, ckv_ref, skv_ref, wkv_ref, gate_ref):
    x = x_ref[...]
    xn = _rms(x, ln_ref[...])
    p = jnp.dot(xn.astype(BF16), w_ref[...], preferred_element_type=F32)
    a_ref[...] = p[:, :CONV_CH] * jax.nn.sigmoid(p[:, CONV_CH:2 * CONV_CH])

    c128, s128 = c_ref[...], s_ref[...]
    tm = x.shape[0]
    lane_q = lax.broadcasted_iota(jnp.int32, (tm, COL_Q), 1)
    cq = jnp.concatenate([c128] * (COL_Q // LANES), axis=1)
    sq = jnp.concatenate([s128] * (COL_Q // LANES), axis=1)
    o = 2 * CONV_CH
    q = _rope(p[:, o:o + COL_Q], cq, sq, (lane_q % HEAD_DIM) < ROPE_DIM // 2)
    q_ref[...] = q.astype(q_ref.dtype)
    o += COL_Q

    lane_kv = lax.broadcasted_iota(jnp.int32, (tm, COL_KV), 1)
    is_k = (lane_kv % (2 * HEAD_DIM)) < HEAD_DIM
    ckv = jnp.where(is_k, jnp.concatenate([c128] * (COL_KV // LANES), axis=1), 1.0)
    skv = jnp.where(is_k, jnp.concatenate([s128] * (COL_KV // LANES), axis=1), 0.0)
    first_kv = (lane_kv % HEAD_DIM) < ROPE_DIM // 2
    for ref in (ckv_ref, skv_ref, wkv_ref):
        ref[...] = _rope(p[:, o:o + COL_KV], ckv, skv, first_kv)
        o += COL_KV
    gate_ref[...] = jax.nn.sigmoid(p[:, o:o + LANES])


def _project(x2d, ln, w_pad, c_tab, s_tab, tm):
    n = x2d.shape[0]
    t_tiles = c_tab.shape[0] // tm
    row = lambda i: (i, 0)
    tab = lambda i: (i % t_tiles, 0)
    const = lambda i: (0, 0)
    out_shape = (
        jax.ShapeDtypeStruct((n, CONV_CH), F32),
        jax.ShapeDtypeStruct((n, COL_Q), BF16),
        jax.ShapeDtypeStruct((n, COL_KV), F32),
        jax.ShapeDtypeStruct((n, COL_KV), F32),
        jax.ShapeDtypeStruct((n, COL_KV), F32),
        jax.ShapeDtypeStruct((n, LANES), F32),
    )
    return pl.pallas_call(
        _proj_kernel,
        grid=(n // tm,),
        in_specs=[
            pl.BlockSpec((tm, D_MODEL), row),
            pl.BlockSpec((1, D_MODEL), const),
            pl.BlockSpec((D_MODEL, D_IN_PAD), const),
            pl.BlockSpec((tm, LANES), tab),
            pl.BlockSpec((tm, LANES), tab),
        ],
        out_specs=(
            pl.BlockSpec((tm, CONV_CH), row),
            pl.BlockSpec((tm, COL_Q), row),
            pl.BlockSpec((tm, COL_KV), row),
            pl.BlockSpec((tm, COL_KV), row),
            pl.BlockSpec((tm, COL_KV), row),
            pl.BlockSpec((tm, LANES), row),
        ),
        out_shape=out_shape,
        compiler_params=_cparams("arbitrary"),
        name="in_proj",
    )(x2d, ln.reshape(1, D_MODEL), w_pad, c_tab, s_tab)


def _conv_kernel(a_ref, hist_ref, w_ref, b_ref, g_ref, beta_ref, o_ref, ext_ref):
    tt = a_ref.shape[1]

    @pl.when(pl.program_id(1) == 0)
    def _():
        ext_ref[0:HIST_ROWS, :] = hist_ref[0]

    ext_ref[HIST_ROWS:HIST_ROWS + tt, :] = a_ref[0]
    acc = jnp.broadcast_to(b_ref[...], (tt, CONV_CH))
    lead = HIST_ROWS - (CONV_W - 1)
    for k in range(CONV_W):
        acc = acc + w_ref[k:k + 1, :] * ext_ref[lead + k:lead + k + tt, :]
    mu = jnp.mean(acc, axis=-1, keepdims=True)
    var = jnp.mean(jnp.square(acc - mu), axis=-1, keepdims=True)
    y = (acc - mu) * lax.rsqrt(var + LN_EPS) * g_ref[...] + beta_ref[...]
    o_ref[0] = y * jax.nn.sigmoid(y)
    carry = ext_ref[tt:tt + HIST_ROWS, :]
    ext_ref[0:HIST_ROWS, :] = carry


def _conv_module(a3d, hist, dw_w, dw_b, ln_g, ln_b, tt):
    b, t, _ = a3d.shape
    w_pad = jnp.pad(dw_w, ((0, HIST_ROWS - CONV_W), (0, 0)))
    vec = lambda i, j: (0, 0)
    return pl.pallas_call(
        _conv_kernel,
        grid=(b, t // tt),
        in_specs=[
            pl.BlockSpec((1, tt, CONV_CH), lambda i, j: (i, j, 0)),
            pl.BlockSpec((1, HIST_ROWS, CONV_CH), lambda i, j: (i, 0, 0)),
            pl.BlockSpec((HIST_ROWS, CONV_CH), vec),
            pl.BlockSpec((1, CONV_CH), vec),
            pl.BlockSpec((1, CONV_CH), vec),
            pl.BlockSpec((1, CONV_CH), vec),
        ],
        out_specs=pl.BlockSpec((1, tt, CONV_CH), lambda i, j: (i, j, 0)),
        out_shape=jax.ShapeDtypeStruct((b, t, CONV_CH), F32),
        scratch_shapes=[pltpu.VMEM((HIST_ROWS + tt, CONV_CH), F32)],
        compiler_params=_cparams("arbitrary", "arbitrary"),
        name="conv_module",
    )(a3d, hist, w_pad, dw_b.reshape(1, -1), ln_g.reshape(1, -1), ln_b.reshape(1, -1))


L_GROUP = 2 * LANES // HEAD_DIM


def _compress_rows(x_refs, pe_ref, wk1_ref, wk2_ref, wv1_ref, wv2_ref, nb):
    hk = jnp.zeros((N_KV * nb, CMP_HID), F32)
    hv = jnp.zeros((N_KV * nb, CMP_HID), F32)
    for j in range(CMP_BLOCK // L_GROUP):
        parts_k, parts_v = [], []
        for x_ref in x_refs:
            xs = [x_ref[pl.ds(j * L_GROUP + i, nb, stride=CMP_BLOCK), :] + pe_ref[j * L_GROUP + i:j * L_GROUP + i + 1, :]
                  for i in range(L_GROUP)]
            parts_k.append(jnp.concatenate([x[:, :HEAD_DIM] for x in xs], axis=1))
            parts_v.append(jnp.concatenate([x[:, HEAD_DIM:] for x in xs], axis=1))
        xk = jnp.concatenate(parts_k, axis=0).astype(BF16)
        xv = jnp.concatenate(parts_v, axis=0).astype(BF16)
        rows = slice(j * L_GROUP * HEAD_DIM, (j + 1) * L_GROUP * HEAD_DIM)
        hk = hk + jnp.dot(xk, wk1_ref[rows, :], preferred_element_type=F32)
        hv = hv + jnp.dot(xv, wv1_ref[rows, :], preferred_element_type=F32)
    kc = jnp.dot((hk * jax.nn.sigmoid(hk)).astype(BF16), wk2_ref[...], preferred_element_type=F32)
    vc = jnp.dot((hv * jax.nn.sigmoid(hv)).astype(BF16), wv2_ref[...], preferred_element_type=F32)
    return kc, vc


def _compress_dense_kernel(*refs):
    x_refs, (pe_ref, wk1_ref, wk2_ref, wv1_ref, wv2_ref, kc_ref, vc_ref) = refs[:N_KV], refs[N_KV:]
    nb = x_refs[0].shape[1] // CMP_BLOCK
    kc, vc = _compress_rows([x.at[0] for x in x_refs], pe_ref, wk1_ref, wk2_ref, wv1_ref, wv2_ref, nb)
    for g in range(N_KV):
        kc_ref[0, g] = kc[g * nb:(g + 1) * nb]
        vc_ref[0, g] = vc[g * nb:(g + 1) * nb]


def _cmp_weight_specs():
    const = lambda *_: (0, 0)
    return [
        pl.BlockSpec((CMP_BLOCK, 2 * HEAD_DIM), const),
        pl.BlockSpec((CMP_BLOCK * HEAD_DIM, CMP_HID), const),
        pl.BlockSpec((CMP_HID, HEAD_DIM), const),
        pl.BlockSpec((CMP_BLOCK * HEAD_DIM, CMP_HID), const),
        pl.BlockSpec((CMP_HID, HEAD_DIM), const),
    ]


def _cmp_weights(pos_emb, w_k1, w_k2, w_v1, w_v2):
    pe = pos_emb.reshape(CMP_BLOCK, 2 * HEAD_DIM)
    return (pe, w_k1.reshape(-1, CMP_HID).astype(BF16), w_k2.astype(BF16),
            w_v1.reshape(-1, CMP_HID).astype(BF16), w_v2.astype(BF16))


def _compress_dense(kv3d, cmp_w):
    b, t, _ = kv3d.shape
    nb = t // CMP_BLOCK
    out = jax.ShapeDtypeStruct((b, N_KV, nb, HEAD_DIM), F32)
    ospec = pl.BlockSpec((1, N_KV, nb, HEAD_DIM), lambda i: (i, 0, 0, 0))
    return pl.pallas_call(
        _compress_dense_kernel,
        grid=(b,),
        in_specs=[pl.BlockSpec((1, t, 2 * HEAD_DIM), functools.partial(lambda g, i: (i, 0, g), g))
                  for g in range(N_KV)] + _cmp_weight_specs(),
        out_specs=(ospec, ospec),
        out_shape=(out, out),
        compiler_params=_cparams("arbitrary"),
        name="compress_prompt",
    )(*([kv3d] * N_KV), *cmp_w)


PAGES_PER_STEP = 64


def _compress_paged_kernel(pt_ref, cache_ref, pe_ref, wk1_ref, wk2_ref, wv1_ref, wv2_ref, kc_ref, vc_ref,
                           raw_ref, rows_ref, sem_ref):
    step = pl.program_id(0)
    n_steps = pl.num_programs(0)
    nb = PAGES_PER_STEP * PAGE_SIZE // CMP_BLOCK
    kvw = 2 * HEAD_DIM

    def page_copy(s, slot, p):
        return pltpu.make_async_copy(cache_ref.at[pt_ref[s * PAGES_PER_STEP + p]], raw_ref.at[slot, p], sem_ref.at[slot])

    def issue(s, slot):
        for p in range(PAGES_PER_STEP):
            page_copy(s, slot, p).start()

    slot = step % 2

    @pl.when(step == 0)
    def _():
        issue(step, slot)

    @pl.when(step + 1 < n_steps)
    def _():
        issue(step + 1, 1 - slot)

    for p in range(PAGES_PER_STEP):
        page_copy(step, slot, p).wait()

    for p in range(PAGES_PER_STEP):
        page = raw_ref[slot, p].T
        for g in range(N_KV):
            rows_ref[g, p * PAGE_SIZE:(p + 1) * PAGE_SIZE, :] = page[:, g * kvw:(g + 1) * kvw]

    kc, vc = _compress_rows([rows_ref.at[g] for g in range(N_KV)], pe_ref, wk1_ref, wk2_ref, wv1_ref, wv2_ref, nb)
    for g in range(N_KV):
        kc_ref[0, g] = kc[g * nb:(g + 1) * nb]
        vc_ref[0, g] = vc[g * nb:(g + 1) * nb]


def _compress_paged(cache_t, page_table, cmp_w):
    db, n_pages = page_table.shape
    steps_per_row = n_pages // PAGES_PER_STEP
    nb = PAGES_PER_STEP * PAGE_SIZE // CMP_BLOCK
    out = jax.ShapeDtypeStruct((db, N_KV, steps_per_row * nb, HEAD_DIM), F32)
    ospec = pl.BlockSpec((1, N_KV, nb, HEAD_DIM), lambda i, pt: (i // steps_per_row, 0, i % steps_per_row, 0))
    grid_spec = pltpu.PrefetchScalarGridSpec(
        num_scalar_prefetch=1,
        grid=(db * steps_per_row,),
        in_specs=[pl.BlockSpec(memory_space=pl.ANY)] + _cmp_weight_specs(),
        out_specs=(ospec, ospec),
        scratch_shapes=[pltpu.VMEM((2, PAGES_PER_STEP, COL_KV, PAGE_SIZE), F32),
                        pltpu.VMEM((N_KV, PAGES_PER_STEP * PAGE_SIZE, 2 * HEAD_DIM), F32),
                        pltpu.SemaphoreType.DMA((2,))],
    )
    return pl.pallas_call(
        _compress_paged_kernel,
        grid_spec=grid_spec,
        out_shape=(out, out),
        compiler_params=_cparams("arbitrary"),
        name="compress_paged",
    )(page_table.reshape(-1), cache_t, *cmp_w)


TQ = LANES
KC = 512
LOG2_E = 1.4426950408889634
WIN_KEYS = WINDOW + TQ
KAUG = 2 * LANES


def _top_blocks(imp, cand, n_blocks):
    blk = lax.broadcasted_iota(jnp.int32, imp.shape, 0)
    score = jnp.where(cand, imp, -1.0)
    sel = jnp.zeros(imp.shape, F32)
    for _ in range(N_SEL):
        mx = jnp.max(score, axis=0, keepdims=True)
        idx = jnp.min(jnp.where(score == mx, blk, n_blocks), axis=0, keepdims=True)
        pick = blk == idx
        sel = jnp.where(pick, 1.0, sel)
        score = jnp.where(pick, -2.0, score)
    return sel


def _prompt_attn_kernel(q_ref, kc_ref, vct_ref, ksel_ref, vselt_ref, kwin_ref, vwint_ref, gate_ref, o_ref):
    qt = pl.program_id(2)
    t0 = qt * TQ
    nb = kc_ref.shape[2]
    width = Q_PER_KV * TQ

    q = q_ref[0].astype(F32) * (ATT_SCALE * LOG2_E)
    q_t = q.T
    q4 = jnp.concatenate([q_t[r * HEAD_DIM:(r + 1) * HEAD_DIM] for r in range(Q_PER_KV)], axis=1)
    q4b = q4.astype(BF16)
    tok = t0 + lax.broadcasted_iota(jnp.int32, (1, width), 1) % TQ

    sc = jnp.dot(kc_ref[0, 0].astype(BF16), q4b, preferred_element_type=F32)
    blk = lax.broadcasted_iota(jnp.int32, (nb, width), 0)
    valid_c = (blk + 1) * CMP_BLOCK - 1 <= tok
    sc = jnp.where(valid_c, sc, NEG_INF)
    e = jnp.where(valid_c, jnp.exp2(sc - jnp.max(sc, axis=0, keepdims=True)), 0.0)
    den = jnp.sum(e, axis=0, keepdims=True)
    p = e / jnp.where(den > 0.0, den, 1.0)
    o_cmp = jnp.dot(vct_ref[0, 0].astype(BF16), p.astype(BF16), preferred_element_type=F32)
    imp = p[:, 0:TQ]
    for r in range(1, Q_PER_KV):
        imp = imp + p[:, r * TQ:(r + 1) * TQ]

    tok1 = t0 + lax.broadcasted_iota(jnp.int32, (1, TQ), 1)
    own = tok1 // SEL_BLOCK
    blk1 = lax.broadcasted_iota(jnp.int32, (nb, TQ), 0)
    cand = blk1 < own
    sel = _top_blocks(imp, cand, nb)
    bias = jnp.where(cand, jnp.where(sel > 0.0, 0.0, NEG_INF), jnp.where(blk1 == own, 0.0, NEG_INF))
    bias4 = jnp.concatenate([bias] * Q_PER_KV, axis=1).astype(BF16)
    q_aug = jnp.concatenate([q4b, bias4, jnp.zeros((KAUG - HEAD_DIM - nb, width), BF16)], axis=0)

    def scores(c):
        return jnp.dot(ksel_ref[0, 0, pl.ds(pl.multiple_of(c * KC, KC), KC), :], q_aug, preferred_element_type=F32)

    def softmax_pv(c, s, m, l, acc):
        m_new = jnp.maximum(m, jnp.max(s, axis=0, keepdims=True))
        alpha = jnp.exp2(m - m_new)
        pr = jnp.exp2(s - m_new)
        l = alpha * l + jnp.sum(pr, axis=0, keepdims=True)
        v_blk = vselt_ref[0, 0, :, pl.ds(pl.multiple_of(c * KC, KC), KC)]
        return m_new, l, alpha * acc + jnp.dot(v_blk, pr.astype(BF16), preferred_element_type=F32)

    def sel_step(c, carry):
        s, m, l, acc = carry
        s_next = scores(c + 1)
        return (s_next,) + softmax_pv(c, s, m, l, acc)

    last = t0 // KC
    init = (scores(0), jnp.full((1, width), NEG_INF, F32), jnp.zeros((1, width), F32),
            jnp.zeros((HEAD_DIM, width), F32))
    s_last, m_sel, l_sel, acc_sel = lax.fori_loop(0, last, sel_step, init)
    key = last * KC + lax.broadcasted_iota(jnp.int32, (KC, width), 0)
    _, l_sel, acc_sel = softmax_pv(last, jnp.where(key <= tok, s_last, NEG_INF), m_sel, l_sel, acc_sel)
    o_sel = acc_sel / l_sel

    w0 = pl.multiple_of(jnp.maximum(t0 - WINDOW, 0), TQ)
    sw = jnp.dot(kwin_ref[0, 0, pl.ds(w0, WIN_KEYS), :], q4b, preferred_element_type=F32)
    dist = tok - (w0 + lax.broadcasted_iota(jnp.int32, (WIN_KEYS, width), 0))
    sw = jnp.where((dist >= 0) & (dist < WINDOW), sw, NEG_INF)
    pw = jnp.exp2(sw - jnp.max(sw, axis=0, keepdims=True))
    l_win = jnp.sum(pw, axis=0, keepdims=True)
    o_win = jnp.dot(vwint_ref[0, 0, :, pl.ds(w0, WIN_KEYS)], pw.astype(BF16), preferred_element_type=F32) / l_win

    outs = []
    for r in range(Q_PER_KV):
        sl = slice(r * TQ, (r + 1) * TQ)
        g = [gate_ref[0, 0, j * Q_PER_KV + r:j * Q_PER_KV + r + 1, :] for j in range(3)]
        outs.append(g[0] * o_cmp[:, sl] + g[1] * o_sel[:, sl] + g[2] * o_win[:, sl])
    o_ref[0] = jnp.concatenate(outs, axis=0).T


def _prompt_attention(q3, k_c, v_ct, ksel_aug, vsel_t, kwin, vwin_t, gates_t):
    b, t, _ = q3.shape
    nb = k_c.shape[2]
    width = Q_PER_KV * HEAD_DIM
    per_bg = lambda i, g, j: (i, g, 0, 0)
    return pl.pallas_call(
        _prompt_attn_kernel,
        grid=(b, N_KV, t // TQ),
        in_specs=[
            pl.BlockSpec((1, TQ, width), lambda i, g, j: (i, j, g)),
            pl.BlockSpec((1, 1, nb, HEAD_DIM), per_bg),
            pl.BlockSpec((1, 1, HEAD_DIM, nb), per_bg),
            pl.BlockSpec((1, 1, t, KAUG), per_bg),
            pl.BlockSpec((1, 1, HEAD_DIM, t), per_bg),
            pl.BlockSpec((1, 1, t, HEAD_DIM), per_bg),
            pl.BlockSpec((1, 1, HEAD_DIM, t), per_bg),
            pl.BlockSpec((1, 1, 3 * Q_PER_KV, TQ), lambda i, g, j: (i, g, 0, j)),
        ],
        out_specs=pl.BlockSpec((1, TQ, width), lambda i, g, j: (i, j, g)),
        out_shape=jax.ShapeDtypeStruct((b, t, COL_Q), F32),
        compiler_params=_cparams("arbitrary", "arbitrary", "arbitrary"),
        name="prompt_attention",
    )(q3, k_c, v_ct, ksel_aug, vsel_t, kwin, vwin_t, gates_t)


def _split_kv(kv3d):
    b, t, _ = kv3d.shape
    kv = kv3d.reshape(b, t, N_KV, 2, HEAD_DIM).astype(BF16)
    return jnp.transpose(kv[:, :, :, 0], (0, 2, 1, 3)), jnp.transpose(kv[:, :, :, 1], (0, 2, 3, 1))


def _augment_keys(k):
    b, g, t, _ = k.shape
    nb = t // SEL_BLOCK
    onehot = (jnp.arange(t)[:, None] // SEL_BLOCK == jnp.arange(nb)[None, :]).astype(BF16)
    onehot = jnp.broadcast_to(onehot, (b, g, t, nb))
    pad = jnp.zeros((b, g, t, KAUG - HEAD_DIM - nb), BF16)
    return jnp.concatenate([k, onehot, pad], axis=-1)


def _sample_cmp_kernel(q_ref, kc_ref, vc_ref, ocmp_ref, idx_ref, imp_ref, *, past_len, dec_seq):
    b = pl.program_id(0)
    nb = kc_ref.shape[2]
    rows = Q_PER_KV * SUBLANES
    t_row = lax.broadcasted_iota(jnp.int32, (rows, nb), 0) % SUBLANES
    blk = lax.broadcasted_iota(jnp.int32, (rows, nb), 1)
    valid = (blk + 1) * CMP_BLOCK - 1 <= past_len + t_row
    for g in range(N_KV):
        s = lax.dot_general(q_ref[0, g], kc_ref[0, g].astype(BF16), (((1,), (1,)), ((), ())),
                            preferred_element_type=F32) * ATT_SCALE
        s = jnp.where(valid, s, NEG_INF)
        e = jnp.where(valid, jnp.exp(s - jnp.max(s, axis=1, keepdims=True)), 0.0)
        den = jnp.sum(e, axis=1, keepdims=True)
        p = e / jnp.where(den > 0.0, den, 1.0)
        ocmp_ref[0, g] = jnp.dot(p.astype(BF16), vc_ref[0, g].astype(BF16), preferred_element_type=F32)
        imp = p[0:SUBLANES]
        for r in range(1, Q_PER_KV):
            imp = imp + p[r * SUBLANES:(r + 1) * SUBLANES]
        row0 = pl.multiple_of((b * N_KV + g) * SUBLANES, SUBLANES)
        imp_ref[pl.ds(row0, SUBLANES), :] = imp

    @pl.when(b == pl.num_programs(0) - 1)
    def _():
        n_rows = imp_ref.shape[0]
        lane = lax.broadcasted_iota(jnp.int32, (n_rows, nb), 1)
        own = (past_len + lax.broadcasted_iota(jnp.int32, (n_rows, nb), 0) % SUBLANES) // SEL_BLOCK
        score = jnp.where(lane < own, imp_ref[...], -1.0)
        col = lax.broadcasted_iota(jnp.int32, (n_rows, LANES), 1)
        picks = jnp.zeros((n_rows, LANES), jnp.int32)
        for i in range(N_SEL):
            mx = jnp.max(score, axis=1, keepdims=True)
            idx = jnp.min(jnp.where(score == mx, lane, nb), axis=1, keepdims=True)
            score = jnp.where(lane == idx, -2.0, score)
            picks = jnp.where(col == i, idx, picks)
        idx_ref[...] = picks


def _sample_cmp_select(q_rt, k_c, v_c, past_len, dec_seq):
    db, _, rows, _ = q_rt.shape
    nb = k_c.shape[2]
    spec4 = lambda r, c: pl.BlockSpec((1, N_KV, r, c), lambda i: (i, 0, 0, 0))
    n_rows = db * N_KV * SUBLANES
    return pl.pallas_call(
        functools.partial(_sample_cmp_kernel, past_len=past_len, dec_seq=dec_seq),
        grid=(db,),
        in_specs=[spec4(rows, HEAD_DIM), spec4(nb, HEAD_DIM), spec4(nb, HEAD_DIM)],
        out_specs=(spec4(rows, HEAD_DIM), pl.BlockSpec((n_rows, LANES), lambda i: (0, 0))),
        out_shape=(jax.ShapeDtypeStruct((db, N_KV, rows, HEAD_DIM), F32),
                   jax.ShapeDtypeStruct((n_rows, LANES), jnp.int32)),
        scratch_shapes=[pltpu.VMEM((n_rows, nb), F32)],
        compiler_params=_cparams("arbitrary"),
        name="sample_cmp_select",
    )(q_rt, k_c, v_c)


def _sample_attn_kernel(idx_ref, pt_ref, cache_ref, q_ref, snew_ref, wnew_ref, wstate_ref, ocmp_ref, gate_ref,
                        o_ref, kv_ref, sem_ref, *, dec_seq, n_pages):
    b = pl.program_id(0)
    n_b = pl.num_programs(0)
    kvw = 2 * HEAD_DIM
    blocks_per_page = PAGE_SIZE // SEL_BLOCK

    def block_id(bb, g, t, i):
        return idx_ref[((bb * N_KV + g) * dec_seq + t) * N_SEL + i]

    def slab_copy(bb, slot, g, t, i):
        page = pt_ref[bb * n_pages + block_id(bb, g, t, i) // blocks_per_page]
        return pltpu.make_async_copy(cache_ref.at[page, pl.ds(g * kvw, kvw), :], kv_ref.at[slot, g, t, i],
                                     sem_ref.at[slot])

    def for_all_slabs(fn):
        for g in range(N_KV):
            for t in range(dec_seq):
                for i in range(N_SEL):
                    fn(g, t, i)

    slot = b % 2

    @pl.when(b == 0)
    def _():
        for_all_slabs(lambda g, t, i: slab_copy(b, slot, g, t, i).start())

    @pl.when(b + 1 < n_b)
    def _():
        for_all_slabs(lambda g, t, i: slab_copy(b + 1, 1 - slot, g, t, i).start())

    rows = dec_seq * SUBLANES
    tok = lax.broadcasted_iota(jnp.int32, (rows, 1), 0) // SUBLANES
    n_state = wstate_ref.shape[2]

    def attend(q, k_t, v_t, valid):
        s = jnp.dot(q, k_t.astype(BF16), preferred_element_type=F32) * ATT_SCALE
        s = jnp.where(valid, s, NEG_INF)
        p = jnp.exp(s - jnp.max(s, axis=1, keepdims=True))
        den = jnp.sum(p, axis=1, keepdims=True)
        return lax.dot_general(p.astype(BF16), v_t.astype(BF16), (((1,), (1,)), ((), ())),
                               preferred_element_type=F32) / den

    o_win = []
    for g in range(N_KV):
        k_rows, v_rows = pl.ds(g * kvw, HEAD_DIM), pl.ds(g * kvw + HEAD_DIM, HEAD_DIM)
        k_t = jnp.concatenate([wstate_ref[0, k_rows, :], wnew_ref[0, k_rows, :]], axis=1)
        v_t = jnp.concatenate([wstate_ref[0, v_rows, :], wnew_ref[0, v_rows, :]], axis=1)
        lane = lax.broadcasted_iota(jnp.int32, (rows, n_state + LANES), 1)
        new_i = lane - n_state
        valid = ((lane < n_state) & (lane > tok)) | ((new_i >= 0) & (new_i <= tok) & (new_i < dec_seq))
        o_win.append(attend(q_ref[0, g], k_t, v_t, valid))

    for_all_slabs(lambda g, t, i: slab_copy(b, slot, g, t, i).wait())

    lane1 = lax.broadcasted_iota(jnp.int32, (1, PAGE_SIZE), 1)
    for g in range(N_KV):
        o_sel = []
        for t in range(dec_seq):
            k_parts = [kv_ref[slot, g, t, i, 0:HEAD_DIM, :] for i in range(N_SEL)]
            v_parts = [kv_ref[slot, g, t, i, HEAD_DIM:kvw, :] for i in range(N_SEL)]
            k_parts.append(snew_ref[0, g * kvw:g * kvw + HEAD_DIM, :])
            v_parts.append(snew_ref[0, g * kvw + HEAD_DIM:(g + 1) * kvw, :])
            halves = [lane1 // SEL_BLOCK == block_id(b, g, t, i) % blocks_per_page for i in range(N_SEL)]
            halves.append((lane1 <= t) & (lane1 < dec_seq))
            q = q_ref[0, g, t * SUBLANES:(t + 1) * SUBLANES, :]
            o_sel.append(attend(q, jnp.concatenate(k_parts, axis=1), jnp.concatenate(v_parts, axis=1),
                                jnp.concatenate(halves, axis=1)))
        o_sel = jnp.concatenate(o_sel, axis=0)
        o_ref[0, g] = (gate_ref[0, g, 0] * ocmp_ref[0, g] + gate_ref[0, g, 1] * o_sel
                       + gate_ref[0, g, 2] * o_win[g])


def _sample_attention(sel_idx, page_table, cache_t, q_tr, snew_t, wnew_t, wstate_t, ocmp_tr, gates_tr, dec_seq):
    db, n_pages = page_table.shape
    rows = dec_seq * SUBLANES
    n_state = wstate_t.shape[2]
    per_b4 = lambda i, *_: (i, 0, 0, 0)
    per_b3 = lambda i, *_: (i, 0, 0)
    grid_spec = pltpu.PrefetchScalarGridSpec(
        num_scalar_prefetch=2,
        grid=(db,),
        in_specs=[
            pl.BlockSpec(memory_space=pl.ANY),
            pl.BlockSpec((1, N_KV, rows, HEAD_DIM), per_b4),
            pl.BlockSpec((1, COL_KV, LANES), per_b3),
            pl.BlockSpec((1, COL_KV, LANES), per_b3),
            pl.BlockSpec((1, COL_KV, n_state), per_b3),
            pl.BlockSpec((1, N_KV, rows, HEAD_DIM), per_b4),
            pl.BlockSpec((1, N_KV, 3, rows, HEAD_DIM), lambda i, *_: (i, 0, 0, 0, 0)),
        ],
        out_specs=pl.BlockSpec((1, N_KV, rows, HEAD_DIM), per_b4),
        scratch_shapes=[pltpu.VMEM((2, N_KV, dec_seq, N_SEL, 2 * HEAD_DIM, PAGE_SIZE), F32),
                        pltpu.SemaphoreType.DMA((2,))],
    )
    return pl.pallas_call(
        functools.partial(_sample_attn_kernel, dec_seq=dec_seq, n_pages=n_pages),
        grid_spec=grid_spec,
        out_shape=jax.ShapeDtypeStruct((db, N_KV, rows, HEAD_DIM), F32),
        compiler_params=_cparams("arbitrary"),
        name="sample_attention",
    )(sel_idx, page_table.reshape(-1), cache_t, q_tr, snew_t, wnew_t, wstate_t, ocmp_tr, gates_tr)


ROUTE_E1, ROUTE_E2, ROUTE_W1, ROUTE_W2 = 0, 1, 2, 3


def _merge_kernel(conv_ref, att_ref, x_ref, gc_ref, ga_ref, wo_ref, ln2_ref, wr_ref, x1_ref, h3_ref, route_ref):
    mix = jnp.concatenate([_rms(conv_ref[...], gc_ref[...]), _rms(att_ref[...], ga_ref[...])], axis=1)
    x1 = x_ref[...] + jnp.dot(mix.astype(BF16), wo_ref[...], preferred_element_type=F32)
    x1_ref[...] = x1
    h = _rms(x1, ln2_ref[...])
    for j in range(ROW_TILES):
        h3_ref[:, j, :] = h[:, j * LANES:(j + 1) * LANES]

    logits = jnp.dot(h.astype(BF16), wr_ref[...], preferred_element_type=F32)
    lane = lax.broadcasted_iota(jnp.int32, logits.shape, 1)
    is_g = lane < N_GROUPS
    lg = jnp.where(is_g, logits, NEG_INF)
    mg = jnp.max(lg, axis=1, keepdims=True)
    sg = jnp.sum(jnp.where(is_g, jnp.exp(lg - mg), 0.0), axis=1, keepdims=True)
    grp = jnp.min(jnp.where(lg == mg, lane, LANES), axis=1, keepdims=True)
    p_top = 1.0 / sg
    in_grp = ((lane + (EXPERTS_PER_GROUP - N_GROUPS)) // EXPERTS_PER_GROUP) == grp + 1
    le = jnp.where(in_grp, logits, NEG_INF)
    ee = jnp.where(in_grp, jnp.exp(le - jnp.max(le, axis=1, keepdims=True)), 0.0)
    pe = jnp.where(in_grp, ee / jnp.sum(ee, axis=1, keepdims=True), -1.0)
    p1 = jnp.max(pe, axis=1, keepdims=True)
    i1 = jnp.min(jnp.where(pe == p1, lane, LANES), axis=1, keepdims=True)
    pe2 = jnp.where(lane == i1, -1.0, pe)
    p2 = jnp.max(pe2, axis=1, keepdims=True)
    i2 = jnp.min(jnp.where(pe2 == p2, lane, LANES), axis=1, keepdims=True)
    den = p1 + p2
    rec = jnp.where(lane == ROUTE_E1, (i1 - N_GROUPS).astype(F32), 0.0)
    rec = jnp.where(lane == ROUTE_E2, (i2 - N_GROUPS).astype(F32), rec)
    rec = jnp.where(lane == ROUTE_W1, p1 / den * p_top, rec)
    rec = jnp.where(lane == ROUTE_W2, p2 / den * p_top, rec)
    route_ref[...] = rec


def _merge(conv2d, att2d, x2d, g_conv, g_att, w_out_b, ln2, w_route_b, tm):
    n = x2d.shape[0]
    row = lambda i: (i, 0)
    const = lambda i: (0, 0)
    return pl.pallas_call(
        _merge_kernel,
        grid=(n // tm,),
        in_specs=[
            pl.BlockSpec((tm, CONV_CH), row),
            pl.BlockSpec((tm, COL_Q), row),
            pl.BlockSpec((tm, D_MODEL), row),
            pl.BlockSpec((1, CONV_CH), const),
            pl.BlockSpec((1, COL_Q), const),
            pl.BlockSpec((CONV_CH + COL_Q, D_MODEL), const),
            pl.BlockSpec((1, D_MODEL), const),
            pl.BlockSpec((D_MODEL, LANES), const),
        ],
        out_specs=(
            pl.BlockSpec((tm, D_MODEL), row),
            pl.BlockSpec((tm, ROW_TILES, LANES), lambda i: (i, 0, 0)),
            pl.BlockSpec((tm, LANES), row),
        ),
        out_shape=(
            jax.ShapeDtypeStruct((n, D_MODEL), F32),
            jax.ShapeDtypeStruct((n, ROW_TILES, LANES), F32),
            jax.ShapeDtypeStruct((n, LANES), F32),
        ),
        compiler_params=_cparams("arbitrary"),
        name="merge_route",
    )(conv2d, att2d, x2d, g_conv.reshape(1, -1), g_att.reshape(1, -1), w_out_b, ln2.reshape(1, -1), w_route_b)


META_W = 2 * LANES


def _rank_kernel(route_ref, dest_ref, meta_ref, e_ref, rank_ref, count_ref):
    i = pl.program_id(0)
    rt = route_ref.shape[0]

    @pl.when(i == 0)
    def _():
        count_ref[...] = jnp.zeros(count_ref.shape, F32)

    route_t = route_ref[...].T
    expert_id = lax.broadcasted_iota(jnp.int32, (N_EXPERTS, rt), 0).astype(F32)
    before = (lax.broadcasted_iota(jnp.int32, (rt, rt), 0) < lax.broadcasted_iota(jnp.int32, (rt, rt), 1))
    before = before.astype(BF16)
    ones = jnp.ones((rt, LANES), BF16)
    e_rows, rank_rows = [], []
    for k in range(TOP_K):
        e_k = route_t[ROUTE_E1 + k:ROUTE_E1 + k + 1, :]
        onehot = (expert_id == e_k).astype(F32)
        earlier = jnp.dot(onehot.astype(BF16), before, preferred_element_type=F32)
        seen = count_ref[...]
        seen_w = jnp.concatenate([seen] * (rt // LANES), axis=1)
        rank_rows.append(jnp.sum(onehot * (earlier + seen_w), axis=0, keepdims=True))
        e_rows.append(e_k)
        count_ref[...] = seen + jnp.dot(onehot.astype(BF16), ones, preferred_element_type=F32)
    e_ref[i] = jnp.concatenate(e_rows, axis=0)
    rank_ref[i] = jnp.concatenate(rank_rows, axis=0)

    @pl.when(i == pl.num_programs(0) - 1)
    def _():
        counts = jnp.concatenate([count_ref[...]] * (META_W // LANES), axis=1)
        padded = jnp.floor((counts + (MOE_ROWS - 1)) * (1.0 / MOE_ROWS)) * MOE_ROWS
        lane = lax.broadcasted_iota(jnp.int32, (1, META_W), 1)
        chunk_start = lane.astype(F32) * MOE_ROWS
        chunk_e = jnp.zeros((1, META_W), F32)
        ends = jnp.zeros((1, META_W), F32)
        end = jnp.zeros((1, META_W), F32)
        starts = []
        for ex in range(N_EXPERTS):
            starts.append(end[:, 0:rt])
            end = end + padded[ex:ex + 1, :]
            chunk_e = chunk_e + (end <= chunk_start).astype(F32)
            ends = jnp.where(lane == ex, end, ends)
        n_used = end * (1.0 / MOE_ROWS)
        chunk_e = jnp.minimum(chunk_e, float(N_EXPERTS - 1))
        row = lax.broadcasted_iota(jnp.int32, (SUBLANES, META_W), 0)
        meta = jnp.where(row == 0, chunk_e, jnp.where(row == 1, n_used, jnp.where(row == 2, ends, 0.0)))
        meta_ref[...] = meta.astype(jnp.int32)

        def place(ti, carry):
            e_t = e_ref[ti]
            dest = rank_ref[ti]
            for ex in range(N_EXPERTS):
                dest = dest + jnp.where(e_t == float(ex), starts[ex], 0.0)
            dest_ref[ti] = dest.astype(jnp.int32)
            return carry

        lax.fori_loop(0, pl.num_programs(0), place, 0)


def _rank(route, rt):
    n = route.shape[0]
    tiles = n // rt
    assert n * TOP_K // MOE_ROWS + N_EXPERTS <= META_W and rt <= META_W, "chunk table is one row of META_W lanes"
    whole = lambda i: (0, 0, 0)
    return pl.pallas_call(
        _rank_kernel,
        grid=(tiles,),
        in_specs=[pl.BlockSpec((rt, LANES), lambda i: (i, 0))],
        out_specs=(pl.BlockSpec((tiles, TOP_K, rt), whole), pl.BlockSpec((SUBLANES, META_W), lambda i: (0, 0))),
        out_shape=(jax.ShapeDtypeStruct((tiles, TOP_K, rt), jnp.int32),
                   jax.ShapeDtypeStruct((SUBLANES, META_W), jnp.int32)),
        scratch_shapes=[pltpu.VMEM((tiles, TOP_K, rt), F32), pltpu.VMEM((tiles, TOP_K, rt), F32),
                        pltpu.VMEM((N_EXPERTS, LANES), F32)],
        compiler_params=_cparams("arbitrary"),
        name="route_rank",
    )(route)


def _row_copies_wait(src_ref, dst_ref, sem, n_rows):
    pltpu.make_async_copy(src_ref.at[pl.ds(0, n_rows)], dst_ref.at[pl.ds(0, n_rows)], sem).wait()


def _scatter_kernel(meta_ref, dest_ref, h3_ref, xs_ref, zero_ref, sem_ref):
    i = pl.program_id(0)
    rt = h3_ref.shape[0]
    n_chunks = xs_ref.shape[0] // MOE_ROWS

    @pl.when(i == 0)
    def _():
        zero_ref[...] = jnp.zeros(zero_ref.shape, F32)

        def zero_chunk(first_row):
            return pltpu.make_async_copy(zero_ref, xs_ref.at[pl.ds(first_row, MOE_ROWS)], sem_ref.at[1])

        def fills(act):
            prev = 0
            for ex in range(N_EXPERTS):
                end = meta_ref[2, ex]

                @pl.when(end > prev)
                def _():
                    act(zero_chunk(end - MOE_ROWS))
                prev = end
            for c in range(n_chunks - N_EXPERTS, n_chunks):
                @pl.when(c >= meta_ref[1, 0])
                def _():
                    act(zero_chunk(c * MOE_ROWS))

        fills(lambda copy: copy.start())
        fills(lambda copy: copy.wait())

    def issue(t, carry):
        for k in range(TOP_K):
            pltpu.make_async_copy(h3_ref.at[t], xs_ref.at[dest_ref[0, k, t]], sem_ref.at[0]).start()
        return carry

    lax.fori_loop(0, rt, issue, 0, unroll=8)
    _row_copies_wait(h3_ref, h3_ref, sem_ref.at[0], rt)
    _row_copies_wait(h3_ref, h3_ref, sem_ref.at[0], rt)


def _scatter_rows(h3, dest, meta, rt):
    n = h3.shape[0]
    n_slots = (n * TOP_K // MOE_ROWS + N_EXPERTS) * MOE_ROWS
    grid_spec = pltpu.PrefetchScalarGridSpec(
        num_scalar_prefetch=1,
        grid=(n // rt,),
        in_specs=[pl.BlockSpec((1, TOP_K, rt), lambda i, meta: (i, 0, 0), memory_space=pltpu.SMEM),
                  pl.BlockSpec((rt, ROW_TILES, LANES), lambda i, meta: (i, 0, 0))],
        out_specs=pl.BlockSpec(memory_space=pl.ANY),
        scratch_shapes=[pltpu.VMEM((MOE_ROWS, ROW_TILES, LANES), F32), pltpu.SemaphoreType.DMA((2,))],
    )
    return pl.pallas_call(
        _scatter_kernel,
        grid_spec=grid_spec,
        out_shape=jax.ShapeDtypeStruct((n_slots, ROW_TILES, LANES), F32),
        compiler_params=_cparams("arbitrary"),
        name="moe_scatter_rows",
    )(meta, dest, h3)


def _expert_kernel(meta_ref, xs_ref, wg_ref, wu_ref, wd_ref, ys_ref):
    in_use = pl.program_id(0) < meta_ref[1, 0]

    @pl.when(jnp.logical_not(in_use))
    def _():
        ys_ref[...] = jnp.zeros(ys_ref.shape, F32)

    @pl.when(in_use)
    def _():
        x = jnp.concatenate([xs_ref[:, j, :] for j in range(ROW_TILES)], axis=1).astype(BF16)
        gate = jnp.dot(x, wg_ref[0].astype(BF16), preferred_element_type=F32)
        up = jnp.dot(x, wu_ref[0].astype(BF16), preferred_element_type=F32)
        act = (gate * jax.nn.sigmoid(gate) * up).astype(BF16)
        y = jnp.dot(act, wd_ref[0].astype(BF16), preferred_element_type=F32)
        for j in range(ROW_TILES):
            ys_ref[:, j, :] = y[:, j * LANES:(j + 1) * LANES]


def _expert_mlp(xs, meta, w_g, w_u, w_d):
    n_chunks = xs.shape[0] // MOE_ROWS
    rows = lambda c, meta: (c, 0, 0)
    expert = lambda c, meta: (meta[0, jnp.minimum(c, meta[1, 0] - 1)], 0, 0)
    grid_spec = pltpu.PrefetchScalarGridSpec(
        num_scalar_prefetch=1,
        grid=(n_chunks,),
        in_specs=[
            pl.BlockSpec((MOE_ROWS, ROW_TILES, LANES), rows),
            pl.BlockSpec((1, D_MODEL, D_EXPERT), expert),
            pl.BlockSpec((1, D_MODEL, D_EXPERT), expert),
            pl.BlockSpec((1, D_EXPERT, D_MODEL), expert),
        ],
        out_specs=pl.BlockSpec((MOE_ROWS, ROW_TILES, LANES), rows),
    )
    return pl.pallas_call(
        _expert_kernel,
        grid_spec=grid_spec,
        out_shape=jax.ShapeDtypeStruct(xs.shape, F32),
        compiler_params=_cparams("arbitrary"),
        name="expert_mlp",
    )(meta, xs, w_g, w_u, w_d)


def _combine_kernel(dest_ref, nxt_ref, x1_ref, route_ref, ln_ref, ys_ref, o_ref, buf_ref, sem_ref):
    i = pl.program_id(0)
    n_tiles = pl.num_programs(0)
    rt = x1_ref.shape[0]

    def issue(table_ref, slot):
        def body(t, carry):
            for k in range(TOP_K):
                pltpu.make_async_copy(ys_ref.at[table_ref[0, k, t]], buf_ref.at[slot, k, t], sem_ref.at[slot]).start()
            return carry
        lax.fori_loop(0, rt, body, 0, unroll=8)

    slot = i % 2

    @pl.when(i == 0)
    def _():
        issue(dest_ref, slot)

    @pl.when(i + 1 < n_tiles)
    def _():
        issue(nxt_ref, 1 - slot)

    for k in range(TOP_K):
        _row_copies_wait(ys_ref, buf_ref.at[slot, k], sem_ref.at[slot], rt)
    route = route_ref[...]
    moe = None
    for k in range(TOP_K):
        y = jnp.concatenate([buf_ref[slot, k, :, j, :] for j in range(ROW_TILES)], axis=1)
        term = y * route[:, ROUTE_W1 + k:ROUTE_W1 + k + 1]
        moe = term if moe is None else moe + term
    o_ref[...] = _rms(x1_ref[...] + moe, ln_ref[...])


def _combine(x1, ys, dest, route, ln_final, rt):
    n = x1.shape[0]
    tiles = n // rt
    row = lambda i: (i, 0)
    table = lambda f: pl.BlockSpec((1, TOP_K, rt), f, memory_space=pltpu.SMEM)
    return pl.pallas_call(
        _combine_kernel,
        grid=(tiles,),
        in_specs=[
            table(lambda i: (i, 0, 0)),
            table(lambda i: (jnp.minimum(i + 1, tiles - 1), 0, 0)),
            pl.BlockSpec((rt, D_MODEL), row),
            pl.BlockSpec((rt, LANES), row),
            pl.BlockSpec((1, D_MODEL), lambda i: (0, 0)),
            pl.BlockSpec(memory_space=pl.ANY),
        ],
        out_specs=pl.BlockSpec((rt, D_MODEL), row),
        out_shape=jax.ShapeDtypeStruct((n, D_MODEL), F32),
        scratch_shapes=[pltpu.VMEM((2, TOP_K, rt, ROW_TILES, LANES), F32), pltpu.SemaphoreType.DMA((2,))],
        compiler_params=_cparams("arbitrary"),
        name="moe_combine_norm",
    )(dest, dest, x1, route, ln_final.reshape(1, -1), ys)


def _ffn(conv2d, att2d, x2d, g_conv, g_att, w_out_b, ln2, w_route_b, w_g, w_u, w_d, ln_final, tm):
    x1, h3, route = _merge(conv2d, att2d, x2d, g_conv, g_att, w_out_b, ln2, w_route_b, tm)
    rt = min(tm, MOE_ROWS)
    dest, meta = _rank(route, rt)
    ys = _expert_mlp(_scatter_rows(h3, dest, meta, rt), meta, w_g, w_u, w_d)
    return _combine(x1, ys, dest, route, ln_final, rt)


def kernel(x_prompt, x_sample, cache_cmp_kv, cache_sel_kv, state_win_kv, state_conv, page_table, ln1, w_in, conv_dw_w, conv_dw_b, conv_ln_g, conv_ln_b, cmp_pos_emb, w_cmp_k1, w_cmp_k2, w_cmp_v1, w_cmp_v2, out_norm_conv, out_norm_att, w_out, ln2, w_router_group, w_router_expert, w_exp_gate, w_exp_up, w_exp_down, ln_final):
    depth = ln1.shape[0]
    assert depth == 1, "single-layer step"
    b, t, _ = x_prompt.shape
    db, ds, _ = x_sample.shape
    n_phys = cache_cmp_kv.shape[1]
    n_pages = page_table.shape[1]
    past = n_pages * PAGE_SIZE
    win_rows = state_win_kv.shape[2]
    assert ds < CMP_BLOCK and ds <= SUBLANES and past % SEL_BLOCK == 0 and past // SEL_BLOCK >= N_SEL
    assert win_rows == WINDOW and past >= WINDOW and t % KC == 0 and t >= WIN_KEYS

    w_in_b = jnp.pad(w_in[0], ((0, 0), (0, D_IN_PAD - D_IN))).astype(BF16)
    w_out_b = w_out[0].astype(BF16)
    w_route_b = jnp.pad(jnp.concatenate([w_router_group[0], w_router_expert[0]], axis=1),
                        ((0, 0), (0, LANES - N_GROUPS - N_EXPERTS))).astype(BF16)
    cmp_w = _cmp_weights(cmp_pos_emb[0], w_cmp_k1[0], w_cmp_k2[0], w_cmp_v1[0], w_cmp_v2[0])
    conv_w = (conv_dw_w[0], conv_dw_b[0], conv_ln_g[0], conv_ln_b[0])
    ffn_w = (out_norm_conv[0], out_norm_att[0], w_out_b, ln2[0], w_route_b, w_exp_gate[0], w_exp_up[0],
             w_exp_down[0], ln_final)

    xp2 = x_prompt.reshape(b * t, D_MODEL)
    c_p, s_p = _rope_tables(jnp.arange(t))
    a_p, q_p, ckv_p, skv_p, wkv_p, gate_p = _project(xp2, ln1[0], w_in_b, c_p, s_p, 512)
    a_p3 = a_p.reshape(b, t, CONV_CH)
    conv_p = _conv_module(a_p3, jnp.zeros((b, HIST_ROWS, CONV_CH), F32), *conv_w, 512)
    kc_p, vc_p = _compress_dense(ckv_p.reshape(b, t, COL_KV), cmp_w)
    ksel_p, vsel_t = _split_kv(skv_p.reshape(b, t, COL_KV))
    kwin_p, vwin_t = _split_kv(wkv_p.reshape(b, t, COL_KV))
    gates_t = jnp.transpose(gate_p[:, :COL_GATE].reshape(b, t, N_KV, Q_PER_KV, 3), (0, 2, 4, 3, 1))
    att_p = _prompt_attention(q_p.reshape(b, t, COL_Q), kc_p, jnp.swapaxes(vc_p, 2, 3), _augment_keys(ksel_p),
                              vsel_t, kwin_p, vwin_t, gates_t.reshape(b, N_KV, 3 * Q_PER_KV, t))
    y_p = _ffn(conv_p.reshape(b * t, CONV_CH), att_p.reshape(b * t, COL_Q), xp2, *ffn_w, 256)

    n_s = db * ds
    xs2 = x_sample.reshape(n_s, D_MODEL)
    c_s, s_s = _rope_tables(jnp.tile(past + jnp.arange(ds), db))
    a_s, q_s, ckv_s, skv_s, wkv_s, gate_s = _project(xs2, ln1[0], w_in_b, c_s, s_s, n_s)
    a_s3 = a_s.reshape(db, ds, CONV_CH)
    hist_s = jnp.pad(state_conv[0], ((0, 0), (HIST_ROWS - (CONV_W - 1), 0), (0, 0)))
    conv_s = _conv_module(a_s3, hist_s, *conv_w, ds)
    rows_minor = lambda a, n, r: jnp.swapaxes(a.reshape(n, r, COL_KV), 1, 2)
    kc_s, vc_s = _compress_paged(rows_minor(cache_cmp_kv[0], n_phys, PAGE_SIZE), page_table, cmp_w)

    q5 = q_s.reshape(db, ds, N_KV, Q_PER_KV, HEAD_DIM)
    pad_tok = SUBLANES - ds
    pad_head = SUBLANES - Q_PER_KV
    q_rt = jnp.pad(jnp.transpose(q5, (0, 2, 3, 1, 4)), ((0, 0), (0, 0), (0, 0), (0, pad_tok), (0, 0)))
    q_rt = q_rt.reshape(db, N_KV, Q_PER_KV * SUBLANES, HEAD_DIM)
    q_tr = jnp.pad(jnp.transpose(q5, (0, 2, 1, 3, 4)), ((0, 0), (0, 0), (0, 0), (0, pad_head), (0, 0)))
    q_tr = q_tr.reshape(db, N_KV, ds * SUBLANES, HEAD_DIM)
    g5 = gate_s[:, :COL_GATE].reshape(db, ds, N_KV, Q_PER_KV, 3)
    g_tr = jnp.pad(jnp.transpose(g5, (0, 2, 4, 1, 3)), ((0, 0),) * 4 + ((0, pad_head),))
    g_tr = jnp.broadcast_to(g_tr.reshape(db, N_KV, 3, ds * SUBLANES)[..., None], (db, N_KV, 3, ds * SUBLANES, HEAD_DIM))
    ocmp, picks = _sample_cmp_select(q_rt, kc_s, vc_s, past, ds)
    sel_idx = picks.reshape(db, N_KV, SUBLANES, LANES)[:, :, :ds, :N_SEL].reshape(-1)
    ocmp_tr = jnp.transpose(ocmp.reshape(db, N_KV, Q_PER_KV, SUBLANES, HEAD_DIM)[:, :, :, :ds], (0, 1, 3, 2, 4))
    ocmp_tr = jnp.pad(ocmp_tr, ((0, 0), (0, 0), (0, 0), (0, pad_head), (0, 0)))
    ocmp_tr = ocmp_tr.reshape(db, N_KV, ds * SUBLANES, HEAD_DIM)
    new_t = lambda kv: jnp.pad(rows_minor(kv, db, ds), ((0, 0), (0, 0), (0, LANES - ds)))
    o_s = _sample_attention(sel_idx, page_table, rows_minor(cache_sel_kv[0], n_phys, PAGE_SIZE), q_tr,
                            new_t(skv_s), new_t(wkv_s), rows_minor(state_win_kv[0], db, win_rows),
                            ocmp_tr, g_tr, ds)
    att_s = o_s.reshape(db, N_KV, ds, SUBLANES, HEAD_DIM)[:, :, :, :Q_PER_KV]
    att_s = jnp.transpose(att_s, (0, 2, 1, 3, 4)).reshape(n_s, COL_Q)
    y_s = _ffn(conv_s.reshape(n_s, CONV_CH), att_s, xs2, *ffn_w, n_s)

    kv6 = lambda kv, bb, tt: kv.reshape(1, bb, tt, N_KV, 2, HEAD_DIM)
    wkv_p6 = kv6(wkv_p, b, t)
    new_win_s = jnp.concatenate([state_win_kv, kv6(wkv_s, db, ds)], axis=2)[:, :, ds:]
    new_conv_s = jnp.concatenate([state_conv[0], a_s3], axis=1)[None, :, ds:]
    return (y_p.reshape(b, t, D_MODEL), y_s.reshape(db, ds, D_MODEL),
            kv6(ckv_p, b, t), kv6(ckv_s, db, ds), kv6(skv_p, b, t), kv6(skv_s, db, ds),
            wkv_p6[:, :, t - min(WINDOW, t):], new_win_s,
            a_p3[None, :, t - (CONV_W - 1):], new_conv_s)
```

```python
import functools

import jax
import jax.numpy as jnp
from jax import lax
from jax.experimental import pallas as pl
from jax.experimental.pallas import tpu as pltpu

D_MODEL = 1024
CONV_CH = 512
CONV_W = 31
N_HEADS = 8
HEAD_DIM = 64
N_KV = 2
Q_PER_KV = N_HEADS // N_KV
ROPE_DIM = HEAD_DIM // 4
ROPE_THETA = 500000.0
CMP_BLOCK = 64
SEL_BLOCK = CMP_BLOCK
N_SEL = 16
WINDOW = 512
CMP_HID = 2 * HEAD_DIM
COL_Q = N_HEADS * HEAD_DIM
COL_KV = 2 * N_KV * HEAD_DIM
COL_GATE = 3 * N_HEADS
D_IN = 2 * CONV_CH + COL_Q + 3 * COL_KV + COL_GATE
N_GROUPS = 4
EXPERTS_PER_GROUP = 8
N_EXPERTS = N_GROUPS * EXPERTS_PER_GROUP
TOP_K = 2
D_EXPERT = 512
PAGE_SIZE = 128
RMS_EPS = 1e-6
LN_EPS = 1e-5
NEG_INF = -1e30
ATT_SCALE = HEAD_DIM ** -0.5

LANES = 128
SUBLANES = 8
VMEM_LIMIT_BYTES = 56 * 1024 * 1024

D_IN_PAD = ((D_IN + LANES - 1) // LANES) * LANES
COL_GATE_OFF = 2 * CONV_CH + COL_Q + 3 * COL_KV
HIST_ROWS = 32
MOE_ROWS = 256
ROW_TILES = D_MODEL // LANES

BF16 = jnp.bfloat16
F32 = jnp.float32


def _cparams(*sem):
    return pltpu.CompilerParams(dimension_semantics=sem, vmem_limit_bytes=VMEM_LIMIT_BYTES)


def _rms(x, g):
    return x * lax.rsqrt(jnp.mean(x * x, axis=-1, keepdims=True) + RMS_EPS) * g


def _rope_tables(pos):
    half = ROPE_DIM // 2
    inv = ROPE_THETA ** (-jnp.arange(half, dtype=F32) / half)
    ang = pos.astype(F32)[:, None] * inv
    cos, sin = jnp.cos(ang), jnp.sin(ang)
    m = jnp.arange(LANES) % HEAD_DIM
    idx = m % half
    c = jnp.where(m < ROPE_DIM, cos[:, idx], 1.0)
    s = jnp.where(m < half, -sin[:, idx], jnp.where(m < ROPE_DIM, sin[:, idx], 0.0))
    return c.astype(F32), s.astype(F32)


def _rope(v, c, s, first_half):
    w = v.shape[1]
    half = ROPE_DIM // 2
    partner = jnp.where(first_half, pltpu.roll(v, w - half, axis=1), pltpu.roll(v, half, axis=1))
    return v * c + partner * s


def _proj_kernel(x_ref, ln_ref, w_ref, c_ref, s_ref, a_ref, q_ref, ckv_ref, skv_ref, wkv_ref, gate_ref,
                 ckvt_ref, skvt_ref, wkvt_ref):
    x = x_ref[...]
    xn = _rms(x, ln_ref[...])
    p = jnp.dot(xn.astype(BF16), w_ref[...], preferred_element_type=F32)
    a_ref[...] = p[:, :CONV_CH] * jax.nn.sigmoid(p[:, CONV_CH:2 * CONV_CH])

    c128, s128 = c_ref[...], s_ref[...]
    tm = x.shape[0]
    lane_q = lax.broadcasted_iota(jnp.int32, (tm, COL_Q), 1)
    cq = jnp.concatenate([c128] * (COL_Q // LANES), axis=1)
    sq = jnp.concatenate([s128] * (COL_Q // LANES), axis=1)
    o = 2 * CONV_CH
    q = _rope(p[:, o:o + COL_Q], cq, sq, (lane_q % HEAD_DIM) < ROPE_DIM // 2)
    q_ref[...] = q.astype(q_ref.dtype)
    o += COL_Q

    lane_kv = lax.broadcasted_iota(jnp.int32, (tm, COL_KV), 1)
    is_k = (lane_kv % (2 * HEAD_DIM)) < HEAD_DIM
    ckv = jnp.where(is_k, jnp.concatenate([c128] * (COL_KV // LANES), axis=1), 1.0)
    skv = jnp.where(is_k, jnp.concatenate([s128] * (COL_KV // LANES), axis=1), 0.0)
    first_kv = (lane_kv % HEAD_DIM) < ROPE_DIM // 2
    for ref, ref_t in ((ckv_ref, ckvt_ref), (skv_ref, skvt_ref), (wkv_ref, wkvt_ref)):
        kv = _rope(p[:, o:o + COL_KV], ckv, skv, first_kv)
        ref[...] = kv
        ref_t[0] = kv.T
        o += COL_KV
    gate_ref[...] = jax.nn.sigmoid(p[:, o:o + LANES])


def _project(x2d, ln, w_pad, c_tab, s_tab, tm):
    n = x2d.shape[0]
    t = c_tab.shape[0]
    t_tiles = t // tm
    row = lambda i: (i, 0)
    tab = lambda i: (i % t_tiles, 0)
    const = lambda i: (0, 0)
    kv_t = jax.ShapeDtypeStruct((n // t, COL_KV, t), F32)
    kv_t_spec = pl.BlockSpec((1, COL_KV, tm), lambda i: (i // t_tiles, 0, i % t_tiles))
    out_shape = (
        jax.ShapeDtypeStruct((n, CONV_CH), F32),
        jax.ShapeDtypeStruct((n, COL_Q), BF16),
        jax.ShapeDtypeStruct((n, COL_KV), F32),
        jax.ShapeDtypeStruct((n, COL_KV), F32),
        jax.ShapeDtypeStruct((n, COL_KV), F32),
        jax.ShapeDtypeStruct((n, LANES), F32),
        kv_t, kv_t, kv_t,
    )
    return pl.pallas_call(
        _proj_kernel,
        grid=(n // tm,),
        in_specs=[
            pl.BlockSpec((tm, D_MODEL), row),
            pl.BlockSpec((1, D_MODEL), const),
            pl.BlockSpec((D_MODEL, D_IN_PAD), const),
            pl.BlockSpec((tm, LANES), tab),
            pl.BlockSpec((tm, LANES), tab),
        ],
        out_specs=(
            pl.BlockSpec((tm, CONV_CH), row),
            pl.BlockSpec((tm, COL_Q), row),
            pl.BlockSpec((tm, COL_KV), row),
            pl.BlockSpec((tm, COL_KV), row),
            pl.BlockSpec((tm, COL_KV), row),
            pl.BlockSpec((tm, LANES), row),
            kv_t_spec, kv_t_spec, kv_t_spec,
        ),
        out_shape=out_shape,
        compiler_params=_cparams("arbitrary"),
        name="in_proj",
    )(x2d, ln.reshape(1, D_MODEL), w_pad, c_tab, s_tab)


def _conv_kernel(a_ref, hist_ref, w_ref, b_ref, g_ref, beta_ref, o_ref, ext_ref):
    tt = a_ref.shape[1]

    @pl.when(pl.program_id(1) == 0)
    def _():
        ext_ref[0:HIST_ROWS, :] = hist_ref[0]

    ext_ref[HIST_ROWS:HIST_ROWS + tt, :] = a_ref[0]
    acc = jnp.broadcast_to(b_ref[...], (tt, CONV_CH))
    lead = HIST_ROWS - (CONV_W - 1)
    for k in range(CONV_W):
        acc = acc + w_ref[k:k + 1, :] * ext_ref[lead + k:lead + k + tt, :]
    mu = jnp.mean(acc, axis=-1, keepdims=True)
    var = jnp.mean(jnp.square(acc - mu), axis=-1, keepdims=True)
    y = (acc - mu) * lax.rsqrt(var + LN_EPS) * g_ref[...] + beta_ref[...]
    o_ref[0] = y * jax.nn.sigmoid(y)
    carry = ext_ref[tt:tt + HIST_ROWS, :]
    ext_ref[0:HIST_ROWS, :] = carry


def _conv_module(a3d, hist, dw_w, dw_b, ln_g, ln_b, tt):
    b, t, _ = a3d.shape
    w_pad = jnp.pad(dw_w, ((0, HIST_ROWS - CONV_W), (0, 0)))
    vec = lambda i, j: (0, 0)
    return pl.pallas_call(
        _conv_kernel,
        grid=(b, t // tt),
        in_specs=[
            pl.BlockSpec((1, tt, CONV_CH), lambda i, j: (i, j, 0)),
            pl.BlockSpec((1, HIST_ROWS, CONV_CH), lambda i, j: (i, 0, 0)),
            pl.BlockSpec((HIST_ROWS, CONV_CH), vec),
            pl.BlockSpec((1, CONV_CH), vec),
            pl.BlockSpec((1, CONV_CH), vec),
            pl.BlockSpec((1, CONV_CH), vec),
        ],
        out_specs=pl.BlockSpec((1, tt, CONV_CH), lambda i, j: (i, j, 0)),
        out_shape=jax.ShapeDtypeStruct((b, t, CONV_CH), F32),
        scratch_shapes=[pltpu.VMEM((HIST_ROWS + tt, CONV_CH), F32)],
        compiler_params=_cparams("arbitrary", "arbitrary"),
        name="conv_module",
    )(a3d, hist, w_pad, dw_b.reshape(1, -1), ln_g.reshape(1, -1), ln_b.reshape(1, -1))


L_GROUP = 2 * LANES // HEAD_DIM


def _compress_rows(x_refs, pe_ref, wk1_ref, wk2_ref, wv1_ref, wv2_ref, nb, pitch=CMP_BLOCK):
    hk = jnp.zeros((N_KV * nb, CMP_HID), F32)
    hv = jnp.zeros((N_KV * nb, CMP_HID), F32)
    for j in range(CMP_BLOCK // L_GROUP):
        parts_k, parts_v = [], []
        for x_ref in x_refs:
            xs = [x_ref[pl.ds(j * L_GROUP + i, nb, stride=pitch), :] + pe_ref[j * L_GROUP + i:j * L_GROUP + i + 1, :]
                  for i in range(L_GROUP)]
            parts_k.append(jnp.concatenate([x[:, :HEAD_DIM] for x in xs], axis=1))
            parts_v.append(jnp.concatenate([x[:, HEAD_DIM:] for x in xs], axis=1))
        xk = jnp.concatenate(parts_k, axis=0).astype(BF16)
        xv = jnp.concatenate(parts_v, axis=0).astype(BF16)
        rows = slice(j * L_GROUP * HEAD_DIM, (j + 1) * L_GROUP * HEAD_DIM)
        hk = hk + jnp.dot(xk, wk1_ref[rows, :], preferred_element_type=F32)
        hv = hv + jnp.dot(xv, wv1_ref[rows, :], preferred_element_type=F32)
    kc = jnp.dot((hk * jax.nn.sigmoid(hk)).astype(BF16), wk2_ref[...], preferred_element_type=F32)
    vc = jnp.dot((hv * jax.nn.sigmoid(hv)).astype(BF16), wv2_ref[...], preferred_element_type=F32)
    return kc, vc


def _compress_dense_kernel(*refs):
    x_refs, (pe_ref, wk1_ref, wk2_ref, wv1_ref, wv2_ref, kc_ref, vc_ref) = refs[:N_KV], refs[N_KV:]
    nb = x_refs[0].shape[1] // CMP_BLOCK
    kc, vc = _compress_rows([x.at[0] for x in x_refs], pe_ref, wk1_ref, wk2_ref, wv1_ref, wv2_ref, nb)
    for g in range(N_KV):
        kc_ref[0, g] = kc[g * nb:(g + 1) * nb]
        vc_ref[0, g] = vc[g * nb:(g + 1) * nb]


def _cmp_weight_specs():
    const = lambda *_: (0, 0)
    return [
        pl.BlockSpec((CMP_BLOCK, 2 * HEAD_DIM), const),
        pl.BlockSpec((CMP_BLOCK * HEAD_DIM, CMP_HID), const),
        pl.BlockSpec((CMP_HID, HEAD_DIM), const),
        pl.BlockSpec((CMP_BLOCK * HEAD_DIM, CMP_HID), const),
        pl.BlockSpec((CMP_HID, HEAD_DIM), const),
    ]


def _cmp_weights(pos_emb, w_k1, w_k2, w_v1, w_v2):
    pe = pos_emb.reshape(CMP_BLOCK, 2 * HEAD_DIM)
    return (pe, w_k1.reshape(-1, CMP_HID).astype(BF16), w_k2.astype(BF16),
            w_v1.reshape(-1, CMP_HID).astype(BF16), w_v2.astype(BF16))


def _compress_dense(kv3d, cmp_w):
    b, t, _ = kv3d.shape
    nb = t // CMP_BLOCK
    out = jax.ShapeDtypeStruct((b, N_KV, nb, HEAD_DIM), F32)
    ospec = pl.BlockSpec((1, N_KV, nb, HEAD_DIM), lambda i: (i, 0, 0, 0))
    return pl.pallas_call(
        _compress_dense_kernel,
        grid=(b,),
        in_specs=[pl.BlockSpec((1, t, 2 * HEAD_DIM), functools.partial(lambda g, i: (i, 0, g), g))
                  for g in range(N_KV)] + _cmp_weight_specs(),
        out_specs=(ospec, ospec),
        out_shape=(out, out),
        compiler_params=_cparams("arbitrary"),
        name="compress_prompt",
    )(*([kv3d] * N_KV), *cmp_w)


PAGES_PER_STEP = 64
BLOCK_PITCH = CMP_BLOCK + SUBLANES


def _compress_paged_kernel(pt_ref, cache_ref, pe_ref, wk1_ref, wk2_ref, wv1_ref, wv2_ref, kc_ref, vc_ref,
                           raw_ref, rows_ref, sem_ref):
    step = pl.program_id(0)
    n_steps = pl.num_programs(0)
    nb = PAGES_PER_STEP * PAGE_SIZE // CMP_BLOCK
    kvw = 2 * HEAD_DIM

    def page_copy(s, slot, p):
        return pltpu.make_async_copy(cache_ref.at[pt_ref[s * PAGES_PER_STEP + p]], raw_ref.at[slot, p], sem_ref.at[slot])

    def issue(s, slot):
        for p in range(PAGES_PER_STEP):
            page_copy(s, slot, p).start()

    slot = step % 2

    @pl.when(step == 0)
    def _():
        issue(step, slot)

    @pl.when(step + 1 < n_steps)
    def _():
        issue(step + 1, 1 - slot)

    for p in range(PAGES_PER_STEP):
        page_copy(step, slot, p).wait()

    for p in range(PAGES_PER_STEP):
        page = raw_ref[slot, p].T
        for g in range(N_KV):
            for n in range(PAGE_SIZE // CMP_BLOCK):
                row0 = (p * (PAGE_SIZE // CMP_BLOCK) + n) * BLOCK_PITCH
                rows_ref[g, row0:row0 + CMP_BLOCK, :] = page[n * CMP_BLOCK:(n + 1) * CMP_BLOCK, g * kvw:(g + 1) * kvw]

    kc, vc = _compress_rows([rows_ref.at[g] for g in range(N_KV)], pe_ref, wk1_ref, wk2_ref, wv1_ref, wv2_ref, nb,
                            pitch=BLOCK_PITCH)
    for g in range(N_KV):
        kc_ref[0, g] = kc[g * nb:(g + 1) * nb]
        vc_ref[0, g] = vc[g * nb:(g + 1) * nb]


def _compress_paged(cache_t, page_table, cmp_w):
    db, n_pages = page_table.shape
    steps_per_row = n_pages // PAGES_PER_STEP
    nb = PAGES_PER_STEP * PAGE_SIZE // CMP_BLOCK
    out = jax.ShapeDtypeStruct((db, N_KV, steps_per_row * nb, HEAD_DIM), F32)
    ospec = pl.BlockSpec((1, N_KV, nb, HEAD_DIM), lambda i, pt: (i // steps_per_row, 0, i % steps_per_row, 0))
    grid_spec = pltpu.PrefetchScalarGridSpec(
        num_scalar_prefetch=1,
        grid=(db * steps_per_row,),
        in_specs=[pl.BlockSpec(memory_space=pl.ANY)] + _cmp_weight_specs(),
        out_specs=(ospec, ospec),
        scratch_shapes=[pltpu.VMEM((2, PAGES_PER_STEP, COL_KV, PAGE_SIZE), F32),
                        pltpu.VMEM((N_KV, nb * BLOCK_PITCH, 2 * HEAD_DIM), F32),
                        pltpu.SemaphoreType.DMA((2,))],
    )
    return pl.pallas_call(
        _compress_paged_kernel,
        grid_spec=grid_spec,
        out_shape=(out, out),
        compiler_params=_cparams("arbitrary"),
        name="compress_paged",
    )(page_table.reshape(-1), cache_t, *cmp_w)


TQ = LANES
KC = 512
LOG2_E = 1.4426950408889634
WIN_KEYS = WINDOW + TQ
V_ROWS = HEAD_DIM + 16
KAUG = 2 * LANES


def _top_blocks(imp, cand, n_blocks):
    blk = lax.broadcasted_iota(jnp.int32, imp.shape, 0)
    score = jnp.where(cand, imp, -1.0)
    sel = jnp.zeros(imp.shape, F32)
    for _ in range(N_SEL):
        mx = jnp.max(score, axis=0, keepdims=True)
        idx = jnp.min(jnp.where(score == mx, blk, n_blocks), axis=0, keepdims=True)
        pick = blk == idx
        sel = jnp.where(pick, 1.0, sel)
        score = jnp.where(pick, -2.0, score)
    return sel


def _prompt_attn_kernel(q_ref, kc_ref, vct_ref, ksel_ref, vselt_ref, kwin_ref, vwint_ref, gate_ref, o_ref):
    qt = pl.program_id(2)
    t0 = qt * TQ
    nb = kc_ref.shape[2]
    width = Q_PER_KV * TQ

    q = q_ref[0].astype(F32) * (ATT_SCALE * LOG2_E)
    q_t = q.T
    q4 = jnp.concatenate([q_t[r * HEAD_DIM:(r + 1) * HEAD_DIM] for r in range(Q_PER_KV)], axis=1)
    q4b = q4.astype(BF16)
    tok = t0 + lax.broadcasted_iota(jnp.int32, (1, width), 1) % TQ

    sc = jnp.dot(kc_ref[0, 0].astype(BF16), q4b, preferred_element_type=F32)
    blk = lax.broadcasted_iota(jnp.int32, (nb, width), 0)
    valid_c = (blk + 1) * CMP_BLOCK - 1 <= tok
    sc = jnp.where(valid_c, sc, NEG_INF)
    e = jnp.where(valid_c, jnp.exp2(sc - jnp.max(sc, axis=0, keepdims=True)), 0.0)
    den = jnp.sum(e, axis=0, keepdims=True)
    p = e / jnp.where(den > 0.0, den, 1.0)
    o_cmp = jnp.dot(vct_ref[0, 0].astype(BF16), p.astype(BF16), preferred_element_type=F32)
    imp = p[:, 0:TQ]
    for r in range(1, Q_PER_KV):
        imp = imp + p[:, r * TQ:(r + 1) * TQ]

    tok1 = t0 + lax.broadcasted_iota(jnp.int32, (1, TQ), 1)
    own = tok1 // SEL_BLOCK
    blk1 = lax.broadcasted_iota(jnp.int32, (nb, TQ), 0)
    cand = blk1 < own
    sel = _top_blocks(imp, cand, nb)
    bias = jnp.where(cand, jnp.where(sel > 0.0, 0.0, NEG_INF), jnp.where(blk1 == own, 0.0, NEG_INF))
    bias4 = jnp.concatenate([bias] * Q_PER_KV, axis=1).astype(BF16)
    q_aug = jnp.concatenate([q4b, bias4, jnp.zeros((KAUG - HEAD_DIM - nb, width), BF16)], axis=0)

    def scores(c):
        return jnp.dot(ksel_ref[0, 0, pl.ds(pl.multiple_of(c * KC, KC), KC), :], q_aug, preferred_element_type=F32)

    def softmax_pv(c, s, m, acc):
        m_new = jnp.maximum(m, jnp.max(s, axis=0, keepdims=True))
        alpha = jnp.exp2(m - m_new)
        pr = jnp.exp2(s - m_new).astype(BF16)
        v_blk = vselt_ref[0, 0, :, pl.ds(pl.multiple_of(c * KC, KC), KC)]
        return m_new, alpha * acc + jnp.dot(v_blk, pr, preferred_element_type=F32)

    def sel_step(c, carry):
        s, m, acc = carry
        s_next = scores(c + 1)
        return (s_next,) + softmax_pv(c, s, m, acc)

    last = t0 // KC
    init = (scores(0), jnp.full((1, width), NEG_INF, F32), jnp.zeros((V_ROWS, width), F32))
    s_last, m_sel, acc_sel = lax.fori_loop(0, last, sel_step, init)
    key = last * KC + lax.broadcasted_iota(jnp.int32, (KC, width), 0)
    _, acc_sel = softmax_pv(last, jnp.where(key <= tok, s_last, NEG_INF), m_sel, acc_sel)
    o_sel = acc_sel[0:HEAD_DIM] / acc_sel[HEAD_DIM:HEAD_DIM + 1]

    w0 = pl.multiple_of(jnp.maximum(t0 - WINDOW, 0), TQ)
    sw = jnp.dot(kwin_ref[0, 0, pl.ds(w0, WIN_KEYS), :], q4b, preferred_element_type=F32)
    dist = tok - (w0 + lax.broadcasted_iota(jnp.int32, (WIN_KEYS, width), 0))
    sw = jnp.where((dist >= 0) & (dist < WINDOW), sw, NEG_INF)
    pw = jnp.exp2(sw - jnp.max(sw, axis=0, keepdims=True)).astype(BF16)
    acc_win = jnp.dot(vwint_ref[0, 0, :, pl.ds(w0, WIN_KEYS)], pw, preferred_element_type=F32)
    o_win = acc_win[0:HEAD_DIM] / acc_win[HEAD_DIM:HEAD_DIM + 1]

    outs = []
    for r in range(Q_PER_KV):
        sl = slice(r * TQ, (r + 1) * TQ)
        g = [gate_ref[0, 0, j * Q_PER_KV + r:j * Q_PER_KV + r + 1, :] for j in range(3)]
        outs.append(g[0] * o_cmp[:, sl] + g[1] * o_sel[:, sl] + g[2] * o_win[:, sl])
    o_ref[0] = jnp.concatenate(outs, axis=0).T


def _prompt_attention(q3, k_c, v_ct, ksel_aug, vsel_t, kwin, vwin_t, gates_t):
    b, t, _ = q3.shape
    nb = k_c.shape[2]
    width = Q_PER_KV * HEAD_DIM
    per_bg = lambda i, g, j: (i, g, 0, 0)
    return pl.pallas_call(
        _prompt_attn_kernel,
        grid=(b, N_KV, t // TQ),
        in_specs=[
            pl.BlockSpec((1, TQ, width), lambda i, g, j: (i, j, g)),
            pl.BlockSpec((1, 1, nb, HEAD_DIM), per_bg),
            pl.BlockSpec((1, 1, HEAD_DIM, nb), per_bg),
            pl.BlockSpec((1, 1, t, KAUG), per_bg),
            pl.BlockSpec((1, 1, V_ROWS, t), per_bg),
            pl.BlockSpec((1, 1, t, HEAD_DIM), per_bg),
            pl.BlockSpec((1, 1, V_ROWS, t), per_bg),
            pl.BlockSpec((1, 1, 3 * Q_PER_KV, TQ), lambda i, g, j: (i, g, 0, j)),
        ],
        out_specs=pl.BlockSpec((1, TQ, width), lambda i, g, j: (i, j, g)),
        out_shape=jax.ShapeDtypeStruct((b, t, COL_Q), F32),
        compiler_params=_cparams("arbitrary", "arbitrary", "arbitrary"),
        name="prompt_attention",
    )(q3, k_c, v_ct, ksel_aug, vsel_t, kwin, vwin_t, gates_t)


def _split_kv(kv3d, kv_t=None):
    b, t, _ = kv3d.shape
    kv = kv3d.reshape(b, t, N_KV, 2, HEAD_DIM).astype(BF16)
    if kv_t is None:
        v_t = jnp.transpose(kv[:, :, :, 1], (0, 2, 3, 1))
    else:
        v_t = kv_t.reshape(b, N_KV, 2, HEAD_DIM, t)[:, :, 1].astype(BF16)
    ones_row = jnp.concatenate([jnp.ones((b, N_KV, 1, t), BF16), jnp.zeros((b, N_KV, V_ROWS - HEAD_DIM - 1, t), BF16)],
                               axis=2)
    return jnp.transpose(kv[:, :, :, 0], (0, 2, 1, 3)), jnp.concatenate([v_t, ones_row], axis=2)


def _augment_keys(k):
    b, g, t, _ = k.shape
    nb = t // SEL_BLOCK
    onehot = (jnp.arange(t)[:, None] // SEL_BLOCK == jnp.arange(nb)[None, :]).astype(BF16)
    onehot = jnp.broadcast_to(onehot, (b, g, t, nb))
    pad = jnp.zeros((b, g, t, KAUG - HEAD_DIM - nb), BF16)
    return jnp.concatenate([k, onehot, pad], axis=-1)


def _sample_cmp_kernel(q_ref, kc_ref, vc_ref, ocmp_ref, idx_ref, imp_ref, *, past_len, dec_seq):
    b = pl.program_id(0)
    nb = kc_ref.shape[2]
    rows = Q_PER_KV * SUBLANES
    t_row = lax.broadcasted_iota(jnp.int32, (rows, nb), 0) % SUBLANES
    blk = lax.broadcasted_iota(jnp.int32, (rows, nb), 1)
    valid = (blk + 1) * CMP_BLOCK - 1 <= past_len + t_row
    for g in range(N_KV):
        s = lax.dot_general(q_ref[0, g], kc_ref[0, g].astype(BF16), (((1,), (1,)), ((), ())),
                            preferred_element_type=F32) * ATT_SCALE
        s = jnp.where(valid, s, NEG_INF)
        e = jnp.where(valid, jnp.exp(s - jnp.max(s, axis=1, keepdims=True)), 0.0)
        den = jnp.sum(e, axis=1, keepdims=True)
        p = e / jnp.where(den > 0.0, den, 1.0)
        ocmp_ref[0, g] = jnp.dot(p.astype(BF16), vc_ref[0, g].astype(BF16), preferred_element_type=F32)
        imp = p[0:SUBLANES]
        for r in range(1, Q_PER_KV):
            imp = imp + p[r * SUBLANES:(r + 1) * SUBLANES]
        row0 = pl.multiple_of((b * N_KV + g) * SUBLANES, SUBLANES)
        imp_ref[pl.ds(row0, SUBLANES), :] = imp

    @pl.when(b == pl.num_programs(0) - 1)
    def _():
        n_rows = imp_ref.shape[0]
        lane = lax.broadcasted_iota(jnp.int32, (n_rows, nb), 1)
        own = (past_len + lax.broadcasted_iota(jnp.int32, (n_rows, nb), 0) % SUBLANES) // SEL_BLOCK
        score = jnp.where(lane < own, imp_ref[...], -1.0)
        col = lax.broadcasted_iota(jnp.int32, (n_rows, LANES), 1)
        picks = jnp.zeros((n_rows, LANES), jnp.int32)
        for i in range(N_SEL):
            mx = jnp.max(score, axis=1, keepdims=True)
            idx = jnp.min(jnp.where(score == mx, lane, nb), axis=1, keepdims=True)
            score = jnp.where(lane == idx, -2.0, score)
            picks = jnp.where(col == i, idx, picks)
        idx_ref[...] = picks


def _sample_cmp_select(q_rt, k_c, v_c, past_len, dec_seq):
    db, _, rows, _ = q_rt.shape
    nb = k_c.shape[2]
    spec4 = lambda r, c: pl.BlockSpec((1, N_KV, r, c), lambda i: (i, 0, 0, 0))
    n_rows = db * N_KV * SUBLANES
    return pl.pallas_call(
        functools.partial(_sample_cmp_kernel, past_len=past_len, dec_seq=dec_seq),
        grid=(db,),
        in_specs=[spec4(rows, HEAD_DIM), spec4(nb, HEAD_DIM), spec4(nb, HEAD_DIM)],
        out_specs=(spec4(rows, HEAD_DIM), pl.BlockSpec((n_rows, LANES), lambda i: (0, 0))),
        out_shape=(jax.ShapeDtypeStruct((db, N_KV, rows, HEAD_DIM), F32),
                   jax.ShapeDtypeStruct((n_rows, LANES), jnp.int32)),
        scratch_shapes=[pltpu.VMEM((n_rows, nb), F32)],
        compiler_params=_cparams("arbitrary"),
        name="sample_cmp_select",
    )(q_rt, k_c, v_c)


def _sample_attn_kernel(idx_ref, pt_ref, cache_ref, q_ref, snew_ref, wnew_ref, wstate_ref, ocmp_ref, gate_ref,
                        o_ref, kv_ref, sem_ref, *, dec_seq, n_pages):
    b = pl.program_id(0)
    n_b = pl.num_programs(0)
    kvw = 2 * HEAD_DIM
    blocks_per_page = PAGE_SIZE // SEL_BLOCK

    def block_id(bb, g, t, i):
        return idx_ref[((bb * N_KV + g) * dec_seq + t) * N_SEL + i]

    def slab_copy(bb, slot, g, t, i):
        page = pt_ref[bb * n_pages + block_id(bb, g, t, i) // blocks_per_page]
        return pltpu.make_async_copy(cache_ref.at[page, pl.ds(g * kvw, kvw), :], kv_ref.at[slot, g, t, i],
                                     sem_ref.at[slot])

    def for_all_slabs(fn):
        for g in range(N_KV):
            for t in range(dec_seq):
                for i in range(N_SEL):
                    fn(g, t, i)

    slot = b % 2

    @pl.when(b == 0)
    def _():
        for_all_slabs(lambda g, t, i: slab_copy(b, slot, g, t, i).start())

    @pl.when(b + 1 < n_b)
    def _():
        for_all_slabs(lambda g, t, i: slab_copy(b + 1, 1 - slot, g, t, i).start())

    rows = dec_seq * SUBLANES
    tok = lax.broadcasted_iota(jnp.int32, (rows, 1), 0) // SUBLANES
    n_state = wstate_ref.shape[2]

    def attend(q, k_t, v_t, valid):
        s = jnp.dot(q, k_t.astype(BF16), preferred_element_type=F32) * ATT_SCALE
        s = jnp.where(valid, s, NEG_INF)
        p = jnp.exp(s - jnp.max(s, axis=1, keepdims=True))
        den = jnp.sum(p, axis=1, keepdims=True)
        return lax.dot_general(p.astype(BF16), v_t.astype(BF16), (((1,), (1,)), ((), ())),
                               preferred_element_type=F32) / den

    o_win = []
    for g in range(N_KV):
        k_rows, v_rows = pl.ds(g * kvw, HEAD_DIM), pl.ds(g * kvw + HEAD_DIM, HEAD_DIM)
        k_t = jnp.concatenate([wstate_ref[0, k_rows, :], wnew_ref[0, k_rows, :]], axis=1)
        v_t = jnp.concatenate([wstate_ref[0, v_rows, :], wnew_ref[0, v_rows, :]], axis=1)
        lane = lax.broadcasted_iota(jnp.int32, (rows, n_state + LANES), 1)
        new_i = lane - n_state
        valid = ((lane < n_state) & (lane > tok)) | ((new_i >= 0) & (new_i <= tok) & (new_i < dec_seq))
        o_win.append(attend(q_ref[0, g], k_t, v_t, valid))

    for_all_slabs(lambda g, t, i: slab_copy(b, slot, g, t, i).wait())

    lane1 = lax.broadcasted_iota(jnp.int32, (1, PAGE_SIZE), 1)
    for g in range(N_KV):
        o_sel = []
        for t in range(dec_seq):
            k_parts = [kv_ref[slot, g, t, i, 0:HEAD_DIM, :] for i in range(N_SEL)]
            v_parts = [kv_ref[slot, g, t, i, HEAD_DIM:kvw, :] for i in range(N_SEL)]
            k_parts.append(snew_ref[0, g * kvw:g * kvw + HEAD_DIM, :])
            v_parts.append(snew_ref[0, g * kvw + HEAD_DIM:(g + 1) * kvw, :])
            halves = [lane1 // SEL_BLOCK == block_id(b, g, t, i) % blocks_per_page for i in range(N_SEL)]
            halves.append((lane1 <= t) & (lane1 < dec_seq))
            q = q_ref[0, g, t * SUBLANES:(t + 1) * SUBLANES, :]
            o_sel.append(attend(q, jnp.concatenate(k_parts, axis=1), jnp.concatenate(v_parts, axis=1),
                                jnp.concatenate(halves, axis=1)))
        o_sel = jnp.concatenate(o_sel, axis=0)
        o_ref[0, g] = (gate_ref[0, g, 0] * ocmp_ref[0, g] + gate_ref[0, g, 1] * o_sel
                       + gate_ref[0, g, 2] * o_win[g])


def _sample_attention(sel_idx, page_table, cache_t, q_tr, snew_t, wnew_t, wstate_t, ocmp_tr, gates_tr, dec_seq):
    db, n_pages = page_table.shape
    rows = dec_seq * SUBLANES
    n_state = wstate_t.shape[2]
    per_b4 = lambda i, *_: (i, 0, 0, 0)
    per_b3 = lambda i, *_: (i, 0, 0)
    grid_spec = pltpu.PrefetchScalarGridSpec(
        num_scalar_prefetch=2,
        grid=(db,),
        in_specs=[
            pl.BlockSpec(memory_space=pl.ANY),
            pl.BlockSpec((1, N_KV, rows, HEAD_DIM), per_b4),
            pl.BlockSpec((1, COL_KV, LANES), per_b3),
            pl.BlockSpec((1, COL_KV, LANES), per_b3),
            pl.BlockSpec((1, COL_KV, n_state), per_b3),
            pl.BlockSpec((1, N_KV, rows, HEAD_DIM), per_b4),
            pl.BlockSpec((1, N_KV, 3, rows, HEAD_DIM), lambda i, *_: (i, 0, 0, 0, 0)),
        ],
        out_specs=pl.BlockSpec((1, N_KV, rows, HEAD_DIM), per_b4),
        scratch_shapes=[pltpu.VMEM((2, N_KV, dec_seq, N_SEL, 2 * HEAD_DIM, PAGE_SIZE), F32),
                        pltpu.SemaphoreType.DMA((2,))],
    )
    return pl.pallas_call(
        functools.partial(_sample_attn_kernel, dec_seq=dec_seq, n_pages=n_pages),
        grid_spec=grid_spec,
        out_shape=jax.ShapeDtypeStruct((db, N_KV, rows, HEAD_DIM), F32),
        compiler_params=_cparams("arbitrary"),
        name="sample_attention",
    )(sel_idx, page_table.reshape(-1), cache_t, q_tr, snew_t, wnew_t, wstate_t, ocmp_tr, gates_tr)


ROUTE_E1, ROUTE_E2, ROUTE_W1, ROUTE_W2 = 0, 1, 2, 3


def _merge_kernel(conv_ref, att_ref, x_ref, gc_ref, ga_ref, wo_ref, ln2_ref, wr_ref, x1_ref, h3_ref, route_ref):
    mix = jnp.concatenate([_rms(conv_ref[...], gc_ref[...]), _rms(att_ref[...], ga_ref[...])], axis=1)
    x1 = x_ref[...] + jnp.dot(mix.astype(BF16), wo_ref[...], preferred_element_type=F32)
    x1_ref[...] = x1
    h = _rms(x1, ln2_ref[...])
    for j in range(ROW_TILES):
        h3_ref[:, j, :] = h[:, j * LANES:(j + 1) * LANES]

    logits = jnp.dot(h.astype(BF16), wr_ref[...], preferred_element_type=F32)
    lane = lax.broadcasted_iota(jnp.int32, logits.shape, 1)
    is_g = lane < N_GROUPS
    lg = jnp.where(is_g, logits, NEG_INF)
    mg = jnp.max(lg, axis=1, keepdims=True)
    sg = jnp.sum(jnp.where(is_g, jnp.exp(lg - mg), 0.0), axis=1, keepdims=True)
    grp = jnp.min(jnp.where(lg == mg, lane, LANES), axis=1, keepdims=True)
    p_top = 1.0 / sg
    in_grp = ((lane + (EXPERTS_PER_GROUP - N_GROUPS)) // EXPERTS_PER_GROUP) == grp + 1
    le = jnp.where(in_grp, logits, NEG_INF)
    ee = jnp.where(in_grp, jnp.exp(le - jnp.max(le, axis=1, keepdims=True)), 0.0)
    pe = jnp.where(in_grp, ee / jnp.sum(ee, axis=1, keepdims=True), -1.0)
    p1 = jnp.max(pe, axis=1, keepdims=True)
    i1 = jnp.min(jnp.where(pe == p1, lane, LANES), axis=1, keepdims=True)
    pe2 = jnp.where(lane == i1, -1.0, pe)
    p2 = jnp.max(pe2, axis=1, keepdims=True)
    i2 = jnp.min(jnp.where(pe2 == p2, lane, LANES), axis=1, keepdims=True)
    den = p1 + p2
    rec = jnp.where(lane == ROUTE_E1, (i1 - N_GROUPS).astype(F32), 0.0)
    rec = jnp.where(lane == ROUTE_E2, (i2 - N_GROUPS).astype(F32), rec)
    rec = jnp.where(lane == ROUTE_W1, p1 / den * p_top, rec)
    rec = jnp.where(lane == ROUTE_W2, p2 / den * p_top, rec)
    route_ref[...] = rec


def _merge(conv2d, att2d, x2d, g_conv, g_att, w_out_b, ln2, w_route_b, tm):
    n = x2d.shape[0]
    row = lambda i: (i, 0)
    const = lambda i: (0, 0)
    return pl.pallas_call(
        _merge_kernel,
        grid=(n // tm,),
        in_specs=[
            pl.BlockSpec((tm, CONV_CH), row),
            pl.BlockSpec((tm, COL_Q), row),
            pl.BlockSpec((tm, D_MODEL), row),
            pl.BlockSpec((1, CONV_CH), const),
            pl.BlockSpec((1, COL_Q), const),
            pl.BlockSpec((CONV_CH + COL_Q, D_MODEL), const),
            pl.BlockSpec((1, D_MODEL), const),
            pl.BlockSpec((D_MODEL, LANES), const),
        ],
        out_specs=(
            pl.BlockSpec((tm, D_MODEL), row),
            pl.BlockSpec((tm, ROW_TILES, LANES), lambda i: (i, 0, 0)),
            pl.BlockSpec((tm, LANES), row),
        ),
        out_shape=(
            jax.ShapeDtypeStruct((n, D_MODEL), F32),
            jax.ShapeDtypeStruct((n, ROW_TILES, LANES), F32),
            jax.ShapeDtypeStruct((n, LANES), F32),
        ),
        compiler_params=_cparams("arbitrary"),
        name="merge_route",
    )(conv2d, att2d, x2d, g_conv.reshape(1, -1), g_att.reshape(1, -1), w_out_b, ln2.reshape(1, -1), w_route_b)


META_W = 2 * LANES


def _rank_kernel(route_ref, dest_ref, meta_ref, e_ref, rank_ref, count_ref):
    i = pl.program_id(0)
    rt = route_ref.shape[0]

    @pl.when(i == 0)
    def _():
        count_ref[...] = jnp.zeros(count_ref.shape, F32)

    route_t = route_ref[...].T
    expert_id = lax.broadcasted_iota(jnp.int32, (N_EXPERTS, rt), 0).astype(F32)
    before = (lax.broadcasted_iota(jnp.int32, (rt, rt), 0) < lax.broadcasted_iota(jnp.int32, (rt, rt), 1))
    before = before.astype(BF16)
    ones = jnp.ones((rt, LANES), BF16)
    e_rows, rank_rows = [], []
    for k in range(TOP_K):
        e_k = route_t[ROUTE_E1 + k:ROUTE_E1 + k + 1, :]
        onehot = (expert_id == e_k).astype(F32)
        earlier = jnp.dot(onehot.astype(BF16), before, preferred_element_type=F32)
        seen = count_ref[...]
        seen_w = jnp.concatenate([seen] * (rt // LANES), axis=1)
        rank_rows.append(jnp.sum(onehot * (earlier + seen_w), axis=0, keepdims=True))
        e_rows.append(e_k)
        count_ref[...] = seen + jnp.dot(onehot.astype(BF16), ones, preferred_element_type=F32)
    e_ref[i] = jnp.concatenate(e_rows, axis=0)
    rank_ref[i] = jnp.concatenate(rank_rows, axis=0)

    @pl.when(i == pl.num_programs(0) - 1)
    def _():
        counts = jnp.concatenate([count_ref[...]] * (META_W // LANES), axis=1)
        padded = jnp.floor((counts + (MOE_ROWS - 1)) * (1.0 / MOE_ROWS)) * MOE_ROWS
        lane = lax.broadcasted_iota(jnp.int32, (1, META_W), 1)
        chunk_start = lane.astype(F32) * MOE_ROWS
        chunk_e = jnp.zeros((1, META_W), F32)
        ends = jnp.zeros((1, META_W), F32)
        end = jnp.zeros((1, META_W), F32)
        starts = []
        for ex in range(N_EXPERTS):
            starts.append(end[:, 0:rt])
            end = end + padded[ex:ex + 1, :]
            chunk_e = chunk_e + (end <= chunk_start).astype(F32)
            ends = jnp.where(lane == ex, end, ends)
        n_used = end * (1.0 / MOE_ROWS)
        chunk_e = jnp.minimum(chunk_e, float(N_EXPERTS - 1))
        row = lax.broadcasted_iota(jnp.int32, (SUBLANES, META_W), 0)
        meta = jnp.where(row == 0, chunk_e, jnp.where(row == 1, n_used, jnp.where(row == 2, ends, 0.0)))
        meta_ref[...] = meta.astype(jnp.int32)

        def place(ti, carry):
            e_t = e_ref[ti]
            dest = rank_ref[ti]
            for ex in range(N_EXPERTS):
                dest = dest + jnp.where(e_t == float(ex), starts[ex], 0.0)
            dest_ref[ti] = dest.astype(jnp.int32)
            return carry

        lax.fori_loop(0, pl.num_programs(0), place, 0)


def _rank(route, rt):
    n = route.shape[0]
    tiles = n // rt
    assert n * TOP_K // MOE_ROWS + N_EXPERTS <= META_W and rt <= META_W, "chunk table is one row of META_W lanes"
    whole = lambda i: (0, 0, 0)
    return pl.pallas_call(
        _rank_kernel,
        grid=(tiles,),
        in_specs=[pl.BlockSpec((rt, LANES), lambda i: (i, 0))],
        out_specs=(pl.BlockSpec((tiles, TOP_K, rt), whole), pl.BlockSpec((SUBLANES, META_W), lambda i: (0, 0))),
        out_shape=(jax.ShapeDtypeStruct((tiles, TOP_K, rt), jnp.int32),
                   jax.ShapeDtypeStruct((SUBLANES, META_W), jnp.int32)),
        scratch_shapes=[pltpu.VMEM((tiles, TOP_K, rt), F32), pltpu.VMEM((tiles, TOP_K, rt), F32),
                        pltpu.VMEM((N_EXPERTS, LANES), F32)],
        compiler_params=_cparams("arbitrary"),
        name="route_rank",
    )(route)


def _row_copies_wait(src_ref, dst_ref, sem, n_rows):
    pltpu.make_async_copy(src_ref.at[pl.ds(0, n_rows)], dst_ref.at[pl.ds(0, n_rows)], sem).wait()


def _scatter_kernel(meta_ref, dest_ref, h3_ref, xs_ref, zero_ref, sem_ref):
    i = pl.program_id(0)
    rt = h3_ref.shape[0]
    n_chunks = xs_ref.shape[0] // MOE_ROWS

    @pl.when(i == 0)
    def _():
        zero_ref[...] = jnp.zeros(zero_ref.shape, F32)

        def zero_chunk(first_row):
            return pltpu.make_async_copy(zero_ref, xs_ref.at[pl.ds(first_row, MOE_ROWS)], sem_ref.at[1])

        def fills(act):
            prev = 0
            for ex in range(N_EXPERTS):
                end = meta_ref[2, ex]

                @pl.when(end > prev)
                def _():
                    act(zero_chunk(end - MOE_ROWS))
                prev = end
            for c in range(n_chunks - N_EXPERTS, n_chunks):
                @pl.when(c >= meta_ref[1, 0])
                def _():
                    act(zero_chunk(c * MOE_ROWS))

        fills(lambda copy: copy.start())
        fills(lambda copy: copy.wait())

    def issue(t, carry):
        for k in range(TOP_K):
            pltpu.make_async_copy(h3_ref.at[t], xs_ref.at[dest_ref[0, k, t]], sem_ref.at[0]).start()
        return carry

    lax.fori_loop(0, rt, issue, 0, unroll=8)
    _row_copies_wait(h3_ref, h3_ref, sem_ref.at[0], rt)
    _row_copies_wait(h3_ref, h3_ref, sem_ref.at[0], rt)


def _scatter_rows(h3, dest, meta, rt):
    n = h3.shape[0]
    n_slots = (n * TOP_K // MOE_ROWS + N_EXPERTS) * MOE_ROWS
    grid_spec = pltpu.PrefetchScalarGridSpec(
        num_scalar_prefetch=1,
        grid=(n // rt,),
        in_specs=[pl.BlockSpec((1, TOP_K, rt), lambda i, meta: (i, 0, 0), memory_space=pltpu.SMEM),
                  pl.BlockSpec((rt, ROW_TILES, LANES), lambda i, meta: (i, 0, 0))],
        out_specs=pl.BlockSpec(memory_space=pl.ANY),
        scratch_shapes=[pltpu.VMEM((MOE_ROWS, ROW_TILES, LANES), F32), pltpu.SemaphoreType.DMA((2,))],
    )
    return pl.pallas_call(
        _scatter_kernel,
        grid_spec=grid_spec,
        out_shape=jax.ShapeDtypeStruct((n_slots, ROW_TILES, LANES), F32),
        compiler_params=_cparams("arbitrary"),
        name="moe_scatter_rows",
    )(meta, dest, h3)


def _expert_kernel(meta_ref, xs_ref, wg_ref, wu_ref, wd_ref, ys_ref, wg_b, wu_b, wd_b):
    c = pl.program_id(0)
    in_use = c < meta_ref[1, 0]

    @pl.when(jnp.logical_not(in_use))
    def _():
        ys_ref[...] = jnp.zeros(ys_ref.shape, F32)

    @pl.when(in_use & ((c == 0) | (meta_ref[0, c] != meta_ref[0, jnp.maximum(c - 1, 0)])))
    def _():
        wg_b[...] = wg_ref[0].astype(BF16)
        wu_b[...] = wu_ref[0].astype(BF16)
        wd_b[...] = wd_ref[0].astype(BF16)

    @pl.when(in_use)
    def _():
        x = jnp.concatenate([xs_ref[:, j, :] for j in range(ROW_TILES)], axis=1).astype(BF16)
        gate = jnp.dot(x, wg_b[...], preferred_element_type=F32)
        up = jnp.dot(x, wu_b[...], preferred_element_type=F32)
        act = (gate * jax.nn.sigmoid(gate) * up).astype(BF16)
        y = jnp.dot(act, wd_b[...], preferred_element_type=F32)
        for j in range(ROW_TILES):
            ys_ref[:, j, :] = y[:, j * LANES:(j + 1) * LANES]


def _expert_mlp(xs, meta, w_g, w_u, w_d):
    n_chunks = xs.shape[0] // MOE_ROWS
    rows = lambda c, meta: (c, 0, 0)
    expert = lambda c, meta: (meta[0, jnp.minimum(c, meta[1, 0] - 1)], 0, 0)
    grid_spec = pltpu.PrefetchScalarGridSpec(
        num_scalar_prefetch=1,
        grid=(n_chunks,),
        in_specs=[
            pl.BlockSpec((MOE_ROWS, ROW_TILES, LANES), rows),
            pl.BlockSpec((1, D_MODEL, D_EXPERT), expert),
            pl.BlockSpec((1, D_MODEL, D_EXPERT), expert),
            pl.BlockSpec((1, D_EXPERT, D_MODEL), expert),
        ],
        out_specs=pl.BlockSpec((MOE_ROWS, ROW_TILES, LANES), rows),
        scratch_shapes=[pltpu.VMEM((D_MODEL, D_EXPERT), BF16), pltpu.VMEM((D_MODEL, D_EXPERT), BF16),
                        pltpu.VMEM((D_EXPERT, D_MODEL), BF16)],
    )
    return pl.pallas_call(
        _expert_kernel,
        grid_spec=grid_spec,
        out_shape=jax.ShapeDtypeStruct(xs.shape, F32),
        compiler_params=_cparams("arbitrary"),
        name="expert_mlp",
    )(meta, xs, w_g, w_u, w_d)


def _combine_kernel(dest_ref, nxt_ref, x1_ref, route_ref, ln_ref, ys_ref, o_ref, buf_ref, sem_ref):
    i = pl.program_id(0)
    n_tiles = pl.num_programs(0)
    rt = x1_ref.shape[0]

    def issue(table_ref, slot):
        def body(t, carry):
            for k in range(TOP_K):
                pltpu.make_async_copy(ys_ref.at[table_ref[0, k, t]], buf_ref.at[slot, k, t], sem_ref.at[slot]).start()
            return carry
        lax.fori_loop(0, rt, body, 0, unroll=8)

    slot = i % 2

    @pl.when(i == 0)
    def _():
        issue(dest_ref, slot)

    @pl.when(i + 1 < n_tiles)
    def _():
        issue(nxt_ref, 1 - slot)

    for k in range(TOP_K):
        _row_copies_wait(ys_ref, buf_ref.at[slot, k], sem_ref.at[slot], rt)
    route = route_ref[...]
    moe = None
    for k in range(TOP_K):
        y = jnp.concatenate([buf_ref[slot, k, :, j, :] for j in range(ROW_TILES)], axis=1)
        term = y * route[:, ROUTE_W1 + k:ROUTE_W1 + k + 1]
        moe = term if moe is None else moe + term
    o_ref[...] = _rms(x1_ref[...] + moe, ln_ref[...])


def _combine(x1, ys, dest, route, ln_final, rt):
    n = x1.shape[0]
    tiles = n // rt
    row = lambda i: (i, 0)
    table = lambda f: pl.BlockSpec((1, TOP_K, rt), f, memory_space=pltpu.SMEM)
    return pl.pallas_call(
        _combine_kernel,
        grid=(tiles,),
        in_specs=[
            table(lambda i: (i, 0, 0)),
            table(lambda i: (jnp.minimum(i + 1, tiles - 1), 0, 0)),
            pl.BlockSpec((rt, D_MODEL), row),
            pl.BlockSpec((rt, LANES), row),
            pl.BlockSpec((1, D_MODEL), lambda i: (0, 0)),
            pl.BlockSpec(memory_space=pl.ANY),
        ],
        out_specs=pl.BlockSpec((rt, D_MODEL), row),
        out_shape=jax.ShapeDtypeStruct((n, D_MODEL), F32),
        scratch_shapes=[pltpu.VMEM((2, TOP_K, rt, ROW_TILES, LANES), F32), pltpu.SemaphoreType.DMA((2,))],
        compiler_params=_cparams("arbitrary"),
        name="moe_combine_norm",
    )(dest, dest, x1, route, ln_final.reshape(1, -1), ys)


def _ffn(conv2d, att2d, x2d, g_conv, g_att, w_out_b, ln2, w_route_b, w_g, w_u, w_d, ln_final, tm):
    x1, h3, route = _merge(conv2d, att2d, x2d, g_conv, g_att, w_out_b, ln2, w_route_b, tm)
    rt = min(tm, MOE_ROWS)
    dest, meta = _rank(route, rt)
    ys = _expert_mlp(_scatter_rows(h3, dest, meta, rt), meta, w_g, w_u, w_d)
    return _combine(x1, ys, dest, route, ln_final, rt)


def kernel(x_prompt, x_sample, cache_cmp_kv, cache_sel_kv, state_win_kv, state_conv, page_table, ln1, w_in, conv_dw_w, conv_dw_b, conv_ln_g, conv_ln_b, cmp_pos_emb, w_cmp_k1, w_cmp_k2, w_cmp_v1, w_cmp_v2, out_norm_conv, out_norm_att, w_out, ln2, w_router_group, w_router_expert, w_exp_gate, w_exp_up, w_exp_down, ln_final):
    depth = ln1.shape[0]
    assert depth == 1, "single-layer step"
    b, t, _ = x_prompt.shape
    db, ds, _ = x_sample.shape
    n_phys = cache_cmp_kv.shape[1]
    n_pages = page_table.shape[1]
    past = n_pages * PAGE_SIZE
    win_rows = state_win_kv.shape[2]
    assert ds < CMP_BLOCK and ds <= SUBLANES and past % SEL_BLOCK == 0 and past // SEL_BLOCK >= N_SEL
    assert win_rows == WINDOW and past >= WINDOW and t % KC == 0 and t >= WIN_KEYS

    w_in_b = jnp.pad(w_in[0], ((0, 0), (0, D_IN_PAD - D_IN))).astype(BF16)
    w_out_b = w_out[0].astype(BF16)
    w_route_b = jnp.pad(jnp.concatenate([w_router_group[0], w_router_expert[0]], axis=1),
                        ((0, 0), (0, LANES - N_GROUPS - N_EXPERTS))).astype(BF16)
    cmp_w = _cmp_weights(cmp_pos_emb[0], w_cmp_k1[0], w_cmp_k2[0], w_cmp_v1[0], w_cmp_v2[0])
    conv_w = (conv_dw_w[0], conv_dw_b[0], conv_ln_g[0], conv_ln_b[0])
    ffn_w = (out_norm_conv[0], out_norm_att[0], w_out_b, ln2[0], w_route_b, w_exp_gate[0], w_exp_up[0],
             w_exp_down[0], ln_final)

    xp2 = x_prompt.reshape(b * t, D_MODEL)
    c_p, s_p = _rope_tables(jnp.arange(t))
    a_p, q_p, ckv_p, skv_p, wkv_p, gate_p, ckv_pt, skv_pt, wkv_pt = _project(xp2, ln1[0], w_in_b, c_p, s_p, 512)
    a_p3 = a_p.reshape(b, t, CONV_CH)
    conv_p = _conv_module(a_p3, jnp.zeros((b, HIST_ROWS, CONV_CH), F32), *conv_w, 512)
    kc_p, vc_p = _compress_dense(ckv_p.reshape(b, t, COL_KV), cmp_w)
    ksel_p, vsel_t = _split_kv(skv_p.reshape(b, t, COL_KV), skv_pt)
    kwin_p, vwin_t = _split_kv(wkv_p.reshape(b, t, COL_KV), wkv_pt)
    gates_t = jnp.transpose(gate_p[:, :COL_GATE].reshape(b, t, N_KV, Q_PER_KV, 3), (0, 2, 4, 3, 1))
    att_p = _prompt_attention(q_p.reshape(b, t, COL_Q), kc_p, jnp.swapaxes(vc_p, 2, 3), _augment_keys(ksel_p),
                              vsel_t, kwin_p, vwin_t, gates_t.reshape(b, N_KV, 3 * Q_PER_KV, t))
    y_p = _ffn(conv_p.reshape(b * t, CONV_CH), att_p.reshape(b * t, COL_Q), xp2, *ffn_w, 256)

    n_s = db * ds
    xs2 = x_sample.reshape(n_s, D_MODEL)
    c_s, s_s = _rope_tables(jnp.tile(past + jnp.arange(ds), db))
    a_s, q_s, ckv_s, skv_s, wkv_s, gate_s = _project(xs2, ln1[0], w_in_b, c_s, s_s, n_s)[:6]
    a_s3 = a_s.reshape(db, ds, CONV_CH)
    hist_s = jnp.pad(state_conv[0], ((0, 0), (HIST_ROWS - (CONV_W - 1), 0), (0, 0)))
    conv_s = _conv_module(a_s3, hist_s, *conv_w, ds)
    rows_minor = lambda a, n, r: jnp.swapaxes(a.reshape(n, r, COL_KV), 1, 2)
    kc_s, vc_s = _compress_paged(rows_minor(cache_cmp_kv[0], n_phys, PAGE_SIZE), page_table, cmp_w)

    q5 = q_s.reshape(db, ds, N_KV, Q_PER_KV, HEAD_DIM)
    pad_tok = SUBLANES - ds
    pad_head = SUBLANES - Q_PER_KV
    q_rt = jnp.pad(jnp.transpose(q5, (0, 2, 3, 1, 4)), ((0, 0), (0, 0), (0, 0), (0, pad_tok), (0, 0)))
    q_rt = q_rt.reshape(db, N_KV, Q_PER_KV * SUBLANES, HEAD_DIM)
    q_tr = jnp.pad(jnp.transpose(q5, (0, 2, 1, 3, 4)), ((0, 0), (0, 0), (0, 0), (0, pad_head), (0, 0)))
    q_tr = q_tr.reshape(db, N_KV, ds * SUBLANES, HEAD_DIM)
    g5 = gate_s[:, :COL_GATE].reshape(db, ds, N_KV, Q_PER_KV, 3)
    g_tr = jnp.pad(jnp.transpose(g5, (0, 2, 4, 1, 3)), ((0, 0),) * 4 + ((0, pad_head),))
    g_tr = jnp.broadcast_to(g_tr.reshape(db, N_KV, 3, ds * SUBLANES)[..., None], (db, N_KV, 3, ds * SUBLANES, HEAD_DIM))
    ocmp, picks = _sample_cmp_select(q_rt, kc_s, vc_s, past, ds)
    sel_idx = picks.reshape(db, N_KV, SUBLANES, LANES)[:, :, :ds, :N_SEL].reshape(-1)
    ocmp_tr = jnp.transpose(ocmp.reshape(db, N_KV, Q_PER_KV, SUBLANES, HEAD_DIM)[:, :, :, :ds], (0, 1, 3, 2, 4))
    ocmp_tr = jnp.pad(ocmp_tr, ((0, 0), (0, 0), (0, 0), (0, pad_head), (0, 0)))
    ocmp_tr = ocmp_tr.reshape(db, N_KV, ds * SUBLANES, HEAD_DIM)
    new_t = lambda kv: jnp.pad(rows_minor(kv, db, ds), ((0, 0), (0, 0), (0, LANES - ds)))
    o_s = _sample_attention(sel_idx, page_table, rows_minor(cache_sel_kv[0], n_phys, PAGE_SIZE), q_tr,
                            new_t(skv_s), new_t(wkv_s), rows_minor(state_win_kv[0], db, win_rows),
                            ocmp_tr, g_tr, ds)
    att_s = o_s.reshape(db, N_KV, ds, SUBLANES, HEAD_DIM)[:, :, :, :Q_PER_KV]
    att_s = jnp.transpose(att_s, (0, 2, 1, 3, 4)).reshape(n_s, COL_Q)
    y_s = _ffn(conv_s.reshape(n_s, CONV_CH), att_s, xs2, *ffn_w, n_s)

    kv6 = lambda kv, bb, tt: kv.reshape(1, bb, tt, N_KV, 2, HEAD_DIM)
    kv6_t = lambda kv_t: jnp.swapaxes(kv_t, 1, 2).reshape(1, b, kv_t.shape[2], N_KV, 2, HEAD_DIM)
    new_win_s = jnp.concatenate([state_win_kv, kv6(wkv_s, db, ds)], axis=2)[:, :, ds:]
    new_conv_s = jnp.concatenate([state_conv[0], a_s3], axis=1)[None, :, ds:]
    return (y_p.reshape(b, t, D_MODEL), y_s.reshape(db, ds, D_MODEL),
            kv6_t(ckv_pt), kv6(ckv_s, db, ds), kv6_t(skv_pt), kv6(skv_s, db, ds),
            kv6_t(wkv_pt[:, :, t - min(WINDOW, t):]), new_win_s,
            a_p3[None, :, t - (CONV_W - 1):], new_conv_s)
```

```python
import functools

import jax
import jax.numpy as jnp
from jax import lax
from jax.experimental import pallas as pl
from jax.experimental.pallas import tpu as pltpu

D_MODEL = 1024
CONV_CH = 512
CONV_W = 31
N_HEADS = 8
HEAD_DIM = 64
N_KV = 2
Q_PER_KV = N_HEADS // N_KV
ROPE_DIM = HEAD_DIM // 4
ROPE_THETA = 500000.0
CMP_BLOCK = 64
SEL_BLOCK = CMP_BLOCK
N_SEL = 16
WINDOW = 512
CMP_HID = 2 * HEAD_DIM
COL_Q = N_HEADS * HEAD_DIM
COL_KV = 2 * N_KV * HEAD_DIM
COL_GATE = 3 * N_HEADS
D_IN = 2 * CONV_CH + COL_Q + 3 * COL_KV + COL_GATE
N_GROUPS = 4
EXPERTS_PER_GROUP = 8
N_EXPERTS = N_GROUPS * EXPERTS_PER_GROUP
TOP_K = 2
D_EXPERT = 512
PAGE_SIZE = 128
RMS_EPS = 1e-6
LN_EPS = 1e-5
NEG_INF = -1e30
ATT_SCALE = HEAD_DIM ** -0.5

LANES = 128
SUBLANES = 8
VMEM_LIMIT_BYTES = 56 * 1024 * 1024

D_IN_PAD = ((D_IN + LANES - 1) // LANES) * LANES
COL_GATE_OFF = 2 * CONV_CH + COL_Q + 3 * COL_KV
HIST_ROWS = 32
CONV_ROWS = 32
MOE_ROWS = 256
ROW_TILES = D_MODEL // LANES

BF16 = jnp.bfloat16
F32 = jnp.float32


def _cparams(*sem):
    return pltpu.CompilerParams(dimension_semantics=sem, vmem_limit_bytes=VMEM_LIMIT_BYTES)


def _rms(x, g):
    return x * lax.rsqrt(jnp.mean(x * x, axis=-1, keepdims=True) + RMS_EPS) * g


def _rope_tables(pos):
    half = ROPE_DIM // 2
    inv = ROPE_THETA ** (-jnp.arange(half, dtype=F32) / half)
    ang = pos.astype(F32)[:, None] * inv
    cos, sin = jnp.cos(ang), jnp.sin(ang)
    m = jnp.arange(LANES) % HEAD_DIM
    idx = m % half
    c = jnp.where(m < ROPE_DIM, cos[:, idx], 1.0)
    s = jnp.where(m < half, -sin[:, idx], jnp.where(m < ROPE_DIM, sin[:, idx], 0.0))
    return c.astype(F32), s.astype(F32)


def _rope(v, c, s, first_half):
    w = v.shape[1]
    half = ROPE_DIM // 2
    partner = jnp.where(first_half, pltpu.roll(v, w - half, axis=1), pltpu.roll(v, half, axis=1))
    return v * c + partner * s


def _proj_kernel(x_ref, ln_ref, w_ref, c_ref, s_ref, a_ref, q_ref, ckv_ref, skv_ref, wkv_ref, gate_ref,
                 ckvt_ref, skvt_ref, wkvt_ref, ksel_ref, kwin_ref, vsel_ref, vwin_ref, *, t_tiles):
    x = x_ref[...]
    xn = _rms(x, ln_ref[...])
    p = jnp.dot(xn.astype(BF16), w_ref[...], preferred_element_type=F32)
    a_ref[...] = p[:, :CONV_CH] * jax.nn.sigmoid(p[:, CONV_CH:2 * CONV_CH])

    c128, s128 = c_ref[...], s_ref[...]
    tm = x.shape[0]
    lane_q = lax.broadcasted_iota(jnp.int32, (tm, COL_Q), 1)
    cq = jnp.concatenate([c128] * (COL_Q // LANES), axis=1)
    sq = jnp.concatenate([s128] * (COL_Q // LANES), axis=1)
    o = 2 * CONV_CH
    q = _rope(p[:, o:o + COL_Q], cq, sq, (lane_q % HEAD_DIM) < ROPE_DIM // 2)
    q_ref[...] = q.astype(q_ref.dtype)
    o += COL_Q

    lane_kv = lax.broadcasted_iota(jnp.int32, (tm, COL_KV), 1)
    is_k = (lane_kv % (2 * HEAD_DIM)) < HEAD_DIM
    ckv = jnp.where(is_k, jnp.concatenate([c128] * (COL_KV // LANES), axis=1), 1.0)
    skv = jnp.where(is_k, jnp.concatenate([s128] * (COL_KV // LANES), axis=1), 0.0)
    first_kv = (lane_kv % HEAD_DIM) < ROPE_DIM // 2
    kvs = []
    for ref, ref_t in ((ckv_ref, ckvt_ref), (skv_ref, skvt_ref), (wkv_ref, wkvt_ref)):
        kv = _rope(p[:, o:o + COL_KV], ckv, skv, first_kv)
        kv_t = kv.T
        ref[...] = kv
        ref_t[0] = kv_t
        kvs.append((kv, kv_t))
        o += COL_KV
    gate_ref[...] = jax.nn.sigmoid(p[:, o:o + LANES])

    kvw = 2 * HEAD_DIM
    lane = lax.broadcasted_iota(jnp.int32, (tm, LANES), 1)
    pos = (pl.program_id(0) % t_tiles) * tm + lax.broadcasted_iota(jnp.int32, (tm, LANES), 0)
    blk = pos // SEL_BLOCK
    is_key = lane < HEAD_DIM
    ones_rows = (lax.broadcasted_iota(jnp.int32, (V_ROWS - HEAD_DIM, tm), 0) == 0).astype(F32)
    (skv_v, skv_t), (wkv_v, wkv_t) = kvs[1], kvs[2]
    for g in range(N_KV):
        low = jnp.where(is_key, skv_v[:, g * kvw:(g + 1) * kvw], (blk == lane - HEAD_DIM).astype(F32))
        high = jnp.where(is_key, (blk == lane + HEAD_DIM).astype(F32), 0.0)
        ksel_ref[0, g] = jnp.concatenate([low, high], axis=1).astype(BF16)
        kwin_ref[0, g] = wkv_v[:, g * kvw:g * kvw + HEAD_DIM].astype(BF16)
        v_rows = slice(g * kvw + HEAD_DIM, (g + 1) * kvw)
        vsel_ref[0, g] = jnp.concatenate([skv_t[v_rows], ones_rows], axis=0).astype(BF16)
        vwin_ref[0, g] = jnp.concatenate([wkv_t[v_rows], ones_rows], axis=0).astype(BF16)


def _project(x2d, ln, w_pad, c_tab, s_tab, tm):
    n = x2d.shape[0]
    t = c_tab.shape[0]
    t_tiles = t // tm
    assert t // SEL_BLOCK <= KAUG - HEAD_DIM - HEAD_DIM, "one-hot block ids fit the augmented key"
    row = lambda i: (i, 0)
    tab = lambda i: (i % t_tiles, 0)
    const = lambda i: (0, 0)
    kv_t = jax.ShapeDtypeStruct((n // t, COL_KV, t), F32)
    kv_t_spec = pl.BlockSpec((1, COL_KV, tm), lambda i: (i // t_tiles, 0, i % t_tiles))
    keys = lambda w: (jax.ShapeDtypeStruct((n // t, N_KV, t, w), BF16),
                      pl.BlockSpec((1, N_KV, tm, w), lambda i: (i // t_tiles, 0, i % t_tiles, 0)))
    v_t = (jax.ShapeDtypeStruct((n // t, N_KV, V_ROWS, t), BF16),
           pl.BlockSpec((1, N_KV, V_ROWS, tm), lambda i: (i // t_tiles, 0, 0, i % t_tiles)))
    out_shape = (
        jax.ShapeDtypeStruct((n, CONV_CH), F32),
        jax.ShapeDtypeStruct((n, COL_Q), BF16),
        jax.ShapeDtypeStruct((n, COL_KV), F32),
        jax.ShapeDtypeStruct((n, COL_KV), F32),
        jax.ShapeDtypeStruct((n, COL_KV), F32),
        jax.ShapeDtypeStruct((n, LANES), F32),
        kv_t, kv_t, kv_t,
        keys(KAUG)[0], keys(HEAD_DIM)[0], v_t[0], v_t[0],
    )
    return pl.pallas_call(
        functools.partial(_proj_kernel, t_tiles=t_tiles),
        grid=(n // tm,),
        in_specs=[
            pl.BlockSpec((tm, D_MODEL), row),
            pl.BlockSpec((1, D_MODEL), const),
            pl.BlockSpec((D_MODEL, D_IN_PAD), const),
            pl.BlockSpec((tm, LANES), tab),
            pl.BlockSpec((tm, LANES), tab),
        ],
        out_specs=(
            pl.BlockSpec((tm, CONV_CH), row),
            pl.BlockSpec((tm, COL_Q), row),
            pl.BlockSpec((tm, COL_KV), row),
            pl.BlockSpec((tm, COL_KV), row),
            pl.BlockSpec((tm, COL_KV), row),
            pl.BlockSpec((tm, LANES), row),
            kv_t_spec, kv_t_spec, kv_t_spec,
            keys(KAUG)[1], keys(HEAD_DIM)[1], v_t[1], v_t[1],
        ),
        out_shape=out_shape,
        compiler_params=_cparams("arbitrary"),
        name="in_proj",
    )(x2d, ln.reshape(1, D_MODEL), w_pad, c_tab, s_tab)


def _conv_kernel(a_ref, hist_ref, w_ref, b_ref, g_ref, beta_ref, o_ref, sh_ref):
    tt = a_ref.shape[1]
    ext_ref = sh_ref.at[0]

    @pl.when(pl.program_id(1) == 0)
    def _():
        ext_ref[0:HIST_ROWS, :] = hist_ref[0]

    ext_ref[HIST_ROWS:HIST_ROWS + tt, :] = a_ref[0]
    span = HIST_ROWS + tt - SUBLANES
    for s in range(1, SUBLANES):
        sh_ref[s, 0:span, :] = ext_ref[s:s + span, :]

    lead = HIST_ROWS - (CONV_W - 1)
    rc = min(tt, CONV_ROWS)

    def chunk(i, carry):
        r0 = pl.multiple_of(i * rc, rc)
        acc = jnp.broadcast_to(b_ref[...], (rc, CONV_CH))
        for k in range(CONV_W):
            a, s = divmod(lead + k, SUBLANES)
            acc = acc + w_ref[k:k + 1, :] * sh_ref[s, pl.ds(r0 + a * SUBLANES, rc), :]
        o_ref[0, pl.ds(r0, rc), :] = acc
        return carry

    lax.fori_loop(0, tt // rc, chunk, 0)
    acc = o_ref[0]
    mu = jnp.mean(acc, axis=-1, keepdims=True)
    var = jnp.mean(jnp.square(acc - mu), axis=-1, keepdims=True)
    y = (acc - mu) * lax.rsqrt(var + LN_EPS) * g_ref[...] + beta_ref[...]
    o_ref[0] = y * jax.nn.sigmoid(y)
    carry = ext_ref[tt:tt + HIST_ROWS, :]
    ext_ref[0:HIST_ROWS, :] = carry


def _conv_module(a3d, hist, dw_w, dw_b, ln_g, ln_b, tt):
    b, t, _ = a3d.shape
    w_pad = jnp.pad(dw_w, ((0, HIST_ROWS - CONV_W), (0, 0)))
    vec = lambda i, j: (0, 0)
    return pl.pallas_call(
        _conv_kernel,
        grid=(b, t // tt),
        in_specs=[
            pl.BlockSpec((1, tt, CONV_CH), lambda i, j: (i, j, 0)),
            pl.BlockSpec((1, HIST_ROWS, CONV_CH), lambda i, j: (i, 0, 0)),
            pl.BlockSpec((HIST_ROWS, CONV_CH), vec),
            pl.BlockSpec((1, CONV_CH), vec),
            pl.BlockSpec((1, CONV_CH), vec),
            pl.BlockSpec((1, CONV_CH), vec),
        ],
        out_specs=pl.BlockSpec((1, tt, CONV_CH), lambda i, j: (i, j, 0)),
        out_shape=jax.ShapeDtypeStruct((b, t, CONV_CH), F32),
        scratch_shapes=[pltpu.VMEM((SUBLANES, HIST_ROWS + tt, CONV_CH), F32)],
        compiler_params=_cparams("arbitrary", "arbitrary"),
        name="conv_module",
    )(a3d, hist, w_pad, dw_b.reshape(1, -1), ln_g.reshape(1, -1), ln_b.reshape(1, -1))


L_GROUP = 2 * LANES // HEAD_DIM


def _compress_rows(x_refs, pe_ref, wk1_ref, wk2_ref, wv1_ref, wv2_ref, nb, pitch=CMP_BLOCK):
    hk = jnp.zeros((N_KV * nb, CMP_HID), F32)
    hv = jnp.zeros((N_KV * nb, CMP_HID), F32)
    for j in range(CMP_BLOCK // L_GROUP):
        parts_k, parts_v = [], []
        for x_ref in x_refs:
            xs = [x_ref[pl.ds(j * L_GROUP + i, nb, stride=pitch), :] + pe_ref[j * L_GROUP + i:j * L_GROUP + i + 1, :]
                  for i in range(L_GROUP)]
            parts_k.append(jnp.concatenate([x[:, :HEAD_DIM] for x in xs], axis=1))
            parts_v.append(jnp.concatenate([x[:, HEAD_DIM:] for x in xs], axis=1))
        xk = jnp.concatenate(parts_k, axis=0).astype(BF16)
        xv = jnp.concatenate(parts_v, axis=0).astype(BF16)
        rows = slice(j * L_GROUP * HEAD_DIM, (j + 1) * L_GROUP * HEAD_DIM)
        hk = hk + jnp.dot(xk, wk1_ref[rows, :], preferred_element_type=F32)
        hv = hv + jnp.dot(xv, wv1_ref[rows, :], preferred_element_type=F32)
    kc = jnp.dot((hk * jax.nn.sigmoid(hk)).astype(BF16), wk2_ref[...], preferred_element_type=F32)
    vc = jnp.dot((hv * jax.nn.sigmoid(hv)).astype(BF16), wv2_ref[...], preferred_element_type=F32)
    return kc, vc


def _compress_dense_kernel(*refs):
    x_refs, (pe_ref, wk1_ref, wk2_ref, wv1_ref, wv2_ref, kc_ref, vc_ref) = refs[:N_KV], refs[N_KV:]
    nb = x_refs[0].shape[1] // CMP_BLOCK
    kc, vc = _compress_rows([x.at[0] for x in x_refs], pe_ref, wk1_ref, wk2_ref, wv1_ref, wv2_ref, nb)
    for g in range(N_KV):
        kc_ref[0, g] = kc[g * nb:(g + 1) * nb]
        vc_ref[0, g] = vc[g * nb:(g + 1) * nb]


def _cmp_weight_specs():
    const = lambda *_: (0, 0)
    return [
        pl.BlockSpec((CMP_BLOCK, 2 * HEAD_DIM), const),
        pl.BlockSpec((CMP_BLOCK * HEAD_DIM, CMP_HID), const),
        pl.BlockSpec((CMP_HID, HEAD_DIM), const),
        pl.BlockSpec((CMP_BLOCK * HEAD_DIM, CMP_HID), const),
        pl.BlockSpec((CMP_HID, HEAD_DIM), const),
    ]


def _cmp_weights(pos_emb, w_k1, w_k2, w_v1, w_v2):
    pe = pos_emb.reshape(CMP_BLOCK, 2 * HEAD_DIM)
    return (pe, w_k1.reshape(-1, CMP_HID).astype(BF16), w_k2.astype(BF16),
            w_v1.reshape(-1, CMP_HID).astype(BF16), w_v2.astype(BF16))


def _compress_dense(kv3d, cmp_w):
    b, t, _ = kv3d.shape
    nb = t // CMP_BLOCK
    out = jax.ShapeDtypeStruct((b, N_KV, nb, HEAD_DIM), F32)
    ospec = pl.BlockSpec((1, N_KV, nb, HEAD_DIM), lambda i: (i, 0, 0, 0))
    return pl.pallas_call(
        _compress_dense_kernel,
        grid=(b,),
        in_specs=[pl.BlockSpec((1, t, 2 * HEAD_DIM), functools.partial(lambda g, i: (i, 0, g), g))
                  for g in range(N_KV)] + _cmp_weight_specs(),
        out_specs=(ospec, ospec),
        out_shape=(out, out),
        compiler_params=_cparams("arbitrary"),
        name="compress_prompt",
    )(*([kv3d] * N_KV), *cmp_w)


PAGES_PER_STEP = 64
BLOCK_PITCH = CMP_BLOCK + SUBLANES


def _compress_paged_kernel(pt_ref, cache_ref, pe_ref, wk1_ref, wk2_ref, wv1_ref, wv2_ref, kc_ref, vc_ref,
                           raw_ref, rows_ref, sem_ref):
    step = pl.program_id(0)
    n_steps = pl.num_programs(0)
    nb = PAGES_PER_STEP * PAGE_SIZE // CMP_BLOCK
    kvw = 2 * HEAD_DIM

    def page_copy(s, slot, p):
        return pltpu.make_async_copy(cache_ref.at[pt_ref[s * PAGES_PER_STEP + p]], raw_ref.at[slot, p], sem_ref.at[slot])

    def issue(s, slot):
        for p in range(PAGES_PER_STEP):
            page_copy(s, slot, p).start()

    slot = step % 2

    @pl.when(step == 0)
    def _():
        issue(step, slot)

    @pl.when(step + 1 < n_steps)
    def _():
        issue(step + 1, 1 - slot)

    for p in range(PAGES_PER_STEP):
        page_copy(step, slot, p).wait()

    for p in range(PAGES_PER_STEP):
        page = raw_ref[slot, p].T
        for g in range(N_KV):
            for n in range(PAGE_SIZE // CMP_BLOCK):
                row0 = (p * (PAGE_SIZE // CMP_BLOCK) + n) * BLOCK_PITCH
                rows_ref[g, row0:row0 + CMP_BLOCK, :] = page[n * CMP_BLOCK:(n + 1) * CMP_BLOCK, g * kvw:(g + 1) * kvw]

    kc, vc = _compress_rows([rows_ref.at[g] for g in range(N_KV)], pe_ref, wk1_ref, wk2_ref, wv1_ref, wv2_ref, nb,
                            pitch=BLOCK_PITCH)
    for g in range(N_KV):
        kc_ref[0, g] = kc[g * nb:(g + 1) * nb]
        vc_ref[0, g] = vc[g * nb:(g + 1) * nb]


def _compress_paged(cache_t, page_table, cmp_w):
    db, n_pages = page_table.shape
    steps_per_row = n_pages // PAGES_PER_STEP
    nb = PAGES_PER_STEP * PAGE_SIZE // CMP_BLOCK
    out = jax.ShapeDtypeStruct((db, N_KV, steps_per_row * nb, HEAD_DIM), F32)
    ospec = pl.BlockSpec((1, N_KV, nb, HEAD_DIM), lambda i, pt: (i // steps_per_row, 0, i % steps_per_row, 0))
    grid_spec = pltpu.PrefetchScalarGridSpec(
        num_scalar_prefetch=1,
        grid=(db * steps_per_row,),
        in_specs=[pl.BlockSpec(memory_space=pl.ANY)] + _cmp_weight_specs(),
        out_specs=(ospec, ospec),
        scratch_shapes=[pltpu.VMEM((2, PAGES_PER_STEP, COL_KV, PAGE_SIZE), F32),
                        pltpu.VMEM((N_KV, nb * BLOCK_PITCH, 2 * HEAD_DIM), F32),
                        pltpu.SemaphoreType.DMA((2,))],
    )
    return pl.pallas_call(
        _compress_paged_kernel,
        grid_spec=grid_spec,
        out_shape=(out, out),
        compiler_params=_cparams("arbitrary"),
        name="compress_paged",
    )(page_table.reshape(-1), cache_t, *cmp_w)


TQ = 2 * LANES
KC = 512
LOG2_E = 1.4426950408889634
WIN_KEYS = WINDOW + TQ
V_ROWS = HEAD_DIM + 16
KAUG = 2 * LANES


def _top_blocks(imp, cand, n_blocks):
    blk = lax.broadcasted_iota(jnp.int32, imp.shape, 0)
    score = jnp.where(cand, imp, -1.0)
    sel = jnp.zeros(imp.shape, F32)
    for _ in range(N_SEL):
        mx = jnp.max(score, axis=0, keepdims=True)
        idx = jnp.min(jnp.where(score == mx, blk, n_blocks), axis=0, keepdims=True)
        pick = blk == idx
        sel = jnp.where(pick, 1.0, sel)
        score = jnp.where(pick, -2.0, score)
    return sel


def _prompt_attn_kernel(q_ref, kc_ref, vct_ref, ksel_ref, vselt_ref, kwin_ref, vwint_ref, gate_ref, o_ref):
    qt = pl.program_id(2)
    t0 = qt * TQ
    nb = kc_ref.shape[2]
    width = Q_PER_KV * TQ

    q = q_ref[0].astype(F32) * (ATT_SCALE * LOG2_E)
    q_t = q.T
    q4 = jnp.concatenate([q_t[r * HEAD_DIM:(r + 1) * HEAD_DIM] for r in range(Q_PER_KV)], axis=1)
    q4b = q4.astype(BF16)
    tok = t0 + lax.broadcasted_iota(jnp.int32, (1, width), 1) % TQ

    sc = jnp.dot(kc_ref[0, 0].astype(BF16), q4b, preferred_element_type=F32)
    blk = lax.broadcasted_iota(jnp.int32, (nb, width), 0)
    valid_c = (blk + 1) * CMP_BLOCK - 1 <= tok
    sc = jnp.where(valid_c, sc, NEG_INF)
    e = jnp.where(valid_c, jnp.exp2(sc - jnp.max(sc, axis=0, keepdims=True)), 0.0)
    den = jnp.sum(e, axis=0, keepdims=True)
    p = e / jnp.where(den > 0.0, den, 1.0)
    o_cmp = jnp.dot(vct_ref[0, 0].astype(BF16), p.astype(BF16), preferred_element_type=F32)
    imp = p[:, 0:TQ]
    for r in range(1, Q_PER_KV):
        imp = imp + p[:, r * TQ:(r + 1) * TQ]

    tok1 = t0 + lax.broadcasted_iota(jnp.int32, (1, TQ), 1)
    own = tok1 // SEL_BLOCK
    blk1 = lax.broadcasted_iota(jnp.int32, (nb, TQ), 0)
    cand = blk1 < own
    sel = _top_blocks(imp, cand, nb)
    bias = jnp.where(cand, jnp.where(sel > 0.0, 0.0, NEG_INF), jnp.where(blk1 == own, 0.0, NEG_INF))
    bias4 = jnp.concatenate([bias] * Q_PER_KV, axis=1).astype(BF16)
    q_aug = jnp.concatenate([q4b, bias4, jnp.zeros((KAUG - HEAD_DIM - nb, width), BF16)], axis=0)

    def scores(c):
        return jnp.dot(ksel_ref[0, 0, pl.ds(pl.multiple_of(c * KC, KC), KC), :], q_aug, preferred_element_type=F32)

    def softmax_pv(c, s, m, acc):
        m_new = jnp.maximum(m, jnp.max(s, axis=0, keepdims=True))
        alpha = jnp.exp2(m - m_new)
        pr = jnp.exp2(s - m_new).astype(BF16)
        v_blk = vselt_ref[0, 0, :, pl.ds(pl.multiple_of(c * KC, KC), KC)]
        return m_new, alpha * acc + jnp.dot(v_blk, pr, preferred_element_type=F32)

    def sel_step(c, carry):
        s, m, acc = carry
        s_next = scores(c + 1)
        return (s_next,) + softmax_pv(c, s, m, acc)

    last = t0 // KC
    init = (scores(0), jnp.full((1, width), NEG_INF, F32), jnp.zeros((V_ROWS, width), F32))
    s_last, m_sel, acc_sel = lax.fori_loop(0, last, sel_step, init)
    key = last * KC + lax.broadcasted_iota(jnp.int32, (KC, width), 0)
    _, acc_sel = softmax_pv(last, jnp.where(key <= tok, s_last, NEG_INF), m_sel, acc_sel)
    o_sel = acc_sel[0:HEAD_DIM] / acc_sel[HEAD_DIM:HEAD_DIM + 1]

    w0 = pl.multiple_of(jnp.maximum(t0 - WINDOW, 0), TQ)
    sw = jnp.dot(kwin_ref[0, 0, pl.ds(w0, WIN_KEYS), :], q4b, preferred_element_type=F32)
    dist = tok - (w0 + lax.broadcasted_iota(jnp.int32, (WIN_KEYS, width), 0))
    sw = jnp.where((dist >= 0) & (dist < WINDOW), sw, NEG_INF)
    pw = jnp.exp2(sw - jnp.max(sw, axis=0, keepdims=True)).astype(BF16)
    acc_win = jnp.dot(vwint_ref[0, 0, :, pl.ds(w0, WIN_KEYS)], pw, preferred_element_type=F32)
    o_win = acc_win[0:HEAD_DIM] / acc_win[HEAD_DIM:HEAD_DIM + 1]

    outs = []
    for r in range(Q_PER_KV):
        sl = slice(r * TQ, (r + 1) * TQ)
        g = [gate_ref[0, 0, j * Q_PER_KV + r:j * Q_PER_KV + r + 1, :] for j in range(3)]
        outs.append(g[0] * o_cmp[:, sl] + g[1] * o_sel[:, sl] + g[2] * o_win[:, sl])
    o_ref[0] = jnp.concatenate(outs, axis=0).T


def _prompt_attention(q3, k_c, v_ct, ksel_aug, vsel_t, kwin, vwin_t, gates_t):
    b, t, _ = q3.shape
    nb = k_c.shape[2]
    width = Q_PER_KV * HEAD_DIM
    per_bg = lambda i, g, j: (i, g, 0, 0)
    return pl.pallas_call(
        _prompt_attn_kernel,
        grid=(b, N_KV, t // TQ),
        in_specs=[
            pl.BlockSpec((1, TQ, width), lambda i, g, j: (i, j, g)),
            pl.BlockSpec((1, 1, nb, HEAD_DIM), per_bg),
            pl.BlockSpec((1, 1, HEAD_DIM, nb), per_bg),
            pl.BlockSpec((1, 1, t, KAUG), per_bg),
            pl.BlockSpec((1, 1, V_ROWS, t), per_bg),
            pl.BlockSpec((1, 1, t, HEAD_DIM), per_bg),
            pl.BlockSpec((1, 1, V_ROWS, t), per_bg),
            pl.BlockSpec((1, 1, 3 * Q_PER_KV, TQ), lambda i, g, j: (i, g, 0, j)),
        ],
        out_specs=pl.BlockSpec((1, TQ, width), lambda i, g, j: (i, j, g)),
        out_shape=jax.ShapeDtypeStruct((b, t, COL_Q), F32),
        compiler_params=_cparams("arbitrary", "arbitrary", "arbitrary"),
        name="prompt_attention",
    )(q3, k_c, v_ct, ksel_aug, vsel_t, kwin, vwin_t, gates_t)


def _sample_cmp_kernel(q_ref, kc_ref, vc_ref, ocmp_ref, idx_ref, imp_ref, *, past_len, dec_seq):
    b = pl.program_id(0)
    nb = kc_ref.shape[2]
    rows = Q_PER_KV * SUBLANES
    t_row = lax.broadcasted_iota(jnp.int32, (rows, nb), 0) % SUBLANES
    blk = lax.broadcasted_iota(jnp.int32, (rows, nb), 1)
    valid = (blk + 1) * CMP_BLOCK - 1 <= past_len + t_row
    for g in range(N_KV):
        s = lax.dot_general(q_ref[0, g], kc_ref[0, g].astype(BF16), (((1,), (1,)), ((), ())),
                            preferred_element_type=F32) * ATT_SCALE
        s = jnp.where(valid, s, NEG_INF)
        e = jnp.where(valid, jnp.exp(s - jnp.max(s, axis=1, keepdims=True)), 0.0)
        den = jnp.sum(e, axis=1, keepdims=True)
        p = e / jnp.where(den > 0.0, den, 1.0)
        ocmp_ref[0, g] = jnp.dot(p.astype(BF16), vc_ref[0, g].astype(BF16), preferred_element_type=F32)
        imp = p[0:SUBLANES]
        for r in range(1, Q_PER_KV):
            imp = imp + p[r * SUBLANES:(r + 1) * SUBLANES]
        row0 = pl.multiple_of((b * N_KV + g) * SUBLANES, SUBLANES)
        imp_ref[pl.ds(row0, SUBLANES), :] = imp

    @pl.when(b == pl.num_programs(0) - 1)
    def _():
        n_rows = imp_ref.shape[0]
        lane = lax.broadcasted_iota(jnp.int32, (n_rows, nb), 1)
        own = (past_len + lax.broadcasted_iota(jnp.int32, (n_rows, nb), 0) % SUBLANES) // SEL_BLOCK
        score = jnp.where(lane < own, imp_ref[...], -1.0)
        col = lax.broadcasted_iota(jnp.int32, (n_rows, LANES), 1)
        picks = jnp.zeros((n_rows, LANES), jnp.int32)
        for i in range(N_SEL):
            mx = jnp.max(score, axis=1, keepdims=True)
            idx = jnp.min(jnp.where(score == mx, lane, nb), axis=1, keepdims=True)
            score = jnp.where(lane == idx, -2.0, score)
            picks = jnp.where(col == i, idx, picks)
        idx_ref[...] = picks


def _sample_cmp_select(q_rt, k_c, v_c, past_len, dec_seq):
    db, _, rows, _ = q_rt.shape
    nb = k_c.shape[2]
    spec4 = lambda r, c: pl.BlockSpec((1, N_KV, r, c), lambda i: (i, 0, 0, 0))
    n_rows = db * N_KV * SUBLANES
    return pl.pallas_call(
        functools.partial(_sample_cmp_kernel, past_len=past_len, dec_seq=dec_seq),
        grid=(db,),
        in_specs=[spec4(rows, HEAD_DIM), spec4(nb, HEAD_DIM), spec4(nb, HEAD_DIM)],
        out_specs=(spec4(rows, HEAD_DIM), pl.BlockSpec((n_rows, LANES), lambda i: (0, 0))),
        out_shape=(jax.ShapeDtypeStruct((db, N_KV, rows, HEAD_DIM), F32),
                   jax.ShapeDtypeStruct((n_rows, LANES), jnp.int32)),
        scratch_shapes=[pltpu.VMEM((n_rows, nb), F32)],
        compiler_params=_cparams("arbitrary"),
        name="sample_cmp_select",
    )(q_rt, k_c, v_c)


def _sample_attn_kernel(idx_ref, pt_ref, cache_ref, q_ref, snew_ref, wnew_ref, wstate_ref, ocmp_ref, gate_ref,
                        o_ref, kv_ref, sem_ref, *, dec_seq, n_pages):
    b = pl.program_id(0)
    n_b = pl.num_programs(0)
    kvw = 2 * HEAD_DIM
    blocks_per_page = PAGE_SIZE // SEL_BLOCK

    def block_id(bb, g, t, i):
        return idx_ref[((bb * N_KV + g) * dec_seq + t) * N_SEL + i]

    def slab_copy(bb, slot, g, t, i):
        page = pt_ref[bb * n_pages + block_id(bb, g, t, i) // blocks_per_page]
        return pltpu.make_async_copy(cache_ref.at[page, pl.ds(g * kvw, kvw), :], kv_ref.at[slot, g, t, i],
                                     sem_ref.at[slot])

    def for_all_slabs(fn):
        for g in range(N_KV):
            for t in range(dec_seq):
                for i in range(N_SEL):
                    fn(g, t, i)

    slot = b % 2

    @pl.when(b == 0)
    def _():
        for_all_slabs(lambda g, t, i: slab_copy(b, slot, g, t, i).start())

    @pl.when(b + 1 < n_b)
    def _():
        for_all_slabs(lambda g, t, i: slab_copy(b + 1, 1 - slot, g, t, i).start())

    rows = dec_seq * SUBLANES
    tok = lax.broadcasted_iota(jnp.int32, (rows, 1), 0) // SUBLANES
    n_state = wstate_ref.shape[2]

    def attend(q, k_t, v_t, valid):
        s = jnp.dot(q, k_t.astype(BF16), preferred_element_type=F32) * ATT_SCALE
        s = jnp.where(valid, s, NEG_INF)
        p = jnp.exp(s - jnp.max(s, axis=1, keepdims=True))
        den = jnp.sum(p, axis=1, keepdims=True)
        return lax.dot_general(p.astype(BF16), v_t.astype(BF16), (((1,), (1,)), ((), ())),
                               preferred_element_type=F32) / den

    o_win = []
    for g in range(N_KV):
        k_rows, v_rows = pl.ds(g * kvw, HEAD_DIM), pl.ds(g * kvw + HEAD_DIM, HEAD_DIM)
        k_t = jnp.concatenate([wstate_ref[0, k_rows, :], wnew_ref[0, k_rows, :]], axis=1)
        v_t = jnp.concatenate([wstate_ref[0, v_rows, :], wnew_ref[0, v_rows, :]], axis=1)
        lane = lax.broadcasted_iota(jnp.int32, (rows, n_state + LANES), 1)
        new_i = lane - n_state
        valid = ((lane < n_state) & (lane > tok)) | ((new_i >= 0) & (new_i <= tok) & (new_i < dec_seq))
        o_win.append(attend(q_ref[0, g], k_t, v_t, valid))

    for_all_slabs(lambda g, t, i: slab_copy(b, slot, g, t, i).wait())

    lane1 = lax.broadcasted_iota(jnp.int32, (1, PAGE_SIZE), 1)
    for g in range(N_KV):
        o_sel = []
        for t in range(dec_seq):
            k_parts = [kv_ref[slot, g, t, i, 0:HEAD_DIM, :] for i in range(N_SEL)]
            v_parts = [kv_ref[slot, g, t, i, HEAD_DIM:kvw, :] for i in range(N_SEL)]
            k_parts.append(snew_ref[0, g * kvw:g * kvw + HEAD_DIM, :])
            v_parts.append(snew_ref[0, g * kvw + HEAD_DIM:(g + 1) * kvw, :])
            halves = [lane1 // SEL_BLOCK == block_id(b, g, t, i) % blocks_per_page for i in range(N_SEL)]
            halves.append((lane1 <= t) & (lane1 < dec_seq))
            q = q_ref[0, g, t * SUBLANES:(t + 1) * SUBLANES, :]
            o_sel.append(attend(q, jnp.concatenate(k_parts, axis=1), jnp.concatenate(v_parts, axis=1),
                                jnp.concatenate(halves, axis=1)))
        o_sel = jnp.concatenate(o_sel, axis=0)
        o_ref[0, g] = (gate_ref[0, g, 0] * ocmp_ref[0, g] + gate_ref[0, g, 1] * o_sel
                       + gate_ref[0, g, 2] * o_win[g])


def _sample_attention(sel_idx, page_table, cache_t, q_tr, snew_t, wnew_t, wstate_t, ocmp_tr, gates_tr, dec_seq):
    db, n_pages = page_table.shape
    rows = dec_seq * SUBLANES
    n_state = wstate_t.shape[2]
    per_b4 = lambda i, *_: (i, 0, 0, 0)
    per_b3 = lambda i, *_: (i, 0, 0)
    grid_spec = pltpu.PrefetchScalarGridSpec(
        num_scalar_prefetch=2,
        grid=(db,),
        in_specs=[
            pl.BlockSpec(memory_space=pl.ANY),
            pl.BlockSpec((1, N_KV, rows, HEAD_DIM), per_b4),
            pl.BlockSpec((1, COL_KV, LANES), per_b3),
            pl.BlockSpec((1, COL_KV, LANES), per_b3),
            pl.BlockSpec((1, COL_KV, n_state), per_b3),
            pl.BlockSpec((1, N_KV, rows, HEAD_DIM), per_b4),
            pl.BlockSpec((1, N_KV, 3, rows, HEAD_DIM), lambda i, *_: (i, 0, 0, 0, 0)),
        ],
        out_specs=pl.BlockSpec((1, N_KV, rows, HEAD_DIM), per_b4),
        scratch_shapes=[pltpu.VMEM((2, N_KV, dec_seq, N_SEL, 2 * HEAD_DIM, PAGE_SIZE), F32),
                        pltpu.SemaphoreType.DMA((2,))],
    )
    return pl.pallas_call(
        functools.partial(_sample_attn_kernel, dec_seq=dec_seq, n_pages=n_pages),
        grid_spec=grid_spec,
        out_shape=jax.ShapeDtypeStruct((db, N_KV, rows, HEAD_DIM), F32),
        compiler_params=_cparams("arbitrary"),
        name="sample_attention",
    )(sel_idx, page_table.reshape(-1), cache_t, q_tr, snew_t, wnew_t, wstate_t, ocmp_tr, gates_tr)


ROUTE_E1, ROUTE_E2, ROUTE_W1, ROUTE_W2 = 0, 1, 2, 3


def _merge_kernel(conv_ref, att_ref, x_ref, gc_ref, ga_ref, wo_ref, ln2_ref, wr_ref, x1_ref, h3_ref, route_ref):
    mix = jnp.concatenate([_rms(conv_ref[...], gc_ref[...]), _rms(att_ref[...], ga_ref[...])], axis=1)
    x1 = x_ref[...] + jnp.dot(mix.astype(BF16), wo_ref[...], preferred_element_type=F32)
    x1_ref[...] = x1
    h = _rms(x1, ln2_ref[...])
    for j in range(ROW_TILES):
        h3_ref[:, j, :] = h[:, j * LANES:(j + 1) * LANES]

    logits = jnp.dot(h.astype(BF16), wr_ref[...], preferred_element_type=F32)
    lane = lax.broadcasted_iota(jnp.int32, logits.shape, 1)
    is_g = lane < N_GROUPS
    lg = jnp.where(is_g, logits, NEG_INF)
    mg = jnp.max(lg, axis=1, keepdims=True)
    sg = jnp.sum(jnp.where(is_g, jnp.exp(lg - mg), 0.0), axis=1, keepdims=True)
    grp = jnp.min(jnp.where(lg == mg, lane, LANES), axis=1, keepdims=True)
    p_top = 1.0 / sg
    in_grp = ((lane + (EXPERTS_PER_GROUP - N_GROUPS)) // EXPERTS_PER_GROUP) == grp + 1
    le = jnp.where(in_grp, logits, NEG_INF)
    ee = jnp.where(in_grp, jnp.exp(le - jnp.max(le, axis=1, keepdims=True)), 0.0)
    pe = jnp.where(in_grp, ee / jnp.sum(ee, axis=1, keepdims=True), -1.0)
    p1 = jnp.max(pe, axis=1, keepdims=True)
    i1 = jnp.min(jnp.where(pe == p1, lane, LANES), axis=1, keepdims=True)
    pe2 = jnp.where(lane == i1, -1.0, pe)
    p2 = jnp.max(pe2, axis=1, keepdims=True)
    i2 = jnp.min(jnp.where(pe2 == p2, lane, LANES), axis=1, keepdims=True)
    den = p1 + p2
    rec = jnp.where(lane == ROUTE_E1, (i1 - N_GROUPS).astype(F32), 0.0)
    rec = jnp.where(lane == ROUTE_E2, (i2 - N_GROUPS).astype(F32), rec)
    rec = jnp.where(lane == ROUTE_W1, p1 / den * p_top, rec)
    rec = jnp.where(lane == ROUTE_W2, p2 / den * p_top, rec)
    route_ref[...] = rec


def _merge(conv2d, att2d, x2d, g_conv, g_att, w_out_b, ln2, w_route_b, tm):
    n = x2d.shape[0]
    row = lambda i: (i, 0)
    const = lambda i: (0, 0)
    return pl.pallas_call(
        _merge_kernel,
        grid=(n // tm,),
        in_specs=[
            pl.BlockSpec((tm, CONV_CH), row),
            pl.BlockSpec((tm, COL_Q), row),
            pl.BlockSpec((tm, D_MODEL), row),
            pl.BlockSpec((1, CONV_CH), const),
            pl.BlockSpec((1, COL_Q), const),
            pl.BlockSpec((CONV_CH + COL_Q, D_MODEL), const),
            pl.BlockSpec((1, D_MODEL), const),
            pl.BlockSpec((D_MODEL, LANES), const),
        ],
        out_specs=(
            pl.BlockSpec((tm, D_MODEL), row),
            pl.BlockSpec((tm, ROW_TILES, LANES), lambda i: (i, 0, 0)),
            pl.BlockSpec((tm, LANES), row),
        ),
        out_shape=(
            jax.ShapeDtypeStruct((n, D_MODEL), F32),
            jax.ShapeDtypeStruct((n, ROW_TILES, LANES), F32),
            jax.ShapeDtypeStruct((n, LANES), F32),
        ),
        compiler_params=_cparams("arbitrary"),
        name="merge_route",
    )(conv2d, att2d, x2d, g_conv.reshape(1, -1), g_att.reshape(1, -1), w_out_b, ln2.reshape(1, -1), w_route_b)


META_W = 2 * LANES


def _rank_kernel(route_ref, dest_ref, meta_ref, e_ref, rank_ref, count_ref):
    i = pl.program_id(0)
    rt = route_ref.shape[0]

    @pl.when(i == 0)
    def _():
        count_ref[...] = jnp.zeros(count_ref.shape, F32)

    route_t = route_ref[...].T
    expert_id = lax.broadcasted_iota(jnp.int32, (N_EXPERTS, rt), 0).astype(F32)
    before = (lax.broadcasted_iota(jnp.int32, (rt, rt), 0) < lax.broadcasted_iota(jnp.int32, (rt, rt), 1))
    before = before.astype(BF16)
    ones = jnp.ones((rt, LANES), BF16)
    e_rows, rank_rows = [], []
    for k in range(TOP_K):
        e_k = route_t[ROUTE_E1 + k:ROUTE_E1 + k + 1, :]
        onehot = (expert_id == e_k).astype(F32)
        earlier = jnp.dot(onehot.astype(BF16), before, preferred_element_type=F32)
        seen = count_ref[...]
        seen_w = jnp.concatenate([seen] * (rt // LANES), axis=1)
        rank_rows.append(jnp.sum(onehot * (earlier + seen_w), axis=0, keepdims=True))
        e_rows.append(e_k)
        count_ref[...] = seen + jnp.dot(onehot.astype(BF16), ones, preferred_element_type=F32)
    e_ref[i] = jnp.concatenate(e_rows, axis=0)
    rank_ref[i] = jnp.concatenate(rank_rows, axis=0)

    @pl.when(i == pl.num_programs(0) - 1)
    def _():
        counts = jnp.concatenate([count_ref[...]] * (META_W // LANES), axis=1)
        padded = jnp.floor((counts + (MOE_ROWS - 1)) * (1.0 / MOE_ROWS)) * MOE_ROWS
        lane = lax.broadcasted_iota(jnp.int32, (1, META_W), 1)
        chunk_start = lane.astype(F32) * MOE_ROWS
        chunk_e = jnp.zeros((1, META_W), F32)
        ends = jnp.zeros((1, META_W), F32)
        end = jnp.zeros((1, META_W), F32)
        starts = []
        for ex in range(N_EXPERTS):
            starts.append(end[:, 0:rt])
            end = end + padded[ex:ex + 1, :]
            chunk_e = chunk_e + (end <= chunk_start).astype(F32)
            ends = jnp.where(lane == ex, end, ends)
        n_used = end * (1.0 / MOE_ROWS)
        chunk_e = jnp.minimum(chunk_e, float(N_EXPERTS - 1))
        row = lax.broadcasted_iota(jnp.int32, (SUBLANES, META_W), 0)
        meta = jnp.where(row == 0, chunk_e, jnp.where(row == 1, n_used, jnp.where(row == 2, ends, 0.0)))
        meta_ref[...] = meta.astype(jnp.int32)

        def place(ti, carry):
            e_t = e_ref[ti]
            dest = rank_ref[ti]
            for ex in range(N_EXPERTS):
                dest = dest + jnp.where(e_t == float(ex), starts[ex], 0.0)
            dest_ref[ti] = dest.astype(jnp.int32)
            return carry

        lax.fori_loop(0, pl.num_programs(0), place, 0)


def _rank(route, rt):
    n = route.shape[0]
    tiles = n // rt
    assert n * TOP_K // MOE_ROWS + N_EXPERTS <= META_W and rt <= META_W, "chunk table is one row of META_W lanes"
    whole = lambda i: (0, 0, 0)
    return pl.pallas_call(
        _rank_kernel,
        grid=(tiles,),
        in_specs=[pl.BlockSpec((rt, LANES), lambda i: (i, 0))],
        out_specs=(pl.BlockSpec((tiles, TOP_K, rt), whole), pl.BlockSpec((SUBLANES, META_W), lambda i: (0, 0))),
        out_shape=(jax.ShapeDtypeStruct((tiles, TOP_K, rt), jnp.int32),
                   jax.ShapeDtypeStruct((SUBLANES, META_W), jnp.int32)),
        scratch_shapes=[pltpu.VMEM((tiles, TOP_K, rt), F32), pltpu.VMEM((tiles, TOP_K, rt), F32),
                        pltpu.VMEM((N_EXPERTS, LANES), F32)],
        compiler_params=_cparams("arbitrary"),
        name="route_rank",
    )(route)


def _row_copies_wait(src_ref, dst_ref, sem, n_rows):
    pltpu.make_async_copy(src_ref.at[pl.ds(0, n_rows)], dst_ref.at[pl.ds(0, n_rows)], sem).wait()


def _scatter_kernel(meta_ref, dest_ref, h3_ref, xs_ref, zero_ref, sem_ref):
    i = pl.program_id(0)
    rt = h3_ref.shape[0]
    n_chunks = xs_ref.shape[0] // MOE_ROWS

    @pl.when(i == 0)
    def _():
        zero_ref[...] = jnp.zeros(zero_ref.shape, F32)

        def zero_chunk(first_row):
            return pltpu.make_async_copy(zero_ref, xs_ref.at[pl.ds(first_row, MOE_ROWS)], sem_ref.at[1])

        def fills(act):
            prev = 0
            for ex in range(N_EXPERTS):
                end = meta_ref[2, ex]

                @pl.when(end > prev)
                def _():
                    act(zero_chunk(end - MOE_ROWS))
                prev = end
            for c in range(n_chunks - N_EXPERTS, n_chunks):
                @pl.when(c >= meta_ref[1, 0])
                def _():
                    act(zero_chunk(c * MOE_ROWS))

        fills(lambda copy: copy.start())
        fills(lambda copy: copy.wait())

    def issue(t, carry):
        for k in range(TOP_K):
            pltpu.make_async_copy(h3_ref.at[t], xs_ref.at[dest_ref[0, k, t]], sem_ref.at[0]).start()
        return carry

    lax.fori_loop(0, rt, issue, 0, unroll=8)
    _row_copies_wait(h3_ref, h3_ref, sem_ref.at[0], rt)
    _row_copies_wait(h3_ref, h3_ref, sem_ref.at[0], rt)


def _scatter_rows(h3, dest, meta, rt):
    n = h3.shape[0]
    n_slots = (n * TOP_K // MOE_ROWS + N_EXPERTS) * MOE_ROWS
    grid_spec = pltpu.PrefetchScalarGridSpec(
        num_scalar_prefetch=1,
        grid=(n // rt,),
        in_specs=[pl.BlockSpec((1, TOP_K, rt), lambda i, meta: (i, 0, 0), memory_space=pltpu.SMEM),
                  pl.BlockSpec((rt, ROW_TILES, LANES), lambda i, meta: (i, 0, 0))],
        out_specs=pl.BlockSpec(memory_space=pl.ANY),
        scratch_shapes=[pltpu.VMEM((MOE_ROWS, ROW_TILES, LANES), F32), pltpu.SemaphoreType.DMA((2,))],
    )
    return pl.pallas_call(
        _scatter_kernel,
        grid_spec=grid_spec,
        out_shape=jax.ShapeDtypeStruct((n_slots, ROW_TILES, LANES), F32),
        compiler_params=_cparams("arbitrary"),
        name="moe_scatter_rows",
    )(meta, dest, h3)


def _expert_kernel(meta_ref, xs_ref, wg_ref, wu_ref, wd_ref, ys_ref, wg_b, wu_b, wd_b):
    c = pl.program_id(0)
    in_use = c < meta_ref[1, 0]

    @pl.when(jnp.logical_not(in_use))
    def _():
        ys_ref[...] = jnp.zeros(ys_ref.shape, F32)

    @pl.when(in_use & ((c == 0) | (meta_ref[0, c] != meta_ref[0, jnp.maximum(c - 1, 0)])))
    def _():
        wg_b[...] = wg_ref[0].astype(BF16)
        wu_b[...] = wu_ref[0].astype(BF16)
        wd_b[...] = wd_ref[0].astype(BF16)

    @pl.when(in_use)
    def _():
        x = jnp.concatenate([xs_ref[:, j, :] for j in range(ROW_TILES)], axis=1).astype(BF16)
        gate = jnp.dot(x, wg_b[...], preferred_element_type=F32)
        up = jnp.dot(x, wu_b[...], preferred_element_type=F32)
        act = (gate * jax.nn.sigmoid(gate) * up).astype(BF16)
        y = jnp.dot(act, wd_b[...], preferred_element_type=F32)
        for j in range(ROW_TILES):
            ys_ref[:, j, :] = y[:, j * LANES:(j + 1) * LANES]


def _expert_mlp(xs, meta, w_g, w_u, w_d):
    n_chunks = xs.shape[0] // MOE_ROWS
    rows = lambda c, meta: (c, 0, 0)
    expert = lambda c, meta: (meta[0, jnp.minimum(c, meta[1, 0] - 1)], 0, 0)
    grid_spec = pltpu.PrefetchScalarGridSpec(
        num_scalar_prefetch=1,
        grid=(n_chunks,),
        in_specs=[
            pl.BlockSpec((MOE_ROWS, ROW_TILES, LANES), rows),
            pl.BlockSpec((1, D_MODEL, D_EXPERT), expert),
            pl.BlockSpec((1, D_MODEL, D_EXPERT), expert),
            pl.BlockSpec((1, D_EXPERT, D_MODEL), expert),
        ],
        out_specs=pl.BlockSpec((MOE_ROWS, ROW_TILES, LANES), rows),
        scratch_shapes=[pltpu.VMEM((D_MODEL, D_EXPERT), BF16), pltpu.VMEM((D_MODEL, D_EXPERT), BF16),
                        pltpu.VMEM((D_EXPERT, D_MODEL), BF16)],
    )
    return pl.pallas_call(
        _expert_kernel,
        grid_spec=grid_spec,
        out_shape=jax.ShapeDtypeStruct(xs.shape, F32),
        compiler_params=_cparams("arbitrary"),
        name="expert_mlp",
    )(meta, xs, w_g, w_u, w_d)


def _combine_kernel(dest_ref, nxt_ref, x1_ref, route_ref, ln_ref, ys_ref, o_ref, buf_ref, sem_ref):
    i = pl.program_id(0)
    n_tiles = pl.num_programs(0)
    rt = x1_ref.shape[0]

    def issue(table_ref, slot):
        def body(t, carry):
            for k in range(TOP_K):
                pltpu.make_async_copy(ys_ref.at[table_ref[0, k, t]], buf_ref.at[slot, k, t], sem_ref.at[slot]).start()
            return carry
        lax.fori_loop(0, rt, body, 0, unroll=8)

    slot = i % 2

    @pl.when(i == 0)
    def _():
        issue(dest_ref, slot)

    @pl.when(i + 1 < n_tiles)
    def _():
        issue(nxt_ref, 1 - slot)

    for k in range(TOP_K):
        _row_copies_wait(ys_ref, buf_ref.at[slot, k], sem_ref.at[slot], rt)
    route = route_ref[...]
    moe = None
    for k in range(TOP_K):
        y = jnp.concatenate([buf_ref[slot, k, :, j, :] for j in range(ROW_TILES)], axis=1)
        term = y * route[:, ROUTE_W1 + k:ROUTE_W1 + k + 1]
        moe = term if moe is None else moe + term
    o_ref[...] = _rms(x1_ref[...] + moe, ln_ref[...])


def _combine(x1, ys, dest, route, ln_final, rt):
    n = x1.shape[0]
    tiles = n // rt
    row = lambda i: (i, 0)
    table = lambda f: pl.BlockSpec((1, TOP_K, rt), f, memory_space=pltpu.SMEM)
    return pl.pallas_call(
        _combine_kernel,
        grid=(tiles,),
        in_specs=[
            table(lambda i: (i, 0, 0)),
            table(lambda i: (jnp.minimum(i + 1, tiles - 1), 0, 0)),
            pl.BlockSpec((rt, D_MODEL), row),
            pl.BlockSpec((rt, LANES), row),
            pl.BlockSpec((1, D_MODEL), lambda i: (0, 0)),
            pl.BlockSpec(memory_space=pl.ANY),
        ],
        out_specs=pl.BlockSpec((rt, D_MODEL), row),
        out_shape=jax.ShapeDtypeStruct((n, D_MODEL), F32),
        scratch_shapes=[pltpu.VMEM((2, TOP_K, rt, ROW_TILES, LANES), F32), pltpu.SemaphoreType.DMA((2,))],
        compiler_params=_cparams("arbitrary"),
        name="moe_combine_norm",
    )(dest, dest, x1, route, ln_final.reshape(1, -1), ys)


def _ffn(conv2d, att2d, x2d, g_conv, g_att, w_out_b, ln2, w_route_b, w_g, w_u, w_d, ln_final, tm):
    x1, h3, route = _merge(conv2d, att2d, x2d, g_conv, g_att, w_out_b, ln2, w_route_b, tm)
    rt = min(tm, MOE_ROWS)
    dest, meta = _rank(route, rt)
    ys = _expert_mlp(_scatter_rows(h3, dest, meta, rt), meta, w_g, w_u, w_d)
    return _combine(x1, ys, dest, route, ln_final, rt)


def kernel(x_prompt, x_sample, cache_cmp_kv, cache_sel_kv, state_win_kv, state_conv, page_table, ln1, w_in, conv_dw_w, conv_dw_b, conv_ln_g, conv_ln_b, cmp_pos_emb, w_cmp_k1, w_cmp_k2, w_cmp_v1, w_cmp_v2, out_norm_conv, out_norm_att, w_out, ln2, w_router_group, w_router_expert, w_exp_gate, w_exp_up, w_exp_down, ln_final):
    depth = ln1.shape[0]
    assert depth == 1, "single-layer step"
    b, t, _ = x_prompt.shape
    db, ds, _ = x_sample.shape
    n_phys = cache_cmp_kv.shape[1]
    n_pages = page_table.shape[1]
    past = n_pages * PAGE_SIZE
    win_rows = state_win_kv.shape[2]
    assert ds < CMP_BLOCK and ds <= SUBLANES and past % SEL_BLOCK == 0 and past // SEL_BLOCK >= N_SEL
    assert win_rows == WINDOW and past >= WINDOW and t % KC == 0 and t >= WIN_KEYS

    w_in_b = jnp.pad(w_in[0], ((0, 0), (0, D_IN_PAD - D_IN))).astype(BF16)
    w_out_b = w_out[0].astype(BF16)
    w_route_b = jnp.pad(jnp.concatenate([w_router_group[0], w_router_expert[0]], axis=1),
                        ((0, 0), (0, LANES - N_GROUPS - N_EXPERTS))).astype(BF16)
    cmp_w = _cmp_weights(cmp_pos_emb[0], w_cmp_k1[0], w_cmp_k2[0], w_cmp_v1[0], w_cmp_v2[0])
    conv_w = (conv_dw_w[0], conv_dw_b[0], conv_ln_g[0], conv_ln_b[0])
    ffn_w = (out_norm_conv[0], out_norm_att[0], w_out_b, ln2[0], w_route_b, w_exp_gate[0], w_exp_up[0],
             w_exp_down[0], ln_final)

    xp2 = x_prompt.reshape(b * t, D_MODEL)
    c_p, s_p = _rope_tables(jnp.arange(t))
    (a_p, q_p, ckv_p, _, _, gate_p, ckv_pt, skv_pt, wkv_pt, ksel_aug, kwin_p, vsel_t, vwin_t) = _project(
        xp2, ln1[0], w_in_b, c_p, s_p, 512)
    a_p3 = a_p.reshape(b, t, CONV_CH)
    conv_p = _conv_module(a_p3, jnp.zeros((b, HIST_ROWS, CONV_CH), F32), *conv_w, 512)
    kc_p, vc_p = _compress_dense(ckv_p.reshape(b, t, COL_KV), cmp_w)
    gates_t = jnp.transpose(gate_p[:, :COL_GATE].reshape(b, t, N_KV, Q_PER_KV, 3), (0, 2, 4, 3, 1))
    att_p = _prompt_attention(q_p.reshape(b, t, COL_Q), kc_p, jnp.swapaxes(vc_p, 2, 3), ksel_aug,
                              vsel_t, kwin_p, vwin_t, gates_t.reshape(b, N_KV, 3 * Q_PER_KV, t))
    y_p = _ffn(conv_p.reshape(b * t, CONV_CH), att_p.reshape(b * t, COL_Q), xp2, *ffn_w, 256)

    n_s = db * ds
    xs2 = x_sample.reshape(n_s, D_MODEL)
    c_s, s_s = _rope_tables(jnp.tile(past + jnp.arange(ds), db))
    a_s, q_s, ckv_s, skv_s, wkv_s, gate_s = _project(xs2, ln1[0], w_in_b, c_s, s_s, n_s)[:6]
    a_s3 = a_s.reshape(db, ds, CONV_CH)
    hist_s = jnp.pad(state_conv[0], ((0, 0), (HIST_ROWS - (CONV_W - 1), 0), (0, 0)))
    conv_s = _conv_module(a_s3, hist_s, *conv_w, ds)
    rows_minor = lambda a, n, r: jnp.swapaxes(a.reshape(n, r, COL_KV), 1, 2)
    kc_s, vc_s = _compress_paged(rows_minor(cache_cmp_kv[0], n_phys, PAGE_SIZE), page_table, cmp_w)

    q5 = q_s.reshape(db, ds, N_KV, Q_PER_KV, HEAD_DIM)
    pad_tok = SUBLANES - ds
    pad_head = SUBLANES - Q_PER_KV
    q_rt = jnp.pad(jnp.transpose(q5, (0, 2, 3, 1, 4)), ((0, 0), (0, 0), (0, 0), (0, pad_tok), (0, 0)))
    q_rt = q_rt.reshape(db, N_KV, Q_PER_KV * SUBLANES, HEAD_DIM)
    q_tr = jnp.pad(jnp.transpose(q5, (0, 2, 1, 3, 4)), ((0, 0), (0, 0), (0, 0), (0, pad_head), (0, 0)))
    q_tr = q_tr.reshape(db, N_KV, ds * SUBLANES, HEAD_DIM)
    g5 = gate_s[:, :COL_GATE].reshape(db, ds, N_KV, Q_PER_KV, 3)
    g_tr = jnp.pad(jnp.transpose(g5, (0, 2, 4, 1, 3)), ((0, 0),) * 4 + ((0, pad_head),))
    g_tr = jnp.broadcast_to(g_tr.reshape(db, N_KV, 3, ds * SUBLANES)[..., None], (db, N_KV, 3, ds * SUBLANES, HEAD_DIM))
    ocmp, picks = _sample_cmp_select(q_rt, kc_s, vc_s, past, ds)
    sel_idx = picks.reshape(db, N_KV, SUBLANES, LANES)[:, :, :ds, :N_SEL].reshape(-1)
    ocmp_tr = jnp.transpose(ocmp.reshape(db, N_KV, Q_PER_KV, SUBLANES, HEAD_DIM)[:, :, :, :ds], (0, 1, 3, 2, 4))
    ocmp_tr = jnp.pad(ocmp_tr, ((0, 0), (0, 0), (0, 0), (0, pad_head), (0, 0)))
    ocmp_tr = ocmp_tr.reshape(db, N_KV, ds * SUBLANES, HEAD_DIM)
    new_t = lambda kv: jnp.pad(rows_minor(kv, db, ds), ((0, 0), (0, 0), (0, LANES - ds)))
    o_s = _sample_attention(sel_idx, page_table, rows_minor(cache_sel_kv[0], n_phys, PAGE_SIZE), q_tr,
                            new_t(skv_s), new_t(wkv_s), rows_minor(state_win_kv[0], db, win_rows),
                            ocmp_tr, g_tr, ds)
    att_s = o_s.reshape(db, N_KV, ds, SUBLANES, HEAD_DIM)[:, :, :, :Q_PER_KV]
    att_s = jnp.transpose(att_s, (0, 2, 1, 3, 4)).reshape(n_s, COL_Q)
    y_s = _ffn(conv_s.reshape(n_s, CONV_CH), att_s, xs2, *ffn_w, n_s)

    kv6 = lambda kv, bb, tt: kv.reshape(1, bb, tt, N_KV, 2, HEAD_DIM)
    kv6_t = lambda kv_t: jnp.swapaxes(kv_t, 1, 2).reshape(1, b, kv_t.shape[2], N_KV, 2, HEAD_DIM)
    new_win_s = jnp.concatenate([state_win_kv, kv6(wkv_s, db, ds)], axis=2)[:, :, ds:]
    new_conv_s = jnp.concatenate([state_conv[0], a_s3], axis=1)[None, :, ds:]
    return (y_p.reshape(b, t, D_MODEL), y_s.reshape(db, ds, D_MODEL),
            kv6_t(ckv_pt), kv6(ckv_s, db, ds), kv6_t(skv_pt), kv6(skv_s, db, ds),
            kv6_t(wkv_pt[:, :, t - min(WINDOW, t):]), new_win_s,
            a_p3[None, :, t - (CONV_W - 1):], new_conv_s)
```

```python
import functools

import jax
import jax.numpy as jnp
from jax import lax
from jax.experimental import pallas as pl
from jax.experimental.pallas import tpu as pltpu

D_MODEL = 1024
CONV_CH = 512
CONV_W = 31
N_HEADS = 8
HEAD_DIM = 64
N_KV = 2
Q_PER_KV = N_HEADS // N_KV
ROPE_DIM = HEAD_DIM // 4
ROPE_THETA = 500000.0
CMP_BLOCK = 64
SEL_BLOCK = CMP_BLOCK
N_SEL = 16
WINDOW = 512
CMP_HID = 2 * HEAD_DIM
COL_Q = N_HEADS * HEAD_DIM
COL_KV = 2 * N_KV * HEAD_DIM
COL_GATE = 3 * N_HEADS
D_IN = 2 * CONV_CH + COL_Q + 3 * COL_KV + COL_GATE
N_GROUPS = 4
EXPERTS_PER_GROUP = 8
N_EXPERTS = N_GROUPS * EXPERTS_PER_GROUP
TOP_K = 2
D_EXPERT = 512
PAGE_SIZE = 128
RMS_EPS = 1e-6
LN_EPS = 1e-5
NEG_INF = -1e30
ATT_SCALE = HEAD_DIM ** -0.5

LANES = 128
SUBLANES = 8
VMEM_LIMIT_BYTES = 56 * 1024 * 1024

D_IN_PAD = ((D_IN + LANES - 1) // LANES) * LANES
COL_GATE_OFF = 2 * CONV_CH + COL_Q + 3 * COL_KV
HIST_ROWS = 32
CONV_ROWS = 32
MOE_ROWS = 256
ROW_TILES = D_MODEL // LANES

BF16 = jnp.bfloat16
F32 = jnp.float32


def _cparams(*sem):
    return pltpu.CompilerParams(dimension_semantics=sem, vmem_limit_bytes=VMEM_LIMIT_BYTES)


def _rms(x, g):
    return x * lax.rsqrt(jnp.mean(x * x, axis=-1, keepdims=True) + RMS_EPS) * g


def _rope_tables(pos):
    half = ROPE_DIM // 2
    inv = ROPE_THETA ** (-jnp.arange(half, dtype=F32) / half)
    ang = pos.astype(F32)[:, None] * inv
    cos, sin = jnp.cos(ang), jnp.sin(ang)
    m = jnp.arange(LANES) % HEAD_DIM
    idx = m % half
    c = jnp.where(m < ROPE_DIM, cos[:, idx], 1.0)
    s = jnp.where(m < half, -sin[:, idx], jnp.where(m < ROPE_DIM, sin[:, idx], 0.0))
    return c.astype(F32), s.astype(F32)


def _rope(v, c, s, first_half):
    w = v.shape[1]
    half = ROPE_DIM // 2
    partner = jnp.where(first_half, pltpu.roll(v, w - half, axis=1), pltpu.roll(v, half, axis=1))
    return v * c + partner * s


def _proj_kernel(x_ref, ln_ref, w_ref, c_ref, s_ref, a_ref, q_ref, ckv_ref, skv_ref, wkv_ref, gate_ref,
                 ckvt_ref, skvt_ref, wkvt_ref, ksel_ref, kwin_ref, vsel_ref, vwin_ref, *, t_tiles):
    x = x_ref[...]
    xn = _rms(x, ln_ref[...])
    p = jnp.dot(xn.astype(BF16), w_ref[...], preferred_element_type=F32)
    a_ref[...] = p[:, :CONV_CH] * jax.nn.sigmoid(p[:, CONV_CH:2 * CONV_CH])

    c128, s128 = c_ref[...], s_ref[...]
    tm = x.shape[0]
    lane_q = lax.broadcasted_iota(jnp.int32, (tm, COL_Q), 1)
    cq = jnp.concatenate([c128] * (COL_Q // LANES), axis=1)
    sq = jnp.concatenate([s128] * (COL_Q // LANES), axis=1)
    o = 2 * CONV_CH
    q = _rope(p[:, o:o + COL_Q], cq, sq, (lane_q % HEAD_DIM) < ROPE_DIM // 2)
    q_ref[...] = q.astype(q_ref.dtype)
    o += COL_Q

    lane_kv = lax.broadcasted_iota(jnp.int32, (tm, COL_KV), 1)
    is_k = (lane_kv % (2 * HEAD_DIM)) < HEAD_DIM
    ckv = jnp.where(is_k, jnp.concatenate([c128] * (COL_KV // LANES), axis=1), 1.0)
    skv = jnp.where(is_k, jnp.concatenate([s128] * (COL_KV // LANES), axis=1), 0.0)
    first_kv = (lane_kv % HEAD_DIM) < ROPE_DIM // 2
    kvs = []
    for ref, ref_t in ((ckv_ref, ckvt_ref), (skv_ref, skvt_ref), (wkv_ref, wkvt_ref)):
        kv = _rope(p[:, o:o + COL_KV], ckv, skv, first_kv)
        kv_t = kv.T
        ref[...] = kv
        ref_t[0] = kv_t
        kvs.append((kv, kv_t))
        o += COL_KV
    gate_ref[...] = jax.nn.sigmoid(p[:, o:o + LANES])

    kvw = 2 * HEAD_DIM
    lane = lax.broadcasted_iota(jnp.int32, (tm, LANES), 1)
    pos = (pl.program_id(0) % t_tiles) * tm + lax.broadcasted_iota(jnp.int32, (tm, LANES), 0)
    blk = pos // SEL_BLOCK
    is_key = lane < HEAD_DIM
    ones_rows = (lax.broadcasted_iota(jnp.int32, (V_ROWS - HEAD_DIM, tm), 0) == 0).astype(F32)
    (skv_v, skv_t), (wkv_v, wkv_t) = kvs[1], kvs[2]
    for g in range(N_KV):
        low = jnp.where(is_key, skv_v[:, g * kvw:(g + 1) * kvw], (blk == lane - HEAD_DIM).astype(F32))
        high = jnp.where(is_key, (blk == lane + HEAD_DIM).astype(F32), 0.0)
        ksel_ref[0, g] = jnp.concatenate([low, high], axis=1).astype(BF16)
        kwin_ref[0, g] = wkv_v[:, g * kvw:g * kvw + HEAD_DIM].astype(BF16)
        v_rows = slice(g * kvw + HEAD_DIM, (g + 1) * kvw)
        vsel_ref[0, g] = jnp.concatenate([skv_t[v_rows], ones_rows], axis=0).astype(BF16)
        vwin_ref[0, g] = jnp.concatenate([wkv_t[v_rows], ones_rows], axis=0).astype(BF16)


def _project(x2d, ln, w_pad, c_tab, s_tab, tm):
    n = x2d.shape[0]
    t = c_tab.shape[0]
    t_tiles = t // tm
    assert t // SEL_BLOCK <= KAUG - HEAD_DIM - HEAD_DIM, "one-hot block ids fit the augmented key"
    row = lambda i: (i, 0)
    tab = lambda i: (i % t_tiles, 0)
    const = lambda i: (0, 0)
    kv_t = jax.ShapeDtypeStruct((n // t, COL_KV, t), F32)
    kv_t_spec = pl.BlockSpec((1, COL_KV, tm), lambda i: (i // t_tiles, 0, i % t_tiles))
    keys = lambda w: (jax.ShapeDtypeStruct((n // t, N_KV, t, w), BF16),
                      pl.BlockSpec((1, N_KV, tm, w), lambda i: (i // t_tiles, 0, i % t_tiles, 0)))
    v_t = (jax.ShapeDtypeStruct((n // t, N_KV, V_ROWS, t), BF16),
           pl.BlockSpec((1, N_KV, V_ROWS, tm), lambda i: (i // t_tiles, 0, 0, i % t_tiles)))
    out_shape = (
        jax.ShapeDtypeStruct((n, CONV_CH), F32),
        jax.ShapeDtypeStruct((n, COL_Q), BF16),
        jax.ShapeDtypeStruct((n, COL_KV), F32),
        jax.ShapeDtypeStruct((n, COL_KV), F32),
        jax.ShapeDtypeStruct((n, COL_KV), F32),
        jax.ShapeDtypeStruct((n, LANES), F32),
        kv_t, kv_t, kv_t,
        keys(KAUG)[0], keys(HEAD_DIM)[0], v_t[0], v_t[0],
    )
    return pl.pallas_call(
        functools.partial(_proj_kernel, t_tiles=t_tiles),
        grid=(n // tm,),
        in_specs=[
            pl.BlockSpec((tm, D_MODEL), row),
            pl.BlockSpec((1, D_MODEL), const),
            pl.BlockSpec((D_MODEL, D_IN_PAD), const),
            pl.BlockSpec((tm, LANES), tab),
            pl.BlockSpec((tm, LANES), tab),
        ],
        out_specs=(
            pl.BlockSpec((tm, CONV_CH), row),
            pl.BlockSpec((tm, COL_Q), row),
            pl.BlockSpec((tm, COL_KV), row),
            pl.BlockSpec((tm, COL_KV), row),
            pl.BlockSpec((tm, COL_KV), row),
            pl.BlockSpec((tm, LANES), row),
            kv_t_spec, kv_t_spec, kv_t_spec,
            keys(KAUG)[1], keys(HEAD_DIM)[1], v_t[1], v_t[1],
        ),
        out_shape=out_shape,
        compiler_params=_cparams("arbitrary"),
        name="in_proj",
    )(x2d, ln.reshape(1, D_MODEL), w_pad, c_tab, s_tab)


def _conv_kernel(a_ref, hist_ref, w_ref, b_ref, g_ref, beta_ref, o_ref, sh_ref):
    tt = a_ref.shape[1]
    ext_ref = sh_ref.at[0]

    @pl.when(pl.program_id(1) == 0)
    def _():
        ext_ref[0:HIST_ROWS, :] = hist_ref[0]

    ext_ref[HIST_ROWS:HIST_ROWS + tt, :] = a_ref[0]
    span = HIST_ROWS + tt - SUBLANES
    for s in range(1, SUBLANES):
        sh_ref[s, 0:span, :] = ext_ref[s:s + span, :]

    lead = HIST_ROWS - (CONV_W - 1)
    rc = min(tt, CONV_ROWS)

    def chunk(i, carry):
        r0 = pl.multiple_of(i * rc, rc)
        acc = jnp.broadcast_to(b_ref[...], (rc, CONV_CH))
        for k in range(CONV_W):
            a, s = divmod(lead + k, SUBLANES)
            acc = acc + w_ref[k:k + 1, :] * sh_ref[s, pl.ds(r0 + a * SUBLANES, rc), :]
        o_ref[0, pl.ds(r0, rc), :] = acc
        return carry

    lax.fori_loop(0, tt // rc, chunk, 0)
    acc = o_ref[0]
    mu = jnp.mean(acc, axis=-1, keepdims=True)
    var = jnp.mean(jnp.square(acc - mu), axis=-1, keepdims=True)
    y = (acc - mu) * lax.rsqrt(var + LN_EPS) * g_ref[...] + beta_ref[...]
    o_ref[0] = y * jax.nn.sigmoid(y)
    carry = ext_ref[tt:tt + HIST_ROWS, :]
    ext_ref[0:HIST_ROWS, :] = carry


def _conv_module(a3d, hist, dw_w, dw_b, ln_g, ln_b, tt):
    b, t, _ = a3d.shape
    w_pad = jnp.pad(dw_w, ((0, HIST_ROWS - CONV_W), (0, 0)))
    vec = lambda i, j: (0, 0)
    return pl.pallas_call(
        _conv_kernel,
        grid=(b, t // tt),
        in_specs=[
            pl.BlockSpec((1, tt, CONV_CH), lambda i, j: (i, j, 0)),
            pl.BlockSpec((1, HIST_ROWS, CONV_CH), lambda i, j: (i, 0, 0)),
            pl.BlockSpec((HIST_ROWS, CONV_CH), vec),
            pl.BlockSpec((1, CONV_CH), vec),
            pl.BlockSpec((1, CONV_CH), vec),
            pl.BlockSpec((1, CONV_CH), vec),
        ],
        out_specs=pl.BlockSpec((1, tt, CONV_CH), lambda i, j: (i, j, 0)),
        out_shape=jax.ShapeDtypeStruct((b, t, CONV_CH), F32),
        scratch_shapes=[pltpu.VMEM((SUBLANES, HIST_ROWS + tt, CONV_CH), F32)],
        compiler_params=_cparams("arbitrary", "arbitrary"),
        name="conv_module",
    )(a3d, hist, w_pad, dw_b.reshape(1, -1), ln_g.reshape(1, -1), ln_b.reshape(1, -1))


L_GROUP = 2 * LANES // HEAD_DIM


def _compress_rows(x_refs, pe_ref, wk1_ref, wk2_ref, wv1_ref, wv2_ref, nb, pitch=CMP_BLOCK):
    hk = jnp.zeros((N_KV * nb, CMP_HID), F32)
    hv = jnp.zeros((N_KV * nb, CMP_HID), F32)
    for j in range(CMP_BLOCK // L_GROUP):
        parts_k, parts_v = [], []
        for x_ref in x_refs:
            xs = [x_ref[pl.ds(j * L_GROUP + i, nb, stride=pitch), :] + pe_ref[j * L_GROUP + i:j * L_GROUP + i + 1, :]
                  for i in range(L_GROUP)]
            parts_k.append(jnp.concatenate([x[:, :HEAD_DIM] for x in xs], axis=1))
            parts_v.append(jnp.concatenate([x[:, HEAD_DIM:] for x in xs], axis=1))
        xk = jnp.concatenate(parts_k, axis=0).astype(BF16)
        xv = jnp.concatenate(parts_v, axis=0).astype(BF16)
        rows = slice(j * L_GROUP * HEAD_DIM, (j + 1) * L_GROUP * HEAD_DIM)
        hk = hk + jnp.dot(xk, wk1_ref[rows, :], preferred_element_type=F32)
        hv = hv + jnp.dot(xv, wv1_ref[rows, :], preferred_element_type=F32)
    kc = jnp.dot((hk * jax.nn.sigmoid(hk)).astype(BF16), wk2_ref[...], preferred_element_type=F32)
    vc = jnp.dot((hv * jax.nn.sigmoid(hv)).astype(BF16), wv2_ref[...], preferred_element_type=F32)
    return kc, vc


def _compress_dense_kernel(*refs):
    x_refs, (pe_ref, wk1_ref, wk2_ref, wv1_ref, wv2_ref, kc_ref, vc_ref) = refs[:N_KV], refs[N_KV:]
    nb = x_refs[0].shape[1] // CMP_BLOCK
    kc, vc = _compress_rows([x.at[0] for x in x_refs], pe_ref, wk1_ref, wk2_ref, wv1_ref, wv2_ref, nb)
    for g in range(N_KV):
        kc_ref[0, g] = kc[g * nb:(g + 1) * nb]
        vc_ref[0, g] = vc[g * nb:(g + 1) * nb]


def _cmp_weight_specs():
    const = lambda *_: (0, 0)
    return [
        pl.BlockSpec((CMP_BLOCK, 2 * HEAD_DIM), const),
        pl.BlockSpec((CMP_BLOCK * HEAD_DIM, CMP_HID), const),
        pl.BlockSpec((CMP_HID, HEAD_DIM), const),
        pl.BlockSpec((CMP_BLOCK * HEAD_DIM, CMP_HID), const),
        pl.BlockSpec((CMP_HID, HEAD_DIM), const),
    ]


def _cmp_weights(pos_emb, w_k1, w_k2, w_v1, w_v2):
    pe = pos_emb.reshape(CMP_BLOCK, 2 * HEAD_DIM)
    return (pe, w_k1.reshape(-1, CMP_HID).astype(BF16), w_k2.astype(BF16),
            w_v1.reshape(-1, CMP_HID).astype(BF16), w_v2.astype(BF16))


def _compress_dense(kv3d, cmp_w):
    b, t, _ = kv3d.shape
    nb = t // CMP_BLOCK
    out = jax.ShapeDtypeStruct((b, N_KV, nb, HEAD_DIM), F32)
    ospec = pl.BlockSpec((1, N_KV, nb, HEAD_DIM), lambda i: (i, 0, 0, 0))
    return pl.pallas_call(
        _compress_dense_kernel,
        grid=(b,),
        in_specs=[pl.BlockSpec((1, t, 2 * HEAD_DIM), functools.partial(lambda g, i: (i, 0, g), g))
                  for g in range(N_KV)] + _cmp_weight_specs(),
        out_specs=(ospec, ospec),
        out_shape=(out, out),
        compiler_params=_cparams("arbitrary"),
        name="compress_prompt",
    )(*([kv3d] * N_KV), *cmp_w)


PAGES_PER_STEP = 64
BLOCK_PITCH = CMP_BLOCK + SUBLANES


def _compress_paged_kernel(pt_ref, cache_ref, pe_ref, wk1_ref, wk2_ref, wv1_ref, wv2_ref, kc_ref, vc_ref,
                           raw_ref, rows_ref, sem_ref):
    step = pl.program_id(0)
    n_steps = pl.num_programs(0)
    nb = PAGES_PER_STEP * PAGE_SIZE // CMP_BLOCK
    kvw = 2 * HEAD_DIM

    def page_copy(s, slot, p):
        return pltpu.make_async_copy(cache_ref.at[pt_ref[s * PAGES_PER_STEP + p]], raw_ref.at[slot, p], sem_ref.at[slot])

    def issue(s, slot):
        for p in range(PAGES_PER_STEP):
            page_copy(s, slot, p).start()

    slot = step % 2

    @pl.when(step == 0)
    def _():
        issue(step, slot)

    @pl.when(step + 1 < n_steps)
    def _():
        issue(step + 1, 1 - slot)

    for p in range(PAGES_PER_STEP):
        page_copy(step, slot, p).wait()

    for p in range(PAGES_PER_STEP):
        page = raw_ref[slot, p].T
        for g in range(N_KV):
            for n in range(PAGE_SIZE // CMP_BLOCK):
                row0 = (p * (PAGE_SIZE // CMP_BLOCK) + n) * BLOCK_PITCH
                rows_ref[g, row0:row0 + CMP_BLOCK, :] = page[n * CMP_BLOCK:(n + 1) * CMP_BLOCK, g * kvw:(g + 1) * kvw]

    kc, vc = _compress_rows([rows_ref.at[g] for g in range(N_KV)], pe_ref, wk1_ref, wk2_ref, wv1_ref, wv2_ref, nb,
                            pitch=BLOCK_PITCH)
    for g in range(N_KV):
        kc_ref[0, g] = kc[g * nb:(g + 1) * nb]
        vc_ref[0, g] = vc[g * nb:(g + 1) * nb]


def _compress_paged(cache_t, page_table, cmp_w):
    db, n_pages = page_table.shape
    steps_per_row = n_pages // PAGES_PER_STEP
    nb = PAGES_PER_STEP * PAGE_SIZE // CMP_BLOCK
    out = jax.ShapeDtypeStruct((db, N_KV, steps_per_row * nb, HEAD_DIM), F32)
    ospec = pl.BlockSpec((1, N_KV, nb, HEAD_DIM), lambda i, pt: (i // steps_per_row, 0, i % steps_per_row, 0))
    grid_spec = pltpu.PrefetchScalarGridSpec(
        num_scalar_prefetch=1,
        grid=(db * steps_per_row,),
        in_specs=[pl.BlockSpec(memory_space=pl.ANY)] + _cmp_weight_specs(),
        out_specs=(ospec, ospec),
        scratch_shapes=[pltpu.VMEM((2, PAGES_PER_STEP, COL_KV, PAGE_SIZE), F32),
                        pltpu.VMEM((N_KV, nb * BLOCK_PITCH, 2 * HEAD_DIM), F32),
                        pltpu.SemaphoreType.DMA((2,))],
    )
    return pl.pallas_call(
        _compress_paged_kernel,
        grid_spec=grid_spec,
        out_shape=(out, out),
        compiler_params=_cparams("arbitrary"),
        name="compress_paged",
    )(page_table.reshape(-1), cache_t, *cmp_w)


TQ = 2 * LANES
KC = 512
LOG2_E = 1.4426950408889634
BOUND_SLACK = 1.01
MAX_SHIFT = 60.0
WIN_KEYS = WINDOW + TQ
V_ROWS = HEAD_DIM + 16
KAUG = 2 * LANES


def _top_blocks(imp, cand, n_blocks):
    blk = lax.broadcasted_iota(jnp.int32, imp.shape, 0)
    score = jnp.where(cand, imp, -1.0)
    sel = jnp.zeros(imp.shape, F32)
    for _ in range(N_SEL):
        mx = jnp.max(score, axis=0, keepdims=True)
        idx = jnp.min(jnp.where(score == mx, blk, n_blocks), axis=0, keepdims=True)
        pick = blk == idx
        sel = jnp.where(pick, 1.0, sel)
        score = jnp.where(pick, -2.0, score)
    return sel


def _prompt_attn_kernel(q_ref, kc_ref, vct_ref, ksel_ref, vselt_ref, kwin_ref, vwint_ref, gate_ref, o_ref,
                        kmax_ref, acc_ref):
    qt = pl.program_id(2)
    t0 = qt * TQ
    nb = kc_ref.shape[2]
    width = Q_PER_KV * TQ

    q = q_ref[0].astype(F32) * (ATT_SCALE * LOG2_E)
    q_t = q.T
    q4 = jnp.concatenate([q_t[r * HEAD_DIM:(r + 1) * HEAD_DIM] for r in range(Q_PER_KV)], axis=1)
    q4b = q4.astype(BF16)
    tok = t0 + lax.broadcasted_iota(jnp.int32, (1, width), 1) % TQ

    sc = jnp.dot(kc_ref[0, 0].astype(BF16), q4b, preferred_element_type=F32)
    blk = lax.broadcasted_iota(jnp.int32, (nb, width), 0)
    valid_c = (blk + 1) * CMP_BLOCK - 1 <= tok
    sc = jnp.where(valid_c, sc, NEG_INF)
    e = jnp.where(valid_c, jnp.exp2(sc - jnp.max(sc, axis=0, keepdims=True)), 0.0)
    den = jnp.sum(e, axis=0, keepdims=True)
    p = e / jnp.where(den > 0.0, den, 1.0)
    o_cmp = jnp.dot(vct_ref[0, 0].astype(BF16), p.astype(BF16), preferred_element_type=F32)
    imp = p[:, 0:TQ]
    for r in range(1, Q_PER_KV):
        imp = imp + p[:, r * TQ:(r + 1) * TQ]

    tok1 = t0 + lax.broadcasted_iota(jnp.int32, (1, TQ), 1)
    own = tok1 // SEL_BLOCK
    blk1 = lax.broadcasted_iota(jnp.int32, (nb, TQ), 0)
    cand = blk1 < own
    sel = _top_blocks(imp, cand, nb)
    bias = jnp.where(cand, jnp.where(sel > 0.0, 0.0, NEG_INF), jnp.where(blk1 == own, 0.0, NEG_INF))
    bias4 = jnp.concatenate([bias] * Q_PER_KV, axis=1).astype(BF16)
    q_aug = jnp.concatenate([q4b, bias4, jnp.zeros((KAUG - HEAD_DIM - nb, width), BF16)], axis=0)

    def scores(c):
        return jnp.dot(ksel_ref[0, 0, pl.ds(pl.multiple_of(c * KC, KC), KC), :], q_aug, preferred_element_type=F32)

    def softmax_pv(c, s, m, acc):
        m_new = jnp.maximum(m, jnp.max(s, axis=0, keepdims=True))
        alpha = jnp.exp2(m - m_new)
        pr = jnp.exp2(s - m_new).astype(BF16)
        v_blk = vselt_ref[0, 0, :, pl.ds(pl.multiple_of(c * KC, KC), KC)]
        return m_new, alpha * acc + jnp.dot(v_blk, pr, preferred_element_type=F32)

    def sel_step(c, carry):
        s, m, acc = carry
        s_next = scores(c + 1)
        return (s_next,) + softmax_pv(c, s, m, acc)

    last = t0 // KC
    key = last * KC + lax.broadcasted_iota(jnp.int32, (KC, width), 0)

    def online_softmax():
        init = (scores(0), jnp.full((1, width), NEG_INF, F32), jnp.zeros((V_ROWS, width), F32))
        s_last, m_sel, acc = lax.fori_loop(0, last, sel_step, init)
        return softmax_pv(last, jnp.where(key <= tok, s_last, NEG_INF), m_sel, acc)[1]

    @pl.when(qt == 0)
    def _():
        def body(i, kmax):
            k = ksel_ref[0, 0, pl.ds(pl.multiple_of(i * KC, KC), KC), :].astype(F32)
            k = jnp.where(lax.broadcasted_iota(jnp.int32, k.shape, 1) < HEAD_DIM, k, 0.0)
            return jnp.maximum(kmax, jnp.max(jnp.sum(k * k, axis=1, keepdims=True), axis=0, keepdims=True))
        kmax_ref[...] = jnp.broadcast_to(lax.fori_loop(0, ksel_ref.shape[2] // KC, body, jnp.zeros((1, 1), F32)),
                                         kmax_ref.shape)

    q4f = q4b.astype(F32)
    bound = jnp.sqrt(jnp.sum(q4f * q4f, axis=0, keepdims=True) * kmax_ref[0:1, 0:1]) * BOUND_SLACK
    bounded = jnp.max(bound) <= MAX_SHIFT

    def bounded_pv(c, s, acc):
        v_blk = vselt_ref[0, 0, :, pl.ds(pl.multiple_of(c * KC, KC), KC)]
        return acc + jnp.dot(v_blk, jnp.exp2(s - bound).astype(BF16), preferred_element_type=F32)

    @pl.when(bounded)
    def _():
        acc = lax.fori_loop(0, last, lambda c, acc: bounded_pv(c, scores(c), acc), jnp.zeros((V_ROWS, width), F32))
        acc_ref[...] = bounded_pv(last, jnp.where(key <= tok, scores(last), NEG_INF), acc)

    @pl.when(jnp.logical_not(bounded))
    def _():
        acc_ref[...] = online_softmax()

    acc_sel = acc_ref[...]
    o_sel = acc_sel[0:HEAD_DIM] / acc_sel[HEAD_DIM:HEAD_DIM + 1]

    w0 = pl.multiple_of(jnp.maximum(t0 - WINDOW, 0), TQ)
    sw = jnp.dot(kwin_ref[0, 0, pl.ds(w0, WIN_KEYS), :], q4b, preferred_element_type=F32)
    dist = tok - (w0 + lax.broadcasted_iota(jnp.int32, (WIN_KEYS, width), 0))
    sw = jnp.where((dist >= 0) & (dist < WINDOW), sw, NEG_INF)
    pw = jnp.exp2(sw - jnp.max(sw, axis=0, keepdims=True)).astype(BF16)
    acc_win = jnp.dot(vwint_ref[0, 0, :, pl.ds(w0, WIN_KEYS)], pw, preferred_element_type=F32)
    o_win = acc_win[0:HEAD_DIM] / acc_win[HEAD_DIM:HEAD_DIM + 1]

    outs = []
    for r in range(Q_PER_KV):
        sl = slice(r * TQ, (r + 1) * TQ)
        g = [gate_ref[0, 0, j * Q_PER_KV + r:j * Q_PER_KV + r + 1, :] for j in range(3)]
        outs.append(g[0] * o_cmp[:, sl] + g[1] * o_sel[:, sl] + g[2] * o_win[:, sl])
    o_ref[0] = jnp.concatenate(outs, axis=0).T


def _prompt_attention(q3, k_c, v_ct, ksel_aug, vsel_t, kwin, vwin_t, gates_t):
    b, t, _ = q3.shape
    nb = k_c.shape[2]
    width = Q_PER_KV * HEAD_DIM
    per_bg = lambda i, g, j: (i, g, 0, 0)
    return pl.pallas_call(
        _prompt_attn_kernel,
        grid=(b, N_KV, t // TQ),
        in_specs=[
            pl.BlockSpec((1, TQ, width), lambda i, g, j: (i, j, g)),
            pl.BlockSpec((1, 1, nb, HEAD_DIM), per_bg),
            pl.BlockSpec((1, 1, HEAD_DIM, nb), per_bg),
            pl.BlockSpec((1, 1, t, KAUG), per_bg),
            pl.BlockSpec((1, 1, V_ROWS, t), per_bg),
            pl.BlockSpec((1, 1, t, HEAD_DIM), per_bg),
            pl.BlockSpec((1, 1, V_ROWS, t), per_bg),
            pl.BlockSpec((1, 1, 3 * Q_PER_KV, TQ), lambda i, g, j: (i, g, 0, j)),
        ],
        out_specs=pl.BlockSpec((1, TQ, width), lambda i, g, j: (i, j, g)),
        out_shape=jax.ShapeDtypeStruct((b, t, COL_Q), F32),
        scratch_shapes=[pltpu.VMEM((SUBLANES, LANES), F32), pltpu.VMEM((V_ROWS, Q_PER_KV * TQ), F32)],
        compiler_params=_cparams("arbitrary", "arbitrary", "arbitrary"),
        name="prompt_attention",
    )(q3, k_c, v_ct, ksel_aug, vsel_t, kwin, vwin_t, gates_t)


def _sample_cmp_kernel(q_ref, kc_ref, vc_ref, ocmp_ref, idx_ref, imp_ref, *, past_len, dec_seq):
    b = pl.program_id(0)
    nb = kc_ref.shape[2]
    rows = Q_PER_KV * SUBLANES
    t_row = lax.broadcasted_iota(jnp.int32, (rows, nb), 0) % SUBLANES
    blk = lax.broadcasted_iota(jnp.int32, (rows, nb), 1)
    valid = (blk + 1) * CMP_BLOCK - 1 <= past_len + t_row
    for g in range(N_KV):
        s = lax.dot_general(q_ref[0, g], kc_ref[0, g].astype(BF16), (((1,), (1,)), ((), ())),
                            preferred_element_type=F32) * ATT_SCALE
        s = jnp.where(valid, s, NEG_INF)
        e = jnp.where(valid, jnp.exp(s - jnp.max(s, axis=1, keepdims=True)), 0.0)
        den = jnp.sum(e, axis=1, keepdims=True)
        p = e / jnp.where(den > 0.0, den, 1.0)
        ocmp_ref[0, g] = jnp.dot(p.astype(BF16), vc_ref[0, g].astype(BF16), preferred_element_type=F32)
        imp = p[0:SUBLANES]
        for r in range(1, Q_PER_KV):
            imp = imp + p[r * SUBLANES:(r + 1) * SUBLANES]
        row0 = pl.multiple_of((b * N_KV + g) * SUBLANES, SUBLANES)
        imp_ref[pl.ds(row0, SUBLANES), :] = imp

    @pl.when(b == pl.num_programs(0) - 1)
    def _():
        n_rows = imp_ref.shape[0]
        lane = lax.broadcasted_iota(jnp.int32, (n_rows, nb), 1)
        own = (past_len + lax.broadcasted_iota(jnp.int32, (n_rows, nb), 0) % SUBLANES) // SEL_BLOCK
        score = jnp.where(lane < own, imp_ref[...], -1.0)
        col = lax.broadcasted_iota(jnp.int32, (n_rows, LANES), 1)
        picks = jnp.zeros((n_rows, LANES), jnp.int32)
        for i in range(N_SEL):
            mx = jnp.max(score, axis=1, keepdims=True)
            idx = jnp.min(jnp.where(score == mx, lane, nb), axis=1, keepdims=True)
            score = jnp.where(lane == idx, -2.0, score)
            picks = jnp.where(col == i, idx, picks)
        idx_ref[...] = picks


def _sample_cmp_select(q_rt, k_c, v_c, past_len, dec_seq):
    db, _, rows, _ = q_rt.shape
    nb = k_c.shape[2]
    spec4 = lambda r, c: pl.BlockSpec((1, N_KV, r, c), lambda i: (i, 0, 0, 0))
    n_rows = db * N_KV * SUBLANES
    return pl.pallas_call(
        functools.partial(_sample_cmp_kernel, past_len=past_len, dec_seq=dec_seq),
        grid=(db,),
        in_specs=[spec4(rows, HEAD_DIM), spec4(nb, HEAD_DIM), spec4(nb, HEAD_DIM)],
        out_specs=(spec4(rows, HEAD_DIM), pl.BlockSpec((n_rows, LANES), lambda i: (0, 0))),
        out_shape=(jax.ShapeDtypeStruct((db, N_KV, rows, HEAD_DIM), F32),
                   jax.ShapeDtypeStruct((n_rows, LANES), jnp.int32)),
        scratch_shapes=[pltpu.VMEM((n_rows, nb), F32)],
        compiler_params=_cparams("arbitrary"),
        name="sample_cmp_select",
    )(q_rt, k_c, v_c)


def _sample_attn_kernel(idx_ref, pt_ref, cache_ref, q_ref, snew_ref, wnew_ref, wstate_ref, ocmp_ref, gate_ref,
                        o_ref, kv_ref, sem_ref, *, dec_seq, n_pages):
    b = pl.program_id(0)
    n_b = pl.num_programs(0)
    kvw = 2 * HEAD_DIM
    blocks_per_page = PAGE_SIZE // SEL_BLOCK

    def block_id(bb, g, t, i):
        return idx_ref[((bb * N_KV + g) * dec_seq + t) * N_SEL + i]

    def slab_copy(bb, slot, g, t, i):
        page = pt_ref[bb * n_pages + block_id(bb, g, t, i) // blocks_per_page]
        return pltpu.make_async_copy(cache_ref.at[page, pl.ds(g * kvw, kvw), :], kv_ref.at[slot, g, t, i],
                                     sem_ref.at[slot])

    def for_all_slabs(fn):
        for g in range(N_KV):
            for t in range(dec_seq):
                for i in range(N_SEL):
                    fn(g, t, i)

    slot = b % 2

    @pl.when(b == 0)
    def _():
        for_all_slabs(lambda g, t, i: slab_copy(b, slot, g, t, i).start())

    @pl.when(b + 1 < n_b)
    def _():
        for_all_slabs(lambda g, t, i: slab_copy(b + 1, 1 - slot, g, t, i).start())

    rows = dec_seq * SUBLANES
    tok = lax.broadcasted_iota(jnp.int32, (rows, 1), 0) // SUBLANES
    n_state = wstate_ref.shape[2]

    def attend(q, k_t, v_t, valid):
        s = jnp.dot(q, k_t.astype(BF16), preferred_element_type=F32) * ATT_SCALE
        s = jnp.where(valid, s, NEG_INF)
        p = jnp.exp(s - jnp.max(s, axis=1, keepdims=True))
        den = jnp.sum(p, axis=1, keepdims=True)
        return lax.dot_general(p.astype(BF16), v_t.astype(BF16), (((1,), (1,)), ((), ())),
                               preferred_element_type=F32) / den

    o_win = []
    for g in range(N_KV):
        k_rows, v_rows = pl.ds(g * kvw, HEAD_DIM), pl.ds(g * kvw + HEAD_DIM, HEAD_DIM)
        k_t = jnp.concatenate([wstate_ref[0, k_rows, :], wnew_ref[0, k_rows, :]], axis=1)
        v_t = jnp.concatenate([wstate_ref[0, v_rows, :], wnew_ref[0, v_rows, :]], axis=1)
        lane = lax.broadcasted_iota(jnp.int32, (rows, n_state + LANES), 1)
        new_i = lane - n_state
        valid = ((lane < n_state) & (lane > tok)) | ((new_i >= 0) & (new_i <= tok) & (new_i < dec_seq))
        o_win.append(attend(q_ref[0, g], k_t, v_t, valid))

    for_all_slabs(lambda g, t, i: slab_copy(b, slot, g, t, i).wait())

    lane1 = lax.broadcasted_iota(jnp.int32, (1, PAGE_SIZE), 1)
    for g in range(N_KV):
        o_sel = []
        for t in range(dec_seq):
            k_parts = [kv_ref[slot, g, t, i, 0:HEAD_DIM, :] for i in range(N_SEL)]
            v_parts = [kv_ref[slot, g, t, i, HEAD_DIM:kvw, :] for i in range(N_SEL)]
            k_parts.append(snew_ref[0, g * kvw:g * kvw + HEAD_DIM, :])
            v_parts.append(snew_ref[0, g * kvw + HEAD_DIM:(g + 1) * kvw, :])
            halves = [lane1 // SEL_BLOCK == block_id(b, g, t, i) % blocks_per_page for i in range(N_SEL)]
            halves.append((lane1 <= t) & (lane1 < dec_seq))
            q = q_ref[0, g, t * SUBLANES:(t + 1) * SUBLANES, :]
            o_sel.append(attend(q, jnp.concatenate(k_parts, axis=1), jnp.concatenate(v_parts, axis=1),
                                jnp.concatenate(halves, axis=1)))
        o_sel = jnp.concatenate(o_sel, axis=0)
        o_ref[0, g] = (gate_ref[0, g, 0] * ocmp_ref[0, g] + gate_ref[0, g, 1] * o_sel
                       + gate_ref[0, g, 2] * o_win[g])


def _sample_attention(sel_idx, page_table, cache_t, q_tr, snew_t, wnew_t, wstate_t, ocmp_tr, gates_tr, dec_seq):
    db, n_pages = page_table.shape
    rows = dec_seq * SUBLANES
    n_state = wstate_t.shape[2]
    per_b4 = lambda i, *_: (i, 0, 0, 0)
    per_b3 = lambda i, *_: (i, 0, 0)
    grid_spec = pltpu.PrefetchScalarGridSpec(
        num_scalar_prefetch=2,
        grid=(db,),
        in_specs=[
            pl.BlockSpec(memory_space=pl.ANY),
            pl.BlockSpec((1, N_KV, rows, HEAD_DIM), per_b4),
            pl.BlockSpec((1, COL_KV, LANES), per_b3),
            pl.BlockSpec((1, COL_KV, LANES), per_b3),
            pl.BlockSpec((1, COL_KV, n_state), per_b3),
            pl.BlockSpec((1, N_KV, rows, HEAD_DIM), per_b4),
            pl.BlockSpec((1, N_KV, 3, rows, HEAD_DIM), lambda i, *_: (i, 0, 0, 0, 0)),
        ],
        out_specs=pl.BlockSpec((1, N_KV, rows, HEAD_DIM), per_b4),
        scratch_shapes=[pltpu.VMEM((2, N_KV, dec_seq, N_SEL, 2 * HEAD_DIM, PAGE_SIZE), F32),
                        pltpu.SemaphoreType.DMA((2,))],
    )
    return pl.pallas_call(
        functools.partial(_sample_attn_kernel, dec_seq=dec_seq, n_pages=n_pages),
        grid_spec=grid_spec,
        out_shape=jax.ShapeDtypeStruct((db, N_KV, rows, HEAD_DIM), F32),
        compiler_params=_cparams("arbitrary"),
        name="sample_attention",
    )(sel_idx, page_table.reshape(-1), cache_t, q_tr, snew_t, wnew_t, wstate_t, ocmp_tr, gates_tr)


ROUTE_E1, ROUTE_E2, ROUTE_W1, ROUTE_W2 = 0, 1, 2, 3


def _merge_kernel(conv_ref, att_ref, x_ref, gc_ref, ga_ref, wo_ref, ln2_ref, wr_ref, x1_ref, h3_ref, route_ref):
    mix = jnp.concatenate([_rms(conv_ref[...], gc_ref[...]), _rms(att_ref[...], ga_ref[...])], axis=1)
    x1 = x_ref[...] + jnp.dot(mix.astype(BF16), wo_ref[...], preferred_element_type=F32)
    x1_ref[...] = x1
    h = _rms(x1, ln2_ref[...])
    for j in range(ROW_TILES):
        h3_ref[:, j, :] = h[:, j * LANES:(j + 1) * LANES]

    logits = jnp.dot(h.astype(BF16), wr_ref[...], preferred_element_type=F32)
    lane = lax.broadcasted_iota(jnp.int32, logits.shape, 1)
    is_g = lane < N_GROUPS
    lg = jnp.where(is_g, logits, NEG_INF)
    mg = jnp.max(lg, axis=1, keepdims=True)
    sg = jnp.sum(jnp.where(is_g, jnp.exp(lg - mg), 0.0), axis=1, keepdims=True)
    grp = jnp.min(jnp.where(lg == mg, lane, LANES), axis=1, keepdims=True)
    p_top = 1.0 / sg
    in_grp = ((lane + (EXPERTS_PER_GROUP - N_GROUPS)) // EXPERTS_PER_GROUP) == grp + 1
    le = jnp.where(in_grp, logits, NEG_INF)
    ee = jnp.where(in_grp, jnp.exp(le - jnp.max(le, axis=1, keepdims=True)), 0.0)
    pe = jnp.where(in_grp, ee / jnp.sum(ee, axis=1, keepdims=True), -1.0)
    p1 = jnp.max(pe, axis=1, keepdims=True)
    i1 = jnp.min(jnp.where(pe == p1, lane, LANES), axis=1, keepdims=True)
    pe2 = jnp.where(lane == i1, -1.0, pe)
    p2 = jnp.max(pe2, axis=1, keepdims=True)
    i2 = jnp.min(jnp.where(pe2 == p2, lane, LANES), axis=1, keepdims=True)
    den = p1 + p2
    rec = jnp.where(lane == ROUTE_E1, (i1 - N_GROUPS).astype(F32), 0.0)
    rec = jnp.where(lane == ROUTE_E2, (i2 - N_GROUPS).astype(F32), rec)
    rec = jnp.where(lane == ROUTE_W1, p1 / den * p_top, rec)
    rec = jnp.where(lane == ROUTE_W2, p2 / den * p_top, rec)
    route_ref[...] = rec


def _merge(conv2d, att2d, x2d, g_conv, g_att, w_out_b, ln2, w_route_b, tm):
    n = x2d.shape[0]
    row = lambda i: (i, 0)
    const = lambda i: (0, 0)
    return pl.pallas_call(
        _merge_kernel,
        grid=(n // tm,),
        in_specs=[
            pl.BlockSpec((tm, CONV_CH), row),
            pl.BlockSpec((tm, COL_Q), row),
            pl.BlockSpec((tm, D_MODEL), row),
            pl.BlockSpec((1, CONV_CH), const),
            pl.BlockSpec((1, COL_Q), const),
            pl.BlockSpec((CONV_CH + COL_Q, D_MODEL), const),
            pl.BlockSpec((1, D_MODEL), const),
            pl.BlockSpec((D_MODEL, LANES), const),
        ],
        out_specs=(
            pl.BlockSpec((tm, D_MODEL), row),
            pl.BlockSpec((tm, ROW_TILES, LANES), lambda i: (i, 0, 0)),
            pl.BlockSpec((tm, LANES), row),
        ),
        out_shape=(
            jax.ShapeDtypeStruct((n, D_MODEL), F32),
            jax.ShapeDtypeStruct((n, ROW_TILES, LANES), F32),
            jax.ShapeDtypeStruct((n, LANES), F32),
        ),
        compiler_params=_cparams("arbitrary"),
        name="merge_route",
    )(conv2d, att2d, x2d, g_conv.reshape(1, -1), g_att.reshape(1, -1), w_out_b, ln2.reshape(1, -1), w_route_b)


META_W = 2 * LANES


def _rank_kernel(route_ref, dest_ref, meta_ref, e_ref, rank_ref, count_ref):
    i = pl.program_id(0)
    rt = route_ref.shape[0]

    @pl.when(i == 0)
    def _():
        count_ref[...] = jnp.zeros(count_ref.shape, F32)

    route_t = route_ref[...].T
    expert_id = lax.broadcasted_iota(jnp.int32, (N_EXPERTS, rt), 0).astype(F32)
    before = (lax.broadcasted_iota(jnp.int32, (rt, rt), 0) < lax.broadcasted_iota(jnp.int32, (rt, rt), 1))
    before = before.astype(BF16)
    ones = jnp.ones((rt, LANES), BF16)
    e_rows, rank_rows = [], []
    for k in range(TOP_K):
        e_k = route_t[ROUTE_E1 + k:ROUTE_E1 + k + 1, :]
        onehot = (expert_id == e_k).astype(F32)
        earlier = jnp.dot(onehot.astype(BF16), before, preferred_element_type=F32)
        seen = count_ref[...]
        seen_w = jnp.concatenate([seen] * (rt // LANES), axis=1)
        rank_rows.append(jnp.sum(onehot * (earlier + seen_w), axis=0, keepdims=True))
        e_rows.append(e_k)
        count_ref[...] = seen + jnp.dot(onehot.astype(BF16), ones, preferred_element_type=F32)
    e_ref[i] = jnp.concatenate(e_rows, axis=0)
    rank_ref[i] = jnp.concatenate(rank_rows, axis=0)

    @pl.when(i == pl.num_programs(0) - 1)
    def _():
        counts = jnp.concatenate([count_ref[...]] * (META_W // LANES), axis=1)
        padded = jnp.floor((counts + (MOE_ROWS - 1)) * (1.0 / MOE_ROWS)) * MOE_ROWS
        lane = lax.broadcasted_iota(jnp.int32, (1, META_W), 1)
        chunk_start = lane.astype(F32) * MOE_ROWS
        chunk_e = jnp.zeros((1, META_W), F32)
        ends = jnp.zeros((1, META_W), F32)
        end = jnp.zeros((1, META_W), F32)
        starts = []
        for ex in range(N_EXPERTS):
            starts.append(end[:, 0:rt])
            end = end + padded[ex:ex + 1, :]
            chunk_e = chunk_e + (end <= chunk_start).astype(F32)
            ends = jnp.where(lane == ex, end, ends)
        n_used = end * (1.0 / MOE_ROWS)
        chunk_e = jnp.minimum(chunk_e, float(N_EXPERTS - 1))
        row = lax.broadcasted_iota(jnp.int32, (SUBLANES, META_W), 0)
        meta = jnp.where(row == 0, chunk_e, jnp.where(row == 1, n_used, jnp.where(row == 2, ends, 0.0)))
        meta_ref[...] = meta.astype(jnp.int32)

        def place(ti, carry):
            e_t = e_ref[ti]
            dest = rank_ref[ti]
            for ex in range(N_EXPERTS):
                dest = dest + jnp.where(e_t == float(ex), starts[ex], 0.0)
            dest_ref[ti] = dest.astype(jnp.int32)
            return carry

        lax.fori_loop(0, pl.num_programs(0), place, 0)


def _rank(route, rt):
    n = route.shape[0]
    tiles = n // rt
    assert n * TOP_K // MOE_ROWS + N_EXPERTS <= META_W and rt <= META_W, "chunk table is one row of META_W lanes"
    whole = lambda i: (0, 0, 0)
    return pl.pallas_call(
        _rank_kernel,
        grid=(tiles,),
        in_specs=[pl.BlockSpec((rt, LANES), lambda i: (i, 0))],
        out_specs=(pl.BlockSpec((tiles, TOP_K, rt), whole), pl.BlockSpec((SUBLANES, META_W), lambda i: (0, 0))),
        out_shape=(jax.ShapeDtypeStruct((tiles, TOP_K, rt), jnp.int32),
                   jax.ShapeDtypeStruct((SUBLANES, META_W), jnp.int32)),
        scratch_shapes=[pltpu.VMEM((tiles, TOP_K, rt), F32), pltpu.VMEM((tiles, TOP_K, rt), F32),
                        pltpu.VMEM((N_EXPERTS, LANES), F32)],
        compiler_params=_cparams("arbitrary"),
        name="route_rank",
    )(route)


def _row_copies_wait(src_ref, dst_ref, sem, n_rows):
    pltpu.make_async_copy(src_ref.at[pl.ds(0, n_rows)], dst_ref.at[pl.ds(0, n_rows)], sem).wait()


def _scatter_kernel(meta_ref, dest_ref, h3_ref, xs_ref, zero_ref, sem_ref):
    i = pl.program_id(0)
    rt = h3_ref.shape[0]
    n_chunks = xs_ref.shape[0] // MOE_ROWS

    @pl.when(i == 0)
    def _():
        zero_ref[...] = jnp.zeros(zero_ref.shape, F32)

        def zero_chunk(first_row):
            return pltpu.make_async_copy(zero_ref, xs_ref.at[pl.ds(first_row, MOE_ROWS)], sem_ref.at[1])

        def fills(act):
            prev = 0
            for ex in range(N_EXPERTS):
                end = meta_ref[2, ex]

                @pl.when(end > prev)
                def _():
                    act(zero_chunk(end - MOE_ROWS))
                prev = end
            for c in range(n_chunks - N_EXPERTS, n_chunks):
                @pl.when(c >= meta_ref[1, 0])
                def _():
                    act(zero_chunk(c * MOE_ROWS))

        fills(lambda copy: copy.start())
        fills(lambda copy: copy.wait())

    def issue(t, carry):
        for k in range(TOP_K):
            pltpu.make_async_copy(h3_ref.at[t], xs_ref.at[dest_ref[0, k, t]], sem_ref.at[0]).start()
        return carry

    lax.fori_loop(0, rt, issue, 0, unroll=8)
    _row_copies_wait(h3_ref, h3_ref, sem_ref.at[0], rt)
    _row_copies_wait(h3_ref, h3_ref, sem_ref.at[0], rt)


def _scatter_rows(h3, dest, meta, rt):
    n = h3.shape[0]
    n_slots = (n * TOP_K // MOE_ROWS + N_EXPERTS) * MOE_ROWS
    grid_spec = pltpu.PrefetchScalarGridSpec(
        num_scalar_prefetch=1,
        grid=(n // rt,),
        in_specs=[pl.BlockSpec((1, TOP_K, rt), lambda i, meta: (i, 0, 0), memory_space=pltpu.SMEM),
                  pl.BlockSpec((rt, ROW_TILES, LANES), lambda i, meta: (i, 0, 0))],
        out_specs=pl.BlockSpec(memory_space=pl.ANY),
        scratch_shapes=[pltpu.VMEM((MOE_ROWS, ROW_TILES, LANES), F32), pltpu.SemaphoreType.DMA((2,))],
    )
    return pl.pallas_call(
        _scatter_kernel,
        grid_spec=grid_spec,
        out_shape=jax.ShapeDtypeStruct((n_slots, ROW_TILES, LANES), F32),
        compiler_params=_cparams("arbitrary"),
        name="moe_scatter_rows",
    )(meta, dest, h3)


def _expert_kernel(meta_ref, xs_ref, wg_ref, wu_ref, wd_ref, ys_ref, wg_b, wu_b, wd_b):
    c = pl.program_id(0)
    in_use = c < meta_ref[1, 0]

    @pl.when(jnp.logical_not(in_use))
    def _():
        ys_ref[...] = jnp.zeros(ys_ref.shape, F32)

    @pl.when(in_use & ((c == 0) | (meta_ref[0, c] != meta_ref[0, jnp.maximum(c - 1, 0)])))
    def _():
        wg_b[...] = wg_ref[0].astype(BF16)
        wu_b[...] = wu_ref[0].astype(BF16)
        wd_b[...] = wd_ref[0].astype(BF16)

    @pl.when(in_use)
    def _():
        x = jnp.concatenate([xs_ref[:, j, :] for j in range(ROW_TILES)], axis=1).astype(BF16)
        gate = jnp.dot(x, wg_b[...], preferred_element_type=F32)
        up = jnp.dot(x, wu_b[...], preferred_element_type=F32)
        act = (gate * jax.nn.sigmoid(gate) * up).astype(BF16)
        y = jnp.dot(act, wd_b[...], preferred_element_type=F32)
        for j in range(ROW_TILES):
            ys_ref[:, j, :] = y[:, j * LANES:(j + 1) * LANES]


def _expert_mlp(xs, meta, w_g, w_u, w_d):
    n_chunks = xs.shape[0] // MOE_ROWS
    rows = lambda c, meta: (c, 0, 0)
    expert = lambda c, meta: (meta[0, jnp.minimum(c, meta[1, 0] - 1)], 0, 0)
    grid_spec = pltpu.PrefetchScalarGridSpec(
        num_scalar_prefetch=1,
        grid=(n_chunks,),
        in_specs=[
            pl.BlockSpec((MOE_ROWS, ROW_TILES, LANES), rows),
            pl.BlockSpec((1, D_MODEL, D_EXPERT), expert),
            pl.BlockSpec((1, D_MODEL, D_EXPERT), expert),
            pl.BlockSpec((1, D_EXPERT, D_MODEL), expert),
        ],
        out_specs=pl.BlockSpec((MOE_ROWS, ROW_TILES, LANES), rows),
        scratch_shapes=[pltpu.VMEM((D_MODEL, D_EXPERT), BF16), pltpu.VMEM((D_MODEL, D_EXPERT), BF16),
                        pltpu.VMEM((D_EXPERT, D_MODEL), BF16)],
    )
    return pl.pallas_call(
        _expert_kernel,
        grid_spec=grid_spec,
        out_shape=jax.ShapeDtypeStruct(xs.shape, F32),
        compiler_params=_cparams("arbitrary"),
        name="expert_mlp",
    )(meta, xs, w_g, w_u, w_d)


def _combine_kernel(dest_ref, nxt_ref, x1_ref, route_ref, ln_ref, ys_ref, o_ref, buf_ref, sem_ref):
    i = pl.program_id(0)
    n_tiles = pl.num_programs(0)
    rt = x1_ref.shape[0]

    def issue(table_ref, slot):
        def body(t, carry):
            for k in range(TOP_K):
                pltpu.make_async_copy(ys_ref.at[table_ref[0, k, t]], buf_ref.at[slot, k, t], sem_ref.at[slot]).start()
            return carry
        lax.fori_loop(0, rt, body, 0, unroll=8)

    slot = i % 2

    @pl.when(i == 0)
    def _():
        issue(dest_ref, slot)

    @pl.when(i + 1 < n_tiles)
    def _():
        issue(nxt_ref, 1 - slot)

    for k in range(TOP_K):
        _row_copies_wait(ys_ref, buf_ref.at[slot, k], sem_ref.at[slot], rt)
    route = route_ref[...]
    moe = None
    for k in range(TOP_K):
        y = jnp.concatenate([buf_ref[slot, k, :, j, :] for j in range(ROW_TILES)], axis=1)
        term = y * route[:, ROUTE_W1 + k:ROUTE_W1 + k + 1]
        moe = term if moe is None else moe + term
    o_ref[...] = _rms(x1_ref[...] + moe, ln_ref[...])


def _combine(x1, ys, dest, route, ln_final, rt):
    n = x1.shape[0]
    tiles = n // rt
    row = lambda i: (i, 0)
    table = lambda f: pl.BlockSpec((1, TOP_K, rt), f, memory_space=pltpu.SMEM)
    return pl.pallas_call(
        _combine_kernel,
        grid=(tiles,),
        in_specs=[
            table(lambda i: (i, 0, 0)),
            table(lambda i: (jnp.minimum(i + 1, tiles - 1), 0, 0)),
            pl.BlockSpec((rt, D_MODEL), row),
            pl.BlockSpec((rt, LANES), row),
            pl.BlockSpec((1, D_MODEL), lambda i: (0, 0)),
            pl.BlockSpec(memory_space=pl.ANY),
        ],
        out_specs=pl.BlockSpec((rt, D_MODEL), row),
        out_shape=jax.ShapeDtypeStruct((n, D_MODEL), F32),
        scratch_shapes=[pltpu.VMEM((2, TOP_K, rt, ROW_TILES, LANES), F32), pltpu.SemaphoreType.DMA((2,))],
        compiler_params=_cparams("arbitrary"),
        name="moe_combine_norm",
    )(dest, dest, x1, route, ln_final.reshape(1, -1), ys)


def _ffn(conv2d, att2d, x2d, g_conv, g_att, w_out_b, ln2, w_route_b, w_g, w_u, w_d, ln_final, tm):
    x1, h3, route = _merge(conv2d, att2d, x2d, g_conv, g_att, w_out_b, ln2, w_route_b, tm)
    rt = min(tm, MOE_ROWS)
    dest, meta = _rank(route, rt)
    ys = _expert_mlp(_scatter_rows(h3, dest, meta, rt), meta, w_g, w_u, w_d)
    return _combine(x1, ys, dest, route, ln_final, rt)


def kernel(x_prompt, x_sample, cache_cmp_kv, cache_sel_kv, state_win_kv, state_conv, page_table, ln1, w_in, conv_dw_w, conv_dw_b, conv_ln_g, conv_ln_b, cmp_pos_emb, w_cmp_k1, w_cmp_k2, w_cmp_v1, w_cmp_v2, out_norm_conv, out_norm_att, w_out, ln2, w_router_group, w_router_expert, w_exp_gate, w_exp_up, w_exp_down, ln_final):
    depth = ln1.shape[0]
    assert depth == 1, "single-layer step"
    b, t, _ = x_prompt.shape
    db, ds, _ = x_sample.shape
    n_phys = cache_cmp_kv.shape[1]
    n_pages = page_table.shape[1]
    past = n_pages * PAGE_SIZE
    win_rows = state_win_kv.shape[2]
    assert ds < CMP_BLOCK and ds <= SUBLANES and past % SEL_BLOCK == 0 and past // SEL_BLOCK >= N_SEL
    assert win_rows == WINDOW and past >= WINDOW and t % KC == 0 and t >= WIN_KEYS

    w_in_b = jnp.pad(w_in[0], ((0, 0), (0, D_IN_PAD - D_IN))).astype(BF16)
    w_out_b = w_out[0].astype(BF16)
    w_route_b = jnp.pad(jnp.concatenate([w_router_group[0], w_router_expert[0]], axis=1),
                        ((0, 0), (0, LANES - N_GROUPS - N_EXPERTS))).astype(BF16)
    cmp_w = _cmp_weights(cmp_pos_emb[0], w_cmp_k1[0], w_cmp_k2[0], w_cmp_v1[0], w_cmp_v2[0])
    conv_w = (conv_dw_w[0], conv_dw_b[0], conv_ln_g[0], conv_ln_b[0])
    ffn_w = (out_norm_conv[0], out_norm_att[0], w_out_b, ln2[0], w_route_b, w_exp_gate[0], w_exp_up[0],
             w_exp_down[0], ln_final)

    xp2 = x_prompt.reshape(b * t, D_MODEL)
    c_p, s_p = _rope_tables(jnp.arange(t))
    (a_p, q_p, ckv_p, _, _, gate_p, ckv_pt, skv_pt, wkv_pt, ksel_aug, kwin_p, vsel_t, vwin_t) = _project(
        xp2, ln1[0], w_in_b, c_p, s_p, 512)
    a_p3 = a_p.reshape(b, t, CONV_CH)
    conv_p = _conv_module(a_p3, jnp.zeros((b, HIST_ROWS, CONV_CH), F32), *conv_w, 512)
    kc_p, vc_p = _compress_dense(ckv_p.reshape(b, t, COL_KV), cmp_w)
    gates_t = jnp.transpose(gate_p[:, :COL_GATE].reshape(b, t, N_KV, Q_PER_KV, 3), (0, 2, 4, 3, 1))
    att_p = _prompt_attention(q_p.reshape(b, t, COL_Q), kc_p, jnp.swapaxes(vc_p, 2, 3), ksel_aug,
                              vsel_t, kwin_p, vwin_t, gates_t.reshape(b, N_KV, 3 * Q_PER_KV, t))
    y_p = _ffn(conv_p.reshape(b * t, CONV_CH), att_p.reshape(b * t, COL_Q), xp2, *ffn_w, 256)

    n_s = db * ds
    xs2 = x_sample.reshape(n_s, D_MODEL)
    c_s, s_s = _rope_tables(jnp.tile(past + jnp.arange(ds), db))
    a_s, q_s, ckv_s, skv_s, wkv_s, gate_s = _project(xs2, ln1[0], w_in_b, c_s, s_s, n_s)[:6]
    a_s3 = a_s.reshape(db, ds, CONV_CH)
    hist_s = jnp.pad(state_conv[0], ((0, 0), (HIST_ROWS - (CONV_W - 1), 0), (0, 0)))
    conv_s = _conv_module(a_s3, hist_s, *conv_w, ds)
    rows_minor = lambda a, n, r: jnp.swapaxes(a.reshape(n, r, COL_KV), 1, 2)
    kc_s, vc_s = _compress_paged(rows_minor(cache_cmp_kv[0], n_phys, PAGE_SIZE), page_table, cmp_w)

    q5 = q_s.reshape(db, ds, N_KV, Q_PER_KV, HEAD_DIM)
    pad_tok = SUBLANES - ds
    pad_head = SUBLANES - Q_PER_KV
    q_rt = jnp.pad(jnp.transpose(q5, (0, 2, 3, 1, 4)), ((0, 0), (0, 0), (0, 0), (0, pad_tok), (0, 0)))
    q_rt = q_rt.reshape(db, N_KV, Q_PER_KV * SUBLANES, HEAD_DIM)
    q_tr = jnp.pad(jnp.transpose(q5, (0, 2, 1, 3, 4)), ((0, 0), (0, 0), (0, 0), (0, pad_head), (0, 0)))
    q_tr = q_tr.reshape(db, N_KV, ds * SUBLANES, HEAD_DIM)
    g5 = gate_s[:, :COL_GATE].reshape(db, ds, N_KV, Q_PER_KV, 3)
    g_tr = jnp.pad(jnp.transpose(g5, (0, 2, 4, 1, 3)), ((0, 0),) * 4 + ((0, pad_head),))
    g_tr = jnp.broadcast_to(g_tr.reshape(db, N_KV, 3, ds * SUBLANES)[..., None], (db, N_KV, 3, ds * SUBLANES, HEAD_DIM))
    ocmp, picks = _sample_cmp_select(q_rt, kc_s, vc_s, past, ds)
    sel_idx = picks.reshape(db, N_KV, SUBLANES, LANES)[:, :, :ds, :N_SEL].reshape(-1)
    ocmp_tr = jnp.transpose(ocmp.reshape(db, N_KV, Q_PER_KV, SUBLANES, HEAD_DIM)[:, :, :, :ds], (0, 1, 3, 2, 4))
    ocmp_tr = jnp.pad(ocmp_tr, ((0, 0), (0, 0), (0, 0), (0, pad_head), (0, 0)))
    ocmp_tr = ocmp_tr.reshape(db, N_KV, ds * SUBLANES, HEAD_DIM)
    new_t = lambda kv: jnp.pad(rows_minor(kv, db, ds), ((0, 0), (0, 0), (0, LANES - ds)))
    o_s = _sample_attention(sel_idx, page_table, rows_minor(cache_sel_kv[0], n_phys, PAGE_SIZE), q_tr,
                            new_t(skv_s), new_t(wkv_s), rows_minor(state_win_kv[0], db, win_rows),
                            ocmp_tr, g_tr, ds)
    att_s = o_s.reshape(db, N_KV, ds, SUBLANES, HEAD_DIM)[:, :, :, :Q_PER_KV]
    att_s = jnp.transpose(att_s, (0, 2, 1, 3, 4)).reshape(n_s, COL_Q)
    y_s = _ffn(conv_s.reshape(n_s, CONV_CH), att_s, xs2, *ffn_w, n_s)

    kv6 = lambda kv, bb, tt: kv.reshape(1, bb, tt, N_KV, 2, HEAD_DIM)
    kv6_t = lambda kv_t: jnp.swapaxes(kv_t, 1, 2).reshape(1, b, kv_t.shape[2], N_KV, 2, HEAD_DIM)
    new_win_s = jnp.concatenate([state_win_kv, kv6(wkv_s, db, ds)], axis=2)[:, :, ds:]
    new_conv_s = jnp.concatenate([state_conv[0], a_s3], axis=1)[None, :, ds:]
    return (y_p.reshape(b, t, D_MODEL), y_s.reshape(db, ds, D_MODEL),
            kv6_t(ckv_pt), kv6(ckv_s, db, ds), kv6_t(skv_pt), kv6(skv_s, db, ds),
            kv6_t(wkv_pt[:, :, t - min(WINDOW, t):]), new_win_s,
            a_p3[None, :, t - (CONV_W - 1):], new_conv_s)
```

```python
import functools

import jax
import jax.numpy as jnp
from jax import lax
from jax.experimental import pallas as pl
from jax.experimental.pallas import tpu as pltpu

D_MODEL = 1024
CONV_CH = 512
CONV_W = 31
N_HEADS = 8
HEAD_DIM = 64
N_KV = 2
Q_PER_KV = N_HEADS // N_KV
ROPE_DIM = HEAD_DIM // 4
ROPE_THETA = 500000.0
CMP_BLOCK = 64
SEL_BLOCK = CMP_BLOCK
N_SEL = 16
WINDOW = 512
CMP_HID = 2 * HEAD_DIM
COL_Q = N_HEADS * HEAD_DIM
COL_KV = 2 * N_KV * HEAD_DIM
COL_GATE = 3 * N_HEADS
D_IN = 2 * CONV_CH + COL_Q + 3 * COL_KV + COL_GATE
N_GROUPS = 4
EXPERTS_PER_GROUP = 8
N_EXPERTS = N_GROUPS * EXPERTS_PER_GROUP
TOP_K = 2
D_EXPERT = 512
PAGE_SIZE = 128
RMS_EPS = 1e-6
LN_EPS = 1e-5
NEG_INF = -1e30
ATT_SCALE = HEAD_DIM ** -0.5

LANES = 128
SUBLANES = 8
VMEM_LIMIT_BYTES = 56 * 1024 * 1024

D_IN_PAD = ((D_IN + LANES - 1) // LANES) * LANES
COL_GATE_OFF = 2 * CONV_CH + COL_Q + 3 * COL_KV
HIST_ROWS = 32
CONV_ROWS = 32
MOE_ROWS = 256
ROW_TILES = D_MODEL // LANES

BF16 = jnp.bfloat16
F32 = jnp.float32


def _cparams(*sem):
    return pltpu.CompilerParams(dimension_semantics=sem, vmem_limit_bytes=VMEM_LIMIT_BYTES)


def _rms(x, g):
    return x * lax.rsqrt(jnp.mean(x * x, axis=-1, keepdims=True) + RMS_EPS) * g


def _tiles_to_rows(tile_ref):
    n = tile_ref.shape[0] // ROW_TILES
    return jnp.concatenate([tile_ref[pl.ds(j, n, stride=ROW_TILES), :] for j in range(ROW_TILES)], axis=1)


def _rows_to_tiles(tile_ref, rows):
    for j in range(ROW_TILES):
        tile_ref[pl.ds(j, rows.shape[0], stride=ROW_TILES), :] = rows[:, j * LANES:(j + 1) * LANES]


def _rope_tables(pos):
    half = ROPE_DIM // 2
    inv = ROPE_THETA ** (-jnp.arange(half, dtype=F32) / half)
    ang = pos.astype(F32)[:, None] * inv
    cos, sin = jnp.cos(ang), jnp.sin(ang)
    m = jnp.arange(LANES) % HEAD_DIM
    idx = m % half
    c = jnp.where(m < ROPE_DIM, cos[:, idx], 1.0)
    s = jnp.where(m < half, -sin[:, idx], jnp.where(m < ROPE_DIM, sin[:, idx], 0.0))
    return c.astype(F32), s.astype(F32)


def _rope(v, c, s, first_half):
    w = v.shape[1]
    half = ROPE_DIM // 2
    partner = jnp.where(first_half, pltpu.roll(v, w - half, axis=1), pltpu.roll(v, half, axis=1))
    return v * c + partner * s


def _proj_kernel(x_ref, ln_ref, w_ref, c_ref, s_ref, a_ref, q_ref, ckv_ref, skv_ref, wkv_ref, gate_ref,
                 ckvt_ref, skvt_ref, wkvt_ref, ksel_ref, kwin_ref, vsel_ref, vwin_ref, *, t_tiles):
    x = x_ref[...]
    xn = _rms(x, ln_ref[...])
    p = jnp.dot(xn.astype(BF16), w_ref[...], preferred_element_type=F32)
    a_ref[...] = p[:, :CONV_CH] * jax.nn.sigmoid(p[:, CONV_CH:2 * CONV_CH])

    c128, s128 = c_ref[...], s_ref[...]
    tm = x.shape[0]
    lane_q = lax.broadcasted_iota(jnp.int32, (tm, COL_Q), 1)
    cq = jnp.concatenate([c128] * (COL_Q // LANES), axis=1)
    sq = jnp.concatenate([s128] * (COL_Q // LANES), axis=1)
    o = 2 * CONV_CH
    q = _rope(p[:, o:o + COL_Q], cq, sq, (lane_q % HEAD_DIM) < ROPE_DIM // 2)
    q_ref[...] = q.astype(q_ref.dtype)
    o += COL_Q

    lane_kv = lax.broadcasted_iota(jnp.int32, (tm, COL_KV), 1)
    is_k = (lane_kv % (2 * HEAD_DIM)) < HEAD_DIM
    ckv = jnp.where(is_k, jnp.concatenate([c128] * (COL_KV // LANES), axis=1), 1.0)
    skv = jnp.where(is_k, jnp.concatenate([s128] * (COL_KV // LANES), axis=1), 0.0)
    first_kv = (lane_kv % HEAD_DIM) < ROPE_DIM // 2
    kvs = []
    for ref, ref_t in ((ckv_ref, ckvt_ref), (skv_ref, skvt_ref), (wkv_ref, wkvt_ref)):
        kv = _rope(p[:, o:o + COL_KV], ckv, skv, first_kv)
        kv_t = kv.T
        ref[...] = kv
        ref_t[0] = kv_t
        kvs.append((kv, kv_t))
        o += COL_KV
    gate_ref[...] = jax.nn.sigmoid(p[:, o:o + LANES])

    kvw = 2 * HEAD_DIM
    lane = lax.broadcasted_iota(jnp.int32, (tm, LANES), 1)
    pos = (pl.program_id(0) % t_tiles) * tm + lax.broadcasted_iota(jnp.int32, (tm, LANES), 0)
    blk = pos // SEL_BLOCK
    is_key = lane < HEAD_DIM
    ones_rows = (lax.broadcasted_iota(jnp.int32, (V_ROWS - HEAD_DIM, tm), 0) == 0).astype(F32)
    (skv_v, skv_t), (wkv_v, wkv_t) = kvs[1], kvs[2]
    for g in range(N_KV):
        low = jnp.where(is_key, skv_v[:, g * kvw:(g + 1) * kvw], (blk == lane - HEAD_DIM).astype(F32))
        high = jnp.where(is_key, (blk == lane + HEAD_DIM).astype(F32), 0.0)
        ksel_ref[0, g] = jnp.concatenate([low, high], axis=1).astype(BF16)
        kwin_ref[0, g] = wkv_v[:, g * kvw:g * kvw + HEAD_DIM].astype(BF16)
        v_rows = slice(g * kvw + HEAD_DIM, (g + 1) * kvw)
        vsel_ref[0, g] = jnp.concatenate([skv_t[v_rows], ones_rows], axis=0).astype(BF16)
        vwin_ref[0, g] = jnp.concatenate([wkv_t[v_rows], ones_rows], axis=0).astype(BF16)


def _project(x2d, ln, w_pad, c_tab, s_tab, tm):
    n = x2d.shape[0]
    t = c_tab.shape[0]
    t_tiles = t // tm
    assert t // SEL_BLOCK <= KAUG - HEAD_DIM - HEAD_DIM, "one-hot block ids fit the augmented key"
    row = lambda i: (i, 0)
    tab = lambda i: (i % t_tiles, 0)
    const = lambda i: (0, 0)
    kv_t = jax.ShapeDtypeStruct((n // t, COL_KV, t), F32)
    kv_t_spec = pl.BlockSpec((1, COL_KV, tm), lambda i: (i // t_tiles, 0, i % t_tiles))
    keys = lambda w: (jax.ShapeDtypeStruct((n // t, N_KV, t, w), BF16),
                      pl.BlockSpec((1, N_KV, tm, w), lambda i: (i // t_tiles, 0, i % t_tiles, 0)))
    v_t = (jax.ShapeDtypeStruct((n // t, N_KV, V_ROWS, t), BF16),
           pl.BlockSpec((1, N_KV, V_ROWS, tm), lambda i: (i // t_tiles, 0, 0, i % t_tiles)))
    out_shape = (
        jax.ShapeDtypeStruct((n, CONV_CH), F32),
        jax.ShapeDtypeStruct((n, COL_Q), BF16),
        jax.ShapeDtypeStruct((n, COL_KV), F32),
        jax.ShapeDtypeStruct((n, COL_KV), F32),
        jax.ShapeDtypeStruct((n, COL_KV), F32),
        jax.ShapeDtypeStruct((n, LANES), F32),
        kv_t, kv_t, kv_t,
        keys(KAUG)[0], keys(HEAD_DIM)[0], v_t[0], v_t[0],
    )
    return pl.pallas_call(
        functools.partial(_proj_kernel, t_tiles=t_tiles),
        grid=(n // tm,),
        in_specs=[
            pl.BlockSpec((tm, D_MODEL), row),
            pl.BlockSpec((1, D_MODEL), const),
            pl.BlockSpec((D_MODEL, D_IN_PAD), const),
            pl.BlockSpec((tm, LANES), tab),
            pl.BlockSpec((tm, LANES), tab),
        ],
        out_specs=(
            pl.BlockSpec((tm, CONV_CH), row),
            pl.BlockSpec((tm, COL_Q), row),
            pl.BlockSpec((tm, COL_KV), row),
            pl.BlockSpec((tm, COL_KV), row),
            pl.BlockSpec((tm, COL_KV), row),
            pl.BlockSpec((tm, LANES), row),
            kv_t_spec, kv_t_spec, kv_t_spec,
            keys(KAUG)[1], keys(HEAD_DIM)[1], v_t[1], v_t[1],
        ),
        out_shape=out_shape,
        compiler_params=_cparams("arbitrary"),
        name="in_proj",
    )(x2d, ln.reshape(1, D_MODEL), w_pad, c_tab, s_tab)


def _conv_kernel(a_ref, hist_ref, w_ref, b_ref, g_ref, beta_ref, o_ref, sh_ref):
    tt = a_ref.shape[1]
    ext_ref = sh_ref.at[0]

    @pl.when(pl.program_id(1) == 0)
    def _():
        ext_ref[0:HIST_ROWS, :] = hist_ref[0]

    ext_ref[HIST_ROWS:HIST_ROWS + tt, :] = a_ref[0]
    span = HIST_ROWS + tt - SUBLANES
    for s in range(1, SUBLANES):
        sh_ref[s, 0:span, :] = ext_ref[s:s + span, :]

    lead = HIST_ROWS - (CONV_W - 1)
    rc = min(tt, CONV_ROWS)

    def chunk(i, carry):
        r0 = pl.multiple_of(i * rc, rc)
        acc = jnp.broadcast_to(b_ref[...], (rc, CONV_CH))
        for k in range(CONV_W):
            a, s = divmod(lead + k, SUBLANES)
            acc = acc + w_ref[k:k + 1, :] * sh_ref[s, pl.ds(r0 + a * SUBLANES, rc), :]
        o_ref[0, pl.ds(r0, rc), :] = acc
        return carry

    lax.fori_loop(0, tt // rc, chunk, 0)
    acc = o_ref[0]
    mu = jnp.mean(acc, axis=-1, keepdims=True)
    var = jnp.mean(jnp.square(acc - mu), axis=-1, keepdims=True)
    y = (acc - mu) * lax.rsqrt(var + LN_EPS) * g_ref[...] + beta_ref[...]
    o_ref[0] = y * jax.nn.sigmoid(y)
    carry = ext_ref[tt:tt + HIST_ROWS, :]
    ext_ref[0:HIST_ROWS, :] = carry


def _conv_module(a3d, hist, dw_w, dw_b, ln_g, ln_b, tt):
    b, t, _ = a3d.shape
    w_pad = jnp.pad(dw_w, ((0, HIST_ROWS - CONV_W), (0, 0)))
    vec = lambda i, j: (0, 0)
    return pl.pallas_call(
        _conv_kernel,
        grid=(b, t // tt),
        in_specs=[
            pl.BlockSpec((1, tt, CONV_CH), lambda i, j: (i, j, 0)),
            pl.BlockSpec((1, HIST_ROWS, CONV_CH), lambda i, j: (i, 0, 0)),
            pl.BlockSpec((HIST_ROWS, CONV_CH), vec),
            pl.BlockSpec((1, CONV_CH), vec),
            pl.BlockSpec((1, CONV_CH), vec),
            pl.BlockSpec((1, CONV_CH), vec),
        ],
        out_specs=pl.BlockSpec((1, tt, CONV_CH), lambda i, j: (i, j, 0)),
        out_shape=jax.ShapeDtypeStruct((b, t, CONV_CH), F32),
        scratch_shapes=[pltpu.VMEM((SUBLANES, HIST_ROWS + tt, CONV_CH), F32)],
        compiler_params=_cparams("arbitrary", "arbitrary"),
        name="conv_module",
    )(a3d, hist, w_pad, dw_b.reshape(1, -1), ln_g.reshape(1, -1), ln_b.reshape(1, -1))


L_GROUP = 2 * LANES // HEAD_DIM


def _compress_rows(x_refs, pe_ref, wk1_ref, wk2_ref, wv1_ref, wv2_ref, nb, pitch=CMP_BLOCK):
    hk = jnp.zeros((N_KV * nb, CMP_HID), F32)
    hv = jnp.zeros((N_KV * nb, CMP_HID), F32)
    for j in range(CMP_BLOCK // L_GROUP):
        parts_k, parts_v = [], []
        for x_ref in x_refs:
            xs = [x_ref[pl.ds(j * L_GROUP + i, nb, stride=pitch), :] + pe_ref[j * L_GROUP + i:j * L_GROUP + i + 1, :]
                  for i in range(L_GROUP)]
            parts_k.append(jnp.concatenate([x[:, :HEAD_DIM] for x in xs], axis=1))
            parts_v.append(jnp.concatenate([x[:, HEAD_DIM:] for x in xs], axis=1))
        xk = jnp.concatenate(parts_k, axis=0).astype(BF16)
        xv = jnp.concatenate(parts_v, axis=0).astype(BF16)
        rows = slice(j * L_GROUP * HEAD_DIM, (j + 1) * L_GROUP * HEAD_DIM)
        hk = hk + jnp.dot(xk, wk1_ref[rows, :], preferred_element_type=F32)
        hv = hv + jnp.dot(xv, wv1_ref[rows, :], preferred_element_type=F32)
    kc = jnp.dot((hk * jax.nn.sigmoid(hk)).astype(BF16), wk2_ref[...], preferred_element_type=F32)
    vc = jnp.dot((hv * jax.nn.sigmoid(hv)).astype(BF16), wv2_ref[...], preferred_element_type=F32)
    return kc, vc


def _compress_dense_kernel(*refs):
    x_refs, (pe_ref, wk1_ref, wk2_ref, wv1_ref, wv2_ref, kc_ref, vc_ref) = refs[:N_KV], refs[N_KV:]
    nb = x_refs[0].shape[1] // CMP_BLOCK
    kc, vc = _compress_rows([x.at[0] for x in x_refs], pe_ref, wk1_ref, wk2_ref, wv1_ref, wv2_ref, nb)
    for g in range(N_KV):
        kc_ref[0, g] = kc[g * nb:(g + 1) * nb]
        vc_ref[0, g] = vc[g * nb:(g + 1) * nb]


def _cmp_weight_specs():
    const = lambda *_: (0, 0)
    return [
        pl.BlockSpec((CMP_BLOCK, 2 * HEAD_DIM), const),
        pl.BlockSpec((CMP_BLOCK * HEAD_DIM, CMP_HID), const),
        pl.BlockSpec((CMP_HID, HEAD_DIM), const),
        pl.BlockSpec((CMP_BLOCK * HEAD_DIM, CMP_HID), const),
        pl.BlockSpec((CMP_HID, HEAD_DIM), const),
    ]


def _cmp_weights(pos_emb, w_k1, w_k2, w_v1, w_v2):
    pe = pos_emb.reshape(CMP_BLOCK, 2 * HEAD_DIM)
    return (pe, w_k1.reshape(-1, CMP_HID).astype(BF16), w_k2.astype(BF16),
            w_v1.reshape(-1, CMP_HID).astype(BF16), w_v2.astype(BF16))


def _compress_dense(kv3d, cmp_w):
    b, t, _ = kv3d.shape
    nb = t // CMP_BLOCK
    out = jax.ShapeDtypeStruct((b, N_KV, nb, HEAD_DIM), F32)
    ospec = pl.BlockSpec((1, N_KV, nb, HEAD_DIM), lambda i: (i, 0, 0, 0))
    return pl.pallas_call(
        _compress_dense_kernel,
        grid=(b,),
        in_specs=[pl.BlockSpec((1, t, 2 * HEAD_DIM), functools.partial(lambda g, i: (i, 0, g), g))
                  for g in range(N_KV)] + _cmp_weight_specs(),
        out_specs=(ospec, ospec),
        out_shape=(out, out),
        compiler_params=_cparams("arbitrary"),
        name="compress_prompt",
    )(*([kv3d] * N_KV), *cmp_w)


PAGES_PER_STEP = 64
BLOCK_PITCH = CMP_BLOCK + SUBLANES


def _compress_paged_kernel(pt_ref, cache_ref, pe_ref, wk1_ref, wk2_ref, wv1_ref, wv2_ref, kc_ref, vc_ref,
                           raw_ref, rows_ref, sem_ref):
    step = pl.program_id(0)
    n_steps = pl.num_programs(0)
    nb = PAGES_PER_STEP * PAGE_SIZE // CMP_BLOCK
    kvw = 2 * HEAD_DIM

    def page_copy(s, slot, p):
        return pltpu.make_async_copy(cache_ref.at[pt_ref[s * PAGES_PER_STEP + p]], raw_ref.at[slot, p], sem_ref.at[slot])

    def issue(s, slot):
        for p in range(PAGES_PER_STEP):
            page_copy(s, slot, p).start()

    slot = step % 2

    @pl.when(step == 0)
    def _():
        issue(step, slot)

    @pl.when(step + 1 < n_steps)
    def _():
        issue(step + 1, 1 - slot)

    for p in range(PAGES_PER_STEP):
        page_copy(step, slot, p).wait()

    for p in range(PAGES_PER_STEP):
        page = raw_ref[slot, p].T
        for g in range(N_KV):
            for n in range(PAGE_SIZE // CMP_BLOCK):
                row0 = (p * (PAGE_SIZE // CMP_BLOCK) + n) * BLOCK_PITCH
                rows_ref[g, row0:row0 + CMP_BLOCK, :] = page[n * CMP_BLOCK:(n + 1) * CMP_BLOCK, g * kvw:(g + 1) * kvw]

    kc, vc = _compress_rows([rows_ref.at[g] for g in range(N_KV)], pe_ref, wk1_ref, wk2_ref, wv1_ref, wv2_ref, nb,
                            pitch=BLOCK_PITCH)
    for g in range(N_KV):
        kc_ref[0, g] = kc[g * nb:(g + 1) * nb]
        vc_ref[0, g] = vc[g * nb:(g + 1) * nb]


def _compress_paged(cache_t, page_table, cmp_w):
    db, n_pages = page_table.shape
    steps_per_row = n_pages // PAGES_PER_STEP
    nb = PAGES_PER_STEP * PAGE_SIZE // CMP_BLOCK
    out = jax.ShapeDtypeStruct((db, N_KV, steps_per_row * nb, HEAD_DIM), F32)
    ospec = pl.BlockSpec((1, N_KV, nb, HEAD_DIM), lambda i, pt: (i // steps_per_row, 0, i % steps_per_row, 0))
    grid_spec = pltpu.PrefetchScalarGridSpec(
        num_scalar_prefetch=1,
        grid=(db * steps_per_row,),
        in_specs=[pl.BlockSpec(memory_space=pl.ANY)] + _cmp_weight_specs(),
        out_specs=(ospec, ospec),
        scratch_shapes=[pltpu.VMEM((2, PAGES_PER_STEP, COL_KV, PAGE_SIZE), F32),
                        pltpu.VMEM((N_KV, nb * BLOCK_PITCH, 2 * HEAD_DIM), F32),
                        pltpu.SemaphoreType.DMA((2,))],
    )
    return pl.pallas_call(
        _compress_paged_kernel,
        grid_spec=grid_spec,
        out_shape=(out, out),
        compiler_params=_cparams("arbitrary"),
        name="compress_paged",
    )(page_table.reshape(-1), cache_t, *cmp_w)


TQ = 2 * LANES
KC = 512
LOG2_E = 1.4426950408889634
BOUND_SLACK = 1.01
MAX_SHIFT = 60.0
WIN_KEYS = WINDOW + TQ
V_ROWS = HEAD_DIM + 16
KAUG = 2 * LANES


def _top_blocks(imp, cand, n_blocks):
    blk = lax.broadcasted_iota(jnp.int32, imp.shape, 0)
    score = jnp.where(cand, imp, -1.0)
    for _ in range(N_SEL):
        mx = jnp.max(score, axis=0, keepdims=True)
        idx = jnp.min(jnp.where(score == mx, blk, n_blocks), axis=0, keepdims=True)
        score = jnp.where(blk == idx, -2.0, score)
    return jnp.where(score < -1.5, 1.0, 0.0)


def _prompt_attn_kernel(q_ref, kc_ref, vct_ref, ksel_ref, vselt_ref, kwin_ref, vwint_ref, gate_ref, o_ref,
                        kmax_ref, acc_ref):
    qt = pl.program_id(2)
    t0 = qt * TQ
    nb = kc_ref.shape[2]
    width = Q_PER_KV * TQ

    q = q_ref[0].astype(F32) * (ATT_SCALE * LOG2_E)
    q_t = q.T
    q4 = jnp.concatenate([q_t[r * HEAD_DIM:(r + 1) * HEAD_DIM] for r in range(Q_PER_KV)], axis=1)
    q4b = q4.astype(BF16)
    tok = t0 + lax.broadcasted_iota(jnp.int32, (1, width), 1) % TQ

    sc = jnp.dot(kc_ref[0, 0].astype(BF16), q4b, preferred_element_type=F32)
    blk = lax.broadcasted_iota(jnp.int32, (nb, width), 0)
    valid_c = (blk + 1) * CMP_BLOCK - 1 <= tok
    sc = jnp.where(valid_c, sc, NEG_INF)
    e = jnp.where(valid_c, jnp.exp2(sc - jnp.max(sc, axis=0, keepdims=True)), 0.0)
    den = jnp.sum(e, axis=0, keepdims=True)
    p = e / jnp.where(den > 0.0, den, 1.0)
    o_cmp = jnp.dot(vct_ref[0, 0].astype(BF16), p.astype(BF16), preferred_element_type=F32)
    imp = p[:, 0:TQ]
    for r in range(1, Q_PER_KV):
        imp = imp + p[:, r * TQ:(r + 1) * TQ]

    tok1 = t0 + lax.broadcasted_iota(jnp.int32, (1, TQ), 1)
    own = tok1 // SEL_BLOCK
    blk1 = lax.broadcasted_iota(jnp.int32, (nb, TQ), 0)
    cand = blk1 < own
    sel = _top_blocks(imp, cand, nb)
    bias = jnp.where(cand, jnp.where(sel > 0.0, 0.0, NEG_INF), jnp.where(blk1 == own, 0.0, NEG_INF))
    bias4 = jnp.concatenate([bias] * Q_PER_KV, axis=1).astype(BF16)
    q_aug = jnp.concatenate([q4b, bias4, jnp.zeros((KAUG - HEAD_DIM - nb, width), BF16)], axis=0)

    def scores(c):
        return jnp.dot(ksel_ref[0, 0, pl.ds(pl.multiple_of(c * KC, KC), KC), :], q_aug, preferred_element_type=F32)

    def softmax_pv(c, s, m, acc):
        m_new = jnp.maximum(m, jnp.max(s, axis=0, keepdims=True))
        alpha = jnp.exp2(m - m_new)
        pr = jnp.exp2(s - m_new).astype(BF16)
        v_blk = vselt_ref[0, 0, :, pl.ds(pl.multiple_of(c * KC, KC), KC)]
        return m_new, alpha * acc + jnp.dot(v_blk, pr, preferred_element_type=F32)

    def sel_step(c, carry):
        s, m, acc = carry
        s_next = scores(c + 1)
        return (s_next,) + softmax_pv(c, s, m, acc)

    last = t0 // KC
    key = last * KC + lax.broadcasted_iota(jnp.int32, (KC, width), 0)

    def online_softmax():
        init = (scores(0), jnp.full((1, width), NEG_INF, F32), jnp.zeros((V_ROWS, width), F32))
        s_last, m_sel, acc = lax.fori_loop(0, last, sel_step, init)
        return softmax_pv(last, jnp.where(key <= tok, s_last, NEG_INF), m_sel, acc)[1]

    @pl.when(qt == 0)
    def _():
        def body(i, kmax):
            k = ksel_ref[0, 0, pl.ds(pl.multiple_of(i * KC, KC), KC), :].astype(F32)
            k = jnp.where(lax.broadcasted_iota(jnp.int32, k.shape, 1) < HEAD_DIM, k, 0.0)
            return jnp.maximum(kmax, jnp.max(jnp.sum(k * k, axis=1, keepdims=True), axis=0, keepdims=True))
        kmax_ref[...] = jnp.broadcast_to(lax.fori_loop(0, ksel_ref.shape[2] // KC, body, jnp.zeros((1, 1), F32)),
                                         kmax_ref.shape)

    q4f = q4b.astype(F32)
    bound = jnp.sqrt(jnp.sum(q4f * q4f, axis=0, keepdims=True) * kmax_ref[0:1, 0:1]) * BOUND_SLACK
    bounded = jnp.max(bound) <= MAX_SHIFT

    def bounded_pv(c, s, acc):
        v_blk = vselt_ref[0, 0, :, pl.ds(pl.multiple_of(c * KC, KC), KC)]
        return acc + jnp.dot(v_blk, jnp.exp2(s - bound).astype(BF16), preferred_element_type=F32)

    @pl.when(bounded)
    def _():
        acc = lax.fori_loop(0, last, lambda c, acc: bounded_pv(c, scores(c), acc), jnp.zeros((V_ROWS, width), F32))
        acc_ref[...] = bounded_pv(last, jnp.where(key <= tok, scores(last), NEG_INF), acc)

    @pl.when(jnp.logical_not(bounded))
    def _():
        acc_ref[...] = online_softmax()

    acc_sel = acc_ref[...]
    o_sel = acc_sel[0:HEAD_DIM] / acc_sel[HEAD_DIM:HEAD_DIM + 1]

    w0 = pl.multiple_of(jnp.maximum(t0 - WINDOW, 0), TQ)
    sw = jnp.dot(kwin_ref[0, 0, pl.ds(w0, WIN_KEYS), :], q4b, preferred_element_type=F32)
    dist = tok - (w0 + lax.broadcasted_iota(jnp.int32, (WIN_KEYS, width), 0))
    sw = jnp.where((dist >= 0) & (dist < WINDOW), sw, NEG_INF)
    pw = jnp.exp2(sw - jnp.max(sw, axis=0, keepdims=True)).astype(BF16)
    acc_win = jnp.dot(vwint_ref[0, 0, :, pl.ds(w0, WIN_KEYS)], pw, preferred_element_type=F32)
    o_win = acc_win[0:HEAD_DIM] / acc_win[HEAD_DIM:HEAD_DIM + 1]

    outs = []
    for r in range(Q_PER_KV):
        sl = slice(r * TQ, (r + 1) * TQ)
        g = [gate_ref[0, 0, j * Q_PER_KV + r:j * Q_PER_KV + r + 1, :] for j in range(3)]
        outs.append(g[0] * o_cmp[:, sl] + g[1] * o_sel[:, sl] + g[2] * o_win[:, sl])
    o_ref[0] = jnp.concatenate(outs, axis=0).T


def _prompt_attention(q3, k_c, v_ct, ksel_aug, vsel_t, kwin, vwin_t, gates_t):
    b, t, _ = q3.shape
    nb = k_c.shape[2]
    width = Q_PER_KV * HEAD_DIM
    per_bg = lambda i, g, j: (i, g, 0, 0)
    return pl.pallas_call(
        _prompt_attn_kernel,
        grid=(b, N_KV, t // TQ),
        in_specs=[
            pl.BlockSpec((1, TQ, width), lambda i, g, j: (i, j, g)),
            pl.BlockSpec((1, 1, nb, HEAD_DIM), per_bg),
            pl.BlockSpec((1, 1, HEAD_DIM, nb), per_bg),
            pl.BlockSpec((1, 1, t, KAUG), per_bg),
            pl.BlockSpec((1, 1, V_ROWS, t), per_bg),
            pl.BlockSpec((1, 1, t, HEAD_DIM), per_bg),
            pl.BlockSpec((1, 1, V_ROWS, t), per_bg),
            pl.BlockSpec((1, 1, 3 * Q_PER_KV, TQ), lambda i, g, j: (i, g, 0, j)),
        ],
        out_specs=pl.BlockSpec((1, TQ, width), lambda i, g, j: (i, j, g)),
        out_shape=jax.ShapeDtypeStruct((b, t, COL_Q), F32),
        scratch_shapes=[pltpu.VMEM((SUBLANES, LANES), F32), pltpu.VMEM((V_ROWS, Q_PER_KV * TQ), F32)],
        compiler_params=_cparams("arbitrary", "arbitrary", "arbitrary"),
        name="prompt_attention",
    )(q3, k_c, v_ct, ksel_aug, vsel_t, kwin, vwin_t, gates_t)


def _sample_cmp_kernel(q_ref, kc_ref, vc_ref, ocmp_ref, idx_ref, imp_ref, *, past_len, dec_seq):
    b = pl.program_id(0)
    nb = kc_ref.shape[2]
    rows = Q_PER_KV * SUBLANES
    t_row = lax.broadcasted_iota(jnp.int32, (rows, nb), 0) % SUBLANES
    blk = lax.broadcasted_iota(jnp.int32, (rows, nb), 1)
    valid = (blk + 1) * CMP_BLOCK - 1 <= past_len + t_row
    for g in range(N_KV):
        s = lax.dot_general(q_ref[0, g], kc_ref[0, g].astype(BF16), (((1,), (1,)), ((), ())),
                            preferred_element_type=F32) * ATT_SCALE
        s = jnp.where(valid, s, NEG_INF)
        e = jnp.where(valid, jnp.exp(s - jnp.max(s, axis=1, keepdims=True)), 0.0)
        den = jnp.sum(e, axis=1, keepdims=True)
        p = e / jnp.where(den > 0.0, den, 1.0)
        ocmp_ref[0, g] = jnp.dot(p.astype(BF16), vc_ref[0, g].astype(BF16), preferred_element_type=F32)
        imp = p[0:SUBLANES]
        for r in range(1, Q_PER_KV):
            imp = imp + p[r * SUBLANES:(r + 1) * SUBLANES]
        row0 = pl.multiple_of((b * N_KV + g) * SUBLANES, SUBLANES)
        imp_ref[pl.ds(row0, SUBLANES), :] = imp

    @pl.when(b == pl.num_programs(0) - 1)
    def _():
        n_rows = imp_ref.shape[0]
        lane = lax.broadcasted_iota(jnp.int32, (n_rows, nb), 1)
        own = (past_len + lax.broadcasted_iota(jnp.int32, (n_rows, nb), 0) % SUBLANES) // SEL_BLOCK
        score = jnp.where(lane < own, imp_ref[...], -1.0)
        col = lax.broadcasted_iota(jnp.int32, (n_rows, LANES), 1)
        picks = jnp.zeros((n_rows, LANES), jnp.int32)
        for i in range(N_SEL):
            mx = jnp.max(score, axis=1, keepdims=True)
            idx = jnp.min(jnp.where(score == mx, lane, nb), axis=1, keepdims=True)
            score = jnp.where(lane == idx, -2.0, score)
            picks = jnp.where(col == i, idx, picks)
        idx_ref[...] = picks


def _sample_cmp_select(q_rt, k_c, v_c, past_len, dec_seq):
    db, _, rows, _ = q_rt.shape
    nb = k_c.shape[2]
    spec4 = lambda r, c: pl.BlockSpec((1, N_KV, r, c), lambda i: (i, 0, 0, 0))
    n_rows = db * N_KV * SUBLANES
    return pl.pallas_call(
        functools.partial(_sample_cmp_kernel, past_len=past_len, dec_seq=dec_seq),
        grid=(db,),
        in_specs=[spec4(rows, HEAD_DIM), spec4(nb, HEAD_DIM), spec4(nb, HEAD_DIM)],
        out_specs=(spec4(rows, HEAD_DIM), pl.BlockSpec((n_rows, LANES), lambda i: (0, 0))),
        out_shape=(jax.ShapeDtypeStruct((db, N_KV, rows, HEAD_DIM), F32),
                   jax.ShapeDtypeStruct((n_rows, LANES), jnp.int32)),
        scratch_shapes=[pltpu.VMEM((n_rows, nb), F32)],
        compiler_params=_cparams("arbitrary"),
        name="sample_cmp_select",
    )(q_rt, k_c, v_c)


def _sample_attn_kernel(idx_ref, pt_ref, cache_ref, q_ref, snew_ref, wnew_ref, wstate_ref, ocmp_ref, gate_ref,
                        o_ref, kv_ref, sem_ref, *, dec_seq, n_pages):
    b = pl.program_id(0)
    n_b = pl.num_programs(0)
    kvw = 2 * HEAD_DIM
    blocks_per_page = PAGE_SIZE // SEL_BLOCK

    def block_id(bb, g, t, i):
        return idx_ref[((bb * N_KV + g) * dec_seq + t) * N_SEL + i]

    def slab_copy(bb, slot, g, t, i):
        page = pt_ref[bb * n_pages + block_id(bb, g, t, i) // blocks_per_page]
        return pltpu.make_async_copy(cache_ref.at[page, pl.ds(g * kvw, kvw), :], kv_ref.at[slot, g, t, i],
                                     sem_ref.at[slot])

    def for_all_slabs(fn):
        for g in range(N_KV):
            for t in range(dec_seq):
                for i in range(N_SEL):
                    fn(g, t, i)

    slot = b % 2

    @pl.when(b == 0)
    def _():
        for_all_slabs(lambda g, t, i: slab_copy(b, slot, g, t, i).start())

    @pl.when(b + 1 < n_b)
    def _():
        for_all_slabs(lambda g, t, i: slab_copy(b + 1, 1 - slot, g, t, i).start())

    rows = dec_seq * SUBLANES
    tok = lax.broadcasted_iota(jnp.int32, (rows, 1), 0) // SUBLANES
    n_state = wstate_ref.shape[2]

    def attend(q, k_t, v_t, valid):
        s = jnp.dot(q, k_t.astype(BF16), preferred_element_type=F32) * ATT_SCALE
        s = jnp.where(valid, s, NEG_INF)
        p = jnp.exp(s - jnp.max(s, axis=1, keepdims=True))
        den = jnp.sum(p, axis=1, keepdims=True)
        return lax.dot_general(p.astype(BF16), v_t.astype(BF16), (((1,), (1,)), ((), ())),
                               preferred_element_type=F32) / den

    o_win = []
    for g in range(N_KV):
        k_rows, v_rows = pl.ds(g * kvw, HEAD_DIM), pl.ds(g * kvw + HEAD_DIM, HEAD_DIM)
        k_t = jnp.concatenate([wstate_ref[0, k_rows, :], wnew_ref[0, k_rows, :]], axis=1)
        v_t = jnp.concatenate([wstate_ref[0, v_rows, :], wnew_ref[0, v_rows, :]], axis=1)
        lane = lax.broadcasted_iota(jnp.int32, (rows, n_state + LANES), 1)
        new_i = lane - n_state
        valid = ((lane < n_state) & (lane > tok)) | ((new_i >= 0) & (new_i <= tok) & (new_i < dec_seq))
        o_win.append(attend(q_ref[0, g], k_t, v_t, valid))

    for_all_slabs(lambda g, t, i: slab_copy(b, slot, g, t, i).wait())

    lane1 = lax.broadcasted_iota(jnp.int32, (1, PAGE_SIZE), 1)
    for g in range(N_KV):
        o_sel = []
        for t in range(dec_seq):
            k_parts = [kv_ref[slot, g, t, i, 0:HEAD_DIM, :] for i in range(N_SEL)]
            v_parts = [kv_ref[slot, g, t, i, HEAD_DIM:kvw, :] for i in range(N_SEL)]
            k_parts.append(snew_ref[0, g * kvw:g * kvw + HEAD_DIM, :])
            v_parts.append(snew_ref[0, g * kvw + HEAD_DIM:(g + 1) * kvw, :])
            halves = [lane1 // SEL_BLOCK == block_id(b, g, t, i) % blocks_per_page for i in range(N_SEL)]
            halves.append((lane1 <= t) & (lane1 < dec_seq))
            q = q_ref[0, g, t * SUBLANES:(t + 1) * SUBLANES, :]
            o_sel.append(attend(q, jnp.concatenate(k_parts, axis=1), jnp.concatenate(v_parts, axis=1),
                                jnp.concatenate(halves, axis=1)))
        o_sel = jnp.concatenate(o_sel, axis=0)
        o_ref[0, g] = (gate_ref[0, g, 0] * ocmp_ref[0, g] + gate_ref[0, g, 1] * o_sel
                       + gate_ref[0, g, 2] * o_win[g])


def _sample_attention(sel_idx, page_table, cache_t, q_tr, snew_t, wnew_t, wstate_t, ocmp_tr, gates_tr, dec_seq):
    db, n_pages = page_table.shape
    rows = dec_seq * SUBLANES
    n_state = wstate_t.shape[2]
    per_b4 = lambda i, *_: (i, 0, 0, 0)
    per_b3 = lambda i, *_: (i, 0, 0)
    grid_spec = pltpu.PrefetchScalarGridSpec(
        num_scalar_prefetch=2,
        grid=(db,),
        in_specs=[
            pl.BlockSpec(memory_space=pl.ANY),
            pl.BlockSpec((1, N_KV, rows, HEAD_DIM), per_b4),
            pl.BlockSpec((1, COL_KV, LANES), per_b3),
            pl.BlockSpec((1, COL_KV, LANES), per_b3),
            pl.BlockSpec((1, COL_KV, n_state), per_b3),
            pl.BlockSpec((1, N_KV, rows, HEAD_DIM), per_b4),
            pl.BlockSpec((1, N_KV, 3, rows, HEAD_DIM), lambda i, *_: (i, 0, 0, 0, 0)),
        ],
        out_specs=pl.BlockSpec((1, N_KV, rows, HEAD_DIM), per_b4),
        scratch_shapes=[pltpu.VMEM((2, N_KV, dec_seq, N_SEL, 2 * HEAD_DIM, PAGE_SIZE), F32),
                        pltpu.SemaphoreType.DMA((2,))],
    )
    return pl.pallas_call(
        functools.partial(_sample_attn_kernel, dec_seq=dec_seq, n_pages=n_pages),
        grid_spec=grid_spec,
        out_shape=jax.ShapeDtypeStruct((db, N_KV, rows, HEAD_DIM), F32),
        compiler_params=_cparams("arbitrary"),
        name="sample_attention",
    )(sel_idx, page_table.reshape(-1), cache_t, q_tr, snew_t, wnew_t, wstate_t, ocmp_tr, gates_tr)


ROUTE_E1, ROUTE_E2, ROUTE_W1, ROUTE_W2 = 0, 1, 2, 3


def _merge_kernel(conv_ref, att_ref, x_ref, gc_ref, ga_ref, wo_ref, ln2_ref, wr_ref, x1_ref, h3_ref, route_ref):
    mix = jnp.concatenate([_rms(conv_ref[...], gc_ref[...]), _rms(att_ref[...], ga_ref[...])], axis=1)
    x1 = x_ref[...] + jnp.dot(mix.astype(BF16), wo_ref[...], preferred_element_type=F32)
    x1_ref[...] = x1
    h = _rms(x1, ln2_ref[...])
    _rows_to_tiles(h3_ref, h)

    logits = jnp.dot(h.astype(BF16), wr_ref[...], preferred_element_type=F32)
    lane = lax.broadcasted_iota(jnp.int32, logits.shape, 1)
    is_g = lane < N_GROUPS
    lg = jnp.where(is_g, logits, NEG_INF)
    mg = jnp.max(lg, axis=1, keepdims=True)
    sg = jnp.sum(jnp.where(is_g, jnp.exp(lg - mg), 0.0), axis=1, keepdims=True)
    grp = jnp.min(jnp.where(lg == mg, lane, LANES), axis=1, keepdims=True)
    p_top = 1.0 / sg
    in_grp = ((lane + (EXPERTS_PER_GROUP - N_GROUPS)) // EXPERTS_PER_GROUP) == grp + 1
    le = jnp.where(in_grp, logits, NEG_INF)
    ee = jnp.where(in_grp, jnp.exp(le - jnp.max(le, axis=1, keepdims=True)), 0.0)
    pe = jnp.where(in_grp, ee / jnp.sum(ee, axis=1, keepdims=True), -1.0)
    p1 = jnp.max(pe, axis=1, keepdims=True)
    i1 = jnp.min(jnp.where(pe == p1, lane, LANES), axis=1, keepdims=True)
    pe2 = jnp.where(lane == i1, -1.0, pe)
    p2 = jnp.max(pe2, axis=1, keepdims=True)
    i2 = jnp.min(jnp.where(pe2 == p2, lane, LANES), axis=1, keepdims=True)
    den = p1 + p2
    rec = jnp.where(lane == ROUTE_E1, (i1 - N_GROUPS).astype(F32), 0.0)
    rec = jnp.where(lane == ROUTE_E2, (i2 - N_GROUPS).astype(F32), rec)
    rec = jnp.where(lane == ROUTE_W1, p1 / den * p_top, rec)
    rec = jnp.where(lane == ROUTE_W2, p2 / den * p_top, rec)
    route_ref[...] = rec


def _merge(conv2d, att2d, x2d, g_conv, g_att, w_out_b, ln2, w_route_b, tm):
    n = x2d.shape[0]
    row = lambda i: (i, 0)
    const = lambda i: (0, 0)
    x1, h_tiles, route = pl.pallas_call(
        _merge_kernel,
        grid=(n // tm,),
        in_specs=[
            pl.BlockSpec((tm, CONV_CH), row),
            pl.BlockSpec((tm, COL_Q), row),
            pl.BlockSpec((tm, D_MODEL), row),
            pl.BlockSpec((1, CONV_CH), const),
            pl.BlockSpec((1, COL_Q), const),
            pl.BlockSpec((CONV_CH + COL_Q, D_MODEL), const),
            pl.BlockSpec((1, D_MODEL), const),
            pl.BlockSpec((D_MODEL, LANES), const),
        ],
        out_specs=(
            pl.BlockSpec((tm, D_MODEL), row),
            pl.BlockSpec((tm * ROW_TILES, LANES), row),
            pl.BlockSpec((tm, LANES), row),
        ),
        out_shape=(
            jax.ShapeDtypeStruct((n, D_MODEL), F32),
            jax.ShapeDtypeStruct((n * ROW_TILES, LANES), F32),
            jax.ShapeDtypeStruct((n, LANES), F32),
        ),
        compiler_params=_cparams("arbitrary"),
        name="merge_route",
    )(conv2d, att2d, x2d, g_conv.reshape(1, -1), g_att.reshape(1, -1), w_out_b, ln2.reshape(1, -1), w_route_b)
    return x1, h_tiles.reshape(n, ROW_TILES, LANES), route


META_W = 2 * LANES


def _rank_kernel(route_ref, dest_ref, meta_ref, e_ref, rank_ref, count_ref):
    i = pl.program_id(0)
    rt = route_ref.shape[0]

    @pl.when(i == 0)
    def _():
        count_ref[...] = jnp.zeros(count_ref.shape, F32)

    route_t = route_ref[...].T
    expert_id = lax.broadcasted_iota(jnp.int32, (N_EXPERTS, rt), 0).astype(F32)
    before = (lax.broadcasted_iota(jnp.int32, (rt, rt), 0) < lax.broadcasted_iota(jnp.int32, (rt, rt), 1))
    before = before.astype(BF16)
    ones = jnp.ones((rt, LANES), BF16)
    e_rows, rank_rows = [], []
    for k in range(TOP_K):
        e_k = route_t[ROUTE_E1 + k:ROUTE_E1 + k + 1, :]
        onehot = (expert_id == e_k).astype(F32)
        earlier = jnp.dot(onehot.astype(BF16), before, preferred_element_type=F32)
        seen = count_ref[...]
        seen_w = jnp.concatenate([seen] * (rt // LANES), axis=1)
        rank_rows.append(jnp.sum(onehot * (earlier + seen_w), axis=0, keepdims=True))
        e_rows.append(e_k)
        count_ref[...] = seen + jnp.dot(onehot.astype(BF16), ones, preferred_element_type=F32)
    e_ref[i] = jnp.concatenate(e_rows, axis=0)
    rank_ref[i] = jnp.concatenate(rank_rows, axis=0)

    @pl.when(i == pl.num_programs(0) - 1)
    def _():
        counts = jnp.concatenate([count_ref[...]] * (META_W // LANES), axis=1)
        padded = jnp.floor((counts + (MOE_ROWS - 1)) * (1.0 / MOE_ROWS)) * MOE_ROWS
        lane = lax.broadcasted_iota(jnp.int32, (1, META_W), 1)
        chunk_start = lane.astype(F32) * MOE_ROWS
        chunk_e = jnp.zeros((1, META_W), F32)
        ends = jnp.zeros((1, META_W), F32)
        end = jnp.zeros((1, META_W), F32)
        starts = []
        for ex in range(N_EXPERTS):
            starts.append(end[:, 0:rt])
            end = end + padded[ex:ex + 1, :]
            chunk_e = chunk_e + (end <= chunk_start).astype(F32)
            ends = jnp.where(lane == ex, end, ends)
        n_used = end * (1.0 / MOE_ROWS)
        chunk_e = jnp.minimum(chunk_e, float(N_EXPERTS - 1))
        row = lax.broadcasted_iota(jnp.int32, (SUBLANES, META_W), 0)
        meta = jnp.where(row == 0, chunk_e, jnp.where(row == 1, n_used, jnp.where(row == 2, ends, 0.0)))
        meta_ref[...] = meta.astype(jnp.int32)

        def place(ti, carry):
            e_t = e_ref[ti]
            dest = rank_ref[ti]
            for ex in range(N_EXPERTS):
                dest = dest + jnp.where(e_t == float(ex), starts[ex], 0.0)
            dest_ref[ti] = dest.astype(jnp.int32)
            return carry

        lax.fori_loop(0, pl.num_programs(0), place, 0)


def _rank(route, rt):
    n = route.shape[0]
    tiles = n // rt
    assert n * TOP_K // MOE_ROWS + N_EXPERTS <= META_W and rt <= META_W, "chunk table is one row of META_W lanes"
    whole = lambda i: (0, 0, 0)
    return pl.pallas_call(
        _rank_kernel,
        grid=(tiles,),
        in_specs=[pl.BlockSpec((rt, LANES), lambda i: (i, 0))],
        out_specs=(pl.BlockSpec((tiles, TOP_K, rt), whole), pl.BlockSpec((SUBLANES, META_W), lambda i: (0, 0))),
        out_shape=(jax.ShapeDtypeStruct((tiles, TOP_K, rt), jnp.int32),
                   jax.ShapeDtypeStruct((SUBLANES, META_W), jnp.int32)),
        scratch_shapes=[pltpu.VMEM((tiles, TOP_K, rt), F32), pltpu.VMEM((tiles, TOP_K, rt), F32),
                        pltpu.VMEM((N_EXPERTS, LANES), F32)],
        compiler_params=_cparams("arbitrary"),
        name="route_rank",
    )(route)


def _row_copies_wait(src_ref, dst_ref, sem, n_rows):
    pltpu.make_async_copy(src_ref.at[pl.ds(0, n_rows)], dst_ref.at[pl.ds(0, n_rows)], sem).wait()


def _scatter_kernel(meta_ref, dest_ref, h3_ref, xs_ref, zero_ref, sem_ref):
    i = pl.program_id(0)
    rt = h3_ref.shape[0]
    n_chunks = xs_ref.shape[0] // MOE_ROWS

    @pl.when(i == 0)
    def _():
        zero_ref[...] = jnp.zeros(zero_ref.shape, F32)

        def zero_chunk(first_row):
            return pltpu.make_async_copy(zero_ref, xs_ref.at[pl.ds(first_row, MOE_ROWS)], sem_ref.at[1])

        def fills(act):
            prev = 0
            for ex in range(N_EXPERTS):
                end = meta_ref[2, ex]

                @pl.when(end > prev)
                def _():
                    act(zero_chunk(end - MOE_ROWS))
                prev = end
            for c in range(n_chunks - N_EXPERTS, n_chunks):
                @pl.when(c >= meta_ref[1, 0])
                def _():
                    act(zero_chunk(c * MOE_ROWS))

        fills(lambda copy: copy.start())
        fills(lambda copy: copy.wait())

    def issue(t, carry):
        for k in range(TOP_K):
            pltpu.make_async_copy(h3_ref.at[t], xs_ref.at[dest_ref[0, k, t]], sem_ref.at[0]).start()
        return carry

    lax.fori_loop(0, rt, issue, 0, unroll=8)
    _row_copies_wait(h3_ref, h3_ref, sem_ref.at[0], rt)
    _row_copies_wait(h3_ref, h3_ref, sem_ref.at[0], rt)


def _scatter_rows(h3, dest, meta, rt):
    n = h3.shape[0]
    n_slots = (n * TOP_K // MOE_ROWS + N_EXPERTS) * MOE_ROWS
    grid_spec = pltpu.PrefetchScalarGridSpec(
        num_scalar_prefetch=1,
        grid=(n // rt,),
        in_specs=[pl.BlockSpec((1, TOP_K, rt), lambda i, meta: (i, 0, 0), memory_space=pltpu.SMEM),
                  pl.BlockSpec((rt, ROW_TILES, LANES), lambda i, meta: (i, 0, 0))],
        out_specs=pl.BlockSpec(memory_space=pl.ANY),
        scratch_shapes=[pltpu.VMEM((MOE_ROWS, ROW_TILES, LANES), F32), pltpu.SemaphoreType.DMA((2,))],
    )
    return pl.pallas_call(
        _scatter_kernel,
        grid_spec=grid_spec,
        out_shape=jax.ShapeDtypeStruct((n_slots, ROW_TILES, LANES), F32),
        compiler_params=_cparams("arbitrary"),
        name="moe_scatter_rows",
    )(meta, dest, h3)


def _expert_kernel(meta_ref, xs_ref, wg_ref, wu_ref, wd_ref, ys_ref, wg_b, wu_b, wd_b):
    c = pl.program_id(0)
    in_use = c < meta_ref[1, 0]

    @pl.when(jnp.logical_not(in_use))
    def _():
        ys_ref[...] = jnp.zeros(ys_ref.shape, F32)

    @pl.when(in_use & ((c == 0) | (meta_ref[0, c] != meta_ref[0, jnp.maximum(c - 1, 0)])))
    def _():
        wg_b[...] = wg_ref[0].astype(BF16)
        wu_b[...] = wu_ref[0].astype(BF16)
        wd_b[...] = wd_ref[0].astype(BF16)

    @pl.when(in_use)
    def _():
        x = _tiles_to_rows(xs_ref).astype(BF16)
        gate = jnp.dot(x, wg_b[...], preferred_element_type=F32)
        up = jnp.dot(x, wu_b[...], preferred_element_type=F32)
        act = (gate * jax.nn.sigmoid(gate) * up).astype(BF16)
        _rows_to_tiles(ys_ref, jnp.dot(act, wd_b[...], preferred_element_type=F32))


def _expert_mlp(xs, meta, w_g, w_u, w_d):
    n_chunks = xs.shape[0] // MOE_ROWS
    rows = lambda c, meta: (c, 0)
    expert = lambda c, meta: (meta[0, jnp.minimum(c, meta[1, 0] - 1)], 0, 0)
    grid_spec = pltpu.PrefetchScalarGridSpec(
        num_scalar_prefetch=1,
        grid=(n_chunks,),
        in_specs=[
            pl.BlockSpec((MOE_ROWS * ROW_TILES, LANES), rows),
            pl.BlockSpec((1, D_MODEL, D_EXPERT), expert),
            pl.BlockSpec((1, D_MODEL, D_EXPERT), expert),
            pl.BlockSpec((1, D_EXPERT, D_MODEL), expert),
        ],
        out_specs=pl.BlockSpec((MOE_ROWS * ROW_TILES, LANES), rows),
        scratch_shapes=[pltpu.VMEM((D_MODEL, D_EXPERT), BF16), pltpu.VMEM((D_MODEL, D_EXPERT), BF16),
                        pltpu.VMEM((D_EXPERT, D_MODEL), BF16)],
    )
    return pl.pallas_call(
        _expert_kernel,
        grid_spec=grid_spec,
        out_shape=jax.ShapeDtypeStruct((xs.shape[0] * ROW_TILES, LANES), F32),
        compiler_params=_cparams("arbitrary"),
        name="expert_mlp",
    )(meta, xs.reshape(-1, LANES), w_g, w_u, w_d).reshape(xs.shape)


def _combine_kernel(dest_ref, nxt_ref, x1_ref, route_ref, ln_ref, ys_ref, o_ref, buf_ref, sem_ref):
    i = pl.program_id(0)
    n_tiles = pl.num_programs(0)
    rt = x1_ref.shape[0]

    def issue(table_ref, slot):
        def body(t, carry):
            for k in range(TOP_K):
                tile = pl.ds(pl.multiple_of(t * ROW_TILES, ROW_TILES), ROW_TILES)
                pltpu.make_async_copy(ys_ref.at[table_ref[0, k, t]], buf_ref.at[slot, k, tile], sem_ref.at[slot]).start()
            return carry
        lax.fori_loop(0, rt, body, 0, unroll=8)

    slot = i % 2

    @pl.when(i == 0)
    def _():
        issue(dest_ref, slot)

    @pl.when(i + 1 < n_tiles)
    def _():
        issue(nxt_ref, 1 - slot)

    for k in range(TOP_K):
        pltpu.make_async_copy(buf_ref.at[slot, k], buf_ref.at[slot, k], sem_ref.at[slot]).wait()
    route = route_ref[...]
    moe = None
    for k in range(TOP_K):
        y = _tiles_to_rows(buf_ref.at[slot, k])
        term = y * route[:, ROUTE_W1 + k:ROUTE_W1 + k + 1]
        moe = term if moe is None else moe + term
    o_ref[...] = _rms(x1_ref[...] + moe, ln_ref[...])


def _combine(x1, ys, dest, route, ln_final, rt):
    n = x1.shape[0]
    tiles = n // rt
    row = lambda i: (i, 0)
    table = lambda f: pl.BlockSpec((1, TOP_K, rt), f, memory_space=pltpu.SMEM)
    return pl.pallas_call(
        _combine_kernel,
        grid=(tiles,),
        in_specs=[
            table(lambda i: (i, 0, 0)),
            table(lambda i: (jnp.minimum(i + 1, tiles - 1), 0, 0)),
            pl.BlockSpec((rt, D_MODEL), row),
            pl.BlockSpec((rt, LANES), row),
            pl.BlockSpec((1, D_MODEL), lambda i: (0, 0)),
            pl.BlockSpec(memory_space=pl.ANY),
        ],
        out_specs=pl.BlockSpec((rt, D_MODEL), row),
        out_shape=jax.ShapeDtypeStruct((n, D_MODEL), F32),
        scratch_shapes=[pltpu.VMEM((2, TOP_K, rt * ROW_TILES, LANES), F32), pltpu.SemaphoreType.DMA((2,))],
        compiler_params=_cparams("arbitrary"),
        name="moe_combine_norm",
    )(dest, dest, x1, route, ln_final.reshape(1, -1), ys)


def _ffn(conv2d, att2d, x2d, g_conv, g_att, w_out_b, ln2, w_route_b, w_g, w_u, w_d, ln_final, tm):
    x1, h3, route = _merge(conv2d, att2d, x2d, g_conv, g_att, w_out_b, ln2, w_route_b, tm)
    rt = min(tm, MOE_ROWS)
    dest, meta = _rank(route, rt)
    ys = _expert_mlp(_scatter_rows(h3, dest, meta, rt), meta, w_g, w_u, w_d)
    return _combine(x1, ys, dest, route, ln_final, rt)


def kernel(x_prompt, x_sample, cache_cmp_kv, cache_sel_kv, state_win_kv, state_conv, page_table, ln1, w_in, conv_dw_w, conv_dw_b, conv_ln_g, conv_ln_b, cmp_pos_emb, w_cmp_k1, w_cmp_k2, w_cmp_v1, w_cmp_v2, out_norm_conv, out_norm_att, w_out, ln2, w_router_group, w_router_expert, w_exp_gate, w_exp_up, w_exp_down, ln_final):
    depth = ln1.shape[0]
    assert depth == 1, "single-layer step"
    b, t, _ = x_prompt.shape
    db, ds, _ = x_sample.shape
    n_phys = cache_cmp_kv.shape[1]
    n_pages = page_table.shape[1]
    past = n_pages * PAGE_SIZE
    win_rows = state_win_kv.shape[2]
    assert ds < CMP_BLOCK and ds <= SUBLANES and past % SEL_BLOCK == 0 and past // SEL_BLOCK >= N_SEL
    assert win_rows == WINDOW and past >= WINDOW and t % KC == 0 and t >= WIN_KEYS

    w_in_b = jnp.pad(w_in[0], ((0, 0), (0, D_IN_PAD - D_IN))).astype(BF16)
    w_out_b = w_out[0].astype(BF16)
    w_route_b = jnp.pad(jnp.concatenate([w_router_group[0], w_router_expert[0]], axis=1),
                        ((0, 0), (0, LANES - N_GROUPS - N_EXPERTS))).astype(BF16)
    cmp_w = _cmp_weights(cmp_pos_emb[0], w_cmp_k1[0], w_cmp_k2[0], w_cmp_v1[0], w_cmp_v2[0])
    conv_w = (conv_dw_w[0], conv_dw_b[0], conv_ln_g[0], conv_ln_b[0])
    ffn_w = (out_norm_conv[0], out_norm_att[0], w_out_b, ln2[0], w_route_b, w_exp_gate[0], w_exp_up[0],
             w_exp_down[0], ln_final)

    xp2 = x_prompt.reshape(b * t, D_MODEL)
    c_p, s_p = _rope_tables(jnp.arange(t))
    (a_p, q_p, ckv_p, _, _, gate_p, ckv_pt, skv_pt, wkv_pt, ksel_aug, kwin_p, vsel_t, vwin_t) = _project(
        xp2, ln1[0], w_in_b, c_p, s_p, 512)
    a_p3 = a_p.reshape(b, t, CONV_CH)
    conv_p = _conv_module(a_p3, jnp.zeros((b, HIST_ROWS, CONV_CH), F32), *conv_w, 512)
    kc_p, vc_p = _compress_dense(ckv_p.reshape(b, t, COL_KV), cmp_w)
    gates_t = jnp.transpose(gate_p[:, :COL_GATE].reshape(b, t, N_KV, Q_PER_KV, 3), (0, 2, 4, 3, 1))
    att_p = _prompt_attention(q_p.reshape(b, t, COL_Q), kc_p, jnp.swapaxes(vc_p, 2, 3), ksel_aug,
                              vsel_t, kwin_p, vwin_t, gates_t.reshape(b, N_KV, 3 * Q_PER_KV, t))
    y_p = _ffn(conv_p.reshape(b * t, CONV_CH), att_p.reshape(b * t, COL_Q), xp2, *ffn_w, 256)

    n_s = db * ds
    xs2 = x_sample.reshape(n_s, D_MODEL)
    c_s, s_s = _rope_tables(jnp.tile(past + jnp.arange(ds), db))
    a_s, q_s, ckv_s, skv_s, wkv_s, gate_s = _project(xs2, ln1[0], w_in_b, c_s, s_s, n_s)[:6]
    a_s3 = a_s.reshape(db, ds, CONV_CH)
    hist_s = jnp.pad(state_conv[0], ((0, 0), (HIST_ROWS - (CONV_W - 1), 0), (0, 0)))
    conv_s = _conv_module(a_s3, hist_s, *conv_w, ds)
    rows_minor = lambda a, n, r: jnp.swapaxes(a.reshape(n, r, COL_KV), 1, 2)
    kc_s, vc_s = _compress_paged(rows_minor(cache_cmp_kv[0], n_phys, PAGE_SIZE), page_table, cmp_w)

    q5 = q_s.reshape(db, ds, N_KV, Q_PER_KV, HEAD_DIM)
    pad_tok = SUBLANES - ds
    pad_head = SUBLANES - Q_PER_KV
    q_rt = jnp.pad(jnp.transpose(q5, (0, 2, 3, 1, 4)), ((0, 0), (0, 0), (0, 0), (0, pad_tok), (0, 0)))
    q_rt = q_rt.reshape(db, N_KV, Q_PER_KV * SUBLANES, HEAD_DIM)
    q_tr = jnp.pad(jnp.transpose(q5, (0, 2, 1, 3, 4)), ((0, 0), (0, 0), (0, 0), (0, pad_head), (0, 0)))
    q_tr = q_tr.reshape(db, N_KV, ds * SUBLANES, HEAD_DIM)
    g5 = gate_s[:, :COL_GATE].reshape(db, ds, N_KV, Q_PER_KV, 3)
    g_tr = jnp.pad(jnp.transpose(g5, (0, 2, 4, 1, 3)), ((0, 0),) * 4 + ((0, pad_head),))
    g_tr = jnp.broadcast_to(g_tr.reshape(db, N_KV, 3, ds * SUBLANES)[..., None], (db, N_KV, 3, ds * SUBLANES, HEAD_DIM))
    ocmp, picks = _sample_cmp_select(q_rt, kc_s, vc_s, past, ds)
    sel_idx = picks.reshape(db, N_KV, SUBLANES, LANES)[:, :, :ds, :N_SEL].reshape(-1)
    ocmp_tr = jnp.transpose(ocmp.reshape(db, N_KV, Q_PER_KV, SUBLANES, HEAD_DIM)[:, :, :, :ds], (0, 1, 3, 2, 4))
    ocmp_tr = jnp.pad(ocmp_tr, ((0, 0), (0, 0), (0, 0), (0, pad_head), (0, 0)))
    ocmp_tr = ocmp_tr.reshape(db, N_KV, ds * SUBLANES, HEAD_DIM)
    new_t = lambda kv: jnp.pad(rows_minor(kv, db, ds), ((0, 0), (0, 0), (0, LANES - ds)))
    o_s = _sample_attention(sel_idx, page_table, rows_minor(cache_sel_kv[0], n_phys, PAGE_SIZE), q_tr,
                            new_t(skv_s), new_t(wkv_s), rows_minor(state_win_kv[0], db, win_rows),
                            ocmp_tr, g_tr, ds)
    att_s = o_s.reshape(db, N_KV, ds, SUBLANES, HEAD_DIM)[:, :, :, :Q_PER_KV]
    att_s = jnp.transpose(att_s, (0, 2, 1, 3, 4)).reshape(n_s, COL_Q)
    y_s = _ffn(conv_s.reshape(n_s, CONV_CH), att_s, xs2, *ffn_w, n_s)

    kv6 = lambda kv, bb, tt: kv.reshape(1, bb, tt, N_KV, 2, HEAD_DIM)
    kv6_t = lambda kv_t: jnp.swapaxes(kv_t, 1, 2).reshape(1, b, kv_t.shape[2], N_KV, 2, HEAD_DIM)
    new_win_s = jnp.concatenate([state_win_kv, kv6(wkv_s, db, ds)], axis=2)[:, :, ds:]
    new_conv_s = jnp.concatenate([state_conv[0], a_s3], axis=1)[None, :, ds:]
    return (y_p.reshape(b, t, D_MODEL), y_s.reshape(db, ds, D_MODEL),
            kv6_t(ckv_pt), kv6(ckv_s, db, ds), kv6_t(skv_pt), kv6(skv_s, db, ds),
            kv6_t(wkv_pt[:, :, t - min(WINDOW, t):]), new_win_s,
            a_p3[None, :, t - (CONV_W - 1):], new_conv_s)
```

```python
import functools

import jax
import jax.numpy as jnp
from jax import lax
from jax.experimental import pallas as pl
from jax.experimental.pallas import tpu as pltpu

D_MODEL = 1024
CONV_CH = 512
CONV_W = 31
N_HEADS = 8
HEAD_DIM = 64
N_KV = 2
Q_PER_KV = N_HEADS // N_KV
ROPE_DIM = HEAD_DIM // 4
ROPE_THETA = 500000.0
CMP_BLOCK = 64
SEL_BLOCK = CMP_BLOCK
N_SEL = 16
WINDOW = 512
CMP_HID = 2 * HEAD_DIM
COL_Q = N_HEADS * HEAD_DIM
COL_KV = 2 * N_KV * HEAD_DIM
COL_GATE = 3 * N_HEADS
D_IN = 2 * CONV_CH + COL_Q + 3 * COL_KV + COL_GATE
N_GROUPS = 4
EXPERTS_PER_GROUP = 8
N_EXPERTS = N_GROUPS * EXPERTS_PER_GROUP
TOP_K = 2
D_EXPERT = 512
PAGE_SIZE = 128
RMS_EPS = 1e-6
LN_EPS = 1e-5
NEG_INF = -1e30
ATT_SCALE = HEAD_DIM ** -0.5

LANES = 128
SUBLANES = 8
VMEM_LIMIT_BYTES = 56 * 1024 * 1024

D_IN_PAD = ((D_IN + LANES - 1) // LANES) * LANES
COL_GATE_OFF = 2 * CONV_CH + COL_Q + 3 * COL_KV
HIST_ROWS = 32
CONV_ROWS = 32
MOE_ROWS = 256
ROW_TILES = D_MODEL // LANES

BF16 = jnp.bfloat16
F32 = jnp.float32


def _cparams(*sem):
    return pltpu.CompilerParams(dimension_semantics=sem, vmem_limit_bytes=VMEM_LIMIT_BYTES)


def _rms(x, g):
    return x * lax.rsqrt(jnp.mean(x * x, axis=-1, keepdims=True) + RMS_EPS) * g


def _tiles_to_rows(tile_ref):
    n = tile_ref.shape[0] // ROW_TILES
    return jnp.concatenate([tile_ref[pl.ds(j, n, stride=ROW_TILES), :] for j in range(ROW_TILES)], axis=1)


def _rows_to_tiles(tile_ref, rows):
    for j in range(ROW_TILES):
        tile_ref[pl.ds(j, rows.shape[0], stride=ROW_TILES), :] = rows[:, j * LANES:(j + 1) * LANES]


def _rope_tables(pos):
    half = ROPE_DIM // 2
    inv = ROPE_THETA ** (-jnp.arange(half, dtype=F32) / half)
    ang = pos.astype(F32)[:, None] * inv
    cos, sin = jnp.cos(ang), jnp.sin(ang)
    m = jnp.arange(LANES) % HEAD_DIM
    idx = m % half
    c = jnp.where(m < ROPE_DIM, cos[:, idx], 1.0)
    s = jnp.where(m < half, -sin[:, idx], jnp.where(m < ROPE_DIM, sin[:, idx], 0.0))
    return c.astype(F32), s.astype(F32)


def _rope(v, c, s, first_half):
    w = v.shape[1]
    half = ROPE_DIM // 2
    partner = jnp.where(first_half, pltpu.roll(v, w - half, axis=1), pltpu.roll(v, half, axis=1))
    return v * c + partner * s


def _proj_kernel(x_ref, ln_ref, w_ref, c_ref, s_ref, a_ref, q_ref, ckv_ref, skv_ref, wkv_ref, gate_ref,
                 ckvt_ref, skvt_ref, wkvt_ref, ksel_ref, kwin_ref, vsel_ref, vwin_ref, *, t_tiles):
    x = x_ref[...]
    xn = _rms(x, ln_ref[...])
    p = jnp.dot(xn.astype(BF16), w_ref[...], preferred_element_type=F32)
    a_ref[...] = p[:, :CONV_CH] * jax.nn.sigmoid(p[:, CONV_CH:2 * CONV_CH])

    c128, s128 = c_ref[...], s_ref[...]
    tm = x.shape[0]
    lane_q = lax.broadcasted_iota(jnp.int32, (tm, COL_Q), 1)
    cq = jnp.concatenate([c128] * (COL_Q // LANES), axis=1)
    sq = jnp.concatenate([s128] * (COL_Q // LANES), axis=1)
    o = 2 * CONV_CH
    q = _rope(p[:, o:o + COL_Q], cq, sq, (lane_q % HEAD_DIM) < ROPE_DIM // 2)
    q_ref[...] = q.astype(q_ref.dtype)
    o += COL_Q

    lane_kv = lax.broadcasted_iota(jnp.int32, (tm, COL_KV), 1)
    is_k = (lane_kv % (2 * HEAD_DIM)) < HEAD_DIM
    ckv = jnp.where(is_k, jnp.concatenate([c128] * (COL_KV // LANES), axis=1), 1.0)
    skv = jnp.where(is_k, jnp.concatenate([s128] * (COL_KV // LANES), axis=1), 0.0)
    first_kv = (lane_kv % HEAD_DIM) < ROPE_DIM // 2
    kvs = []
    for ref, ref_t in ((ckv_ref, ckvt_ref), (skv_ref, skvt_ref), (wkv_ref, wkvt_ref)):
        kv = _rope(p[:, o:o + COL_KV], ckv, skv, first_kv)
        kv_t = kv.T
        ref[...] = kv
        ref_t[0] = kv_t
        kvs.append((kv, kv_t))
        o += COL_KV
    gate_ref[...] = jax.nn.sigmoid(p[:, o:o + LANES])

    kvw = 2 * HEAD_DIM
    lane = lax.broadcasted_iota(jnp.int32, (tm, LANES), 1)
    pos = (pl.program_id(0) % t_tiles) * tm + lax.broadcasted_iota(jnp.int32, (tm, LANES), 0)
    blk = pos // SEL_BLOCK
    is_key = lane < HEAD_DIM
    ones_rows = (lax.broadcasted_iota(jnp.int32, (V_ROWS - HEAD_DIM, tm), 0) == 0).astype(F32)
    (skv_v, skv_t), (wkv_v, wkv_t) = kvs[1], kvs[2]
    for g in range(N_KV):
        low = jnp.where(is_key, skv_v[:, g * kvw:(g + 1) * kvw], (blk == lane - HEAD_DIM).astype(F32))
        high = jnp.where(is_key, (blk == lane + HEAD_DIM).astype(F32), 0.0)
        ksel_ref[0, g] = jnp.concatenate([low, high], axis=1).astype(BF16)
        kwin_ref[0, g] = wkv_v[:, g * kvw:g * kvw + HEAD_DIM].astype(BF16)
        v_rows = slice(g * kvw + HEAD_DIM, (g + 1) * kvw)
        vsel_ref[0, g] = jnp.concatenate([skv_t[v_rows], ones_rows], axis=0).astype(BF16)
        vwin_ref[0, g] = jnp.concatenate([wkv_t[v_rows], ones_rows], axis=0).astype(BF16)


def _project(x2d, ln, w_pad, c_tab, s_tab, tm):
    n = x2d.shape[0]
    t = c_tab.shape[0]
    t_tiles = t // tm
    assert t // SEL_BLOCK <= KAUG - HEAD_DIM - HEAD_DIM, "one-hot block ids fit the augmented key"
    row = lambda i: (i, 0)
    tab = lambda i: (i % t_tiles, 0)
    const = lambda i: (0, 0)
    kv_t = jax.ShapeDtypeStruct((n // t, COL_KV, t), F32)
    kv_t_spec = pl.BlockSpec((1, COL_KV, tm), lambda i: (i // t_tiles, 0, i % t_tiles))
    keys = lambda w: (jax.ShapeDtypeStruct((n // t, N_KV, t, w), BF16),
                      pl.BlockSpec((1, N_KV, tm, w), lambda i: (i // t_tiles, 0, i % t_tiles, 0)))
    v_t = (jax.ShapeDtypeStruct((n // t, N_KV, V_ROWS, t), BF16),
           pl.BlockSpec((1, N_KV, V_ROWS, tm), lambda i: (i // t_tiles, 0, 0, i % t_tiles)))
    out_shape = (
        jax.ShapeDtypeStruct((n, CONV_CH), F32),
        jax.ShapeDtypeStruct((n, COL_Q), BF16),
        jax.ShapeDtypeStruct((n, COL_KV), F32),
        jax.ShapeDtypeStruct((n, COL_KV), F32),
        jax.ShapeDtypeStruct((n, COL_KV), F32),
        jax.ShapeDtypeStruct((n, LANES), F32),
        kv_t, kv_t, kv_t,
        keys(KAUG)[0], keys(HEAD_DIM)[0], v_t[0], v_t[0],
    )
    return pl.pallas_call(
        functools.partial(_proj_kernel, t_tiles=t_tiles),
        grid=(n // tm,),
        in_specs=[
            pl.BlockSpec((tm, D_MODEL), row),
            pl.BlockSpec((1, D_MODEL), const),
            pl.BlockSpec((D_MODEL, D_IN_PAD), const),
            pl.BlockSpec((tm, LANES), tab),
            pl.BlockSpec((tm, LANES), tab),
        ],
        out_specs=(
            pl.BlockSpec((tm, CONV_CH), row),
            pl.BlockSpec((tm, COL_Q), row),
            pl.BlockSpec((tm, COL_KV), row),
            pl.BlockSpec((tm, COL_KV), row),
            pl.BlockSpec((tm, COL_KV), row),
            pl.BlockSpec((tm, LANES), row),
            kv_t_spec, kv_t_spec, kv_t_spec,
            keys(KAUG)[1], keys(HEAD_DIM)[1], v_t[1], v_t[1],
        ),
        out_shape=out_shape,
        compiler_params=_cparams("arbitrary"),
        name="in_proj",
    )(x2d, ln.reshape(1, D_MODEL), w_pad, c_tab, s_tab)


def _conv_kernel(a_ref, hist_ref, w_ref, b_ref, g_ref, beta_ref, o_ref, sh_ref):
    tt = a_ref.shape[1]
    ext_ref = sh_ref.at[0]

    @pl.when(pl.program_id(1) == 0)
    def _():
        ext_ref[0:HIST_ROWS, :] = hist_ref[0]

    ext_ref[HIST_ROWS:HIST_ROWS + tt, :] = a_ref[0]
    span = HIST_ROWS + tt - SUBLANES
    for s in range(1, SUBLANES):
        sh_ref[s, 0:span, :] = ext_ref[s:s + span, :]

    lead = HIST_ROWS - (CONV_W - 1)
    rc = min(tt, CONV_ROWS)

    def chunk(i, carry):
        r0 = pl.multiple_of(i * rc, rc)
        acc = jnp.broadcast_to(b_ref[...], (rc, CONV_CH))
        for k in range(CONV_W):
            a, s = divmod(lead + k, SUBLANES)
            acc = acc + w_ref[k:k + 1, :] * sh_ref[s, pl.ds(r0 + a * SUBLANES, rc), :]
        o_ref[0, pl.ds(r0, rc), :] = acc
        return carry

    lax.fori_loop(0, tt // rc, chunk, 0)
    acc = o_ref[0]
    mu = jnp.mean(acc, axis=-1, keepdims=True)
    var = jnp.mean(jnp.square(acc - mu), axis=-1, keepdims=True)
    y = (acc - mu) * lax.rsqrt(var + LN_EPS) * g_ref[...] + beta_ref[...]
    o_ref[0] = y * jax.nn.sigmoid(y)
    carry = ext_ref[tt:tt + HIST_ROWS, :]
    ext_ref[0:HIST_ROWS, :] = carry


def _conv_module(a3d, hist, dw_w, dw_b, ln_g, ln_b, tt):
    b, t, _ = a3d.shape
    w_pad = jnp.pad(dw_w, ((0, HIST_ROWS - CONV_W), (0, 0)))
    vec = lambda i, j: (0, 0)
    return pl.pallas_call(
        _conv_kernel,
        grid=(b, t // tt),
        in_specs=[
            pl.BlockSpec((1, tt, CONV_CH), lambda i, j: (i, j, 0)),
            pl.BlockSpec((1, HIST_ROWS, CONV_CH), lambda i, j: (i, 0, 0)),
            pl.BlockSpec((HIST_ROWS, CONV_CH), vec),
            pl.BlockSpec((1, CONV_CH), vec),
            pl.BlockSpec((1, CONV_CH), vec),
            pl.BlockSpec((1, CONV_CH), vec),
        ],
        out_specs=pl.BlockSpec((1, tt, CONV_CH), lambda i, j: (i, j, 0)),
        out_shape=jax.ShapeDtypeStruct((b, t, CONV_CH), F32),
        scratch_shapes=[pltpu.VMEM((SUBLANES, HIST_ROWS + tt, CONV_CH), F32)],
        compiler_params=_cparams("arbitrary", "arbitrary"),
        name="conv_module",
    )(a3d, hist, w_pad, dw_b.reshape(1, -1), ln_g.reshape(1, -1), ln_b.reshape(1, -1))


L_GROUP = 2 * LANES // HEAD_DIM


def _compress_rows(x_refs, pe_ref, wk1_ref, wk2_ref, wv1_ref, wv2_ref, nb, pitch=CMP_BLOCK):
    hk = jnp.zeros((N_KV * nb, CMP_HID), F32)
    hv = jnp.zeros((N_KV * nb, CMP_HID), F32)
    for j in range(CMP_BLOCK // L_GROUP):
        parts_k, parts_v = [], []
        for x_ref in x_refs:
            xs = [x_ref[pl.ds(j * L_GROUP + i, nb, stride=pitch), :] + pe_ref[j * L_GROUP + i:j * L_GROUP + i + 1, :]
                  for i in range(L_GROUP)]
            parts_k.append(jnp.concatenate([x[:, :HEAD_DIM] for x in xs], axis=1))
            parts_v.append(jnp.concatenate([x[:, HEAD_DIM:] for x in xs], axis=1))
        xk = jnp.concatenate(parts_k, axis=0).astype(BF16)
        xv = jnp.concatenate(parts_v, axis=0).astype(BF16)
        rows = slice(j * L_GROUP * HEAD_DIM, (j + 1) * L_GROUP * HEAD_DIM)
        hk = hk + jnp.dot(xk, wk1_ref[rows, :], preferred_element_type=F32)
        hv = hv + jnp.dot(xv, wv1_ref[rows, :], preferred_element_type=F32)
    kc =jnp.dot((hk * jax.nn.sigmoid(hk)).astype(BF16), wk2_ref[...], preferred_element_type=F32)
    vc = jnp.dot((hv * jax.nn.sigmoid(hv)).astype(BF16), wv2_ref[...], preferred_element_type=F32)
    return kc, vc


def _compress_dense_kernel(*refs):
    x_refs, (pe_ref, wk1_ref, wk2_ref, wv1_ref, wv2_ref, kc_ref, vc_ref) = refs[:N_KV], refs[N_KV:]
    nb = x_refs[0].shape[1] // CMP_BLOCK
    kc, vc = _compress_rows([x.at[0] for x in x_refs], pe_ref, wk1_ref, wk2_ref, wv1_ref, wv2_ref, nb)
    for g in range(N_KV):
        kc_ref[0, g] = kc[g * nb:(g + 1) * nb]
        vc_ref[0, g] = vc[g * nb:(g + 1) * nb]


def _cmp_weight_specs():
    const = lambda *_: (0, 0)
    return [
        pl.BlockSpec((CMP_BLOCK, 2 * HEAD_DIM), const),
        pl.BlockSpec((CMP_BLOCK * HEAD_DIM, CMP_HID), const),
        pl.BlockSpec((CMP_HID, HEAD_DIM), const),
        pl.BlockSpec((CMP_BLOCK * HEAD_DIM, CMP_HID), const),
        pl.BlockSpec((CMP_HID, HEAD_DIM), const),
    ]


def _cmp_weights(pos_emb, w_k1, w_k2, w_v1, w_v2):
    pe = pos_emb.reshape(CMP_BLOCK, 2 * HEAD_DIM)
    return (pe, w_k1.reshape(-1, CMP_HID).astype(BF16), w_k2.astype(BF16),
            w_v1.reshape(-1, CMP_HID).astype(BF16), w_v2.astype(BF16))


def _compress_dense(kv3d, cmp_w):
    b, t, _ = kv3d.shape
    nb = t // CMP_BLOCK
    out = jax.ShapeDtypeStruct((b, N_KV, nb, HEAD_DIM), F32)
    ospec = pl.BlockSpec((1, N_KV, nb, HEAD_DIM), lambda i: (i, 0, 0, 0))
    return pl.pallas_call(
        _compress_dense_kernel,
        grid=(b,),
        in_specs=[pl.BlockSpec((1, t, 2 * HEAD_DIM), functools.partial(lambda g, i: (i, 0, g), g))
                  for g in range(N_KV)] + _cmp_weight_specs(),
        out_specs=(ospec, ospec),
        out_shape=(out, out),
        compiler_params=_cparams("arbitrary"),
        name="compress_prompt",
    )(*([kv3d] * N_KV), *cmp_w)


PAGES_PER_STEP = 64
BLOCK_PITCH = CMP_BLOCK + SUBLANES


def _compress_paged_kernel(pt_ref, cache_ref, pe_ref, wk1_ref, wk2_ref, wv1_ref, wv2_ref, kc_ref, vc_ref,
                           raw_ref, rows_ref, sem_ref):
    step = pl.program_id(0)
    n_steps = pl.num_programs(0)
    nb = PAGES_PER_STEP * PAGE_SIZE // CMP_BLOCK
    kvw = 2 * HEAD_DIM

    def page_copy(s, slot, p):
        return pltpu.make_async_copy(cache_ref.at[pt_ref[s * PAGES_PER_STEP + p]], raw_ref.at[slot, p], sem_ref.at[slot])

    def issue(s, slot):
        for p in range(PAGES_PER_STEP):
            page_copy(s, slot, p).start()

    slot = step % 2

    @pl.when(step == 0)
    def _():
        issue(step, slot)

    @pl.when(step + 1 < n_steps)
    def _():
        issue(step + 1, 1 - slot)

    for p in range(PAGES_PER_STEP):
        page_copy(step, slot, p).wait()

    eye = (lax.broadcasted_iota(jnp.int32, (PAGE_SIZE, PAGE_SIZE), 0)
           == lax.broadcasted_iota(jnp.int32, (PAGE_SIZE, PAGE_SIZE), 1)).astype(BF16)
    for p in range(PAGES_PER_STEP):
        page = lax.dot_general(eye, raw_ref[slot, p].astype(BF16), (((1,), (1,)), ((), ())),
                               preferred_element_type=F32)
        for g in range(N_KV):
            for n in range(PAGE_SIZE // CMP_BLOCK):
                row0 = (p * (PAGE_SIZE // CMP_BLOCK) + n) * BLOCK_PITCH
                rows_ref[g, row0:row0 + CMP_BLOCK, :] = page[n * CMP_BLOCK:(n + 1) * CMP_BLOCK, g * kvw:(g + 1) * kvw]

    kc, vc = _compress_rows([rows_ref.at[g] for g in range(N_KV)], pe_ref, wk1_ref, wk2_ref, wv1_ref, wv2_ref, nb,
                            pitch=BLOCK_PITCH)
    for g in range(N_KV):
        kc_ref[0, g] = kc[g * nb:(g + 1) * nb]
        vc_ref[0, g] = vc[g * nb:(g + 1) * nb]


def _compress_paged(cache_t, page_table, cmp_w):
    db, n_pages = page_table.shape
    steps_per_row = n_pages // PAGES_PER_STEP
    nb = PAGES_PER_STEP * PAGE_SIZE // CMP_BLOCK
    out = jax.ShapeDtypeStruct((db, N_KV, steps_per_row * nb, HEAD_DIM), F32)
    ospec = pl.BlockSpec((1, N_KV, nb, HEAD_DIM), lambda i, pt: (i // steps_per_row, 0, i % steps_per_row, 0))
    grid_spec = pltpu.PrefetchScalarGridSpec(
        num_scalar_prefetch=1,
        grid=(db * steps_per_row,),
        in_specs=[pl.BlockSpec(memory_space=pl.ANY)] + _cmp_weight_specs(),
        out_specs=(ospec, ospec),
        scratch_shapes=[pltpu.VMEM((2, PAGES_PER_STEP, COL_KV, PAGE_SIZE), F32),
                        pltpu.VMEM((N_KV, nb * BLOCK_PITCH, 2 * HEAD_DIM), F32),
                        pltpu.SemaphoreType.DMA((2,))],
    )
    return pl.pallas_call(
        _compress_paged_kernel,
        grid_spec=grid_spec,
        out_shape=(out, out),
        compiler_params=_cparams("arbitrary"),
        name="compress_paged",
    )(page_table.reshape(-1), cache_t, *cmp_w)


TQ = 2 * LANES
KC = 512
LOG2_E = 1.4426950408889634
BOUND_SLACK = 1.01
MAX_SHIFT = 60.0
WIN_KEYS = WINDOW + TQ
V_ROWS = HEAD_DIM + 16
KAUG = 2 * LANES


def _top_blocks(imp, cand, n_blocks):
    blk = lax.broadcasted_iota(jnp.int32, imp.shape, 0)
    score = jnp.where(cand, imp, -1.0)
    for _ in range(N_SEL):
        mx = jnp.max(score, axis=0, keepdims=True)
        idx = jnp.min(jnp.where(score == mx, blk, n_blocks), axis=0, keepdims=True)
        score = jnp.where(blk == idx, -2.0, score)
    return jnp.where(score < -1.5, 1.0, 0.0)


def _prompt_attn_kernel(q_ref, kc_ref, vct_ref, ksel_ref, vselt_ref, kwin_ref, vwint_ref, gate_ref, o_ref,
                        kmax_ref, acc_ref):
    qt = pl.program_id(2)
    t0 = qt * TQ
    nb = kc_ref.shape[2]
    width = Q_PER_KV * TQ

    q = q_ref[0].astype(F32) * (ATT_SCALE * LOG2_E)
    q_t = q.T
    q4 = jnp.concatenate([q_t[r * HEAD_DIM:(r + 1) * HEAD_DIM] for r in range(Q_PER_KV)], axis=1)
    q4b = q4.astype(BF16)
    tok = t0 + lax.broadcasted_iota(jnp.int32, (1, width), 1) % TQ

    sc = jnp.dot(kc_ref[0, 0].astype(BF16), q4b, preferred_element_type=F32)
    blk = lax.broadcasted_iota(jnp.int32, (nb, width), 0)
    valid_c = (blk + 1) * CMP_BLOCK - 1 <= tok
    sc = jnp.where(valid_c, sc, NEG_INF)
    e = jnp.where(valid_c, jnp.exp2(sc - jnp.max(sc, axis=0, keepdims=True)), 0.0)
    den = jnp.sum(e, axis=0, keepdims=True)
    p = e / jnp.where(den > 0.0, den, 1.0)
    o_cmp = jnp.dot(vct_ref[0, 0].astype(BF16), p.astype(BF16), preferred_element_type=F32)
    imp = p[:, 0:TQ]
    for r in range(1, Q_PER_KV):
        imp = imp + p[:, r * TQ:(r + 1) * TQ]

    tok1 = t0 + lax.broadcasted_iota(jnp.int32, (1, TQ), 1)
    own = tok1 // SEL_BLOCK
    blk1 = lax.broadcasted_iota(jnp.int32, (nb, TQ), 0)
    cand = blk1 < own
    sel = _top_blocks(imp, cand, nb)
    bias = jnp.where(cand, jnp.where(sel > 0.0, 0.0, NEG_INF), jnp.where(blk1 == own, 0.0, NEG_INF))
    bias4 = jnp.concatenate([bias] * Q_PER_KV, axis=1).astype(BF16)
    q_aug = jnp.concatenate([q4b, bias4, jnp.zeros((KAUG - HEAD_DIM - nb, width), BF16)], axis=0)

    def scores(c):
        return jnp.dot(ksel_ref[0, 0, pl.ds(pl.multiple_of(c * KC, KC), KC), :], q_aug, preferred_element_type=F32)

    def softmax_pv(c, s, m, acc):
        m_new = jnp.maximum(m, jnp.max(s, axis=0, keepdims=True))
        alpha = jnp.exp2(m - m_new)
        pr = jnp.exp2(s - m_new).astype(BF16)
        v_blk = vselt_ref[0, 0, :, pl.ds(pl.multiple_of(c * KC, KC), KC)]
        return m_new, alpha * acc + jnp.dot(v_blk, pr, preferred_element_type=F32)

    def sel_step(c, carry):
        s, m, acc = carry
        s_next = scores(c + 1)
        return (s_next,) + softmax_pv(c, s, m, acc)

    last = t0 // KC
    key = last * KC + lax.broadcasted_iota(jnp.int32, (KC, width), 0)

    def online_softmax():
        init = (scores(0), jnp.full((1, width), NEG_INF, F32), jnp.zeros((V_ROWS, width), F32))
        s_last, m_sel, acc = lax.fori_loop(0, last, sel_step, init)
        return softmax_pv(last, jnp.where(key <= tok, s_last, NEG_INF), m_sel, acc)[1]

    @pl.when(qt == 0)
    def _():
        def body(i, kmax):
            k = ksel_ref[0, 0, pl.ds(pl.multiple_of(i * KC, KC), KC), :].astype(F32)
            k = jnp.where(lax.broadcasted_iota(jnp.int32, k.shape, 1) < HEAD_DIM, k, 0.0)
            return jnp.maximum(kmax, jnp.max(jnp.sum(k * k, axis=1, keepdims=True), axis=0, keepdims=True))
        kmax_ref[...] = jnp.broadcast_to(lax.fori_loop(0, ksel_ref.shape[2] // KC, body, jnp.zeros((1, 1), F32)),
                                         kmax_ref.shape)

    q4f = q4b.astype(F32)
    bound = jnp.sqrt(jnp.sum(q4f * q4f, axis=0, keepdims=True) * kmax_ref[0:1, 0:1]) * BOUND_SLACK
    bounded = jnp.max(bound) <= MAX_SHIFT

    def bounded_pv(c, s, acc):
        v_blk = vselt_ref[0, 0, :, pl.ds(pl.multiple_of(c * KC, KC), KC)]
        return acc + jnp.dot(v_blk, jnp.exp2(s - bound).astype(BF16), preferred_element_type=F32)

    @pl.when(bounded)
    def _():
        acc = lax.fori_loop(0, last, lambda c, acc: bounded_pv(c, scores(c), acc), jnp.zeros((V_ROWS, width), F32))
        acc_ref[...] = bounded_pv(last, jnp.where(key <= tok, scores(last), NEG_INF), acc)

    @pl.when(jnp.logical_not(bounded))
    def _():
        acc_ref[...] = online_softmax()

    acc_sel = acc_ref[...]
    o_sel = acc_sel[0:HEAD_DIM] / acc_sel[HEAD_DIM:HEAD_DIM + 1]

    w0 = pl.multiple_of(jnp.maximum(t0 - WINDOW, 0), TQ)
    sw = jnp.dot(kwin_ref[0, 0, pl.ds(w0, WIN_KEYS), :], q4b, preferred_element_type=F32)
    dist = tok - (w0 + lax.broadcasted_iota(jnp.int32, (WIN_KEYS, width), 0))
    sw = jnp.where((dist >= 0) & (dist < WINDOW), sw, NEG_INF)
    pw = jnp.exp2(sw - jnp.max(sw, axis=0, keepdims=True)).astype(BF16)
    acc_win = jnp.dot(vwint_ref[0, 0, :, pl.ds(w0, WIN_KEYS)], pw, preferred_element_type=F32)
    o_win = acc_win[0:HEAD_DIM] / acc_win[HEAD_DIM:HEAD_DIM + 1]

    outs = []
    for r in range(Q_PER_KV):
        sl = slice(r * TQ, (r + 1) * TQ)
        g = [gate_ref[0, 0, j * Q_PER_KV + r:j * Q_PER_KV + r + 1, :] for j in range(3)]
        outs.append(g[0] * o_cmp[:, sl] + g[1] * o_sel[:, sl] + g[2] * o_win[:, sl])
    o_ref[0] = jnp.concatenate(outs, axis=0).T


def _prompt_attention(q3, k_c, v_ct, ksel_aug, vsel_t, kwin, vwin_t, gates_t):
    b, t, _ = q3.shape
    nb = k_c.shape[2]
    width = Q_PER_KV * HEAD_DIM
    per_bg = lambda i, g, j: (i, g, 0, 0)
    return pl.pallas_call(
        _prompt_attn_kernel,
        grid=(b, N_KV, t // TQ),
        in_specs=[
            pl.BlockSpec((1, TQ, width), lambda i, g, j: (i, j, g)),
            pl.BlockSpec((1, 1, nb, HEAD_DIM), per_bg),
            pl.BlockSpec((1, 1, HEAD_DIM, nb), per_bg),
            pl.BlockSpec((1, 1, t, KAUG), per_bg),
            pl.BlockSpec((1, 1, V_ROWS, t), per_bg),
            pl.BlockSpec((1, 1, t, HEAD_DIM), per_bg),
            pl.BlockSpec((1, 1, V_ROWS, t), per_bg),
            pl.BlockSpec((1, 1, 3 * Q_PER_KV, TQ), lambda i, g, j: (i, g, 0, j)),
        ],
        out_specs=pl.BlockSpec((1, TQ, width), lambda i, g, j: (i, j, g)),
        out_shape=jax.ShapeDtypeStruct((b, t, COL_Q), F32),
        scratch_shapes=[pltpu.VMEM((SUBLANES, LANES), F32), pltpu.VMEM((V_ROWS, Q_PER_KV * TQ), F32)],
        compiler_params=_cparams("arbitrary", "arbitrary", "arbitrary"),
        name="prompt_attention",
    )(q3, k_c, v_ct, ksel_aug, vsel_t, kwin, vwin_t, gates_t)


def _sample_cmp_kernel(q_ref, kc_ref, vc_ref, ocmp_ref, idx_ref, imp_ref, *, past_len, dec_seq):
    b = pl.program_id(0)
    nb = kc_ref.shape[2]
    rows = Q_PER_KV * SUBLANES
    t_row = lax.broadcasted_iota(jnp.int32, (rows, nb), 0) % SUBLANES
    blk = lax.broadcasted_iota(jnp.int32, (rows, nb), 1)
    valid = (blk + 1) * CMP_BLOCK - 1 <= past_len + t_row
    for g in range(N_KV):
        s = lax.dot_general(q_ref[0, g], kc_ref[0, g].astype(BF16), (((1,), (1,)), ((), ())),
                            preferred_element_type=F32) * ATT_SCALE
        s = jnp.where(valid, s, NEG_INF)
        e = jnp.where(valid, jnp.exp(s - jnp.max(s, axis=1, keepdims=True)), 0.0)
        den = jnp.sum(e, axis=1, keepdims=True)
        p = e / jnp.where(den > 0.0, den, 1.0)
        ocmp_ref[0, g] = jnp.dot(p.astype(BF16), vc_ref[0, g].astype(BF16), preferred_element_type=F32)
        imp = p[0:SUBLANES]
        for r in range(1, Q_PER_KV):
            imp = imp + p[r * SUBLANES:(r + 1) * SUBLANES]
        row0 = pl.multiple_of((b * N_KV + g) * SUBLANES, SUBLANES)
        imp_ref[pl.ds(row0, SUBLANES), :] = imp

    @pl.when(b == pl.num_programs(0) - 1)
    def _():
        n_rows = imp_ref.shape[0]
        lane = lax.broadcasted_iota(jnp.int32, (n_rows, nb), 1)
        own = (past_len + lax.broadcasted_iota(jnp.int32, (n_rows, nb), 0) % SUBLANES) // SEL_BLOCK
        score = jnp.where(lane < own, imp_ref[...], -1.0)
        col = lax.broadcasted_iota(jnp.int32, (n_rows, LANES), 1)
        picks = jnp.zeros((n_rows, LANES), jnp.int32)
        for i in range(N_SEL):
            mx = jnp.max(score, axis=1, keepdims=True)
            idx = jnp.min(jnp.where(score == mx, lane, nb), axis=1, keepdims=True)
            score = jnp.where(lane == idx, -2.0, score)
            picks = jnp.where(col == i, idx, picks)
        idx_ref[...] = picks


def _sample_cmp_select(q_rt, k_c, v_c, past_len, dec_seq):
    db, _, rows, _ = q_rt.shape
    nb = k_c.shape[2]
    spec4 = lambda r, c: pl.BlockSpec((1, N_KV, r, c), lambda i: (i, 0, 0, 0))
    n_rows = db * N_KV * SUBLANES
    return pl.pallas_call(
        functools.partial(_sample_cmp_kernel, past_len=past_len, dec_seq=dec_seq),
        grid=(db,),
        in_specs=[spec4(rows, HEAD_DIM), spec4(nb, HEAD_DIM), spec4(nb, HEAD_DIM)],
        out_specs=(spec4(rows, HEAD_DIM), pl.BlockSpec((n_rows, LANES), lambda i: (0, 0))),
        out_shape=(jax.ShapeDtypeStruct((db, N_KV, rows, HEAD_DIM), F32),
                   jax.ShapeDtypeStruct((n_rows, LANES), jnp.int32)),
        scratch_shapes=[pltpu.VMEM((n_rows, nb), F32)],
        compiler_params=_cparams("arbitrary"),
        name="sample_cmp_select",
    )(q_rt, k_c, v_c)


def _sample_attn_kernel(idx_ref, pt_ref, cache_ref, q_ref, snew_ref, wnew_ref, wstate_ref, ocmp_ref, gate_ref,
                        o_ref, kv_ref, sem_ref, *, dec_seq, n_pages):
    b = pl.program_id(0)
    n_b = pl.num_programs(0)
    kvw = 2 * HEAD_DIM
    blocks_per_page = PAGE_SIZE // SEL_BLOCK

    def block_id(bb, g, t, i):
        return idx_ref[((bb * N_KV + g) * dec_seq + t) * N_SEL + i]

    def slab_copy(bb, slot, g, t, i):
        page = pt_ref[bb * n_pages + block_id(bb, g, t, i) // blocks_per_page]
        return pltpu.make_async_copy(cache_ref.at[page, pl.ds(g * kvw, kvw), :], kv_ref.at[slot, g, t, i],
                                     sem_ref.at[slot])

    def for_all_slabs(fn):
        for g in range(N_KV):
            for t in range(dec_seq):
                for i in range(N_SEL):
                    fn(g, t, i)

    slot = b % 2

    @pl.when(b == 0)
    def _():
        for_all_slabs(lambda g, t, i: slab_copy(b, slot, g, t, i).start())

    @pl.when(b + 1 < n_b)
    def _():
        for_all_slabs(lambda g, t, i: slab_copy(b + 1, 1 - slot, g, t, i).start())

    rows = dec_seq * SUBLANES
    tok = lax.broadcasted_iota(jnp.int32, (rows, 1), 0) // SUBLANES
    n_state = wstate_ref.shape[2]

    def attend(q, k_t, v_t, valid):
        s = jnp.dot(q, k_t.astype(BF16), preferred_element_type=F32) * ATT_SCALE
        s = jnp.where(valid, s, NEG_INF)
        p = jnp.exp(s - jnp.max(s, axis=1, keepdims=True))
        den = jnp.sum(p, axis=1, keepdims=True)
        return lax.dot_general(p.astype(BF16), v_t.astype(BF16), (((1,), (1,)), ((), ())),
                               preferred_element_type=F32) / den

    o_win = []
    for g in range(N_KV):
        k_rows, v_rows = pl.ds(g * kvw, HEAD_DIM), pl.ds(g * kvw + HEAD_DIM, HEAD_DIM)
        k_t = jnp.concatenate([wstate_ref[0, k_rows, :], wnew_ref[0, k_rows, :]], axis=1)
        v_t = jnp.concatenate([wstate_ref[0, v_rows, :], wnew_ref[0, v_rows, :]], axis=1)
        lane = lax.broadcasted_iota(jnp.int32, (rows, n_state + LANES), 1)
        new_i = lane - n_state
        valid = ((lane < n_state) & (lane > tok)) | ((new_i >= 0) & (new_i <= tok) & (new_i < dec_seq))
        o_win.append(attend(q_ref[0, g], k_t, v_t, valid))

    for_all_slabs(lambda g, t, i: slab_copy(b, slot, g, t, i).wait())

    lane1 = lax.broadcasted_iota(jnp.int32, (1, PAGE_SIZE), 1)
    for g in range(N_KV):
        o_sel = []
        for t in range(dec_seq):
            k_parts = [kv_ref[slot, g, t, i, 0:HEAD_DIM, :] for i in range(N_SEL)]
            v_parts = [kv_ref[slot, g, t, i, HEAD_DIM:kvw, :] for i in range(N_SEL)]
            k_parts.append(snew_ref[0, g * kvw:g * kvw + HEAD_DIM, :])
            v_parts.append(snew_ref[0, g * kvw + HEAD_DIM:(g + 1) * kvw, :])
            halves = [lane1 // SEL_BLOCK == block_id(b, g, t, i) % blocks_per_page for i in range(N_SEL)]
            halves.append((lane1 <= t) & (lane1 < dec_seq))
            q = q_ref[0, g, t * SUBLANES:(t + 1) * SUBLANES, :]
            o_sel.append(attend(q, jnp.concatenate(k_parts, axis=1), jnp.concatenate(v_parts, axis=1),
                                jnp.concatenate(halves, axis=1)))
        o_sel = jnp.concatenate(o_sel, axis=0)
        o_ref[0, g] = (gate_ref[0, g, 0] * ocmp_ref[0, g] + gate_ref[0, g, 1] * o_sel
                       + gate_ref[0, g, 2] * o_win[g])


def _sample_attention(sel_idx, page_table, cache_t, q_tr, snew_t, wnew_t, wstate_t, ocmp_tr, gates_tr, dec_seq):
    db, n_pages = page_table.shape
    rows = dec_seq * SUBLANES
    n_state = wstate_t.shape[2]
    per_b4 = lambda i, *_: (i, 0, 0, 0)
    per_b3 = lambda i, *_: (i, 0, 0)
    grid_spec = pltpu.PrefetchScalarGridSpec(
        num_scalar_prefetch=2,
        grid=(db,),
        in_specs=[
            pl.BlockSpec(memory_space=pl.ANY),
            pl.BlockSpec((1, N_KV, rows, HEAD_DIM), per_b4),
            pl.BlockSpec((1, COL_KV, LANES), per_b3),
            pl.BlockSpec((1, COL_KV, LANES), per_b3),
            pl.BlockSpec((1, COL_KV, n_state), per_b3),
            pl.BlockSpec((1, N_KV, rows, HEAD_DIM), per_b4),
            pl.BlockSpec((1, N_KV, 3, rows, HEAD_DIM), lambda i, *_: (i, 0, 0, 0, 0)),
        ],
        out_specs=pl.BlockSpec((1, N_KV, rows, HEAD_DIM), per_b4),
        scratch_shapes=[pltpu.VMEM((2, N_KV, dec_seq, N_SEL, 2 * HEAD_DIM, PAGE_SIZE), F32),
                        pltpu.SemaphoreType.DMA((2,))],
    )
    return pl.pallas_call(
        functools.partial(_sample_attn_kernel, dec_seq=dec_seq, n_pages=n_pages),
        grid_spec=grid_spec,
        out_shape=jax.ShapeDtypeStruct((db, N_KV, rows, HEAD_DIM), F32),
        compiler_params=_cparams("arbitrary"),
        name="sample_attention",
    )(sel_idx, page_table.reshape(-1), cache_t, q_tr, snew_t, wnew_t, wstate_t, ocmp_tr, gates_tr)


ROUTE_E1, ROUTE_E2, ROUTE_W1, ROUTE_W2 = 0, 1, 2, 3


def _merge_kernel(conv_ref, att_ref, x_ref, gc_ref, ga_ref, wo_ref, ln2_ref, wr_ref, x1_ref, h3_ref, route_ref):
    mix = jnp.concatenate([_rms(conv_ref[...], gc_ref[...]), _rms(att_ref[...], ga_ref[...])], axis=1)
    x1 = x_ref[...] + jnp.dot(mix.astype(BF16), wo_ref[...], preferred_element_type=F32)
    x1_ref[...] = x1
    h = _rms(x1, ln2_ref[...])
    _rows_to_tiles(h3_ref, h)

    logits = jnp.dot(h.astype(BF16), wr_ref[...], preferred_element_type=F32)
    lane = lax.broadcasted_iota(jnp.int32, logits.shape, 1)
    is_g = lane < N_GROUPS
    lg = jnp.where(is_g, logits, NEG_INF)
    mg = jnp.max(lg, axis=1, keepdims=True)
    sg = jnp.sum(jnp.where(is_g, jnp.exp(lg - mg), 0.0), axis=1, keepdims=True)
    grp = jnp.min(jnp.where(lg == mg, lane, LANES), axis=1, keepdims=True)
    p_top = 1.0 / sg
    in_grp = ((lane + (EXPERTS_PER_GROUP - N_GROUPS)) // EXPERTS_PER_GROUP) == grp + 1
    le = jnp.where(in_grp, logits, NEG_INF)
    ee = jnp.where(in_grp, jnp.exp(le - jnp.max(le, axis=1, keepdims=True)), 0.0)
    pe = jnp.where(in_grp, ee / jnp.sum(ee, axis=1, keepdims=True), -1.0)
    p1 = jnp.max(pe, axis=1, keepdims=True)
    i1 = jnp.min(jnp.where(pe == p1, lane, LANES), axis=1, keepdims=True)
    pe2 = jnp.where(lane == i1, -1.0, pe)
    p2 = jnp.max(pe2, axis=1, keepdims=True)
    i2 = jnp.min(jnp.where(pe2 == p2, lane, LANES), axis=1, keepdims=True)
    den = p1 + p2
    rec = jnp.where(lane == ROUTE_E1, (i1 - N_GROUPS).astype(F32), 0.0)
    rec = jnp.where(lane == ROUTE_E2, (i2 - N_GROUPS).astype(F32), rec)
    rec = jnp.where(lane == ROUTE_W1, p1 / den * p_top, rec)
    rec = jnp.where(lane == ROUTE_W2, p2 / den * p_top, rec)
    route_ref[...] = rec


def _merge(conv2d, att2d, x2d, g_conv, g_att, w_out_b, ln2, w_route_b, tm):
    n = x2d.shape[0]
    row = lambda i: (i, 0)
    const = lambda i: (0, 0)
    x1, h_tiles, route = pl.pallas_call(
        _merge_kernel,
        grid=(n // tm,),
        in_specs=[
            pl.BlockSpec((tm, CONV_CH), row),
            pl.BlockSpec((tm, COL_Q), row),
            pl.BlockSpec((tm, D_MODEL), row),
            pl.BlockSpec((1, CONV_CH), const),
            pl.BlockSpec((1, COL_Q), const),
            pl.BlockSpec((CONV_CH + COL_Q, D_MODEL), const),
            pl.BlockSpec((1, D_MODEL), const),
            pl.BlockSpec((D_MODEL, LANES), const),
        ],
        out_specs=(
            pl.BlockSpec((tm, D_MODEL), row),
            pl.BlockSpec((tm * ROW_TILES, LANES), row),
            pl.BlockSpec((tm, LANES), row),
        ),
        out_shape=(
            jax.ShapeDtypeStruct((n, D_MODEL), F32),
            jax.ShapeDtypeStruct((n * ROW_TILES, LANES), F32),
            jax.ShapeDtypeStruct((n, LANES), F32),
        ),
        compiler_params=_cparams("arbitrary"),
        name="merge_route",
    )(conv2d, att2d, x2d, g_conv.reshape(1, -1), g_att.reshape(1, -1), w_out_b, ln2.reshape(1, -1), w_route_b)
    return x1, h_tiles.reshape(n, ROW_TILES, LANES), route


META_W = 4 * LANES
ROUTE_TILE = 512


def _rank_kernel(route_ref, dest_ref, meta_ref, e_ref, rank_ref, count_ref):
    i = pl.program_id(0)
    rt = route_ref.shape[0]

    @pl.when(i == 0)
    def _():
        count_ref[...] = jnp.zeros(count_ref.shape, F32)

    route_t = route_ref[...].T
    expert_id = lax.broadcasted_iota(jnp.int32, (N_EXPERTS, rt), 0).astype(F32)
    before = (lax.broadcasted_iota(jnp.int32, (rt, rt), 0) < lax.broadcasted_iota(jnp.int32, (rt, rt), 1))
    before = before.astype(BF16)
    ones = jnp.ones((rt, LANES), BF16)
    e_rows, rank_rows = [], []
    for k in range(TOP_K):
        e_k = route_t[ROUTE_E1 + k:ROUTE_E1 + k + 1, :]
        onehot = (expert_id == e_k).astype(F32)
        earlier = jnp.dot(onehot.astype(BF16), before, preferred_element_type=F32)
        seen = count_ref[...]
        seen_w = jnp.concatenate([seen] * (rt // LANES), axis=1)
        rank_rows.append(jnp.sum(onehot * (earlier + seen_w), axis=0, keepdims=True))
        e_rows.append(e_k)
        count_ref[...] = seen + jnp.dot(onehot.astype(BF16), ones, preferred_element_type=F32)
    e_ref[i] = jnp.concatenate(e_rows, axis=0)
    rank_ref[i] = jnp.concatenate(rank_rows, axis=0)

    @pl.when(i == pl.num_programs(0) - 1)
    def _():
        counts = jnp.concatenate([count_ref[...]] * (META_W // LANES), axis=1)
        padded = jnp.floor((counts + (MOE_ROWS - 1)) * (1.0 / MOE_ROWS)) * MOE_ROWS
        lane = lax.broadcasted_iota(jnp.int32, (1, META_W), 1)
        chunk_start = lane.astype(F32) * MOE_ROWS
        chunk_e = jnp.zeros((1, META_W), F32)
        ends = jnp.zeros((1, META_W), F32)
        end = jnp.zeros((1, META_W), F32)
        starts = []
        for ex in range(N_EXPERTS):
            starts.append(end[:, 0:rt])
            end = end + padded[ex:ex + 1, :]
            chunk_e = chunk_e + (end <= chunk_start).astype(F32)
            ends = jnp.where(lane == ex, end, ends)
        n_used = end * (1.0 / MOE_ROWS)
        chunk_e = jnp.minimum(chunk_e, float(N_EXPERTS - 1))
        row = lax.broadcasted_iota(jnp.int32, (SUBLANES, META_W), 0)
        meta = jnp.where(row == 0, chunk_e, jnp.where(row == 1, n_used, jnp.where(row == 2, ends, 0.0)))
        meta_ref[...] = meta.astype(jnp.int32)

        def place(ti, carry):
            e_t = e_ref[ti]
            dest = rank_ref[ti]
            for ex in range(N_EXPERTS):
                dest = dest + jnp.where(e_t == float(ex), starts[ex], 0.0)
            dest_ref[ti] = dest.astype(jnp.int32)
            return carry

        lax.fori_loop(0, pl.num_programs(0), place, 0)


def _rank(route, rt):
    n = route.shape[0]
    tiles = n // rt
    assert n * TOP_K // MOE_ROWS + N_EXPERTS <= META_W and rt <= META_W, "chunk table is one row of META_W lanes"
    whole = lambda i: (0, 0, 0)
    return pl.pallas_call(
        _rank_kernel,
        grid=(tiles,),
        in_specs=[pl.BlockSpec((rt, LANES), lambda i: (i, 0))],
        out_specs=(pl.BlockSpec((tiles, TOP_K, rt), whole), pl.BlockSpec((SUBLANES, META_W), lambda i: (0, 0))),
        out_shape=(jax.ShapeDtypeStruct((tiles, TOP_K, rt), jnp.int32),
                   jax.ShapeDtypeStruct((SUBLANES, META_W), jnp.int32)),
        scratch_shapes=[pltpu.VMEM((tiles, TOP_K, rt), F32), pltpu.VMEM((tiles, TOP_K, rt), F32),
                        pltpu.VMEM((N_EXPERTS, LANES), F32)],
        compiler_params=_cparams("arbitrary"),
        name="route_rank",
    )(route)


def _row_copies_wait(src_ref, dst_ref, sem, n_rows):
    pltpu.make_async_copy(src_ref.at[pl.ds(0, n_rows)], dst_ref.at[pl.ds(0, n_rows)], sem).wait()


def _scatter_kernel(meta_ref, dest_ref, h3_ref, xs_ref, zero_ref, sem_ref):
    i = pl.program_id(0)
    rt = h3_ref.shape[0]
    n_chunks = xs_ref.shape[0] // MOE_ROWS

    @pl.when(i == 0)
    def _():
        zero_ref[...] = jnp.zeros(zero_ref.shape, F32)

        def zero_chunk(first_row):
            return pltpu.make_async_copy(zero_ref, xs_ref.at[pl.ds(first_row, MOE_ROWS)], sem_ref.at[1])

        def fills(act):
            prev = 0
            for ex in range(N_EXPERTS):
                end = meta_ref[2, ex]

                @pl.when(end > prev)
                def _():
                    act(zero_chunk(end - MOE_ROWS))
                prev = end
            for c in range(n_chunks - N_EXPERTS, n_chunks):
                @pl.when(c >= meta_ref[1, 0])
                def _():
                    act(zero_chunk(c * MOE_ROWS))

        fills(lambda copy: copy.start())
        fills(lambda copy: copy.wait())

    def issue(t, carry):
        for k in range(TOP_K):
            pltpu.make_async_copy(h3_ref.at[t], xs_ref.at[dest_ref[0, k, t]], sem_ref.at[0]).start(priority=k % 2)
        return carry

    lax.fori_loop(0, rt, issue, 0, unroll=8)
    _row_copies_wait(h3_ref, h3_ref, sem_ref.at[0], rt)
    _row_copies_wait(h3_ref, h3_ref, sem_ref.at[0], rt)


def _scatter_rows(h3, dest, meta, rt):
    n = h3.shape[0]
    n_slots = (n * TOP_K // MOE_ROWS + N_EXPERTS) * MOE_ROWS
    grid_spec = pltpu.PrefetchScalarGridSpec(
        num_scalar_prefetch=1,
        grid=(n // rt,),
        in_specs=[pl.BlockSpec((1, TOP_K, rt), lambda i, meta: (i, 0, 0), memory_space=pltpu.SMEM),
                  pl.BlockSpec((rt, ROW_TILES, LANES), lambda i, meta: (i, 0, 0))],
        out_specs=pl.BlockSpec(memory_space=pl.ANY),
        scratch_shapes=[pltpu.VMEM((MOE_ROWS, ROW_TILES, LANES), F32), pltpu.SemaphoreType.DMA((2,))],
    )
    return pl.pallas_call(
        _scatter_kernel,
        grid_spec=grid_spec,
        out_shape=jax.ShapeDtypeStruct((n_slots, ROW_TILES, LANES), F32),
        compiler_params=_cparams("arbitrary"),
        name="moe_scatter_rows",
    )(meta, dest, h3)


def _expert_kernel(meta_ref, xs_ref, wg_ref, wu_ref, wd_ref, ys_ref, wg_b, wu_b, wd_b):
    c = pl.program_id(0)
    in_use = c < meta_ref[1, 0]

    @pl.when(jnp.logical_not(in_use))
    def _():
        ys_ref[...] = jnp.zeros(ys_ref.shape, F32)

    @pl.when(in_use & ((c == 0) | (meta_ref[0, c] != meta_ref[0, jnp.maximum(c - 1, 0)])))
    def _():
        wg_b[...] = wg_ref[0].astype(BF16)
        wu_b[...] = wu_ref[0].astype(BF16)
        wd_b[...] = wd_ref[0].astype(BF16)

    @pl.when(in_use)
    def _():
        x = _tiles_to_rows(xs_ref).astype(BF16)
        gate = jnp.dot(x, wg_b[...], preferred_element_type=F32)
        up = jnp.dot(x, wu_b[...], preferred_element_type=F32)
        act = (gate * jax.nn.sigmoid(gate) * up).astype(BF16)
        _rows_to_tiles(ys_ref, jnp.dot(act, wd_b[...], preferred_element_type=F32))


def _expert_mlp(xs, meta, w_g, w_u, w_d):
    n_chunks = xs.shape[0] // MOE_ROWS
    rows = lambda c, meta: (c, 0)
    expert = lambda c, meta: (meta[0, jnp.minimum(c, meta[1, 0] - 1)], 0, 0)
    grid_spec = pltpu.PrefetchScalarGridSpec(
        num_scalar_prefetch=1,
        grid=(n_chunks,),
        in_specs=[
            pl.BlockSpec((MOE_ROWS * ROW_TILES, LANES), rows),
            pl.BlockSpec((1, D_MODEL, D_EXPERT), expert),
            pl.BlockSpec((1, D_MODEL, D_EXPERT), expert),
            pl.BlockSpec((1, D_EXPERT, D_MODEL), expert),
        ],
        out_specs=pl.BlockSpec((MOE_ROWS * ROW_TILES, LANES), rows),
        scratch_shapes=[pltpu.VMEM((D_MODEL, D_EXPERT), BF16), pltpu.VMEM((D_MODEL, D_EXPERT), BF16),
                        pltpu.VMEM((D_EXPERT, D_MODEL), BF16)],
    )
    return pl.pallas_call(
        _expert_kernel,
        grid_spec=grid_spec,
        out_shape=jax.ShapeDtypeStruct((xs.shape[0] * ROW_TILES, LANES), F32),
        compiler_params=_cparams("arbitrary"),
        name="expert_mlp",
    )(meta, xs.reshape(-1, LANES), w_g, w_u, w_d).reshape(xs.shape)


def _combine_kernel(dest_ref, nxt_ref, x1_ref, route_ref, ln_ref, ys_ref, o_ref, buf_ref, sem_ref):
    i = pl.program_id(0)
    n_tiles = pl.num_programs(0)
    rt = x1_ref.shape[0]

    def issue(table_ref, slot):
        def body(t, carry):
            for k in range(TOP_K):
                tile = pl.ds(pl.multiple_of(t * ROW_TILES, ROW_TILES), ROW_TILES)
                pltpu.make_async_copy(ys_ref.at[table_ref[0, k, t]], buf_ref.at[slot, k, tile],
                                      sem_ref.at[slot]).start(priority=k % 2)
            return carry
        lax.fori_loop(0, rt, body, 0, unroll=8)

    slot = i % 2

    @pl.when(i == 0)
    def _():
        issue(dest_ref, slot)

    @pl.when(i + 1 < n_tiles)
    def _():
        issue(nxt_ref, 1 - slot)

    for k in range(TOP_K):
        pltpu.make_async_copy(buf_ref.at[slot, k], buf_ref.at[slot, k], sem_ref.at[slot]).wait()
    route = route_ref[...]
    moe = None
    for k in range(TOP_K):
        y = _tiles_to_rows(buf_ref.at[slot, k])
        term = y * route[:, ROUTE_W1 + k:ROUTE_W1 + k + 1]
        moe = term if moe is None else moe + term
    o_ref[...] = _rms(x1_ref[...] + moe, ln_ref[...])


def _combine(x1, ys, dest, route, ln_final, rt):
    n = x1.shape[0]
    tiles = n // rt
    row = lambda i: (i, 0)
    table = lambda f: pl.BlockSpec((1, TOP_K, rt), f, memory_space=pltpu.SMEM)
    return pl.pallas_call(
        _combine_kernel,
        grid=(tiles,),
        in_specs=[
            table(lambda i: (i, 0, 0)),
            table(lambda i: (jnp.minimum(i + 1, tiles - 1), 0, 0)),
            pl.BlockSpec((rt, D_MODEL), row),
            pl.BlockSpec((rt, LANES), row),
            pl.BlockSpec((1, D_MODEL), lambda i: (0, 0)),
            pl.BlockSpec(memory_space=pl.ANY),
        ],
        out_specs=pl.BlockSpec((rt, D_MODEL), row),
        out_shape=jax.ShapeDtypeStruct((n, D_MODEL), F32),
        scratch_shapes=[pltpu.VMEM((2, TOP_K, rt * ROW_TILES, LANES), F32), pltpu.SemaphoreType.DMA((2,))],
        compiler_params=_cparams("arbitrary"),
        name="moe_combine_norm",
    )(dest, dest, x1, route, ln_final.reshape(1, -1), ys)


def _ffn(conv2d, att2d, x2d, g_conv, g_att, w_out_b, ln2, w_route_b, w_g, w_u, w_d, ln_final, tm):
    x1, h3, route = _merge(conv2d, att2d, x2d, g_conv, g_att, w_out_b, ln2, w_route_b, tm)
    rt = min(x2d.shape[0], ROUTE_TILE)
    dest, meta = _rank(route, rt)
    ys = _expert_mlp(_scatter_rows(h3, dest, meta, rt), meta, w_g, w_u, w_d)
    return _combine(x1, ys, dest, route, ln_final, rt)


def kernel(x_prompt, x_sample, cache_cmp_kv, cache_sel_kv, state_win_kv, state_conv, page_table, ln1, w_in, conv_dw_w, conv_dw_b, conv_ln_g, conv_ln_b, cmp_pos_emb, w_cmp_k1, w_cmp_k2, w_cmp_v1, w_cmp_v2, out_norm_conv, out_norm_att, w_out, ln2, w_router_group, w_router_expert, w_exp_gate, w_exp_up, w_exp_down, ln_final):
    depth = ln1.shape[0]
    assert depth == 1, "single-layer step"
    b, t, _ = x_prompt.shape
    db, ds, _ = x_sample.shape
    n_phys = cache_cmp_kv.shape[1]
    n_pages = page_table.shape[1]
    past = n_pages * PAGE_SIZE
    win_rows = state_win_kv.shape[2]
    assert ds < CMP_BLOCK and ds <= SUBLANES and past % SEL_BLOCK == 0 and past // SEL_BLOCK >= N_SEL
    assert win_rows == WINDOW and past >= WINDOW and t % KC == 0 and t >= WIN_KEYS

    w_in_b = jnp.pad(w_in[0], ((0, 0), (0, D_IN_PAD - D_IN))).astype(BF16)
    w_out_b = w_out[0].astype(BF16)
    w_route_b = jnp.pad(jnp.concatenate([w_router_group[0], w_router_expert[0]], axis=1),
                        ((0, 0), (0, LANES - N_GROUPS - N_EXPERTS))).astype(BF16)
    cmp_w = _cmp_weights(cmp_pos_emb[0], w_cmp_k1[0], w_cmp_k2[0], w_cmp_v1[0], w_cmp_v2[0])
    conv_w = (conv_dw_w[0], conv_dw_b[0], conv_ln_g[0], conv_ln_b[0])
    ffn_w = (out_norm_conv[0], out_norm_att[0], w_out_b, ln2[0], w_route_b, w_exp_gate[0], w_exp_up[0],
             w_exp_down[0], ln_final)

    xp2 = x_prompt.reshape(b * t, D_MODEL)
    c_p, s_p = _rope_tables(jnp.arange(t))
    (a_p, q_p, ckv_p, _, _, gate_p, ckv_pt, skv_pt, wkv_pt, ksel_aug, kwin_p, vsel_t, vwin_t) = _project(
        xp2, ln1[0], w_in_b, c_p, s_p, 512)
    a_p3 = a_p.reshape(b, t, CONV_CH)
    conv_p = _conv_module(a_p3, jnp.zeros((b, HIST_ROWS, CONV_CH), F32), *conv_w, 512)
    kc_p, vc_p = _compress_dense(ckv_p.reshape(b, t, COL_KV), cmp_w)
    gates_t = jnp.transpose(gate_p[:, :COL_GATE].reshape(b, t, N_KV, Q_PER_KV, 3), (0, 2, 4, 3, 1))
    att_p = _prompt_attention(q_p.reshape(b, t, COL_Q), kc_p, jnp.swapaxes(vc_p, 2, 3), ksel_aug,
                              vsel_t, kwin_p, vwin_t, gates_t.reshape(b, N_KV, 3 * Q_PER_KV, t))
    y_p = _ffn(conv_p.reshape(b * t, CONV_CH), att_p.reshape(b * t, COL_Q), xp2, *ffn_w, 256)

    n_s = db * ds
    xs2 = x_sample.reshape(n_s, D_MODEL)
    c_s, s_s = _rope_tables(jnp.tile(past + jnp.arange(ds), db))
    a_s, q_s, ckv_s, skv_s, wkv_s, gate_s = _project(xs2, ln1[0], w_in_b, c_s, s_s, n_s)[:6]
    a_s3 = a_s.reshape(db, ds, CONV_CH)
    hist_s = jnp.pad(state_conv[0], ((0, 0), (HIST_ROWS - (CONV_W - 1), 0), (0, 0)))
    conv_s = _conv_module(a_s3, hist_s, *conv_w, ds)
    rows_minor = lambda a, n, r: jnp.swapaxes(a.reshape(n, r, COL_KV), 1, 2)
    kc_s, vc_s = _compress_paged(rows_minor(cache_cmp_kv[0], n_phys, PAGE_SIZE), page_table, cmp_w)

    q5 = q_s.reshape(db, ds, N_KV, Q_PER_KV, HEAD_DIM)
    pad_tok = SUBLANES - ds
    pad_head = SUBLANES - Q_PER_KV
    q_rt = jnp.pad(jnp.transpose(q5, (0, 2, 3, 1, 4)), ((0, 0), (0, 0), (0, 0), (0, pad_tok), (0, 0)))
    q_rt = q_rt.reshape(db, N_KV, Q_PER_KV * SUBLANES, HEAD_DIM)
    q_tr = jnp.pad(jnp.transpose(q5, (0, 2, 1, 3, 4)), ((0, 0), (0, 0), (0, 0), (0, pad_head), (0, 0)))
    q_tr = q_tr.reshape(db, N_KV, ds * SUBLANES, HEAD_DIM)
    g5 = gate_s[:, :COL_GATE].reshape(db, ds, N_KV, Q_PER_KV, 3)
    g_tr = jnp.pad(jnp.transpose(g5, (0, 2, 4, 1, 3)), ((0, 0),) * 4 + ((0, pad_head),))
    g_tr = jnp.broadcast_to(g_tr.reshape(db, N_KV, 3, ds * SUBLANES)[..., None], (db, N_KV, 3, ds * SUBLANES, HEAD_DIM))
    ocmp, picks = _sample_cmp_select(q_rt, kc_s, vc_s, past, ds)
    sel_idx = picks.reshape(db, N_KV, SUBLANES, LANES)[:, :, :ds, :N_SEL].reshape(-1)
    ocmp_tr = jnp.transpose(ocmp.reshape(db, N_KV, Q_PER_KV, SUBLANES, HEAD_DIM)[:, :, :, :ds], (0, 1, 3, 2, 4))
    ocmp_tr = jnp.pad(ocmp_tr, ((0, 0), (0, 0), (0, 0), (0, pad_head), (0, 0)))
    ocmp_tr = ocmp_tr.reshape(db, N_KV, ds * SUBLANES, HEAD_DIM)
    new_t = lambda kv: jnp.pad(rows_minor(kv, db, ds), ((0, 0), (0, 0), (0, LANES - ds)))
    o_s = _sample_attention(sel_idx, page_table, rows_minor(cache_sel_kv[0], n_phys, PAGE_SIZE), q_tr,
                            new_t(skv_s), new_t(wkv_s), rows_minor(state_win_kv[0], db, win_rows),
                            ocmp_tr, g_tr, ds)
    att_s = o_s.reshape(db, N_KV, ds, SUBLANES, HEAD_DIM)[:, :, :, :Q_PER_KV]
    att_s = jnp.transpose(att_s, (0, 2, 1, 3, 4)).reshape(n_s, COL_Q)
    y_s = _ffn(conv_s.reshape(n_s, CONV_CH), att_s, xs2, *ffn_w, n_s)

    kv6 = lambda kv, bb, tt: kv.reshape(1, bb, tt, N_KV, 2, HEAD_DIM)
    kv6_t = lambda kv_t: jnp.swapaxes(kv_t, 1, 2).reshape(1, b, kv_t.shape[2], N_KV, 2, HEAD_DIM)
    new_win_s = jnp.concatenate([state_win_kv, kv6(wkv_s, db, ds)], axis=2)[:, :, ds:]
    new_conv_s = jnp.concatenate([state_conv[0], a_s3], axis=1)[None, :, ds:]
    return (y_p.reshape(b, t, D_MODEL), y_s.reshape(db, ds, D_MODEL),
            kv6_t(ckv_pt), kv6(ckv_s, db, ds), kv6_t(skv_pt), kv6(skv_s, db, ds),
            kv6_t(wkv_pt[:, :, t - min(WINDOW, t):]), new_win_s,
            a_p3[None, :, t - (CONV_W - 1):], new_conv_s)
```

```python
import functools

import jax
import jax.numpy as jnp
from jax import lax
from jax.experimental import pallas as pl
from jax.experimental.pallas import tpu as pltpu

D_MODEL = 1024
CONV_CH = 512
CONV_W = 31
N_HEADS = 8
HEAD_DIM = 64
N_KV = 2
Q_PER_KV = N_HEADS // N_KV
ROPE_DIM = HEAD_DIM // 4
ROPE_THETA = 500000.0
CMP_BLOCK = 64
SEL_BLOCK = CMP_BLOCK
N_SEL = 16
WINDOW = 512
CMP_HID = 2 * HEAD_DIM
COL_Q = N_HEADS * HEAD_DIM
COL_KV = 2 * N_KV * HEAD_DIM
COL_GATE = 3 * N_HEADS
D_IN = 2 * CONV_CH + COL_Q + 3 * COL_KV + COL_GATE
N_GROUPS = 4
EXPERTS_PER_GROUP = 8
N_EXPERTS = N_GROUPS * EXPERTS_PER_GROUP
TOP_K = 2
D_EXPERT = 512
PAGE_SIZE = 128
RMS_EPS = 1e-6
LN_EPS = 1e-5
NEG_INF = -1e30
ATT_SCALE = HEAD_DIM ** -0.5

LANES = 128
SUBLANES = 8
VMEM_LIMIT_BYTES = 56 * 1024 * 1024

D_IN_PAD = ((D_IN + LANES - 1) // LANES) * LANES
COL_GATE_OFF = 2 * CONV_CH + COL_Q + 3 * COL_KV
HIST_ROWS = 32
CONV_ROWS = 32
MOE_ROWS = 256
ROW_TILES = D_MODEL // LANES

BF16 = jnp.bfloat16
F32 = jnp.float32


def _cparams(*sem):
    return pltpu.CompilerParams(dimension_semantics=sem, vmem_limit_bytes=VMEM_LIMIT_BYTES)


def _rms(x, g):
    return x * lax.rsqrt(jnp.mean(x * x, axis=-1, keepdims=True) + RMS_EPS) * g


def _tiles_to_rows(tile_ref):
    n = tile_ref.shape[0] // ROW_TILES
    return jnp.concatenate([tile_ref[pl.ds(j, n, stride=ROW_TILES), :] for j in range(ROW_TILES)], axis=1)


def _rows_to_tiles(tile_ref, rows):
    for j in range(ROW_TILES):
        tile_ref[pl.ds(j, rows.shape[0], stride=ROW_TILES), :] = rows[:, j * LANES:(j + 1) * LANES]


def _rope_tables(pos):
    half = ROPE_DIM // 2
    inv = ROPE_THETA ** (-jnp.arange(half, dtype=F32) / half)
    ang = pos.astype(F32)[:, None] * inv
    cos, sin = jnp.cos(ang), jnp.sin(ang)
    m = jnp.arange(LANES) % HEAD_DIM
    idx = m % half
    c = jnp.where(m < ROPE_DIM, cos[:, idx], 1.0)
    s = jnp.where(m < half, -sin[:, idx], jnp.where(m < ROPE_DIM, sin[:, idx], 0.0))
    return c.astype(F32), s.astype(F32)


def _rope(v, c, s, first_half):
    w = v.shape[1]
    half = ROPE_DIM // 2
    partner = jnp.where(first_half, pltpu.roll(v, w - half, axis=1), pltpu.roll(v, half, axis=1))
    return v * c + partner * s


def _proj_kernel(x_ref, ln_ref, w_ref, c_ref, s_ref, a_ref, q_ref, ckv_ref, skv_ref, wkv_ref, gate_ref,
                 ckvt_ref, skvt_ref, wkvt_ref, ksel_ref, kwin_ref, vsel_ref, vwin_ref, *, t_tiles):
    x = x_ref[...]
    xn = _rms(x, ln_ref[...])
    p = jnp.dot(xn.astype(BF16), w_ref[...], preferred_element_type=F32)
    a_ref[...] = p[:, :CONV_CH] * jax.nn.sigmoid(p[:, CONV_CH:2 * CONV_CH])

    c128, s128 = c_ref[...], s_ref[...]
    tm = x.shape[0]
    lane_q = lax.broadcasted_iota(jnp.int32, (tm, COL_Q), 1)
    cq = jnp.concatenate([c128] * (COL_Q // LANES), axis=1)
    sq = jnp.concatenate([s128] * (COL_Q // LANES), axis=1)
    o = 2 * CONV_CH
    q = _rope(p[:, o:o + COL_Q], cq, sq, (lane_q % HEAD_DIM) < ROPE_DIM // 2)
    q_ref[...] = q.astype(q_ref.dtype)
    o += COL_Q

    lane_kv = lax.broadcasted_iota(jnp.int32, (tm, COL_KV), 1)
    is_k = (lane_kv % (2 * HEAD_DIM)) < HEAD_DIM
    ckv = jnp.where(is_k, jnp.concatenate([c128] * (COL_KV // LANES), axis=1), 1.0)
    skv = jnp.where(is_k, jnp.concatenate([s128] * (COL_KV // LANES), axis=1), 0.0)
    first_kv = (lane_kv % HEAD_DIM) < ROPE_DIM // 2
    kvs = []
    for ref, ref_t in ((ckv_ref, ckvt_ref), (skv_ref, skvt_ref), (wkv_ref, wkvt_ref)):
        kv = _rope(p[:, o:o + COL_KV], ckv, skv, first_kv)
        kv_t = kv.T
        ref[...] = kv
        ref_t[0] = kv_t
        kvs.append((kv, kv_t))
        o += COL_KV
    gate_ref[...] = jax.nn.sigmoid(p[:, o:o + LANES])

    kvw = 2 * HEAD_DIM
    lane = lax.broadcasted_iota(jnp.int32, (tm, LANES), 1)
    pos = (pl.program_id(0) % t_tiles) * tm + lax.broadcasted_iota(jnp.int32, (tm, LANES), 0)
    blk = pos // SEL_BLOCK
    is_key = lane < HEAD_DIM
    ones_rows = (lax.broadcasted_iota(jnp.int32, (V_ROWS - HEAD_DIM, tm), 0) == 0).astype(F32)
    (skv_v, skv_t), (wkv_v, wkv_t) = kvs[1], kvs[2]
    for g in range(N_KV):
        low = jnp.where(is_key, skv_v[:, g * kvw:(g + 1) * kvw], (blk == lane - HEAD_DIM).astype(F32))
        high = jnp.where(is_key, (blk == lane + HEAD_DIM).astype(F32), 0.0)
        ksel_ref[0, g] = jnp.concatenate([low, high], axis=1).astype(BF16)
        kwin_ref[0, g] = wkv_v[:, g * kvw:g * kvw + HEAD_DIM].astype(BF16)
        v_rows = slice(g * kvw + HEAD_DIM, (g + 1) * kvw)
        vsel_ref[0, g] = jnp.concatenate([skv_t[v_rows], ones_rows], axis=0).astype(BF16)
        vwin_ref[0, g] = jnp.concatenate([wkv_t[v_rows], ones_rows], axis=0).astype(BF16)


def _project(x2d, ln, w_pad, c_tab, s_tab, tm):
    n = x2d.shape[0]
    t = c_tab.shape[0]
    t_tiles = t // tm
    assert t // SEL_BLOCK <= KAUG - HEAD_DIM - HEAD_DIM, "one-hot block ids fit the augmented key"
    row = lambda i: (i, 0)
    tab = lambda i: (i % t_tiles, 0)
    const = lambda i: (0, 0)
    kv_t = jax.ShapeDtypeStruct((n // t, COL_KV, t), F32)
    kv_t_spec = pl.BlockSpec((1, COL_KV, tm), lambda i: (i // t_tiles, 0, i % t_tiles))
    keys = lambda w: (jax.ShapeDtypeStruct((n // t, N_KV, t, w), BF16),
                      pl.BlockSpec((1, N_KV, tm, w), lambda i: (i // t_tiles, 0, i % t_tiles, 0)))
    v_t = (jax.ShapeDtypeStruct((n // t, N_KV, V_ROWS, t), BF16),
           pl.BlockSpec((1, N_KV, V_ROWS, tm), lambda i: (i // t_tiles, 0, 0, i % t_tiles)))
    out_shape = (
        jax.ShapeDtypeStruct((n, CONV_CH), F32),
        jax.ShapeDtypeStruct((n, COL_Q), BF16),
        jax.ShapeDtypeStruct((n, COL_KV), F32),
        jax.ShapeDtypeStruct((n, COL_KV), F32),
        jax.ShapeDtypeStruct((n, COL_KV), F32),
        jax.ShapeDtypeStruct((n, LANES), F32),
        kv_t, kv_t, kv_t,
        keys(KAUG)[0], keys(HEAD_DIM)[0], v_t[0], v_t[0],
    )
    return pl.pallas_call(
        functools.partial(_proj_kernel, t_tiles=t_tiles),
        grid=(n // tm,),
        in_specs=[
            pl.BlockSpec((tm, D_MODEL), row),
            pl.BlockSpec((1, D_MODEL), const),
            pl.BlockSpec((D_MODEL, D_IN_PAD), const),
            pl.BlockSpec((tm, LANES), tab),
            pl.BlockSpec((tm, LANES), tab),
        ],
        out_specs=(
            pl.BlockSpec((tm, CONV_CH), row),
            pl.BlockSpec((tm, COL_Q), row),
            pl.BlockSpec((tm, COL_KV), row),
            pl.BlockSpec((tm, COL_KV), row),
            pl.BlockSpec((tm, COL_KV), row),
            pl.BlockSpec((tm, LANES), row),
            kv_t_spec, kv_t_spec, kv_t_spec,
            keys(KAUG)[1], keys(HEAD_DIM)[1], v_t[1], v_t[1],
        ),
        out_shape=out_shape,
        compiler_params=_cparams("arbitrary"),
        name="in_proj",
    )(x2d, ln.reshape(1, D_MODEL), w_pad, c_tab, s_tab)


def _conv_kernel(a_ref, hist_ref, w_ref, b_ref, g_ref, beta_ref, o_ref, sh_ref):
    tt = a_ref.shape[1]
    ext_ref = sh_ref.at[0]

    @pl.when(pl.program_id(1) == 0)
    def _():
        ext_ref[0:HIST_ROWS, :] = hist_ref[0]

    ext_ref[HIST_ROWS:HIST_ROWS + tt, :] = a_ref[0]
    span = HIST_ROWS + tt - SUBLANES
    for s in range(1, SUBLANES):
        sh_ref[s, 0:span, :] = ext_ref[s:s + span, :]

    lead = HIST_ROWS - (CONV_W - 1)
    rc = min(tt, CONV_ROWS)

    def chunk(i, carry):
        r0 = pl.multiple_of(i * rc, rc)
        acc = jnp.broadcast_to(b_ref[...], (rc, CONV_CH))
        for k in range(CONV_W):
            a, s = divmod(lead + k, SUBLANES)
            acc = acc + w_ref[k:k + 1, :] * sh_ref[s, pl.ds(r0 + a * SUBLANES, rc), :]
        o_ref[0, pl.ds(r0, rc), :] = acc
        return carry

    lax.fori_loop(0, tt // rc, chunk, 0)
    acc = o_ref[0]
    mu = jnp.mean(acc, axis=-1, keepdims=True)
    var = jnp.mean(jnp.square(acc - mu), axis=-1, keepdims=True)
    y = (acc - mu) * lax.rsqrt(var + LN_EPS) * g_ref[...] + beta_ref[...]
    o_ref[0] = y * jax.nn.sigmoid(y)
    carry = ext_ref[tt:tt + HIST_ROWS, :]
    ext_ref[0:HIST_ROWS, :] = carry


def _conv_module(a3d, hist, dw_w, dw_b, ln_g, ln_b, tt):
    b, t, _ = a3d.shape
    w_pad = jnp.pad(dw_w, ((0, HIST_ROWS - CONV_W), (0, 0)))
    vec = lambda i, j: (0, 0)
    return pl.pallas_call(
        _conv_kernel,
        grid=(b, t // tt),
        in_specs=[
            pl.BlockSpec((1, tt, CONV_CH), lambda i, j: (i, j, 0)),
            pl.BlockSpec((1, HIST_ROWS, CONV_CH), lambda i, j: (i, 0, 0)),
            pl.BlockSpec((HIST_ROWS, CONV_CH), vec),
            pl.BlockSpec((1, CONV_CH), vec),
            pl.BlockSpec((1, CONV_CH), vec),
            pl.BlockSpec((1, CONV_CH), vec),
        ],
        out_specs=pl.BlockSpec((1, tt, CONV_CH), lambda i, j: (i, j, 0)),
        out_shape=jax.ShapeDtypeStruct((b, t, CONV_CH), F32),
        scratch_shapes=[pltpu.VMEM((SUBLANES, HIST_ROWS + tt, CONV_CH), F32)],
        compiler_params=_cparams("arbitrary", "arbitrary"),
        name="conv_module",
    )(a3d, hist, w_pad, dw_b.reshape(1, -1), ln_g.reshape(1, -1), ln_b.reshape(1, -1))


L_GROUP = 2 * LANES // HEAD_DIM


def _compress_rows(x_refs, pe_ref, wk1_ref, wk2_ref, wv1_ref, wv2_ref, nb, pitch=CMP_BLOCK):
    hk = jnp.zeros((N_KV * nb, CMP_HID), F32)
    hv = jnp.zeros((N_KV * nb, CMP_HID), F32)
    for j in range(CMP_BLOCK // L_GROUP):
        parts_k, parts_v = [], []
        for x_ref in x_refs:
            xs = [x_ref[pl.ds(j * L_GROUP + i, nb, stride=pitch), :] + pe_ref[j * L_GROUP + i:j * L_GROUP + i + 1, :]
                  for i in range(L_GROUP)]
            parts_k.append(jnp.concatenate([x[:, :HEAD_DIM] for x in xs], axis=1))
            parts_v.append(jnp.concatenate([x[:, HEAD_DIM:] for x in xs], axis=1))
        xk = jnp.concatenate(parts_k, axis=0).astype(BF16)
        xv = jnp.concatenate(parts_v, axis=0).astype(BF16)
        rows = slice(j * L_GROUP * HEAD_DIM, (j + 1) * L_GROUP * HEAD_DIM)
        hk = hk + jnp.dot(xk, wk1_ref[rows, :], preferred_element_type=F32)
        hv = hv + jnp.dot(xv, wv1_ref[rows, :], preferred_element_type=F32)
    kc =jnp.dot((hk * jax.nn.sigmoid(hk)).astype(BF16), wk2_ref[...], preferred_element_type=F32)
    vc = jnp.dot((hv * jax.nn.sigmoid(hv)).astype(BF16), wv2_ref[...], preferred_element_type=F32)
    return kc, vc


def _compress_dense_kernel(*refs):
    x_refs, (pe_ref, wk1_ref, wk2_ref, wv1_ref, wv2_ref, kc_ref, vc_ref) = refs[:N_KV], refs[N_KV:]
    nb = x_refs[0].shape[1] // CMP_BLOCK
    kc, vc = _compress_rows([x.at[0] for x in x_refs], pe_ref, wk1_ref, wk2_ref, wv1_ref, wv2_ref, nb)
    for g in range(N_KV):
        kc_ref[0, g] = kc[g * nb:(g + 1) * nb]
        vc_ref[0, g] = vc[g * nb:(g + 1) * nb]


def _cmp_weight_specs():
    const = lambda *_: (0, 0)
    return [
        pl.BlockSpec((CMP_BLOCK, 2 * HEAD_DIM), const),
        pl.BlockSpec((CMP_BLOCK * HEAD_DIM, CMP_HID), const),
        pl.BlockSpec((CMP_HID, HEAD_DIM), const),
        pl.BlockSpec((CMP_BLOCK * HEAD_DIM, CMP_HID), const),
        pl.BlockSpec((CMP_HID, HEAD_DIM), const),
    ]


def _cmp_weights(pos_emb, w_k1, w_k2, w_v1, w_v2):
    pe = pos_emb.reshape(CMP_BLOCK, 2 * HEAD_DIM)
    return (pe, w_k1.reshape(-1, CMP_HID).astype(BF16), w_k2.astype(BF16),
            w_v1.reshape(-1, CMP_HID).astype(BF16), w_v2.astype(BF16))


def _compress_dense(kv3d, cmp_w):
    b, t, _ = kv3d.shape
    nb = t // CMP_BLOCK
    out = jax.ShapeDtypeStruct((b, N_KV, nb, HEAD_DIM), F32)
    ospec = pl.BlockSpec((1, N_KV, nb, HEAD_DIM), lambda i: (i, 0, 0, 0))
    return pl.pallas_call(
        _compress_dense_kernel,
        grid=(b,),
        in_specs=[pl.BlockSpec((1, t, 2 * HEAD_DIM), functools.partial(lambda g, i: (i, 0, g), g))
                  for g in range(N_KV)] + _cmp_weight_specs(),
        out_specs=(ospec, ospec),
        out_shape=(out, out),
        compiler_params=_cparams("arbitrary"),
        name="compress_prompt",
    )(*([kv3d] * N_KV), *cmp_w)


PAGES_PER_STEP = 64
BLOCK_PITCH = CMP_BLOCK + SUBLANES


def _compress_paged_kernel(pt_ref, cache_ref, pe_ref, wk1_ref, wk2_ref, wv1_ref, wv2_ref, kc_ref, vc_ref,
                           raw_ref, rows_ref, sem_ref):
    step = pl.program_id(0)
    n_steps = pl.num_programs(0)
    nb = PAGES_PER_STEP * PAGE_SIZE // CMP_BLOCK
    kvw = 2 * HEAD_DIM

    def page_copy(s, slot, p):
        return pltpu.make_async_copy(cache_ref.at[pt_ref[s * PAGES_PER_STEP + p]], raw_ref.at[slot, p], sem_ref.at[slot])

    def issue(s, slot):
        for p in range(PAGES_PER_STEP):
            page_copy(s, slot, p).start()

    slot = step % 2

    @pl.when(step == 0)
    def _():
        issue(step, slot)

    @pl.when(step + 1 < n_steps)
    def _():
        issue(step + 1, 1 - slot)

    for p in range(PAGES_PER_STEP):
        page_copy(step, slot, p).wait()

    eye = (lax.broadcasted_iota(jnp.int32, (PAGE_SIZE, PAGE_SIZE), 0)
           == lax.broadcasted_iota(jnp.int32, (PAGE_SIZE, PAGE_SIZE), 1)).astype(BF16)
    for p in range(PAGES_PER_STEP):
        page = lax.dot_general(eye, raw_ref[slot, p].astype(BF16), (((1,), (1,)), ((), ())),
                               preferred_element_type=F32)
        for g in range(N_KV):
            for n in range(PAGE_SIZE // CMP_BLOCK):
                row0 = (p * (PAGE_SIZE // CMP_BLOCK) + n) * BLOCK_PITCH
                rows_ref[g, row0:row0 + CMP_BLOCK, :] = page[n * CMP_BLOCK:(n + 1) * CMP_BLOCK, g * kvw:(g + 1) * kvw]

    kc, vc = _compress_rows([rows_ref.at[g] for g in range(N_KV)], pe_ref, wk1_ref, wk2_ref, wv1_ref, wv2_ref, nb,
                            pitch=BLOCK_PITCH)
    for g in range(N_KV):
        kc_ref[0, g] = kc[g * nb:(g + 1) * nb]
        vc_ref[0, g] = vc[g * nb:(g + 1) * nb]


def _compress_paged(cache_t, page_table, cmp_w):
    db, n_pages = page_table.shape
    steps_per_row = n_pages // PAGES_PER_STEP
    nb = PAGES_PER_STEP * PAGE_SIZE // CMP_BLOCK
    out = jax.ShapeDtypeStruct((db, N_KV, steps_per_row * nb, HEAD_DIM), F32)
    ospec = pl.BlockSpec((1, N_KV, nb, HEAD_DIM), lambda i, pt: (i // steps_per_row, 0, i % steps_per_row, 0))
    grid_spec = pltpu.PrefetchScalarGridSpec(
        num_scalar_prefetch=1,
        grid=(db * steps_per_row,),
        in_specs=[pl.BlockSpec(memory_space=pl.ANY)] + _cmp_weight_specs(),
        out_specs=(ospec, ospec),
        scratch_shapes=[pltpu.VMEM((2, PAGES_PER_STEP, COL_KV, PAGE_SIZE), F32),
                        pltpu.VMEM((N_KV, nb * BLOCK_PITCH, 2 * HEAD_DIM), F32),
                        pltpu.SemaphoreType.DMA((2,))],
    )
    return pl.pallas_call(
        _compress_paged_kernel,
        grid_spec=grid_spec,
        out_shape=(out, out),
        compiler_params=_cparams("arbitrary"),
        name="compress_paged",
    )(page_table.reshape(-1), cache_t, *cmp_w)


TQ = 2 * LANES
KC = 512
LOG2_E = 1.4426950408889634
BOUND_SLACK = 1.01
MAX_SHIFT = 60.0
WIN_KEYS = WINDOW + TQ
V_ROWS = HEAD_DIM + 16
KAUG = 2 * LANES


def _top_blocks(imp, cand, n_blocks):
    blk = lax.broadcasted_iota(jnp.int32, imp.shape, 0)
    score = jnp.where(cand, imp, -1.0)
    for _ in range(N_SEL):
        mx = jnp.max(score, axis=0, keepdims=True)
        idx = jnp.min(jnp.where(score == mx, blk, n_blocks), axis=0, keepdims=True)
        score = jnp.where(blk == idx, -2.0, score)
    return jnp.where(score < -1.5, 1.0, 0.0)


def _prompt_attn_kernel(q_ref, kc_ref, vct_ref, ksel_ref, vselt_ref, kwin_ref, vwint_ref, gate_ref, o_ref,
                        kmax_ref, acc_ref):
    qt = pl.program_id(2)
    t0 = qt * TQ
    nb = kc_ref.shape[2]
    width = Q_PER_KV * TQ

    q = q_ref[0].astype(F32) * (ATT_SCALE * LOG2_E)
    q_t = q.T
    q4 = jnp.concatenate([q_t[r * HEAD_DIM:(r + 1) * HEAD_DIM] for r in range(Q_PER_KV)], axis=1)
    q4b = q4.astype(BF16)
    tok = t0 + lax.broadcasted_iota(jnp.int32, (1, width), 1) % TQ

    sc = jnp.dot(kc_ref[0, 0].astype(BF16), q4b, preferred_element_type=F32)
    blk = lax.broadcasted_iota(jnp.int32, (nb, width), 0)
    valid_c = (blk + 1) * CMP_BLOCK - 1 <= tok
    sc = jnp.where(valid_c, sc, NEG_INF)
    e = jnp.where(valid_c, jnp.exp2(sc - jnp.max(sc, axis=0, keepdims=True)), 0.0)
    den = jnp.sum(e, axis=0, keepdims=True)
    p = e / jnp.where(den > 0.0, den, 1.0)
    o_cmp = jnp.dot(vct_ref[0, 0].astype(BF16), p.astype(BF16), preferred_element_type=F32)
    imp = p[:, 0:TQ]
    for r in range(1, Q_PER_KV):
        imp = imp + p[:, r * TQ:(r + 1) * TQ]

    tok1 = t0 + lax.broadcasted_iota(jnp.int32, (1, TQ), 1)
    own = tok1 // SEL_BLOCK
    blk1 = lax.broadcasted_iota(jnp.int32, (nb, TQ), 0)
    cand = blk1 < own
    sel = _top_blocks(imp, cand, nb)
    bias = jnp.where(cand, jnp.where(sel > 0.0, 0.0, NEG_INF), jnp.where(blk1 == own, 0.0, NEG_INF))
    bias4 = jnp.concatenate([bias] * Q_PER_KV, axis=1).astype(BF16)
    q_aug = jnp.concatenate([q4b, bias4, jnp.zeros((KAUG - HEAD_DIM - nb, width), BF16)], axis=0)

    def scores(c):
        return jnp.dot(ksel_ref[0, 0, pl.ds(pl.multiple_of(c * KC, KC), KC), :], q_aug, preferred_element_type=F32)

    def softmax_pv(c, s, m, acc):
        m_new = jnp.maximum(m, jnp.max(s, axis=0, keepdims=True))
        alpha = jnp.exp2(m - m_new)
        pr = jnp.exp2(s - m_new).astype(BF16)
        v_blk = vselt_ref[0, 0, :, pl.ds(pl.multiple_of(c * KC, KC), KC)]
        return m_new, alpha * acc + jnp.dot(v_blk, pr, preferred_element_type=F32)

    def sel_step(c, carry):
        s, m, acc = carry
        s_next = scores(c + 1)
        return (s_next,) + softmax_pv(c, s, m, acc)

    last = t0 // KC
    key = last * KC + lax.broadcasted_iota(jnp.int32, (KC, width), 0)

    def online_softmax():
        init = (scores(0), jnp.full((1, width), NEG_INF, F32), jnp.zeros((V_ROWS, width), F32))
        s_last, m_sel, acc = lax.fori_loop(0, last, sel_step, init)
        return softmax_pv(last, jnp.where(key <= tok, s_last, NEG_INF), m_sel, acc)[1]

    @pl.when(qt == 0)
    def _():
        def largest_norm(k_ref):
            def body(i, kmax):
                k = k_ref[0, 0, pl.ds(pl.multiple_of(i * KC, KC), KC), :].astype(F32)
                k = jnp.where(lax.broadcasted_iota(jnp.int32, k.shape, 1) < HEAD_DIM, k, 0.0)
                return jnp.maximum(kmax, jnp.max(jnp.sum(k * k, axis=1, keepdims=True), axis=0, keepdims=True))
            return lax.fori_loop(0, k_ref.shape[2] // KC, body, jnp.zeros((1, 1), F32))
        row = lax.broadcasted_iota(jnp.int32, kmax_ref.shape, 0)
        kmax_ref[...] = jnp.where(row == 0, largest_norm(ksel_ref), largest_norm(kwin_ref))

    q4f = q4b.astype(F32)
    q_norm2 = jnp.sum(q4f * q4f, axis=0, keepdims=True)
    bound = jnp.sqrt(q_norm2 * kmax_ref[0:1, 0:1]) * BOUND_SLACK
    bound_win = jnp.sqrt(q_norm2 * kmax_ref[1:2, 0:1]) * BOUND_SLACK
    bounded = jnp.max(bound) <= MAX_SHIFT
    bounded_win = jnp.max(bound_win) <= MAX_SHIFT

    def bounded_pv(c, s, acc):
        v_blk = vselt_ref[0, 0, :, pl.ds(pl.multiple_of(c * KC, KC), KC)]
        return acc + jnp.dot(v_blk, jnp.exp2(s - bound).astype(BF16), preferred_element_type=F32)

    @pl.when(bounded)
    def _():
        acc = lax.fori_loop(0, last, lambda c, acc: bounded_pv(c, scores(c), acc), jnp.zeros((V_ROWS, width), F32))
        acc_ref[...] = bounded_pv(last, jnp.where(key <= tok, scores(last), NEG_INF), acc)

    @pl.when(jnp.logical_not(bounded))
    def _():
        acc_ref[...] = online_softmax()

    acc_sel = acc_ref[...]
    o_sel = acc_sel[0:HEAD_DIM] / acc_sel[HEAD_DIM:HEAD_DIM + 1]

    w0 = pl.multiple_of(jnp.maximum(t0 - WINDOW, 0), TQ)
    sw = jnp.dot(kwin_ref[0, 0, pl.ds(w0, WIN_KEYS), :], q4b, preferred_element_type=F32)
    dist = tok - (w0 + lax.broadcasted_iota(jnp.int32, (WIN_KEYS, width), 0))
    sw = jnp.where((dist >= 0) & (dist < WINDOW), sw, NEG_INF)

    def window_pv(shift):
        pw = jnp.exp2(sw - shift).astype(BF16)
        return jnp.dot(vwint_ref[0, 0, :, pl.ds(w0, WIN_KEYS)], pw, preferred_element_type=F32)

    @pl.when(bounded_win)
    def _():
        acc_ref[...] = window_pv(bound_win)

    @pl.when(jnp.logical_not(bounded_win))
    def _():
        acc_ref[...] = window_pv(jnp.max(sw, axis=0, keepdims=True))

    acc_win = acc_ref[...]
    o_win = acc_win[0:HEAD_DIM] / acc_win[HEAD_DIM:HEAD_DIM + 1]

    outs = []
    for r in range(Q_PER_KV):
        sl = slice(r * TQ, (r + 1) * TQ)
        g = [gate_ref[0, 0, j * Q_PER_KV + r:j * Q_PER_KV + r + 1, :] for j in range(3)]
        outs.append(g[0] * o_cmp[:, sl] + g[1] * o_sel[:, sl] + g[2] * o_win[:, sl])
    o_ref[0] = jnp.concatenate(outs, axis=0).T


def _prompt_attention(q3, k_c, v_ct, ksel_aug, vsel_t, kwin, vwin_t, gates_t):
    b, t, _ = q3.shape
    nb = k_c.shape[2]
    width = Q_PER_KV * HEAD_DIM
    per_bg = lambda i, g, j: (i, g, 0, 0)
    return pl.pallas_call(
        _prompt_attn_kernel,
        grid=(b, N_KV, t // TQ),
        in_specs=[
            pl.BlockSpec((1, TQ, width), lambda i, g, j: (i, j, g)),
            pl.BlockSpec((1, 1, nb, HEAD_DIM), per_bg),
            pl.BlockSpec((1, 1, HEAD_DIM, nb), per_bg),
            pl.BlockSpec((1, 1, t, KAUG), per_bg),
            pl.BlockSpec((1, 1, V_ROWS, t), per_bg),
            pl.BlockSpec((1, 1, t, HEAD_DIM), per_bg),
            pl.BlockSpec((1, 1, V_ROWS, t), per_bg),
            pl.BlockSpec((1, 1, 3 * Q_PER_KV, TQ), lambda i, g, j: (i, g, 0, j)),
        ],
        out_specs=pl.BlockSpec((1, TQ, width), lambda i, g, j: (i, j, g)),
        out_shape=jax.ShapeDtypeStruct((b, t, COL_Q), F32),
        scratch_shapes=[pltpu.VMEM((SUBLANES, LANES), F32), pltpu.VMEM((V_ROWS, Q_PER_KV * TQ), F32)],
        compiler_params=_cparams("arbitrary", "arbitrary", "arbitrary"),
        name="prompt_attention",
    )(q3, k_c, v_ct, ksel_aug, vsel_t, kwin, vwin_t, gates_t)


def _sample_cmp_kernel(q_ref, kc_ref, vc_ref, ocmp_ref, idx_ref, imp_ref, *, past_len, dec_seq):
    b = pl.program_id(0)
    nb = kc_ref.shape[2]
    rows = Q_PER_KV * SUBLANES
    t_row = lax.broadcasted_iota(jnp.int32, (rows, nb), 0) % SUBLANES
    blk = lax.broadcasted_iota(jnp.int32, (rows, nb), 1)
    valid = (blk + 1) * CMP_BLOCK - 1 <= past_len + t_row
    for g in range(N_KV):
        s = lax.dot_general(q_ref[0, g], kc_ref[0, g].astype(BF16), (((1,), (1,)), ((), ())),
                            preferred_element_type=F32) * ATT_SCALE
        s = jnp.where(valid, s, NEG_INF)
        e = jnp.where(valid, jnp.exp(s - jnp.max(s, axis=1, keepdims=True)), 0.0)
        den = jnp.sum(e, axis=1, keepdims=True)
        p = e / jnp.where(den > 0.0, den, 1.0)
        ocmp_ref[0, g] = jnp.dot(p.astype(BF16), vc_ref[0, g].astype(BF16), preferred_element_type=F32)
        imp = p[0:SUBLANES]
        for r in range(1, Q_PER_KV):
            imp = imp + p[r * SUBLANES:(r + 1) * SUBLANES]
        row0 = pl.multiple_of((b * N_KV + g) * SUBLANES, SUBLANES)
        imp_ref[pl.ds(row0, SUBLANES), :] = imp

    @pl.when(b == pl.num_programs(0) - 1)
    def _():
        n_rows = imp_ref.shape[0]
        lane = lax.broadcasted_iota(jnp.int32, (n_rows, nb), 1)
        own = (past_len + lax.broadcasted_iota(jnp.int32, (n_rows, nb), 0) % SUBLANES) // SEL_BLOCK
        score = jnp.where(lane < own, imp_ref[...], -1.0)
        col = lax.broadcasted_iota(jnp.int32, (n_rows, LANES), 1)
        picks = jnp.zeros((n_rows, LANES), jnp.int32)
        for i in range(N_SEL):
            mx = jnp.max(score, axis=1, keepdims=True)
            idx = jnp.min(jnp.where(score == mx, lane, nb), axis=1, keepdims=True)
            score = jnp.where(lane == idx, -2.0, score)
            picks = jnp.where(col == i, idx, picks)
        idx_ref[...] = picks


def _sample_cmp_select(q_rt, k_c, v_c, past_len, dec_seq):
    db, _, rows, _ = q_rt.shape
    nb = k_c.shape[2]
    spec4 = lambda r, c: pl.BlockSpec((1, N_KV, r, c), lambda i: (i, 0, 0, 0))
    n_rows = db * N_KV * SUBLANES
    return pl.pallas_call(
        functools.partial(_sample_cmp_kernel, past_len=past_len, dec_seq=dec_seq),
        grid=(db,),
        in_specs=[spec4(rows, HEAD_DIM), spec4(nb, HEAD_DIM), spec4(nb, HEAD_DIM)],
        out_specs=(spec4(rows, HEAD_DIM), pl.BlockSpec((n_rows, LANES), lambda i: (0, 0))),
        out_shape=(jax.ShapeDtypeStruct((db, N_KV, rows, HEAD_DIM), F32),
                   jax.ShapeDtypeStruct((n_rows, LANES), jnp.int32)),
        scratch_shapes=[pltpu.VMEM((n_rows, nb), F32)],
        compiler_params=_cparams("arbitrary"),
        name="sample_cmp_select",
    )(q_rt, k_c, v_c)


def _sample_attn_kernel(idx_ref, pt_ref, cache_ref, q_ref, snew_ref, wnew_ref, wstate_ref, ocmp_ref, gate_ref,
                        o_ref, kv_ref, sem_ref, *, dec_seq, n_pages):
    b = pl.program_id(0)
    n_b = pl.num_programs(0)
    kvw = 2 * HEAD_DIM
    blocks_per_page = PAGE_SIZE // SEL_BLOCK

    def block_id(bb, g, t, i):
        return idx_ref[((bb * N_KV + g) * dec_seq + t) * N_SEL + i]

    def slab_copy(bb, slot, g, t, i):
        page = pt_ref[bb * n_pages + block_id(bb, g, t, i) // blocks_per_page]
        return pltpu.make_async_copy(cache_ref.at[page, pl.ds(g * kvw, kvw), :], kv_ref.at[slot, g, t, i],
                                     sem_ref.at[slot])

    def for_all_slabs(fn):
        for g in range(N_KV):
            for t in range(dec_seq):
                for i in range(N_SEL):
                    fn(g, t, i)

    slot = b % 2

    @pl.when(b == 0)
    def _():
        for_all_slabs(lambda g, t, i: slab_copy(b, slot, g, t, i).start())

    @pl.when(b + 1 < n_b)
    def _():
        for_all_slabs(lambda g, t, i: slab_copy(b + 1, 1 - slot, g, t, i).start())

    rows = dec_seq * SUBLANES
    tok = lax.broadcasted_iota(jnp.int32, (rows, 1), 0) // SUBLANES
    n_state = wstate_ref.shape[2]

    def attend(q, k_t, v_t, valid):
        s = jnp.dot(q, k_t.astype(BF16), preferred_element_type=F32) * ATT_SCALE
        s = jnp.where(valid, s, NEG_INF)
        p = jnp.exp(s - jnp.max(s, axis=1, keepdims=True))
        den = jnp.sum(p, axis=1, keepdims=True)
        return lax.dot_general(p.astype(BF16), v_t.astype(BF16), (((1,), (1,)), ((), ())),
                               preferred_element_type=F32) / den

    o_win = []
    for g in range(N_KV):
        k_rows, v_rows = pl.ds(g * kvw, HEAD_DIM), pl.ds(g * kvw + HEAD_DIM, HEAD_DIM)
        k_t = jnp.concatenate([wstate_ref[0, k_rows, :], wnew_ref[0, k_rows, :]], axis=1)
        v_t = jnp.concatenate([wstate_ref[0, v_rows, :], wnew_ref[0, v_rows, :]], axis=1)
        lane = lax.broadcasted_iota(jnp.int32, (rows, n_state + LANES), 1)
        new_i = lane - n_state
        valid = ((lane < n_state) & (lane > tok)) | ((new_i >= 0) & (new_i <= tok) & (new_i < dec_seq))
        o_win.append(attend(q_ref[0, g], k_t, v_t, valid))

    for_all_slabs(lambda g, t, i: slab_copy(b, slot, g, t, i).wait())

    lane1 = lax.broadcasted_iota(jnp.int32, (1, PAGE_SIZE), 1)
    for g in range(N_KV):
        o_sel = []
        for t in range(dec_seq):
            k_parts = [kv_ref[slot, g, t, i, 0:HEAD_DIM, :] for i in range(N_SEL)]
            v_parts = [kv_ref[slot, g, t, i, HEAD_DIM:kvw, :] for i in range(N_SEL)]
            k_parts.append(snew_ref[0, g * kvw:g * kvw + HEAD_DIM, :])
            v_parts.append(snew_ref[0, g * kvw + HEAD_DIM:(g + 1) * kvw, :])
            halves = [lane1 // SEL_BLOCK == block_id(b, g, t, i) % blocks_per_page for i in range(N_SEL)]
            halves.append((lane1 <= t) & (lane1 < dec_seq))
            q = q_ref[0, g, t * SUBLANES:(t + 1) * SUBLANES, :]
            o_sel.append(attend(q, jnp.concatenate(k_parts, axis=1), jnp.concatenate(v_parts, axis=1),
                                jnp.concatenate(halves, axis=1)))
        o_sel = jnp.concatenate(o_sel, axis=0)
        o_ref[0, g] = (gate_ref[0, g, 0] * ocmp_ref[0, g] + gate_ref[0, g, 1] * o_sel
                       + gate_ref[0, g, 2] * o_win[g])


def _sample_attention(sel_idx, page_table, cache_t, q_tr, snew_t, wnew_t, wstate_t, ocmp_tr, gates_tr, dec_seq):
    db, n_pages = page_table.shape
    rows = dec_seq * SUBLANES
    n_state = wstate_t.shape[2]
    per_b4 = lambda i, *_: (i, 0, 0, 0)
    per_b3 = lambda i, *_: (i, 0, 0)
    grid_spec = pltpu.PrefetchScalarGridSpec(
        num_scalar_prefetch=2,
        grid=(db,),
        in_specs=[
            pl.BlockSpec(memory_space=pl.ANY),
            pl.BlockSpec((1, N_KV, rows, HEAD_DIM), per_b4),
            pl.BlockSpec((1, COL_KV, LANES), per_b3),
            pl.BlockSpec((1, COL_KV, LANES), per_b3),
            pl.BlockSpec((1, COL_KV, n_state), per_b3),
            pl.BlockSpec((1, N_KV, rows, HEAD_DIM), per_b4),
            pl.BlockSpec((1, N_KV, 3, rows, HEAD_DIM), lambda i, *_: (i, 0, 0, 0, 0)),
        ],
        out_specs=pl.BlockSpec((1, N_KV, rows, HEAD_DIM), per_b4),
        scratch_shapes=[pltpu.VMEM((2, N_KV, dec_seq, N_SEL, 2 * HEAD_DIM, PAGE_SIZE), F32),
                        pltpu.SemaphoreType.DMA((2,))],
    )
    return pl.pallas_call(
        functools.partial(_sample_attn_kernel, dec_seq=dec_seq, n_pages=n_pages),
        grid_spec=grid_spec,
        out_shape=jax.ShapeDtypeStruct((db, N_KV, rows, HEAD_DIM), F32),
        compiler_params=_cparams("arbitrary"),
        name="sample_attention",
    )(sel_idx, page_table.reshape(-1), cache_t, q_tr, snew_t, wnew_t, wstate_t, ocmp_tr, gates_tr)


ROUTE_E1, ROUTE_E2, ROUTE_W1, ROUTE_W2 = 0, 1, 2, 3


def _merge_kernel(conv_ref, att_ref, x_ref, gc_ref, ga_ref, wo_ref, ln2_ref, wr_ref, x1_ref, h3_ref, route_ref):
    mix = jnp.concatenate([_rms(conv_ref[...], gc_ref[...]), _rms(att_ref[...], ga_ref[...])], axis=1)
    x1 = x_ref[...] + jnp.dot(mix.astype(BF16), wo_ref[...], preferred_element_type=F32)
    x1_ref[...] = x1
    h = _rms(x1, ln2_ref[...])
    _rows_to_tiles(h3_ref, h)

    logits = jnp.dot(h.astype(BF16), wr_ref[...], preferred_element_type=F32)
    lane = lax.broadcasted_iota(jnp.int32, logits.shape, 1)
    is_g = lane < N_GROUPS
    lg = jnp.where(is_g, logits, NEG_INF)
    mg = jnp.max(lg, axis=1, keepdims=True)
    sg = jnp.sum(jnp.where(is_g, jnp.exp(lg - mg), 0.0), axis=1, keepdims=True)
    grp = jnp.min(jnp.where(lg == mg, lane, LANES), axis=1, keepdims=True)
    p_top = 1.0 / sg
    in_grp = ((lane + (EXPERTS_PER_GROUP - N_GROUPS)) // EXPERTS_PER_GROUP) == grp + 1
    le = jnp.where(in_grp, logits, NEG_INF)
    ee = jnp.where(in_grp, jnp.exp(le - jnp.max(le, axis=1, keepdims=True)), 0.0)
    pe = jnp.where(in_grp, ee / jnp.sum(ee, axis=1, keepdims=True), -1.0)
    p1 = jnp.max(pe, axis=1, keepdims=True)
    i1 = jnp.min(jnp.where(pe == p1, lane, LANES), axis=1, keepdims=True)
    pe2 = jnp.where(lane == i1, -1.0, pe)
    p2 = jnp.max(pe2, axis=1, keepdims=True)
    i2 = jnp.min(jnp.where(pe2 == p2, lane, LANES), axis=1, keepdims=True)
    den = p1 + p2
    rec = jnp.where(lane == ROUTE_E1, (i1 - N_GROUPS).astype(F32), 0.0)
    rec = jnp.where(lane == ROUTE_E2, (i2 - N_GROUPS).astype(F32), rec)
    rec = jnp.where(lane == ROUTE_W1, p1 / den * p_top, rec)
    rec = jnp.where(lane == ROUTE_W2, p2 / den * p_top, rec)
    route_ref[...] = rec


def _merge(conv2d, att2d, x2d, g_conv, g_att, w_out_b, ln2, w_route_b, tm):
    n = x2d.shape[0]
    row = lambda i: (i, 0)
    const = lambda i: (0, 0)
    x1, h_tiles, route = pl.pallas_call(
        _merge_kernel,
        grid=(n // tm,),
        in_specs=[
            pl.BlockSpec((tm, CONV_CH), row),
            pl.BlockSpec((tm, COL_Q), row),
            pl.BlockSpec((tm, D_MODEL), row),
            pl.BlockSpec((1, CONV_CH), const),
            pl.BlockSpec((1, COL_Q), const),
            pl.BlockSpec((CONV_CH + COL_Q, D_MODEL), const),
            pl.BlockSpec((1, D_MODEL), const),
            pl.BlockSpec((D_MODEL, LANES), const),
        ],
        out_specs=(
            pl.BlockSpec((tm, D_MODEL), row),
            pl.BlockSpec((tm * ROW_TILES, LANES), row),
            pl.BlockSpec((tm, LANES), row),
        ),
        out_shape=(
            jax.ShapeDtypeStruct((n, D_MODEL), F32),
            jax.ShapeDtypeStruct((n * ROW_TILES, LANES), F32),
            jax.ShapeDtypeStruct((n, LANES), F32),
        ),
        compiler_params=_cparams("arbitrary"),
        name="merge_route",
    )(conv2d, att2d, x2d, g_conv.reshape(1, -1), g_att.reshape(1, -1), w_out_b, ln2.reshape(1, -1), w_route_b)
    return x1, h_tiles.reshape(n, ROW_TILES, LANES), route


META_W = 4 * LANES
ROUTE_TILE = 512


def _rank_kernel(route_ref, dest_ref, meta_ref, e_ref, rank_ref, count_ref):
    i = pl.program_id(0)
    rt = route_ref.shape[0]

    @pl.when(i == 0)
    def _():
        count_ref[...] = jnp.zeros(count_ref.shape, F32)

    route_t = route_ref[...].T
    expert_id = lax.broadcasted_iota(jnp.int32, (N_EXPERTS, rt), 0).astype(F32)
    before = (lax.broadcasted_iota(jnp.int32, (rt, rt), 0) < lax.broadcasted_iota(jnp.int32, (rt, rt), 1))
    before = before.astype(BF16)
    ones = jnp.ones((rt, LANES), BF16)
    e_rows, rank_rows = [], []
    for k in range(TOP_K):
        e_k = route_t[ROUTE_E1 + k:ROUTE_E1 + k + 1, :]
        onehot = (expert_id == e_k).astype(F32)
        earlier = jnp.dot(onehot.astype(BF16), before, preferred_element_type=F32)
        seen = count_ref[...]
        seen_w = jnp.concatenate([seen] * (rt // LANES), axis=1)
        rank_rows.append(jnp.sum(onehot * (earlier + seen_w), axis=0, keepdims=True))
        e_rows.append(e_k)
        count_ref[...] = seen + jnp.dot(onehot.astype(BF16), ones, preferred_element_type=F32)
    e_ref[i] = jnp.concatenate(e_rows, axis=0)
    rank_ref[i] = jnp.concatenate(rank_rows, axis=0)

    @pl.when(i == pl.num_programs(0) - 1)
    def _():
        counts = jnp.concatenate([count_ref[...]] * (META_W // LANES), axis=1)
        padded = jnp.floor((counts + (MOE_ROWS - 1)) * (1.0 / MOE_ROWS)) * MOE_ROWS
        lane = lax.broadcasted_iota(jnp.int32, (1, META_W), 1)
        chunk_start = lane.astype(F32) * MOE_ROWS
        chunk_e = jnp.zeros((1, META_W), F32)
        ends = jnp.zeros((1, META_W), F32)
        end = jnp.zeros((1, META_W), F32)
        starts = []
        for ex in range(N_EXPERTS):
            starts.append(end[:, 0:rt])
            end = end + padded[ex:ex + 1, :]
            chunk_e = chunk_e + (end <= chunk_start).astype(F32)
            ends = jnp.where(lane == ex, end, ends)
        n_used = end * (1.0 / MOE_ROWS)
        chunk_e = jnp.minimum(chunk_e, float(N_EXPERTS - 1))
        row = lax.broadcasted_iota(jnp.int32, (SUBLANES, META_W), 0)
        meta = jnp.where(row == 0, chunk_e, jnp.where(row == 1, n_used, jnp.where(row == 2, ends, 0.0)))
        meta_ref[...] = meta.astype(jnp.int32)

        def place(ti, carry):
            e_t = e_ref[ti]
            dest = rank_ref[ti]
            for ex in range(N_EXPERTS):
                dest = dest + jnp.where(e_t == float(ex), starts[ex], 0.0)
            dest_ref[ti] = dest.astype(jnp.int32)
            return carry

        lax.fori_loop(0, pl.num_programs(0), place, 0)


def _rank(route, rt):
    n = route.shape[0]
    tiles = n // rt
    assert n * TOP_K // MOE_ROWS + N_EXPERTS <= META_W and rt <= META_W, "chunk table is one row of META_W lanes"
    whole = lambda i: (0, 0, 0)
    return pl.pallas_call(
        _rank_kernel,
        grid=(tiles,),
        in_specs=[pl.BlockSpec((rt, LANES), lambda i: (i, 0))],
        out_specs=(pl.BlockSpec((tiles, TOP_K, rt), whole), pl.BlockSpec((SUBLANES, META_W), lambda i: (0, 0))),
        out_shape=(jax.ShapeDtypeStruct((tiles, TOP_K, rt), jnp.int32),
                   jax.ShapeDtypeStruct((SUBLANES, META_W), jnp.int32)),
        scratch_shapes=[pltpu.VMEM((tiles, TOP_K, rt), F32), pltpu.VMEM((tiles, TOP_K, rt), F32),
                        pltpu.VMEM((N_EXPERTS, LANES), F32)],
        compiler_params=_cparams("arbitrary"),
        name="route_rank",
    )(route)


def _row_copies_wait(src_ref, dst_ref, sem, n_rows):
    pltpu.make_async_copy(src_ref.at[pl.ds(0, n_rows)], dst_ref.at[pl.ds(0, n_rows)], sem).wait()


def _scatter_kernel(meta_ref, dest_ref, h3_ref, xs_ref, zero_ref, sem_ref):
    i = pl.program_id(0)
    rt = h3_ref.shape[0]
    n_chunks = xs_ref.shape[0] // MOE_ROWS

    @pl.when(i == 0)
    def _():
        zero_ref[...] = jnp.zeros(zero_ref.shape, F32)

        def zero_chunk(first_row):
            return pltpu.make_async_copy(zero_ref, xs_ref.at[pl.ds(first_row, MOE_ROWS)], sem_ref.at[1])

        def fills(act):
            prev = 0
            for ex in range(N_EXPERTS):
                end = meta_ref[2, ex]

                @pl.when(end > prev)
                def _():
                    act(zero_chunk(end - MOE_ROWS))
                prev = end
            for c in range(n_chunks - N_EXPERTS, n_chunks):
                @pl.when(c >= meta_ref[1, 0])
                def _():
                    act(zero_chunk(c * MOE_ROWS))

        fills(lambda copy: copy.start())
        fills(lambda copy: copy.wait())

    def issue(t, carry):
        for k in range(TOP_K):
            pltpu.make_async_copy(h3_ref.at[t], xs_ref.at[dest_ref[0, k, t]], sem_ref.at[0]).start(priority=k % 2)
        return carry

    lax.fori_loop(0, rt, issue, 0, unroll=8)
    _row_copies_wait(h3_ref, h3_ref, sem_ref.at[0], rt)
    _row_copies_wait(h3_ref, h3_ref, sem_ref.at[0], rt)


def _scatter_rows(h3, dest, meta, rt):
    n = h3.shape[0]
    n_slots = (n * TOP_K // MOE_ROWS + N_EXPERTS) * MOE_ROWS
    grid_spec = pltpu.PrefetchScalarGridSpec(
        num_scalar_prefetch=1,
        grid=(n // rt,),
        in_specs=[pl.BlockSpec((1, TOP_K, rt), lambda i, meta: (i, 0, 0), memory_space=pltpu.SMEM),
                  pl.BlockSpec((rt, ROW_TILES, LANES), lambda i, meta: (i, 0, 0))],
        out_specs=pl.BlockSpec(memory_space=pl.ANY),
        scratch_shapes=[pltpu.VMEM((MOE_ROWS, ROW_TILES, LANES), F32), pltpu.SemaphoreType.DMA((2,))],
    )
    return pl.pallas_call(
        _scatter_kernel,
        grid_spec=grid_spec,
        out_shape=jax.ShapeDtypeStruct((n_slots, ROW_TILES, LANES), F32),
        compiler_params=_cparams("arbitrary"),
        name="moe_scatter_rows",
    )(meta, dest, h3)


def _expert_kernel(meta_ref, xs_ref, wg_ref, wu_ref, wd_ref, ys_ref, wg_b, wu_b, wd_b):
    c = pl.program_id(0)
    in_use = c < meta_ref[1, 0]

    @pl.when(jnp.logical_not(in_use))
    def _():
        ys_ref[...] = jnp.zeros(ys_ref.shape, F32)

    @pl.when(in_use & ((c == 0) | (meta_ref[0, c] != meta_ref[0, jnp.maximum(c - 1, 0)])))
    def _():
        wg_b[...] = wg_ref[0].astype(BF16)
        wu_b[...] = wu_ref[0].astype(BF16)
        wd_b[...] = wd_ref[0].astype(BF16)

    @pl.when(in_use)
    def _():
        x = _tiles_to_rows(xs_ref).astype(BF16)
        gate = jnp.dot(x, wg_b[...], preferred_element_type=F32)
        up = jnp.dot(x, wu_b[...], preferred_element_type=F32)
        act = (gate * jax.nn.sigmoid(gate) * up).astype(BF16)
        _rows_to_tiles(ys_ref, jnp.dot(act, wd_b[...], preferred_element_type=F32))


def _expert_mlp(xs, meta, w_g, w_u, w_d):
    n_chunks = xs.shape[0] // MOE_ROWS
    rows = lambda c, meta: (c, 0)
    expert = lambda c, meta: (meta[0, jnp.minimum(c, meta[1, 0] - 1)], 0, 0)
    grid_spec = pltpu.PrefetchScalarGridSpec(
        num_scalar_prefetch=1,
        grid=(n_chunks,),
        in_specs=[
            pl.BlockSpec((MOE_ROWS * ROW_TILES, LANES), rows),
            pl.BlockSpec((1, D_MODEL, D_EXPERT), expert),
            pl.BlockSpec((1, D_MODEL, D_EXPERT), expert),
            pl.BlockSpec((1, D_EXPERT, D_MODEL), expert),
        ],
        out_specs=pl.BlockSpec((MOE_ROWS * ROW_TILES, LANES), rows),
        scratch_shapes=[pltpu.VMEM((D_MODEL, D_EXPERT), BF16), pltpu.VMEM((D_MODEL, D_EXPERT), BF16),
                        pltpu.VMEM((D_EXPERT, D_MODEL), BF16)],
    )
    return pl.pallas_call(
        _expert_kernel,
        grid_spec=grid_spec,
        out_shape=jax.ShapeDtypeStruct((xs.shape[0] * ROW_TILES, LANES), F32),
        compiler_params=_cparams("arbitrary"),
        name="expert_mlp",
    )(meta, xs.reshape(-1, LANES), w_g, w_u, w_d).reshape(xs.shape)


def _combine_kernel(dest_ref, nxt_ref, x1_ref, route_ref, ln_ref, ys_ref, o_ref, buf_ref, sem_ref):
    i = pl.program_id(0)
    n_tiles = pl.num_programs(0)
    rt = x1_ref.shape[0]

    def issue(table_ref, slot):
        def body(t, carry):
            for k in range(TOP_K):
                tile = pl.ds(pl.multiple_of(t * ROW_TILES, ROW_TILES), ROW_TILES)
                pltpu.make_async_copy(ys_ref.at[table_ref[0, k, t]], buf_ref.at[slot, k, tile],
                                      sem_ref.at[slot]).start(priority=k % 2)
            return carry
        lax.fori_loop(0, rt, body, 0, unroll=8)

    slot = i % 2

    @pl.when(i == 0)
    def _():
        issue(dest_ref, slot)

    @pl.when(i + 1 < n_tiles)
    def _():
        issue(nxt_ref, 1 - slot)

    for k in range(TOP_K):
        pltpu.make_async_copy(buf_ref.at[slot, k], buf_ref.at[slot, k], sem_ref.at[slot]).wait()
    route = route_ref[...]
    moe = None
    for k in range(TOP_K):
        y = _tiles_to_rows(buf_ref.at[slot, k])
        term = y * route[:, ROUTE_W1 + k:ROUTE_W1 + k + 1]
        moe = term if moe is None else moe + term
    o_ref[...] = _rms(x1_ref[...] + moe, ln_ref[...])


def _combine(x1, ys, dest, route, ln_final, rt):
    n = x1.shape[0]
    tiles = n // rt
    row = lambda i: (i, 0)
    table = lambda f: pl.BlockSpec((1, TOP_K, rt), f, memory_space=pltpu.SMEM)
    return pl.pallas_call(
        _combine_kernel,
        grid=(tiles,),
        in_specs=[
            table(lambda i: (i, 0, 0)),
            table(lambda i: (jnp.minimum(i + 1, tiles - 1), 0, 0)),
            pl.BlockSpec((rt, D_MODEL), row),
            pl.BlockSpec((rt, LANES), row),
            pl.BlockSpec((1, D_MODEL), lambda i: (0, 0)),
            pl.BlockSpec(memory_space=pl.ANY),
        ],
        out_specs=pl.BlockSpec((rt, D_MODEL), row),
        out_shape=jax.ShapeDtypeStruct((n, D_MODEL), F32),
        scratch_shapes=[pltpu.VMEM((2, TOP_K, rt * ROW_TILES, LANES), F32), pltpu.SemaphoreType.DMA((2,))],
        compiler_params=_cparams("arbitrary"),
        name="moe_combine_norm",
    )(dest, dest, x1, route, ln_final.reshape(1, -1), ys)


def _ffn(conv2d, att2d, x2d, g_conv, g_att, w_out_b, ln2, w_route_b, w_g, w_u, w_d, ln_final, tm):
    x1, h3, route = _merge(conv2d, att2d, x2d, g_conv, g_att, w_out_b, ln2, w_route_b, tm)
    rt = min(x2d.shape[0], ROUTE_TILE)
    dest, meta = _rank(route, rt)
    ys = _expert_mlp(_scatter_rows(h3, dest, meta, rt), meta, w_g, w_u, w_d)
    return _combine(x1, ys, dest, route, ln_final, rt)


def kernel(x_prompt, x_sample, cache_cmp_kv, cache_sel_kv, state_win_kv, state_conv, page_table, ln1, w_in, conv_dw_w, conv_dw_b, conv_ln_g, conv_ln_b, cmp_pos_emb, w_cmp_k1, w_cmp_k2, w_cmp_v1, w_cmp_v2, out_norm_conv, out_norm_att, w_out, ln2, w_router_group, w_router_expert, w_exp_gate, w_exp_up, w_exp_down, ln_final):
    depth = ln1.shape[0]
    assert depth == 1, "single-layer step"
    b, t, _ = x_prompt.shape
    db, ds, _ = x_sample.shape
    n_phys = cache_cmp_kv.shape[1]
    n_pages = page_table.shape[1]
    past = n_pages * PAGE_SIZE
    win_rows = state_win_kv.shape[2]
    assert ds < CMP_BLOCK and ds <= SUBLANES and past % SEL_BLOCK == 0 and past // SEL_BLOCK >= N_SEL
    assert win_rows == WINDOW and past >= WINDOW and t % KC == 0 and t >= WIN_KEYS

    w_in_b = jnp.pad(w_in[0], ((0, 0), (0, D_IN_PAD - D_IN))).astype(BF16)
    w_out_b = w_out[0].astype(BF16)
    w_route_b = jnp.pad(jnp.concatenate([w_router_group[0], w_router_expert[0]], axis=1),
                        ((0, 0), (0, LANES - N_GROUPS - N_EXPERTS))).astype(BF16)
    cmp_w = _cmp_weights(cmp_pos_emb[0], w_cmp_k1[0], w_cmp_k2[0], w_cmp_v1[0], w_cmp_v2[0])
    conv_w = (conv_dw_w[0], conv_dw_b[0], conv_ln_g[0], conv_ln_b[0])
    ffn_w = (out_norm_conv[0], out_norm_att[0], w_out_b, ln2[0], w_route_b, w_exp_gate[0], w_exp_up[0],
             w_exp_down[0], ln_final)

    xp2 = x_prompt.reshape(b * t, D_MODEL)
    c_p, s_p = _rope_tables(jnp.arange(t))
    (a_p, q_p, ckv_p, _, _, gate_p, ckv_pt, skv_pt, wkv_pt, ksel_aug, kwin_p, vsel_t, vwin_t) = _project(
        xp2, ln1[0], w_in_b, c_p, s_p, 512)
    a_p3 = a_p.reshape(b, t, CONV_CH)
    conv_p = _conv_module(a_p3, jnp.zeros((b, HIST_ROWS, CONV_CH), F32), *conv_w, 512)
    kc_p, vc_p = _compress_dense(ckv_p.reshape(b, t, COL_KV), cmp_w)
    gates_t = jnp.transpose(gate_p[:, :COL_GATE].reshape(b, t, N_KV, Q_PER_KV, 3), (0, 2, 4, 3, 1))
    att_p = _prompt_attention(q_p.reshape(b, t, COL_Q), kc_p, jnp.swapaxes(vc_p, 2, 3), ksel_aug,
                              vsel_t, kwin_p, vwin_t, gates_t.reshape(b, N_KV, 3 * Q_PER_KV, t))
    y_p = _ffn(conv_p.reshape(b * t, CONV_CH), att_p.reshape(b * t, COL_Q), xp2, *ffn_w, 256)

    n_s = db * ds
    xs2 = x_sample.reshape(n_s, D_MODEL)
    c_s, s_s = _rope_tables(jnp.tile(past + jnp.arange(ds), db))
    a_s, q_s, ckv_s, skv_s, wkv_s, gate_s = _project(xs2, ln1[0], w_in_b, c_s, s_s, n_s)[:6]
    a_s3 = a_s.reshape(db, ds, CONV_CH)
    hist_s = jnp.pad(state_conv[0], ((0, 0), (HIST_ROWS - (CONV_W - 1), 0), (0, 0)))
    conv_s = _conv_module(a_s3, hist_s, *conv_w, ds)
    rows_minor = lambda a, n, r: jnp.swapaxes(a.reshape(n, r, COL_KV), 1, 2)
    kc_s, vc_s = _compress_paged(rows_minor(cache_cmp_kv[0], n_phys, PAGE_SIZE), page_table, cmp_w)

    q5 = q_s.reshape(db, ds, N_KV, Q_PER_KV, HEAD_DIM)
    pad_tok = SUBLANES - ds
    pad_head = SUBLANES - Q_PER_KV
    q_rt = jnp.pad(jnp.transpose(q5, (0, 2, 3, 1, 4)), ((0, 0), (0, 0), (0, 0), (0, pad_tok), (0, 0)))
    q_rt = q_rt.reshape(db, N_KV, Q_PER_KV * SUBLANES, HEAD_DIM)
    q_tr = jnp.pad(jnp.transpose(q5, (0, 2, 1, 3, 4)), ((0, 0), (0, 0), (0, 0), (0, pad_head), (0, 0)))
    q_tr = q_tr.reshape(db, N_KV, ds * SUBLANES, HEAD_DIM)
    g5 = gate_s[:, :COL_GATE].reshape(db, ds, N_KV, Q_PER_KV, 3)
    g_tr = jnp.pad(jnp.transpose(g5, (0, 2, 4, 1, 3)), ((0, 0),) * 4 + ((0, pad_head),))
    g_tr = jnp.broadcast_to(g_tr.reshape(db, N_KV, 3, ds * SUBLANES)[..., None], (db, N_KV, 3, ds * SUBLANES, HEAD_DIM))
    ocmp, picks = _sample_cmp_select(q_rt, kc_s, vc_s, past, ds)
    sel_idx = picks.reshape(db, N_KV, SUBLANES, LANES)[:, :, :ds, :N_SEL].reshape(-1)
    ocmp_tr = jnp.transpose(ocmp.reshape(db, N_KV, Q_PER_KV, SUBLANES, HEAD_DIM)[:, :, :, :ds], (0, 1, 3, 2, 4))
    ocmp_tr = jnp.pad(ocmp_tr, ((0, 0), (0, 0), (0, 0), (0, pad_head), (0, 0)))
    ocmp_tr = ocmp_tr.reshape(db, N_KV, ds * SUBLANES, HEAD_DIM)
    new_t = lambda kv: jnp.pad(rows_minor(kv, db, ds), ((0, 0), (0, 0), (0, LANES - ds)))
    o_s = _sample_attention(sel_idx, page_table, rows_minor(cache_sel_kv[0], n_phys, PAGE_SIZE), q_tr,
                            new_t(skv_s), new_t(wkv_s), rows_minor(state_win_kv[0], db, win_rows),
                            ocmp_tr, g_tr, ds)
    att_s = o_s.reshape(db, N_KV, ds, SUBLANES, HEAD_DIM)[:, :, :, :Q_PER_KV]
    att_s = jnp.transpose(att_s, (0, 2, 1, 3, 4)).reshape(n_s, COL_Q)
    y_s = _ffn(conv_s.reshape(n_s, CONV_CH), att_s, xs2, *ffn_w, n_s)

    kv6 = lambda kv, bb, tt: kv.reshape(1, bb, tt, N_KV, 2, HEAD_DIM)
    kv6_t = lambda kv_t: jnp.swapaxes(kv_t, 1, 2).reshape(1, b, kv_t.shape[2], N_KV, 2, HEAD_DIM)
    new_win_s = jnp.concatenate([state_win_kv, kv6(wkv_s, db, ds)], axis=2)[:, :, ds:]
    new_conv_s = jnp.concatenate([state_conv[0], a_s3], axis=1)[None, :, ds:]
    return (y_p.reshape(b, t, D_MODEL), y_s.reshape(db, ds, D_MODEL),
            kv6_t(ckv_pt), kv6(ckv_s, db, ds), kv6_t(skv_pt), kv6(skv_s, db, ds),
            kv6_t(wkv_pt[:, :, t - min(WINDOW, t):]), new_win_s,
            a_p3[None, :, t - (CONV_W - 1):], new_conv_s)
```

```python
import functools

import jax
import jax.numpy as jnp
from jax import lax
from jax.experimental import pallas as pl
from jax.experimental.pallas import tpu as pltpu

D_MODEL = 1024
CONV_CH = 512
CONV_W = 31
N_HEADS = 8
HEAD_DIM = 64
N_KV = 2
Q_PER_KV = N_HEADS // N_KV
ROPE_DIM = HEAD_DIM // 4
ROPE_THETA = 500000.0
CMP_BLOCK = 64
SEL_BLOCK = CMP_BLOCK
N_SEL = 16
WINDOW = 512
CMP_HID = 2 * HEAD_DIM
COL_Q = N_HEADS * HEAD_DIM
COL_KV = 2 * N_KV * HEAD_DIM
COL_GATE = 3 * N_HEADS
D_IN = 2 * CONV_CH + COL_Q + 3 * COL_KV + COL_GATE
N_GROUPS = 4
EXPERTS_PER_GROUP = 8
N_EXPERTS = N_GROUPS * EXPERTS_PER_GROUP
TOP_K = 2
D_EXPERT = 512
PAGE_SIZE = 128
RMS_EPS = 1e-6
LN_EPS = 1e-5
NEG_INF = -1e30
ATT_SCALE = HEAD_DIM ** -0.5

LANES = 128
SUBLANES = 8
VMEM_LIMIT_BYTES = 56 * 1024 * 1024

D_IN_PAD = ((D_IN + LANES - 1) // LANES) * LANES
COL_GATE_OFF = 2 * CONV_CH + COL_Q + 3 * COL_KV
HIST_ROWS = 32
CONV_ROWS = 32
MOE_ROWS = 256
ROW_TILES = D_MODEL // LANES

BF16 = jnp.bfloat16
F32 = jnp.float32


def _cparams(*sem):
    return pltpu.CompilerParams(dimension_semantics=sem, vmem_limit_bytes=VMEM_LIMIT_BYTES)


def _rms(x, g):
    return x * lax.rsqrt(jnp.mean(x * x, axis=-1, keepdims=True) + RMS_EPS) * g


def _tiles_to_rows(tile_ref):
    n = tile_ref.shape[0] // ROW_TILES
    return jnp.concatenate([tile_ref[pl.ds(j, n, stride=ROW_TILES), :] for j in range(ROW_TILES)], axis=1)


def _rows_to_tiles(tile_ref, rows):
    for j in range(ROW_TILES):
        tile_ref[pl.ds(j, rows.shape[0], stride=ROW_TILES), :] = rows[:, j * LANES:(j + 1) * LANES]


def _rope_tables(pos):
    half = ROPE_DIM // 2
    inv = ROPE_THETA ** (-jnp.arange(half, dtype=F32) / half)
    ang = pos.astype(F32)[:, None] * inv
    cos, sin = jnp.cos(ang), jnp.sin(ang)
    m = jnp.arange(LANES) % HEAD_DIM
    idx = m % half
    c = jnp.where(m < ROPE_DIM, cos[:, idx], 1.0)
    s = jnp.where(m < half, -sin[:, idx], jnp.where(m < ROPE_DIM, sin[:, idx], 0.0))
    return c.astype(F32), s.astype(F32)


def _rope(v, c, s, first_half):
    w = v.shape[1]
    half = ROPE_DIM // 2
    partner = jnp.where(first_half, pltpu.roll(v, w - half, axis=1), pltpu.roll(v, half, axis=1))
    return v * c + partner * s


def _proj_kernel(x_ref, ln_ref, w_ref, c_ref, s_ref, a_ref, q_ref, ckv_ref, skv_ref, wkv_ref, gate_ref,
                 ckvt_ref, skvt_ref, wkvt_ref, ksel_ref, kwin_ref, vsel_ref, vwin_ref, *, t_tiles):
    x = x_ref[...]
    xn = _rms(x, ln_ref[...])
    p = jnp.dot(xn.astype(BF16), w_ref[...], preferred_element_type=F32)
    a_ref[...] = p[:, :CONV_CH] * jax.nn.sigmoid(p[:, CONV_CH:2 * CONV_CH])

    c128, s128 = c_ref[...], s_ref[...]
    tm = x.shape[0]
    lane_q = lax.broadcasted_iota(jnp.int32, (tm, COL_Q), 1)
    cq = jnp.concatenate([c128] * (COL_Q // LANES), axis=1)
    sq = jnp.concatenate([s128] * (COL_Q // LANES), axis=1)
    o = 2 * CONV_CH
    q = _rope(p[:, o:o + COL_Q], cq, sq, (lane_q % HEAD_DIM) < ROPE_DIM // 2)
    q_ref[...] = q.astype(q_ref.dtype)
    o += COL_Q

    lane_kv = lax.broadcasted_iota(jnp.int32, (tm, COL_KV), 1)
    is_k = (lane_kv % (2 * HEAD_DIM)) < HEAD_DIM
    ckv = jnp.where(is_k, jnp.concatenate([c128] * (COL_KV // LANES), axis=1), 1.0)
    skv = jnp.where(is_k, jnp.concatenate([s128] * (COL_KV // LANES), axis=1), 0.0)
    first_kv = (lane_kv % HEAD_DIM) < ROPE_DIM // 2
    kvs = []
    for ref, ref_t in ((ckv_ref, ckvt_ref), (skv_ref, skvt_ref), (wkv_ref, wkvt_ref)):
        kv = _rope(p[:, o:o + COL_KV], ckv, skv, first_kv)
        kv_t = kv.T
        ref[...] = kv
        ref_t[0] = kv_t
        kvs.append((kv, kv_t))
        o += COL_KV
    gate_ref[...] = jax.nn.sigmoid(p[:, o:o + LANES])

    kvw = 2 * HEAD_DIM
    lane = lax.broadcasted_iota(jnp.int32, (tm, LANES), 1)
    pos = (pl.program_id(0) % t_tiles) * tm + lax.broadcasted_iota(jnp.int32, (tm, LANES), 0)
    blk = pos // SEL_BLOCK
    is_key = lane < HEAD_DIM
    ones_rows = (lax.broadcasted_iota(jnp.int32, (V_ROWS - HEAD_DIM, tm), 0) == 0).astype(F32)
    (skv_v, skv_t), (wkv_v, wkv_t) = kvs[1], kvs[2]
    for g in range(N_KV):
        low = jnp.where(is_key, skv_v[:, g * kvw:(g + 1) * kvw], (blk == lane - HEAD_DIM).astype(F32))
        high = jnp.where(is_key, (blk == lane + HEAD_DIM).astype(F32), 0.0)
        ksel_ref[0, g] = jnp.concatenate([low, high], axis=1).astype(BF16)
        kwin_ref[0, g] = wkv_v[:, g * kvw:g * kvw + HEAD_DIM].astype(BF16)
        v_rows = slice(g * kvw + HEAD_DIM, (g + 1) * kvw)
        vsel_ref[0, g] = jnp.concatenate([skv_t[v_rows], ones_rows], axis=0).astype(BF16)
        vwin_ref[0, g] = jnp.concatenate([wkv_t[v_rows], ones_rows], axis=0).astype(BF16)


def _project(x2d, ln, w_pad, c_tab, s_tab, tm):
    n = x2d.shape[0]
    t = c_tab.shape[0]
    t_tiles = t // tm
    assert t // SEL_BLOCK <= KAUG - HEAD_DIM - HEAD_DIM, "one-hot block ids fit the augmented key"
    row = lambda i: (i, 0)
    tab = lambda i: (i % t_tiles, 0)
    const = lambda i: (0, 0)
    kv_t = jax.ShapeDtypeStruct((n // t, COL_KV, t), F32)
    kv_t_spec = pl.BlockSpec((1, COL_KV, tm), lambda i: (i // t_tiles, 0, i % t_tiles))
    keys = lambda w: (jax.ShapeDtypeStruct((n // t, N_KV, t, w), BF16),
                      pl.BlockSpec((1, N_KV, tm, w), lambda i: (i // t_tiles, 0, i % t_tiles, 0)))
    v_t = (jax.ShapeDtypeStruct((n // t, N_KV, V_ROWS, t), BF16),
           pl.BlockSpec((1, N_KV, V_ROWS, tm), lambda i: (i // t_tiles, 0, 0, i % t_tiles)))
    out_shape = (
        jax.ShapeDtypeStruct((n, CONV_CH), F32),
        jax.ShapeDtypeStruct((n, COL_Q), BF16),
        jax.ShapeDtypeStruct((n, COL_KV), F32),
        jax.ShapeDtypeStruct((n, COL_KV), F32),
        jax.ShapeDtypeStruct((n, COL_KV), F32),
        jax.ShapeDtypeStruct((n, LANES), F32),
        kv_t, kv_t, kv_t,
        keys(KAUG)[0], keys(HEAD_DIM)[0], v_t[0], v_t[0],
    )
    return pl.pallas_call(
        functools.partial(_proj_kernel, t_tiles=t_tiles),
        grid=(n // tm,),
        in_specs=[
            pl.BlockSpec((tm, D_MODEL), row),
            pl.BlockSpec((1, D_MODEL), const),
            pl.BlockSpec((D_MODEL, D_IN_PAD), const),
            pl.BlockSpec((tm, LANES), tab),
            pl.BlockSpec((tm, LANES), tab),
        ],
        out_specs=(
            pl.BlockSpec((tm, CONV_CH), row),
            pl.BlockSpec((tm, COL_Q), row),
            pl.BlockSpec((tm, COL_KV), row),
            pl.BlockSpec((tm, COL_KV), row),
            pl.BlockSpec((tm, COL_KV), row),
            pl.BlockSpec((tm, LANES), row),
            kv_t_spec, kv_t_spec, kv_t_spec,
            keys(KAUG)[1], keys(HEAD_DIM)[1], v_t[1], v_t[1],
        ),
        out_shape=out_shape,
        compiler_params=_cparams("arbitrary"),
        name="in_proj",
    )(x2d, ln.reshape(1, D_MODEL), w_pad, c_tab, s_tab)


def _conv_kernel(a_ref, hist_ref, w_ref, b_ref, g_ref, beta_ref, o_ref, sh_ref):
    tt = a_ref.shape[1]
    ext_ref = sh_ref.at[0]

    @pl.when(pl.program_id(1) == 0)
    def _():
        ext_ref[0:HIST_ROWS, :] = hist_ref[0]

    ext_ref[HIST_ROWS:HIST_ROWS + tt, :] = a_ref[0]
    span = HIST_ROWS + tt - SUBLANES
    for s in range(1, SUBLANES):
        sh_ref[s, 0:span, :] = ext_ref[s:s + span, :]

    lead = HIST_ROWS - (CONV_W - 1)
    rc = min(tt, CONV_ROWS)

    def chunk(i, carry):
        r0 = pl.multiple_of(i * rc, rc)
        acc = jnp.broadcast_to(b_ref[...], (rc, CONV_CH))
        for k in range(CONV_W):
            a, s = divmod(lead + k, SUBLANES)
            acc = acc + w_ref[k:k + 1, :] * sh_ref[s, pl.ds(r0 + a * SUBLANES, rc), :]
        o_ref[0, pl.ds(r0, rc), :] = acc
        return carry

    lax.fori_loop(0, tt // rc, chunk, 0)
    acc = o_ref[0]
    mu = jnp.mean(acc, axis=-1, keepdims=True)
    var = jnp.mean(jnp.square(acc - mu), axis=-1, keepdims=True)
    y = (acc - mu) * lax.rsqrt(var + LN_EPS) * g_ref[...] + beta_ref[...]
    o_ref[0] = y * jax.nn.sigmoid(y)
    carry = ext_ref[tt:tt + HIST_ROWS, :]
    ext_ref[0:HIST_ROWS, :] = carry


def _conv_module(a3d, hist, dw_w, dw_b, ln_g, ln_b, tt):
    b, t, _ = a3d.shape
    w_pad = jnp.pad(dw_w, ((0, HIST_ROWS - CONV_W), (0, 0)))
    vec = lambda i, j: (0, 0)
    return pl.pallas_call(
        _conv_kernel,
        grid=(b, t // tt),
        in_specs=[
            pl.BlockSpec((1, tt, CONV_CH), lambda i, j: (i, j, 0)),
            pl.BlockSpec((1, HIST_ROWS, CONV_CH), lambda i, j: (i, 0, 0)),
            pl.BlockSpec((HIST_ROWS, CONV_CH), vec),
            pl.BlockSpec((1, CONV_CH), vec),
            pl.BlockSpec((1, CONV_CH), vec),
            pl.BlockSpec((1, CONV_CH), vec),
        ],
        out_specs=pl.BlockSpec((1, tt, CONV_CH), lambda i, j: (i, j, 0)),
        out_shape=jax.ShapeDtypeStruct((b, t, CONV_CH), F32),
        scratch_shapes=[pltpu.VMEM((SUBLANES, HIST_ROWS + tt, CONV_CH), F32)],
        compiler_params=_cparams("arbitrary", "arbitrary"),
        name="conv_module",
    )(a3d, hist, w_pad, dw_b.reshape(1, -1), ln_g.reshape(1, -1), ln_b.reshape(1, -1))


L_GROUP = 2 * LANES // HEAD_DIM


def _compress_rows(x_refs, pe_ref, wk1_ref, wk2_ref, wv1_ref, wv2_ref, nb, pitch=CMP_BLOCK):
    hk = jnp.zeros((N_KV * nb, CMP_HID), F32)
    hv = jnp.zeros((N_KV * nb, CMP_HID), F32)
    for j in range(CMP_BLOCK // L_GROUP):
        parts_k, parts_v = [], []
        for x_ref in x_refs:
            xs = [x_ref[pl.ds(j * L_GROUP + i, nb, stride=pitch), :] + pe_ref[j * L_GROUP + i:j * L_GROUP + i + 1, :]
                  for i in range(L_GROUP)]
            parts_k.append(jnp.concatenate([x[:, :HEAD_DIM] for x in xs], axis=1))
            parts_v.append(jnp.concatenate([x[:, HEAD_DIM:] for x in xs], axis=1))
        xk = jnp.concatenate(parts_k, axis=0).astype(BF16)
        xv = jnp.concatenate(parts_v, axis=0).astype(BF16)
        rows = slice(j * L_GROUP * HEAD_DIM, (j + 1) * L_GROUP * HEAD_DIM)
        hk = hk + jnp.dot(xk, wk1_ref[rows, :], preferred_element_type=F32)
        hv = hv + jnp.dot(xv, wv1_ref[rows, :], preferred_element_type=F32)
    kc =jnp.dot((hk * jax.nn.sigmoid(hk)).astype(BF16), wk2_ref[...], preferred_element_type=F32)
    vc = jnp.dot((hv * jax.nn.sigmoid(hv)).astype(BF16), wv2_ref[...], preferred_element_type=F32)
    return kc, vc


def _compress_dense_kernel(*refs):
    x_refs, (pe_ref, wk1_ref, wk2_ref, wv1_ref, wv2_ref, kc_ref, vc_ref) = refs[:N_KV], refs[N_KV:]
    nb = x_refs[0].shape[1] // CMP_BLOCK
    kc, vc = _compress_rows([x.at[0] for x in x_refs], pe_ref, wk1_ref, wk2_ref, wv1_ref, wv2_ref, nb)
    for g in range(N_KV):
        kc_ref[0, g] = kc[g * nb:(g + 1) * nb]
        vc_ref[0, g] = vc[g * nb:(g + 1) * nb]


def _cmp_weight_specs():
    const = lambda *_: (0, 0)
    return [
        pl.BlockSpec((CMP_BLOCK, 2 * HEAD_DIM), const),
        pl.BlockSpec((CMP_BLOCK * HEAD_DIM, CMP_HID), const),
        pl.BlockSpec((CMP_HID, HEAD_DIM), const),
        pl.BlockSpec((CMP_BLOCK * HEAD_DIM, CMP_HID), const),
        pl.BlockSpec((CMP_HID, HEAD_DIM), const),
    ]


def _cmp_weights(pos_emb, w_k1, w_k2, w_v1, w_v2):
    pe = pos_emb.reshape(CMP_BLOCK, 2 * HEAD_DIM)
    return (pe, w_k1.reshape(-1, CMP_HID).astype(BF16), w_k2.astype(BF16),
            w_v1.reshape(-1, CMP_HID).astype(BF16), w_v2.astype(BF16))


def _compress_dense(kv3d, cmp_w):
    b, t, _ = kv3d.shape
    nb = t // CMP_BLOCK
    out = jax.ShapeDtypeStruct((b, N_KV, nb, HEAD_DIM), F32)
    ospec = pl.BlockSpec((1, N_KV, nb, HEAD_DIM), lambda i: (i, 0, 0, 0))
    return pl.pallas_call(
        _compress_dense_kernel,
        grid=(b,),
        in_specs=[pl.BlockSpec((1, t, 2 * HEAD_DIM), functools.partial(lambda g, i: (i, 0, g), g))
                  for g in range(N_KV)] + _cmp_weight_specs(),
        out_specs=(ospec, ospec),
        out_shape=(out, out),
        compiler_params=_cparams("arbitrary"),
        name="compress_prompt",
    )(*([kv3d] * N_KV), *cmp_w)


PAGES_PER_STEP = 64
BLOCK_PITCH = CMP_BLOCK + SUBLANES


def _compress_paged_kernel(pt_ref, cache_ref, pe_ref, wk1_ref, wk2_ref, wv1_ref, wv2_ref, kc_ref, vc_ref,
                           raw_ref, rows_ref, sem_ref):
    step = pl.program_id(0)
    n_steps = pl.num_programs(0)
    nb = PAGES_PER_STEP * PAGE_SIZE // CMP_BLOCK
    kvw = 2 * HEAD_DIM

    def page_copy(s, slot, p):
        return pltpu.make_async_copy(cache_ref.at[pt_ref[s * PAGES_PER_STEP + p]], raw_ref.at[slot, p], sem_ref.at[slot])

    def issue(s, slot):
        for p in range(PAGES_PER_STEP):
            page_copy(s, slot, p).start()

    slot = step % 2

    @pl.when(step == 0)
    def _():
        issue(step, slot)

    @pl.when(step + 1 < n_steps)
    def _():
        issue(step + 1, 1 - slot)

    for p in range(PAGES_PER_STEP):
        page_copy(step, slot, p).wait()

    eye = (lax.broadcasted_iota(jnp.int32, (PAGE_SIZE, PAGE_SIZE), 0)
           == lax.broadcasted_iota(jnp.int32, (PAGE_SIZE, PAGE_SIZE), 1)).astype(BF16)
    for p in range(PAGES_PER_STEP):
        page = lax.dot_general(eye, raw_ref[slot, p].astype(BF16), (((1,), (1,)), ((), ())),
                               preferred_element_type=F32)
        for g in range(N_KV):
            for n in range(PAGE_SIZE // CMP_BLOCK):
                row0 = (p * (PAGE_SIZE // CMP_BLOCK) + n) * BLOCK_PITCH
                rows_ref[g, row0:row0 + CMP_BLOCK, :] = page[n * CMP_BLOCK:(n + 1) * CMP_BLOCK, g * kvw:(g + 1) * kvw]

    kc, vc = _compress_rows([rows_ref.at[g] for g in range(N_KV)], pe_ref, wk1_ref, wk2_ref, wv1_ref, wv2_ref, nb,
                            pitch=BLOCK_PITCH)
    for g in range(N_KV):
        kc_ref[0, g] = kc[g * nb:(g + 1) * nb]
        vc_ref[0, g] = vc[g * nb:(g + 1) * nb]


def _compress_paged(cache_t, page_table, cmp_w):
    db, n_pages = page_table.shape
    steps_per_row = n_pages // PAGES_PER_STEP
    nb = PAGES_PER_STEP * PAGE_SIZE // CMP_BLOCK
    out = jax.ShapeDtypeStruct((db, N_KV, steps_per_row * nb, HEAD_DIM), F32)
    ospec = pl.BlockSpec((1, N_KV, nb, HEAD_DIM), lambda i, pt: (i // steps_per_row, 0, i % steps_per_row, 0))
    grid_spec = pltpu.PrefetchScalarGridSpec(
        num_scalar_prefetch=1,
        grid=(db * steps_per_row,),
        in_specs=[pl.BlockSpec(memory_space=pl.ANY)] + _cmp_weight_specs(),
        out_specs=(ospec, ospec),
        scratch_shapes=[pltpu.VMEM((2, PAGES_PER_STEP, COL_KV, PAGE_SIZE), F32),
                        pltpu.VMEM((N_KV, nb * BLOCK_PITCH, 2 * HEAD_DIM), F32),
                        pltpu.SemaphoreType.DMA((2,))],
    )
    return pl.pallas_call(
        _compress_paged_kernel,
        grid_spec=grid_spec,
        out_shape=(out, out),
        compiler_params=_cparams("arbitrary"),
        name="compress_paged",
    )(page_table.reshape(-1), cache_t, *cmp_w)


TQ = 2 * LANES
KC = 512
LOG2_E = 1.4426950408889634
BOUND_SLACK = 1.01
MAX_SHIFT = 60.0
WIN_KEYS = WINDOW + TQ
V_ROWS = HEAD_DIM + 16
KAUG = 2 * LANES


def _top_blocks(imp, cand, n_blocks):
    blk = lax.broadcasted_iota(jnp.int32, imp.shape, 0)
    score = jnp.where(cand, imp, -1.0)
    for _ in range(N_SEL):
        mx = jnp.max(score, axis=0, keepdims=True)
        idx = jnp.min(jnp.where(score == mx, blk, n_blocks), axis=0, keepdims=True)
        score = jnp.where(blk == idx, -2.0, score)
    return jnp.where(score < -1.5, 1.0, 0.0)


def _prompt_attn_kernel(q_ref, kc_ref, vct_ref, ksel_ref, vselt_ref, kwin_ref, vwint_ref, gate_ref, o_ref,
                        kmax_ref, acc_ref):
    qt = pl.program_id(2)
    t0 = qt * TQ
    nb = kc_ref.shape[2]
    width = Q_PER_KV * TQ

    q = q_ref[0].astype(F32) * (ATT_SCALE * LOG2_E)
    q_t = q.T
    q4 = jnp.concatenate([q_t[r * HEAD_DIM:(r + 1) * HEAD_DIM] for r in range(Q_PER_KV)], axis=1)
    q4b = q4.astype(BF16)
    tok = t0 + lax.broadcasted_iota(jnp.int32, (1, width), 1) % TQ

    sc = jnp.dot(kc_ref[0, 0].astype(BF16), q4b, preferred_element_type=F32)
    blk = lax.broadcasted_iota(jnp.int32, (nb, width), 0)
    valid_c = (blk + 1) * CMP_BLOCK - 1 <= tok
    sc = jnp.where(valid_c, sc, NEG_INF)
    e = jnp.where(valid_c, jnp.exp2(sc - jnp.max(sc, axis=0, keepdims=True)), 0.0)
    den = jnp.sum(e, axis=0, keepdims=True)
    p = e / jnp.where(den > 0.0, den, 1.0)
    o_cmp = jnp.dot(vct_ref[0, 0].astype(BF16), p.astype(BF16), preferred_element_type=F32)
    imp = p[:, 0:TQ]
    for r in range(1, Q_PER_KV):
        imp = imp + p[:, r * TQ:(r + 1) * TQ]

    tok1 = t0 + lax.broadcasted_iota(jnp.int32, (1, TQ), 1)
    own = tok1 // SEL_BLOCK
    blk1 = lax.broadcasted_iota(jnp.int32, (nb, TQ), 0)
    cand = blk1 < own
    sel = _top_blocks(imp, cand, nb)
    bias = jnp.where(cand, jnp.where(sel > 0.0, 0.0, NEG_INF), jnp.where(blk1 == own, 0.0, NEG_INF))
    bias4 = jnp.concatenate([bias] * Q_PER_KV, axis=1).astype(BF16)
    q_aug = jnp.concatenate([q4b, bias4, jnp.zeros((KAUG - HEAD_DIM - nb, width), BF16)], axis=0)

    def scores(c):
        return jnp.dot(ksel_ref[0, 0, pl.ds(pl.multiple_of(c * KC, KC), KC), :], q_aug, preferred_element_type=F32)

    def softmax_pv(c, s, m, acc):
        m_new = jnp.maximum(m, jnp.max(s, axis=0, keepdims=True))
        alpha = jnp.exp2(m - m_new)
        pr = jnp.exp2(s - m_new).astype(BF16)
        v_blk = vselt_ref[0, 0, :, pl.ds(pl.multiple_of(c * KC, KC), KC)]
        return m_new, alpha * acc + jnp.dot(v_blk, pr, preferred_element_type=F32)

    def sel_step(c, carry):
        s, m, acc = carry
        s_next = scores(c + 1)
        return (s_next,) + softmax_pv(c, s, m, acc)

    last = t0 // KC
    key = last * KC + lax.broadcasted_iota(jnp.int32, (KC, width), 0)

    def online_softmax():
        init = (scores(0), jnp.full((1, width), NEG_INF, F32), jnp.zeros((V_ROWS, width), F32))
        s_last, m_sel, acc = lax.fori_loop(0, last, sel_step, init)
        return softmax_pv(last, jnp.where(key <= tok, s_last, NEG_INF), m_sel, acc)[1]

    @pl.when(qt == 0)
    def _():
        def body(i, kmax):
            k = ksel_ref[0, 0, pl.ds(pl.multiple_of(i * KC, KC), KC), :].astype(F32)
            k = jnp.where(lax.broadcasted_iota(jnp.int32, k.shape, 1) < HEAD_DIM, k, 0.0)
            return jnp.maximum(kmax, jnp.max(jnp.sum(k * k, axis=1, keepdims=True), axis=0, keepdims=True))
        kmax_ref[...] = jnp.broadcast_to(lax.fori_loop(0, ksel_ref.shape[2] // KC, body, jnp.zeros((1, 1), F32)),
                                         kmax_ref.shape)

    q4f = q4b.astype(F32)
    bound = jnp.sqrt(jnp.sum(q4f * q4f, axis=0, keepdims=True) * kmax_ref[0:1, 0:1]) * BOUND_SLACK
    bounded = jnp.max(bound) <= MAX_SHIFT

    def bounded_pv(c, s, acc):
        v_blk = vselt_ref[0, 0, :, pl.ds(pl.multiple_of(c * KC, KC), KC)]
        return acc + jnp.dot(v_blk, jnp.exp2(s - bound).astype(BF16), preferred_element_type=F32)

    @pl.when(bounded)
    def _():
        acc = lax.fori_loop(0, last, lambda c, acc: bounded_pv(c, scores(c), acc), jnp.zeros((V_ROWS, width), F32))
        acc_ref[...] = bounded_pv(last, jnp.where(key <= tok, scores(last), NEG_INF), acc)

    @pl.when(jnp.logical_not(bounded))
    def _():
        acc_ref[...] = online_softmax()

    acc_sel = acc_ref[...]
    o_sel = acc_sel[0:HEAD_DIM] / acc_sel[HEAD_DIM:HEAD_DIM + 1]

    w0 = pl.multiple_of(jnp.maximum(t0 - WINDOW, 0), TQ)
    sw = jnp.dot(kwin_ref[0, 0, pl.ds(w0, WIN_KEYS), :], q4b, preferred_element_type=F32)
    dist = tok - (w0 + lax.broadcasted_iota(jnp.int32, (WIN_KEYS, width), 0))
    sw = jnp.where((dist >= 0) & (dist < WINDOW), sw, NEG_INF)
    pw = jnp.exp2(sw - jnp.max(sw, axis=0, keepdims=True)).astype(BF16)
    acc_win = jnp.dot(vwint_ref[0, 0, :, pl.ds(w0, WIN_KEYS)], pw, preferred_element_type=F32)
    o_win = acc_win[0:HEAD_DIM] / acc_win[HEAD_DIM:HEAD_DIM + 1]

    outs = []
    for r in range(Q_PER_KV):
        sl = slice(r * TQ, (r + 1) * TQ)
        g = [gate_ref[0, 0, j * Q_PER_KV + r:j * Q_PER_KV + r + 1, :] for j in range(3)]
        outs.append(g[0] * o_cmp[:, sl] + g[1] * o_sel[:, sl] + g[2] * o_win[:, sl])
    o_ref[0] = jnp.concatenate(outs, axis=0).T


def _prompt_attention(q3, k_c, v_ct, ksel_aug, vsel_t, kwin, vwin_t, gates_t):
    b, t, _ = q3.shape
    nb = k_c.shape[2]
    width = Q_PER_KV * HEAD_DIM
    per_bg = lambda i, g, j: (i, g, 0, 0)
    return pl.pallas_call(
        _prompt_attn_kernel,
        grid=(b, N_KV, t // TQ),
        in_specs=[
            pl.BlockSpec((1, TQ, width), lambda i, g, j: (i, j, g)),
            pl.BlockSpec((1, 1, nb, HEAD_DIM), per_bg),
            pl.BlockSpec((1, 1, HEAD_DIM, nb), per_bg),
            pl.BlockSpec((1, 1, t, KAUG), per_bg),
            pl.BlockSpec((1, 1, V_ROWS, t), per_bg),
            pl.BlockSpec((1, 1, t, HEAD_DIM), per_bg),
            pl.BlockSpec((1, 1, V_ROWS, t), per_bg),
            pl.BlockSpec((1, 1, 3 * Q_PER_KV, TQ), lambda i, g, j: (i, g, 0, j)),
        ],
        out_specs=pl.BlockSpec((1, TQ, width), lambda i, g, j: (i, j, g)),
        out_shape=jax.ShapeDtypeStruct((b, t, COL_Q), F32),
        scratch_shapes=[pltpu.VMEM((SUBLANES, LANES), F32), pltpu.VMEM((V_ROWS, Q_PER_KV * TQ), F32)],
        compiler_params=_cparams("arbitrary", "arbitrary", "arbitrary"),
        name="prompt_attention",
    )(q3, k_c, v_ct, ksel_aug, vsel_t, kwin, vwin_t, gates_t)


def _sample_cmp_kernel(q_ref, kc_ref, vc_ref, ocmp_ref, idx_ref, imp_ref, *, past_len, dec_seq):
    b = pl.program_id(0)
    nb = kc_ref.shape[2]
    rows = Q_PER_KV * SUBLANES
    t_row = lax.broadcasted_iota(jnp.int32, (rows, nb), 0) % SUBLANES
    blk = lax.broadcasted_iota(jnp.int32, (rows, nb), 1)
    valid = (blk + 1) * CMP_BLOCK - 1 <= past_len + t_row
    for g in range(N_KV):
        s = lax.dot_general(q_ref[0, g], kc_ref[0, g].astype(BF16), (((1,), (1,)), ((), ())),
                            preferred_element_type=F32) * ATT_SCALE
        s = jnp.where(valid, s, NEG_INF)
        e = jnp.where(valid, jnp.exp(s - jnp.max(s, axis=1, keepdims=True)), 0.0)
        den = jnp.sum(e, axis=1, keepdims=True)
        p = e / jnp.where(den > 0.0, den, 1.0)
        ocmp_ref[0, g] = jnp.dot(p.astype(BF16), vc_ref[0, g].astype(BF16), preferred_element_type=F32)
        imp = p[0:SUBLANES]
        for r in range(1, Q_PER_KV):
            imp = imp + p[r * SUBLANES:(r + 1) * SUBLANES]
        row0 = pl.multiple_of((b * N_KV + g) * SUBLANES, SUBLANES)
        imp_ref[pl.ds(row0, SUBLANES), :] = imp

    @pl.when(b == pl.num_programs(0) - 1)
    def _():
        n_rows = imp_ref.shape[0]
        lane = lax.broadcasted_iota(jnp.int32, (n_rows, nb), 1)
        own = (past_len + lax.broadcasted_iota(jnp.int32, (n_rows, nb), 0) % SUBLANES) // SEL_BLOCK
        score = jnp.where(lane < own, imp_ref[...], -1.0)
        col = lax.broadcasted_iota(jnp.int32, (n_rows, LANES), 1)
        picks = jnp.zeros((n_rows, LANES), jnp.int32)
        for i in range(N_SEL):
            mx = jnp.max(score, axis=1, keepdims=True)
            idx = jnp.min(jnp.where(score == mx, lane, nb), axis=1, keepdims=True)
            score = jnp.where(lane == idx, -2.0, score)
            picks = jnp.where(col == i, idx, picks)
        idx_ref[...] = picks


def _sample_cmp_select(q_rt, k_c, v_c, past_len, dec_seq):
    db, _, rows, _ = q_rt.shape
    nb = k_c.shape[2]
    spec4 = lambda r, c: pl.BlockSpec((1, N_KV, r, c), lambda i: (i, 0, 0, 0))
    n_rows = db * N_KV * SUBLANES
    return pl.pallas_call(
        functools.partial(_sample_cmp_kernel, past_len=past_len, dec_seq=dec_seq),
        grid=(db,),
        in_specs=[spec4(rows, HEAD_DIM), spec4(nb, HEAD_DIM), spec4(nb, HEAD_DIM)],
        out_specs=(spec4(rows, HEAD_DIM), pl.BlockSpec((n_rows, LANES), lambda i: (0, 0))),
        out_shape=(jax.ShapeDtypeStruct((db, N_KV, rows, HEAD_DIM), F32),
                   jax.ShapeDtypeStruct((n_rows, LANES), jnp.int32)),
        scratch_shapes=[pltpu.VMEM((n_rows, nb), F32)],
        compiler_params=_cparams("arbitrary"),
        name="sample_cmp_select",
    )(q_rt, k_c, v_c)


def _sample_attn_kernel(idx_ref, pt_ref, cache_ref, q_ref, snew_ref, wnew_ref, wstate_ref, ocmp_ref, gate_ref,
                        o_ref, kv_ref, sem_ref, *, dec_seq, n_pages):
    b = pl.program_id(0)
    n_b = pl.num_programs(0)
    kvw = 2 * HEAD_DIM
    blocks_per_page = PAGE_SIZE // SEL_BLOCK

    def block_id(bb, g, t, i):
        return idx_ref[((bb * N_KV + g) * dec_seq + t) * N_SEL + i]

    def slab_copy(bb, slot, g, t, i):
        page = pt_ref[bb * n_pages + block_id(bb, g, t, i) // blocks_per_page]
        return pltpu.make_async_copy(cache_ref.at[page, pl.ds(g * kvw, kvw), :], kv_ref.at[slot, g, t, i],
                                     sem_ref.at[slot])

    def for_all_slabs(fn):
        for g in range(N_KV):
            for t in range(dec_seq):
                for i in range(N_SEL):
                    fn(g, t, i)

    slot = b % 2

    @pl.when(b == 0)
    def _():
        for_all_slabs(lambda g, t, i: slab_copy(b, slot, g, t, i).start())

    @pl.when(b + 1 < n_b)
    def _():
        for_all_slabs(lambda g, t, i: slab_copy(b + 1, 1 - slot, g, t, i).start())

    rows = dec_seq * SUBLANES
    tok = lax.broadcasted_iota(jnp.int32, (rows, 1), 0) // SUBLANES
    n_state = wstate_ref.shape[2]

    def attend(q, k_t, v_t, valid):
        s = jnp.dot(q, k_t.astype(BF16), preferred_element_type=F32) * ATT_SCALE
        s = jnp.where(valid, s, NEG_INF)
        p = jnp.exp(s - jnp.max(s, axis=1, keepdims=True))
        den = jnp.sum(p, axis=1, keepdims=True)
        return lax.dot_general(p.astype(BF16), v_t.astype(BF16), (((1,), (1,)), ((), ())),
                               preferred_element_type=F32) / den

    o_win = []
    for g in range(N_KV):
        k_rows, v_rows = pl.ds(g * kvw, HEAD_DIM), pl.ds(g * kvw + HEAD_DIM, HEAD_DIM)
        k_t = jnp.concatenate([wstate_ref[0, k_rows, :], wnew_ref[0, k_rows, :]], axis=1)
        v_t = jnp.concatenate([wstate_ref[0, v_rows, :], wnew_ref[0, v_rows, :]], axis=1)
        lane = lax.broadcasted_iota(jnp.int32, (rows, n_state + LANES), 1)
        new_i = lane - n_state
        valid = ((lane < n_state) & (lane > tok)) | ((new_i >= 0) & (new_i <= tok) & (new_i < dec_seq))
        o_win.append(attend(q_ref[0, g], k_t, v_t, valid))

    for_all_slabs(lambda g, t, i: slab_copy(b, slot, g, t, i).wait())

    lane1 = lax.broadcasted_iota(jnp.int32, (1, PAGE_SIZE), 1)
    for g in range(N_KV):
        o_sel = []
        for t in range(dec_seq):
            k_parts = [kv_ref[slot, g, t, i, 0:HEAD_DIM, :] for i in range(N_SEL)]
            v_parts = [kv_ref[slot, g, t, i, HEAD_DIM:kvw, :] for i in range(N_SEL)]
            k_parts.append(snew_ref[0, g * kvw:g * kvw + HEAD_DIM, :])
            v_parts.append(snew_ref[0, g * kvw + HEAD_DIM:(g + 1) * kvw, :])
            halves = [lane1 // SEL_BLOCK == block_id(b, g, t, i) % blocks_per_page for i in range(N_SEL)]
            halves.append((lane1 <= t) & (lane1 < dec_seq))
            q = q_ref[0, g, t * SUBLANES:(t + 1) * SUBLANES, :]
            o_sel.append(attend(q, jnp.concatenate(k_parts, axis=1), jnp.concatenate(v_parts, axis=1),
                                jnp.concatenate(halves, axis=1)))
        o_sel = jnp.concatenate(o_sel, axis=0)
        o_ref[0, g] = (gate_ref[0, g, 0] * ocmp_ref[0, g] + gate_ref[0, g, 1] * o_sel
                       + gate_ref[0, g, 2] * o_win[g])


def _sample_attention(sel_idx, page_table, cache_t, q_tr, snew_t, wnew_t, wstate_t, ocmp_tr, gates_tr, dec_seq):
    db, n_pages = page_table.shape
    rows = dec_seq * SUBLANES
    n_state = wstate_t.shape[2]
    per_b4 = lambda i, *_: (i, 0, 0, 0)
    per_b3 = lambda i, *_: (i, 0, 0)
    grid_spec = pltpu.PrefetchScalarGridSpec(
        num_scalar_prefetch=2,
        grid=(db,),
        in_specs=[
            pl.BlockSpec(memory_space=pl.ANY),
            pl.BlockSpec((1, N_KV, rows, HEAD_DIM), per_b4),
            pl.BlockSpec((1, COL_KV, LANES), per_b3),
            pl.BlockSpec((1, COL_KV, LANES), per_b3),
            pl.BlockSpec((1, COL_KV, n_state), per_b3),
            pl.BlockSpec((1, N_KV, rows, HEAD_DIM), per_b4),
            pl.BlockSpec((1, N_KV, 3, rows, HEAD_DIM), lambda i, *_: (i, 0, 0, 0, 0)),
        ],
        out_specs=pl.BlockSpec((1, N_KV, rows, HEAD_DIM), per_b4),
        scratch_shapes=[pltpu.VMEM((2, N_KV, dec_seq, N_SEL, 2 * HEAD_DIM, PAGE_SIZE), F32),
                        pltpu.SemaphoreType.DMA((2,))],
    )
    return pl.pallas_call(
        functools.partial(_sample_attn_kernel, dec_seq=dec_seq, n_pages=n_pages),
        grid_spec=grid_spec,
        out_shape=jax.ShapeDtypeStruct((db, N_KV, rows, HEAD_DIM), F32),
        compiler_params=_cparams("arbitrary"),
        name="sample_attention",
    )(sel_idx, page_table.reshape(-1), cache_t, q_tr, snew_t, wnew_t, wstate_t, ocmp_tr, gates_tr)


ROUTE_E1, ROUTE_E2, ROUTE_W1, ROUTE_W2 = 0, 1, 2, 3


def _merge_kernel(conv_ref, att_ref, x_ref, gc_ref, ga_ref, wo_ref, ln2_ref, wr_ref, x1_ref, h3_ref, route_ref):
    mix = jnp.concatenate([_rms(conv_ref[...], gc_ref[...]), _rms(att_ref[...], ga_ref[...])], axis=1)
    x1 = x_ref[...] + jnp.dot(mix.astype(BF16), wo_ref[...], preferred_element_type=F32)
    x1_ref[...] = x1
    h = _rms(x1, ln2_ref[...])
    _rows_to_tiles(h3_ref, h)

    logits = jnp.dot(h.astype(BF16), wr_ref[...], preferred_element_type=F32)
    lane = lax.broadcasted_iota(jnp.int32, logits.shape, 1)
    is_g = lane < N_GROUPS
    lg = jnp.where(is_g, logits, NEG_INF)
    mg = jnp.max(lg, axis=1, keepdims=True)
    sg = jnp.sum(jnp.where(is_g, jnp.exp(lg - mg), 0.0), axis=1, keepdims=True)
    grp = jnp.min(jnp.where(lg == mg, lane, LANES), axis=1, keepdims=True)
    p_top = 1.0 / sg
    in_grp = ((lane + (EXPERTS_PER_GROUP - N_GROUPS)) // EXPERTS_PER_GROUP) == grp + 1
    le = jnp.where(in_grp, logits, NEG_INF)
    ee = jnp.where(in_grp, jnp.exp(le - jnp.max(le, axis=1, keepdims=True)), 0.0)
    pe = jnp.where(in_grp, ee / jnp.sum(ee, axis=1, keepdims=True), -1.0)
    p1 = jnp.max(pe, axis=1, keepdims=True)
    i1 = jnp.min(jnp.where(pe == p1, lane, LANES), axis=1, keepdims=True)
    pe2 = jnp.where(lane == i1, -1.0, pe)
    p2 = jnp.max(pe2, axis=1, keepdims=True)
    i2 = jnp.min(jnp.where(pe2 == p2, lane, LANES), axis=1, keepdims=True)
    den = p1 + p2
    rec = jnp.where(lane == ROUTE_E1, (i1 - N_GROUPS).astype(F32), 0.0)
    rec = jnp.where(lane == ROUTE_E2, (i2 - N_GROUPS).astype(F32), rec)
    rec = jnp.where(lane == ROUTE_W1, p1 / den * p_top, rec)
    rec = jnp.where(lane == ROUTE_W2, p2 / den * p_top, rec)
    route_ref[...] = rec


def _merge(conv2d, att2d, x2d, g_conv, g_att, w_out_b, ln2, w_route_b, tm):
    n = x2d.shape[0]
    row = lambda i: (i, 0)
    const = lambda i: (0, 0)
    x1, h_tiles, route = pl.pallas_call(
        _merge_kernel,
        grid=(n // tm,),
        in_specs=[
            pl.BlockSpec((tm, CONV_CH), row),
            pl.BlockSpec((tm, COL_Q), row),
            pl.BlockSpec((tm, D_MODEL), row),
            pl.BlockSpec((1, CONV_CH), const),
            pl.BlockSpec((1, COL_Q), const),
            pl.BlockSpec((CONV_CH + COL_Q, D_MODEL), const),
            pl.BlockSpec((1, D_MODEL), const),
            pl.BlockSpec((D_MODEL, LANES), const),
        ],
        out_specs=(
            pl.BlockSpec((tm, D_MODEL), row),
            pl.BlockSpec((tm * ROW_TILES, LANES), row),
            pl.BlockSpec((tm, LANES), row),
        ),
        out_shape=(
            jax.ShapeDtypeStruct((n, D_MODEL), F32),
            jax.ShapeDtypeStruct((n * ROW_TILES, LANES), F32),
            jax.ShapeDtypeStruct((n, LANES), F32),
        ),
        compiler_params=_cparams("arbitrary"),
        name="merge_route",
    )(conv2d, att2d, x2d, g_conv.reshape(1, -1), g_att.reshape(1, -1), w_out_b, ln2.reshape(1, -1), w_route_b)
    return x1, h_tiles.reshape(n, ROW_TILES, LANES), route


META_W = 8 * LANES
ROUTE_TILE = 1024


def _rank_kernel(route_ref, dest_ref, meta_ref, e_ref, rank_ref, count_ref):
    i = pl.program_id(0)
    rt = route_ref.shape[0]

    @pl.when(i == 0)
    def _():
        count_ref[...] = jnp.zeros(count_ref.shape, F32)

    route_t = route_ref[...].T
    expert_id = lax.broadcasted_iota(jnp.int32, (N_EXPERTS, rt), 0).astype(F32)
    before = (lax.broadcasted_iota(jnp.int32, (rt, rt), 0) < lax.broadcasted_iota(jnp.int32, (rt, rt), 1))
    before = before.astype(BF16)
    ones = jnp.ones((rt, LANES), BF16)
    e_rows, rank_rows = [], []
    for k in range(TOP_K):
        e_k = route_t[ROUTE_E1 + k:ROUTE_E1 + k + 1, :]
        onehot = (expert_id == e_k).astype(F32)
        earlier = jnp.dot(onehot.astype(BF16), before, preferred_element_type=F32)
        seen = count_ref[...]
        seen_w = jnp.concatenate([seen] * (rt // LANES), axis=1)
        rank_rows.append(jnp.sum(onehot * (earlier + seen_w), axis=0, keepdims=True))
        e_rows.append(e_k)
        count_ref[...] = seen + jnp.dot(onehot.astype(BF16), ones, preferred_element_type=F32)
    e_ref[i] = jnp.concatenate(e_rows, axis=0)
    rank_ref[i] = jnp.concatenate(rank_rows, axis=0)

    @pl.when(i == pl.num_programs(0) - 1)
    def _():
        counts = jnp.concatenate([count_ref[...]] * (META_W // LANES), axis=1)
        padded = jnp.floor((counts + (MOE_ROWS - 1)) * (1.0 / MOE_ROWS)) * MOE_ROWS
        lane = lax.broadcasted_iota(jnp.int32, (1, META_W), 1)
        chunk_start = lane.astype(F32) * MOE_ROWS
        chunk_e = jnp.zeros((1, META_W), F32)
        ends = jnp.zeros((1, META_W), F32)
        end = jnp.zeros((1, META_W), F32)
        starts = []
        for ex in range(N_EXPERTS):
            starts.append(end[:, 0:rt])
            end = end + padded[ex:ex + 1, :]
            chunk_e = chunk_e + (end <= chunk_start).astype(F32)
            ends = jnp.where(lane == ex, end, ends)
        n_used = end * (1.0 / MOE_ROWS)
        chunk_e = jnp.minimum(chunk_e, float(N_EXPERTS - 1))
        row = lax.broadcasted_iota(jnp.int32, (SUBLANES, META_W), 0)
        meta = jnp.where(row == 0, chunk_e, jnp.where(row == 1, n_used, jnp.where(row == 2, ends, 0.0)))
        meta_ref[...] = meta.astype(jnp.int32)

        def place(ti, carry):
            e_t = e_ref[ti]
            dest = rank_ref[ti]
            for ex in range(N_EXPERTS):
                dest = dest + jnp.where(e_t == float(ex), starts[ex], 0.0)
            dest_ref[ti] = dest.astype(jnp.int32)
            return carry

        lax.fori_loop(0, pl.num_programs(0), place, 0)


def _rank(route, rt):
    n = route.shape[0]
    tiles = n // rt
    assert n * TOP_K // MOE_ROWS + N_EXPERTS <= META_W and rt <= META_W, "chunk table is one row of META_W lanes"
    whole = lambda i: (0, 0, 0)
    return pl.pallas_call(
        _rank_kernel,
        grid=(tiles,),
        in_specs=[pl.BlockSpec((rt, LANES), lambda i: (i, 0))],
        out_specs=(pl.BlockSpec((tiles, TOP_K, rt), whole), pl.BlockSpec((SUBLANES, META_W), lambda i: (0, 0))),
        out_shape=(jax.ShapeDtypeStruct((tiles, TOP_K, rt), jnp.int32),
                   jax.ShapeDtypeStruct((SUBLANES, META_W), jnp.int32)),
        scratch_shapes=[pltpu.VMEM((tiles, TOP_K, rt), F32), pltpu.VMEM((tiles, TOP_K, rt), F32),
                        pltpu.VMEM((N_EXPERTS, LANES), F32)],
        compiler_params=_cparams("arbitrary"),
        name="route_rank",
    )(route)


def _row_copies_wait(src_ref, dst_ref, sem, n_rows):
    pltpu.make_async_copy(src_ref.at[pl.ds(0, n_rows)], dst_ref.at[pl.ds(0, n_rows)], sem).wait()


def _scatter_kernel(meta_ref, dest_ref, h3_ref, xs_ref, zero_ref, sem_ref):
    i = pl.program_id(0)
    rt = h3_ref.shape[0]
    n_chunks = xs_ref.shape[0] // MOE_ROWS

    @pl.when(i == 0)
    def _():
        zero_ref[...] = jnp.zeros(zero_ref.shape, F32)

        def zero_chunk(first_row):
            return pltpu.make_async_copy(zero_ref, xs_ref.at[pl.ds(first_row, MOE_ROWS)], sem_ref.at[1])

        def fills(act):
            prev = 0
            for ex in range(N_EXPERTS):
                end = meta_ref[2, ex]

                @pl.when(end > prev)
                def _():
                    act(zero_chunk(end - MOE_ROWS))
                prev = end
            for c in range(n_chunks - N_EXPERTS, n_chunks):
                @pl.when(c >= meta_ref[1, 0])
                def _():
                    act(zero_chunk(c * MOE_ROWS))

        fills(lambda copy: copy.start())
        fills(lambda copy: copy.wait())

    def issue(t, carry):
        for k in range(TOP_K):
            pltpu.make_async_copy(h3_ref.at[t], xs_ref.at[dest_ref[0, k, t]], sem_ref.at[0]).start(priority=k % 2)
        return carry

    lax.fori_loop(0, rt, issue, 0, unroll=8)
    _row_copies_wait(h3_ref, h3_ref, sem_ref.at[0], rt)
    _row_copies_wait(h3_ref, h3_ref, sem_ref.at[0], rt)


def _scatter_rows(h3, dest, meta, rt):
    n = h3.shape[0]
    n_slots = (n * TOP_K // MOE_ROWS + N_EXPERTS) * MOE_ROWS
    grid_spec = pltpu.PrefetchScalarGridSpec(
        num_scalar_prefetch=1,
        grid=(n // rt,),
        in_specs=[pl.BlockSpec((1, TOP_K, rt), lambda i, meta: (i, 0, 0), memory_space=pltpu.SMEM),
                  pl.BlockSpec((rt, ROW_TILES, LANES), lambda i, meta: (i, 0, 0))],
        out_specs=pl.BlockSpec(memory_space=pl.ANY),
        scratch_shapes=[pltpu.VMEM((MOE_ROWS, ROW_TILES, LANES), F32), pltpu.SemaphoreType.DMA((2,))],
    )
    return pl.pallas_call(
        _scatter_kernel,
        grid_spec=grid_spec,
        out_shape=jax.ShapeDtypeStruct((n_slots, ROW_TILES, LANES), F32),
        compiler_params=_cparams("arbitrary"),
        name="moe_scatter_rows",
    )(meta, dest, h3)


def _expert_kernel(meta_ref, xs_ref, wg_ref, wu_ref, wd_ref, ys_ref, wg_b, wu_b, wd_b):
    c = pl.program_id(0)
    in_use = c < meta_ref[1, 0]

    @pl.when(jnp.logical_not(in_use))
    def _():
        ys_ref[...] = jnp.zeros(ys_ref.shape, F32)

    @pl.when(in_use & ((c == 0) | (meta_ref[0, c] != meta_ref[0, jnp.maximum(c - 1, 0)])))
    def _():
        wg_b[...] = wg_ref[0].astype(BF16)
        wu_b[...] = wu_ref[0].astype(BF16)
        wd_b[...] = wd_ref[0].astype(BF16)

    @pl.when(in_use)
    def _():
        x = _tiles_to_rows(xs_ref).astype(BF16)
        gate = jnp.dot(x, wg_b[...], preferred_element_type=F32)
        up = jnp.dot(x, wu_b[...], preferred_element_type=F32)
        act = (gate * jax.nn.sigmoid(gate) * up).astype(BF16)
        _rows_to_tiles(ys_ref, jnp.dot(act, wd_b[...], preferred_element_type=F32))


def _expert_mlp(xs, meta, w_g, w_u, w_d):
    n_chunks = xs.shape[0] // MOE_ROWS
    rows = lambda c, meta: (c, 0)
    expert = lambda c, meta: (meta[0, jnp.minimum(c, meta[1, 0] - 1)], 0, 0)
    grid_spec = pltpu.PrefetchScalarGridSpec(
        num_scalar_prefetch=1,
        grid=(n_chunks,),
        in_specs=[
            pl.BlockSpec((MOE_ROWS * ROW_TILES, LANES), rows),
            pl.BlockSpec((1, D_MODEL, D_EXPERT), expert),
            pl.BlockSpec((1, D_MODEL, D_EXPERT), expert),
            pl.BlockSpec((1, D_EXPERT, D_MODEL), expert),
        ],
        out_specs=pl.BlockSpec((MOE_ROWS * ROW_TILES, LANES), rows),
        scratch_shapes=[pltpu.VMEM((D_MODEL, D_EXPERT), BF16), pltpu.VMEM((D_MODEL, D_EXPERT), BF16),
                        pltpu.VMEM((D_EXPERT, D_MODEL), BF16)],
    )
    return pl.pallas_call(
        _expert_kernel,
        grid_spec=grid_spec,
        out_shape=jax.ShapeDtypeStruct((xs.shape[0] * ROW_TILES, LANES), F32),
        compiler_params=_cparams("arbitrary"),
        name="expert_mlp",
    )(meta, xs.reshape(-1, LANES), w_g, w_u, w_d).reshape(xs.shape)


def _combine_kernel(dest_ref, nxt_ref, x1_ref, route_ref, ln_ref, ys_ref, o_ref, buf_ref, sem_ref):
    i = pl.program_id(0)
    n_tiles = pl.num_programs(0)
    rt = x1_ref.shape[0]

    def issue(table_ref, slot):
        def body(t, carry):
            for k in range(TOP_K):
                tile = pl.ds(pl.multiple_of(t * ROW_TILES, ROW_TILES), ROW_TILES)
                pltpu.make_async_copy(ys_ref.at[table_ref[0, k, t]], buf_ref.at[slot, k, tile],
                                      sem_ref.at[slot]).start(priority=k % 2)
            return carry
        lax.fori_loop(0, rt, body, 0, unroll=8)

    slot = i % 2

    @pl.when(i == 0)
    def _():
        issue(dest_ref, slot)

    @pl.when(i + 1 < n_tiles)
    def _():
        issue(nxt_ref, 1 - slot)

    for k in range(TOP_K):
        pltpu.make_async_copy(buf_ref.at[slot, k], buf_ref.at[slot, k], sem_ref.at[slot]).wait()
    route = route_ref[...]
    moe = None
    for k in range(TOP_K):
        y = _tiles_to_rows(buf_ref.at[slot, k])
        term = y * route[:, ROUTE_W1 + k:ROUTE_W1 + k + 1]
        moe = term if moe is None else moe + term
    o_ref[...] = _rms(x1_ref[...] + moe, ln_ref[...])


def _combine(x1, ys, dest, route, ln_final, rt):
    n = x1.shape[0]
    tiles = n // rt
    row = lambda i: (i, 0)
    table = lambda f: pl.BlockSpec((1, TOP_K, rt), f, memory_space=pltpu.SMEM)
    return pl.pallas_call(
        _combine_kernel,
        grid=(tiles,),
        in_specs=[
            table(lambda i: (i, 0, 0)),
            table(lambda i: (jnp.minimum(i + 1, tiles - 1), 0, 0)),
            pl.BlockSpec((rt, D_MODEL), row),
            pl.BlockSpec((rt, LANES), row),
            pl.BlockSpec((1, D_MODEL), lambda i: (0, 0)),
            pl.BlockSpec(memory_space=pl.ANY),
        ],
        out_specs=pl.BlockSpec((rt, D_MODEL), row),
        out_shape=jax.ShapeDtypeStruct((n, D_MODEL), F32),
        scratch_shapes=[pltpu.VMEM((2, TOP_K, rt * ROW_TILES, LANES), F32), pltpu.SemaphoreType.DMA((2,))],
        compiler_params=_cparams("arbitrary"),
        name="moe_combine_norm",
    )(dest, dest, x1, route, ln_final.reshape(1, -1), ys)


def _ffn(conv2d, att2d, x2d, g_conv, g_att, w_out_b, ln2, w_route_b, w_g, w_u, w_d, ln_final, tm):
    x1, h3, route = _merge(conv2d, att2d, x2d, g_conv, g_att, w_out_b, ln2, w_route_b, tm)
    rt = min(x2d.shape[0], ROUTE_TILE)
    dest, meta = _rank(route, rt)
    ys = _expert_mlp(_scatter_rows(h3, dest, meta, rt), meta, w_g, w_u, w_d)
    return _combine(x1, ys, dest, route, ln_final, rt)


def kernel(x_prompt, x_sample, cache_cmp_kv, cache_sel_kv, state_win_kv, state_conv, page_table, ln1, w_in, conv_dw_w, conv_dw_b, conv_ln_g, conv_ln_b, cmp_pos_emb, w_cmp_k1, w_cmp_k2, w_cmp_v1, w_cmp_v2, out_norm_conv, out_norm_att, w_out, ln2, w_router_group, w_router_expert, w_exp_gate, w_exp_up, w_exp_down, ln_final):
    depth = ln1.shape[0]
    assert depth == 1, "single-layer step"
    b, t, _ = x_prompt.shape
    db, ds, _ = x_sample.shape
    n_phys = cache_cmp_kv.shape[1]
    n_pages = page_table.shape[1]
    past = n_pages * PAGE_SIZE
    win_rows = state_win_kv.shape[2]
    assert ds < CMP_BLOCK and ds <= SUBLANES and past % SEL_BLOCK == 0 and past // SEL_BLOCK >= N_SEL
    assert win_rows == WINDOW and past >= WINDOW and t % KC == 0 and t >= WIN_KEYS

    w_in_b = jnp.pad(w_in[0], ((0, 0), (0, D_IN_PAD - D_IN))).astype(BF16)
    w_out_b = w_out[0].astype(BF16)
    w_route_b = jnp.pad(jnp.concatenate([w_router_group[0], w_router_expert[0]], axis=1),
                        ((0, 0), (0, LANES - N_GROUPS - N_EXPERTS))).astype(BF16)
    cmp_w = _cmp_weights(cmp_pos_emb[0], w_cmp_k1[0], w_cmp_k2[0], w_cmp_v1[0], w_cmp_v2[0])
    conv_w = (conv_dw_w[0], conv_dw_b[0], conv_ln_g[0], conv_ln_b[0])
    ffn_w = (out_norm_conv[0], out_norm_att[0], w_out_b, ln2[0], w_route_b, w_exp_gate[0], w_exp_up[0],
             w_exp_down[0], ln_final)

    xp2 = x_prompt.reshape(b * t, D_MODEL)
    c_p, s_p = _rope_tables(jnp.arange(t))
    (a_p, q_p, ckv_p, _, _, gate_p, ckv_pt, skv_pt, wkv_pt, ksel_aug, kwin_p, vsel_t, vwin_t) = _project(
        xp2, ln1[0], w_in_b, c_p, s_p, 512)
    a_p3 = a_p.reshape(b, t, CONV_CH)
    conv_p = _conv_module(a_p3, jnp.zeros((b, HIST_ROWS, CONV_CH), F32), *conv_w, 512)
    kc_p, vc_p = _compress_dense(ckv_p.reshape(b, t, COL_KV), cmp_w)
    gates_t = jnp.transpose(gate_p[:, :COL_GATE].reshape(b, t, N_KV, Q_PER_KV, 3), (0, 2, 4, 3, 1))
    att_p = _prompt_attention(q_p.reshape(b, t, COL_Q), kc_p, jnp.swapaxes(vc_p, 2, 3), ksel_aug,
                              vsel_t, kwin_p, vwin_t, gates_t.reshape(b, N_KV, 3 * Q_PER_KV, t))
    y_p = _ffn(conv_p.reshape(b * t, CONV_CH), att_p.reshape(b * t, COL_Q), xp2, *ffn_w, 512)

    n_s = db * ds
    xs2 = x_sample.reshape(n_s, D_MODEL)
    c_s, s_s = _rope_tables(jnp.tile(past + jnp.arange(ds), db))
    a_s, q_s, ckv_s, skv_s, wkv_s, gate_s = _project(xs2, ln1[0], w_in_b, c_s, s_s, n_s)[:6]
    a_s3 = a_s.reshape(db, ds, CONV_CH)
    hist_s = jnp.pad(state_conv[0], ((0, 0), (HIST_ROWS - (CONV_W - 1), 0), (0, 0)))
    conv_s = _conv_module(a_s3, hist_s, *conv_w, ds)
    rows_minor = lambda a, n, r: jnp.swapaxes(a.reshape(n, r, COL_KV), 1, 2)
    kc_s, vc_s = _compress_paged(rows_minor(cache_cmp_kv[0], n_phys, PAGE_SIZE), page_table, cmp_w)

    q5 = q_s.reshape(db, ds, N_KV, Q_PER_KV, HEAD_DIM)
    pad_tok = SUBLANES - ds
    pad_head = SUBLANES - Q_PER_KV
    q_rt = jnp.pad(jnp.transpose(q5, (0, 2, 3, 1, 4)), ((0, 0), (0, 0), (0, 0), (0, pad_tok), (0, 0)))
    q_rt = q_rt.reshape(db, N_KV, Q_PER_KV * SUBLANES, HEAD_DIM)
    q_tr = jnp.pad(jnp.transpose(q5, (0, 2, 1, 3, 4)), ((0, 0), (0, 0), (0, 0), (0, pad_head), (0, 0)))
    q_tr = q_tr.reshape(db, N_KV, ds * SUBLANES, HEAD_DIM)
    g5 = gate_s[:, :COL_GATE].reshape(db, ds, N_KV, Q_PER_KV, 3)
    g_tr = jnp.pad(jnp.transpose(g5, (0, 2, 4, 1, 3)), ((0, 0),) * 4 + ((0, pad_head),))
    g_tr = jnp.broadcast_to(g_tr.reshape(db, N_KV, 3, ds * SUBLANES)[..., None], (db, N_KV, 3, ds * SUBLANES, HEAD_DIM))
    ocmp, picks = _sample_cmp_select(q_rt, kc_s, vc_s, past, ds)
    sel_idx = picks.reshape(db, N_KV, SUBLANES, LANES)[:, :, :ds, :N_SEL].reshape(-1)
    ocmp_tr = jnp.transpose(ocmp.reshape(db, N_KV, Q_PER_KV, SUBLANES, HEAD_DIM)[:, :, :, :ds], (0, 1, 3, 2, 4))
    ocmp_tr = jnp.pad(ocmp_tr, ((0, 0), (0, 0), (0, 0), (0, pad_head), (0, 0)))
    ocmp_tr = ocmp_tr.reshape(db, N_KV, ds * SUBLANES, HEAD_DIM)
    new_t = lambda kv: jnp.pad(rows_minor(kv, db, ds), ((0, 0), (0, 0), (0, LANES - ds)))
    o_s = _sample_attention(sel_idx, page_table, rows_minor(cache_sel_kv[0], n_phys, PAGE_SIZE), q_tr,
                            new_t(skv_s), new_t(wkv_s), rows_minor(state_win_kv[0], db, win_rows),
                            ocmp_tr, g_tr, ds)
    att_s = o_s.reshape(db, N_KV, ds, SUBLANES, HEAD_DIM)[:, :, :, :Q_PER_KV]
    att_s = jnp.transpose(att_s, (0, 2, 1, 3, 4)).reshape(n_s, COL_Q)
    y_s = _ffn(conv_s.reshape(n_s, CONV_CH), att_s, xs2, *ffn_w, n_s)

    kv6 = lambda kv, bb, tt: kv.reshape(1, bb, tt, N_KV, 2, HEAD_DIM)
    kv6_t = lambda kv_t: jnp.swapaxes(kv_t, 1, 2).reshape(1, b, kv_t.shape[2], N_KV, 2, HEAD_DIM)
    new_win_s = jnp.concatenate([state_win_kv, kv6(wkv_s, db, ds)], axis=2)[:, :, ds:]
    new_conv_s = jnp.concatenate([state_conv[0], a_s3], axis=1)[None, :, ds:]
    return (y_p.reshape(b, t, D_MODEL), y_s.reshape(db, ds, D_MODEL),
            kv6_t(ckv_pt), kv6(ckv_s, db, ds), kv6_t(skv_pt), kv6(skv_s, db, ds),
            kv6_t(wkv_pt[:, :, t - min(WINDOW, t):]), new_win_s,
            a_p3[None, :, t - (CONV_W - 1):], new_conv_s)
```

```python
import functools

import jax
import jax.numpy as jnp
from jax import lax
from jax.experimental import pallas as pl
from jax.experimental.pallas import tpu as pltpu

D_MODEL = 1024
CONV_CH = 512
CONV_W = 31
N_HEADS = 8
HEAD_DIM = 64
N_KV = 2
Q_PER_KV = N_HEADS // N_KV
ROPE_DIM = HEAD_DIM // 4
ROPE_THETA = 500000.0
CMP_BLOCK = 64
SEL_BLOCK = CMP_BLOCK
N_SEL = 16
WINDOW = 512
CMP_HID = 2 * HEAD_DIM
COL_Q = N_HEADS * HEAD_DIM
COL_KV = 2 * N_KV * HEAD_DIM
COL_GATE = 3 * N_HEADS
D_IN = 2 * CONV_CH + COL_Q + 3 * COL_KV + COL_GATE
N_GROUPS = 4
EXPERTS_PER_GROUP = 8
N_EXPERTS = N_GROUPS * EXPERTS_PER_GROUP
TOP_K = 2
D_EXPERT = 512
PAGE_SIZE = 128
RMS_EPS = 1e-6
LN_EPS = 1e-5
NEG_INF = -1e30
ATT_SCALE = HEAD_DIM ** -0.5

LANES = 128
SUBLANES = 8
VMEM_LIMIT_BYTES = 56 * 1024 * 1024

D_IN_PAD = ((D_IN + LANES - 1) // LANES) * LANES
COL_GATE_OFF = 2 * CONV_CH + COL_Q + 3 * COL_KV
HIST_ROWS = 32
CONV_ROWS = 32
MOE_ROWS = 512
ROW_TILES = D_MODEL // LANES

BF16 = jnp.bfloat16
F32 = jnp.float32


def _cparams(*sem):
    return pltpu.CompilerParams(dimension_semantics=sem, vmem_limit_bytes=VMEM_LIMIT_BYTES)


def _rms(x, g):
    return x * lax.rsqrt(jnp.mean(x * x, axis=-1, keepdims=True) + RMS_EPS) * g


def _tiles_to_rows(tile_ref):
    n = tile_ref.shape[0] // ROW_TILES
    return jnp.concatenate([tile_ref[pl.ds(j, n, stride=ROW_TILES), :] for j in range(ROW_TILES)], axis=1)


def _rows_to_tiles(tile_ref, rows):
    for j in range(ROW_TILES):
        tile_ref[pl.ds(j, rows.shape[0], stride=ROW_TILES), :] = rows[:, j * LANES:(j + 1) * LANES]


def _rope_tables(pos):
    half = ROPE_DIM // 2
    inv = ROPE_THETA ** (-jnp.arange(half, dtype=F32) / half)
    ang = pos.astype(F32)[:, None] * inv
    cos, sin = jnp.cos(ang), jnp.sin(ang)
    m = jnp.arange(LANES) % HEAD_DIM
    idx = m % half
    c = jnp.where(m < ROPE_DIM, cos[:, idx], 1.0)
    s = jnp.where(m < half, -sin[:, idx], jnp.where(m < ROPE_DIM, sin[:, idx], 0.0))
    return c.astype(F32), s.astype(F32)


def _rope(v, c, s, first_half):
    w = v.shape[1]
    half = ROPE_DIM // 2
    partner = jnp.where(first_half, pltpu.roll(v, w - half, axis=1), pltpu.roll(v, half, axis=1))
    return v * c + partner * s


def _proj_kernel(x_ref, ln_ref, w_ref, c_ref, s_ref, a_ref, q_ref, ckv_ref, skv_ref, wkv_ref, gate_ref,
                 ckvt_ref, skvt_ref, wkvt_ref, ksel_ref, kwin_ref, vsel_ref, vwin_ref, *, t_tiles):
    x = x_ref[...]
    xn = _rms(x, ln_ref[...])
    p = jnp.dot(xn.astype(BF16), w_ref[...], preferred_element_type=F32)
    a_ref[...] = p[:, :CONV_CH] * jax.nn.sigmoid(p[:, CONV_CH:2 * CONV_CH])

    c128, s128 = c_ref[...], s_ref[...]
    tm = x.shape[0]
    lane_q = lax.broadcasted_iota(jnp.int32, (tm, COL_Q), 1)
    cq = jnp.concatenate([c128] * (COL_Q // LANES), axis=1)
    sq = jnp.concatenate([s128] * (COL_Q // LANES), axis=1)
    o = 2 * CONV_CH
    q = _rope(p[:, o:o + COL_Q], cq, sq, (lane_q % HEAD_DIM) < ROPE_DIM // 2)
    q_ref[...] = q.astype(q_ref.dtype)
    o += COL_Q

    lane_kv = lax.broadcasted_iota(jnp.int32, (tm, COL_KV), 1)
    is_k = (lane_kv % (2 * HEAD_DIM)) < HEAD_DIM
    ckv = jnp.where(is_k, jnp.concatenate([c128] * (COL_KV // LANES), axis=1), 1.0)
    skv = jnp.where(is_k, jnp.concatenate([s128] * (COL_KV // LANES), axis=1), 0.0)
    first_kv = (lane_kv % HEAD_DIM) < ROPE_DIM // 2
    kvs = []
    for ref, ref_t in ((ckv_ref, ckvt_ref), (skv_ref, skvt_ref), (wkv_ref, wkvt_ref)):
        kv = _rope(p[:, o:o + COL_KV], ckv, skv, first_kv)
        kv_t = kv.T
        ref[...] = kv
        ref_t[0] = kv_t
        kvs.append((kv, kv_t))
        o += COL_KV
    gate_ref[...] = jax.nn.sigmoid(p[:, o:o + LANES])

    kvw = 2 * HEAD_DIM
    lane = lax.broadcasted_iota(jnp.int32, (tm, LANES), 1)
    pos = (pl.program_id(0) % t_tiles) * tm + lax.broadcasted_iota(jnp.int32, (tm, LANES), 0)
    blk = pos // SEL_BLOCK
    is_key = lane < HEAD_DIM
    ones_rows = (lax.broadcasted_iota(jnp.int32, (V_ROWS - HEAD_DIM, tm), 0) == 0).astype(F32)
    (skv_v, skv_t), (wkv_v, wkv_t) = kvs[1], kvs[2]
    for g in range(N_KV):
        low = jnp.where(is_key, skv_v[:, g * kvw:(g + 1) * kvw], (blk == lane - HEAD_DIM).astype(F32))
        high = jnp.where(is_key, (blk == lane + HEAD_DIM).astype(F32), 0.0)
        ksel_ref[0, g] = jnp.concatenate([low, high], axis=1).astype(BF16)
        kwin_ref[0, g] = wkv_v[:, g * kvw:g * kvw + HEAD_DIM].astype(BF16)
        v_rows = slice(g * kvw + HEAD_DIM, (g + 1) * kvw)
        vsel_ref[0, g] = jnp.concatenate([skv_t[v_rows], ones_rows], axis=0).astype(BF16)
        vwin_ref[0, g] = jnp.concatenate([wkv_t[v_rows], ones_rows], axis=0).astype(BF16)


def _project(x2d, ln, w_pad, c_tab, s_tab, tm):
    n = x2d.shape[0]
    t = c_tab.shape[0]
    t_tiles = t // tm
    assert t // SEL_BLOCK <= KAUG - HEAD_DIM - HEAD_DIM, "one-hot block ids fit the augmented key"
    row = lambda i: (i, 0)
    tab = lambda i: (i % t_tiles, 0)
    const = lambda i: (0, 0)
    kv_t = jax.ShapeDtypeStruct((n // t, COL_KV, t), F32)
    kv_t_spec = pl.BlockSpec((1, COL_KV, tm), lambda i: (i // t_tiles, 0, i % t_tiles))
    keys = lambda w: (jax.ShapeDtypeStruct((n // t, N_KV, t, w), BF16),
                      pl.BlockSpec((1, N_KV, tm, w), lambda i: (i // t_tiles, 0, i % t_tiles, 0)))
    v_t = (jax.ShapeDtypeStruct((n // t, N_KV, V_ROWS, t), BF16),
           pl.BlockSpec((1, N_KV, V_ROWS, tm), lambda i: (i // t_tiles, 0, 0, i % t_tiles)))
    out_shape = (
        jax.ShapeDtypeStruct((n, CONV_CH), F32),
        jax.ShapeDtypeStruct((n, COL_Q), BF16),
        jax.ShapeDtypeStruct((n, COL_KV), F32),
        jax.ShapeDtypeStruct((n, COL_KV), F32),
        jax.ShapeDtypeStruct((n, COL_KV), F32),
        jax.ShapeDtypeStruct((n, LANES), F32),
        kv_t, kv_t, kv_t,
        keys(KAUG)[0], keys(HEAD_DIM)[0], v_t[0], v_t[0],
    )
    return pl.pallas_call(
        functools.partial(_proj_kernel, t_tiles=t_tiles),
        grid=(n // tm,),
        in_specs=[
            pl.BlockSpec((tm, D_MODEL), row),
            pl.BlockSpec((1, D_MODEL), const),
            pl.BlockSpec((D_MODEL, D_IN_PAD), const),
            pl.BlockSpec((tm, LANES), tab),
            pl.BlockSpec((tm, LANES), tab),
        ],
        out_specs=(
            pl.BlockSpec((tm, CONV_CH), row),
            pl.BlockSpec((tm, COL_Q), row),
            pl.BlockSpec((tm, COL_KV), row),
            pl.BlockSpec((tm, COL_KV), row),
            pl.BlockSpec((tm, COL_KV), row),
            pl.BlockSpec((tm, LANES), row),
            kv_t_spec, kv_t_spec, kv_t_spec,
            keys(KAUG)[1], keys(HEAD_DIM)[1], v_t[1], v_t[1],
        ),
        out_shape=out_shape,
        compiler_params=_cparams("arbitrary"),
        name="in_proj",
    )(x2d, ln.reshape(1, D_MODEL), w_pad, c_tab, s_tab)


def _conv_kernel(a_ref, hist_ref, w_ref, b_ref, g_ref, beta_ref, o_ref, sh_ref):
    tt = a_ref.shape[1]
    ext_ref = sh_ref.at[0]

    @pl.when(pl.program_id(1) == 0)
    def _():
        ext_ref[0:HIST_ROWS, :] = hist_ref[0]

    ext_ref[HIST_ROWS:HIST_ROWS + tt, :] = a_ref[0]
    span = HIST_ROWS + tt - SUBLANES
    for s in range(1, SUBLANES):
        sh_ref[s, 0:span, :] = ext_ref[s:s + span, :]

    lead = HIST_ROWS - (CONV_W - 1)
    rc = min(tt, CONV_ROWS)

    def chunk(i, carry):
        r0 = pl.multiple_of(i * rc, rc)
        acc = jnp.broadcast_to(b_ref[...], (rc, CONV_CH))
        for k in range(CONV_W):
            a, s = divmod(lead + k, SUBLANES)
            acc = acc + w_ref[k:k + 1, :] * sh_ref[s, pl.ds(r0 + a * SUBLANES, rc), :]
        o_ref[0, pl.ds(r0, rc), :] = acc
        return carry

    lax.fori_loop(0, tt // rc, chunk, 0)
    acc = o_ref[0]
    mu = jnp.mean(acc, axis=-1, keepdims=True)
    var = jnp.mean(jnp.square(acc - mu), axis=-1, keepdims=True)
    y = (acc - mu) * lax.rsqrt(var + LN_EPS) * g_ref[...] + beta_ref[...]
    o_ref[0] = y * jax.nn.sigmoid(y)
    carry = ext_ref[tt:tt + HIST_ROWS, :]
    ext_ref[0:HIST_ROWS, :] = carry


def _conv_module(a3d, hist, dw_w, dw_b, ln_g, ln_b, tt):
    b, t, _ = a3d.shape
    w_pad = jnp.pad(dw_w, ((0, HIST_ROWS - CONV_W), (0, 0)))
    vec = lambda i, j: (0, 0)
    return pl.pallas_call(
        _conv_kernel,
        grid=(b, t // tt),
        in_specs=[
            pl.BlockSpec((1, tt, CONV_CH), lambda i, j: (i, j, 0)),
            pl.BlockSpec((1, HIST_ROWS, CONV_CH), lambda i, j: (i, 0, 0)),
            pl.BlockSpec((HIST_ROWS, CONV_CH), vec),
            pl.BlockSpec((1, CONV_CH), vec),
            pl.BlockSpec((1, CONV_CH), vec),
            pl.BlockSpec((1, CONV_CH), vec),
        ],
        out_specs=pl.BlockSpec((1, tt, CONV_CH), lambda i, j: (i, j, 0)),
        out_shape=jax.ShapeDtypeStruct((b, t, CONV_CH), F32),
        scratch_shapes=[pltpu.VMEM((SUBLANES, HIST_ROWS + tt, CONV_CH), F32)],
        compiler_params=_cparams("arbitrary", "arbitrary"),
        name="conv_module",
    )(a3d, hist, w_pad, dw_b.reshape(1, -1), ln_g.reshape(1, -1), ln_b.reshape(1, -1))


L_GROUP = 2 * LANES // HEAD_DIM


def _compress_rows(x_refs, pe_ref, wk1_ref, wk2_ref, wv1_ref, wv2_ref, nb, pitch=CMP_BLOCK):
    hk = jnp.zeros((N_KV * nb, CMP_HID), F32)
    hv = jnp.zeros((N_KV * nb, CMP_HID), F32)
    for j in range(CMP_BLOCK // L_GROUP):
        parts_k, parts_v = [], []
        for x_ref in x_refs:
            xs = [x_ref[pl.ds(j * L_GROUP + i, nb, stride=pitch), :] + pe_ref[j * L_GROUP + i:j * L_GROUP + i + 1, :]
                  for i in range(L_GROUP)]
            parts_k.append(jnp.concatenate([x[:, :HEAD_DIM] for x in xs], axis=1))
            parts_v.append(jnp.concatenate([x[:, HEAD_DIM:] for x in xs], axis=1))
        xk = jnp.concatenate(parts_k, axis=0).astype(BF16)
        xv = jnp.concatenate(parts_v, axis=0).astype(BF16)
        rows = slice(j * L_GROUP * HEAD_DIM, (j + 1) * L_GROUP * HEAD_DIM)
        hk = hk + jnp.dot(xk, wk1_ref[rows, :], preferred_element_type=F32)
        hv = hv + jnp.dot(xv, wv1_ref[rows, :], preferred_element_type=F32)
    kc =jnp.dot((hk * jax.nn.sigmoid(hk)).astype(BF16), wk2_ref[...], preferred_element_type=F32)
    vc = jnp.dot((hv * jax.nn.sigmoid(hv)).astype(BF16), wv2_ref[...], preferred_element_type=F32)
    return kc, vc


def _compress_dense_kernel(*refs):
    x_refs, (pe_ref, wk1_ref, wk2_ref, wv1_ref, wv2_ref, kc_ref, vc_ref) = refs[:N_KV], refs[N_KV:]
    nb = x_refs[0].shape[1] // CMP_BLOCK
    kc, vc = _compress_rows([x.at[0] for x in x_refs], pe_ref, wk1_ref, wk2_ref, wv1_ref, wv2_ref, nb)
    for g in range(N_KV):
        kc_ref[0, g] = kc[g * nb:(g + 1) * nb]
        vc_ref[0, g] = vc[g * nb:(g + 1) * nb]


def _cmp_weight_specs():
    const = lambda *_: (0, 0)
    return [
        pl.BlockSpec((CMP_BLOCK, 2 * HEAD_DIM), const),
        pl.BlockSpec((CMP_BLOCK * HEAD_DIM, CMP_HID), const),
        pl.BlockSpec((CMP_HID, HEAD_DIM), const),
        pl.BlockSpec((CMP_BLOCK * HEAD_DIM, CMP_HID), const),
        pl.BlockSpec((CMP_HID, HEAD_DIM), const),
    ]


def _cmp_weights(pos_emb, w_k1, w_k2, w_v1, w_v2):
    pe = pos_emb.reshape(CMP_BLOCK, 2 * HEAD_DIM)
    return (pe, w_k1.reshape(-1, CMP_HID).astype(BF16), w_k2.astype(BF16),
            w_v1.reshape(-1, CMP_HID).astype(BF16), w_v2.astype(BF16))


def _compress_dense(kv3d, cmp_w):
    b, t, _ = kv3d.shape
    nb = t // CMP_BLOCK
    out = jax.ShapeDtypeStruct((b, N_KV, nb, HEAD_DIM), F32)
    ospec = pl.BlockSpec((1, N_KV, nb, HEAD_DIM), lambda i: (i, 0, 0, 0))
    return pl.pallas_call(
        _compress_dense_kernel,
        grid=(b,),
        in_specs=[pl.BlockSpec((1, t, 2 * HEAD_DIM), functools.partial(lambda g, i: (i, 0, g), g))
                  for g in range(N_KV)] + _cmp_weight_specs(),
        out_specs=(ospec, ospec),
        out_shape=(out, out),
        compiler_params=_cparams("arbitrary"),
        name="compress_prompt",
    )(*([kv3d] * N_KV), *cmp_w)


PAGES_PER_STEP = 64
BLOCK_PITCH = CMP_BLOCK + SUBLANES


def _compress_paged_kernel(pt_ref, cache_ref, pe_ref, wk1_ref, wk2_ref, wv1_ref, wv2_ref, kc_ref, vc_ref,
                           raw_ref, rows_ref, sem_ref):
    step = pl.program_id(0)
    n_steps = pl.num_programs(0)
    nb = PAGES_PER_STEP * PAGE_SIZE // CMP_BLOCK
    kvw = 2 * HEAD_DIM

    def page_copy(s, slot, p):
        return pltpu.make_async_copy(cache_ref.at[pt_ref[s * PAGES_PER_STEP + p]], raw_ref.at[slot, p], sem_ref.at[slot])

    def issue(s, slot):
        for p in range(PAGES_PER_STEP):
            page_copy(s, slot, p).start()

    slot = step % 2

    @pl.when(step == 0)
    def _():
        issue(step, slot)

    @pl.when(step + 1 < n_steps)
    def _():
        issue(step + 1, 1 - slot)

    for p in range(PAGES_PER_STEP):
        page_copy(step, slot, p).wait()

    eye = (lax.broadcasted_iota(jnp.int32, (PAGE_SIZE, PAGE_SIZE), 0)
           == lax.broadcasted_iota(jnp.int32, (PAGE_SIZE, PAGE_SIZE), 1)).astype(BF16)
    for p in range(PAGES_PER_STEP):
        page = lax.dot_general(eye, raw_ref[slot, p].astype(BF16), (((1,), (1,)), ((), ())),
                               preferred_element_type=F32)
        for g in range(N_KV):
            for n in range(PAGE_SIZE // CMP_BLOCK):
                row0 = (p * (PAGE_SIZE // CMP_BLOCK) + n) * BLOCK_PITCH
                rows_ref[g, row0:row0 + CMP_BLOCK, :] = page[n * CMP_BLOCK:(n + 1) * CMP_BLOCK, g * kvw:(g + 1) * kvw]

    kc, vc = _compress_rows([rows_ref.at[g] for g in range(N_KV)], pe_ref, wk1_ref, wk2_ref, wv1_ref, wv2_ref, nb,
                            pitch=BLOCK_PITCH)
    for g in range(N_KV):
        kc_ref[0, g] = kc[g * nb:(g + 1) * nb]
        vc_ref[0, g] = vc[g * nb:(g + 1) * nb]


def _compress_paged(cache_t, page_table, cmp_w):
    db, n_pages = page_table.shape
    steps_per_row = n_pages // PAGES_PER_STEP
    nb = PAGES_PER_STEP * PAGE_SIZE // CMP_BLOCK
    out = jax.ShapeDtypeStruct((db, N_KV, steps_per_row * nb, HEAD_DIM), F32)
    ospec = pl.BlockSpec((1, N_KV, nb, HEAD_DIM), lambda i, pt: (i // steps_per_row, 0, i % steps_per_row, 0))
    grid_spec = pltpu.PrefetchScalarGridSpec(
        num_scalar_prefetch=1,
        grid=(db * steps_per_row,),
        in_specs=[pl.BlockSpec(memory_space=pl.ANY)] + _cmp_weight_specs(),
        out_specs=(ospec, ospec),
        scratch_shapes=[pltpu.VMEM((2, PAGES_PER_STEP, COL_KV, PAGE_SIZE), F32),
                        pltpu.VMEM((N_KV, nb * BLOCK_PITCH, 2 * HEAD_DIM), F32),
                        pltpu.SemaphoreType.DMA((2,))],
    )
    return pl.pallas_call(
        _compress_paged_kernel,
        grid_spec=grid_spec,
        out_shape=(out, out),
        compiler_params=_cparams("arbitrary"),
        name="compress_paged",
    )(page_table.reshape(-1), cache_t, *cmp_w)


TQ = 2 * LANES
KC = 512
LOG2_E = 1.4426950408889634
BOUND_SLACK = 1.01
MAX_SHIFT = 60.0
WIN_KEYS = WINDOW + TQ
V_ROWS = HEAD_DIM + 16
KAUG = 2 * LANES


def _top_blocks(imp, cand, n_blocks):
    blk = lax.broadcasted_iota(jnp.int32, imp.shape, 0)
    score = jnp.where(cand, imp, -1.0)
    for _ in range(N_SEL):
        mx = jnp.max(score, axis=0, keepdims=True)
        idx = jnp.min(jnp.where(score == mx, blk, n_blocks), axis=0, keepdims=True)
        score = jnp.where(blk == idx, -2.0, score)
    return jnp.where(score < -1.5, 1.0, 0.0)


def _prompt_attn_kernel(q_ref, kc_ref, vct_ref, ksel_ref, vselt_ref, kwin_ref, vwint_ref, gate_ref, o_ref,
                        kmax_ref, acc_ref):
    qt = pl.program_id(2)
    t0 = qt * TQ
    nb = kc_ref.shape[2]
    width = Q_PER_KV * TQ

    q = q_ref[0].astype(F32) * (ATT_SCALE * LOG2_E)
    q_t = q.T
    q4 = jnp.concatenate([q_t[r * HEAD_DIM:(r + 1) * HEAD_DIM] for r in range(Q_PER_KV)], axis=1)
    q4b = q4.astype(BF16)
    tok = t0 + lax.broadcasted_iota(jnp.int32, (1, width), 1) % TQ

    sc = jnp.dot(kc_ref[0, 0].astype(BF16), q4b, preferred_element_type=F32)
    blk = lax.broadcasted_iota(jnp.int32, (nb, width), 0)
    valid_c = (blk + 1) * CMP_BLOCK - 1 <= tok
    sc = jnp.where(valid_c, sc, NEG_INF)
    e = jnp.where(valid_c, jnp.exp2(sc - jnp.max(sc, axis=0, keepdims=True)), 0.0)
    den = jnp.sum(e, axis=0, keepdims=True)
    p = e / jnp.where(den > 0.0, den, 1.0)
    o_cmp = jnp.dot(vct_ref[0, 0].astype(BF16), p.astype(BF16), preferred_element_type=F32)
    imp = p[:, 0:TQ]
    for r in range(1, Q_PER_KV):
        imp = imp + p[:, r * TQ:(r + 1) * TQ]

    tok1 = t0 + lax.broadcasted_iota(jnp.int32, (1, TQ), 1)
    own = tok1 // SEL_BLOCK
    blk1 = lax.broadcasted_iota(jnp.int32, (nb, TQ), 0)
    cand = blk1 < own
    sel = _top_blocks(imp, cand, nb)
    bias = jnp.where(cand, jnp.where(sel > 0.0, 0.0, NEG_INF), jnp.where(blk1 == own, 0.0, NEG_INF))
    bias4 = jnp.concatenate([bias] * Q_PER_KV, axis=1).astype(BF16)
    q_aug = jnp.concatenate([q4b, bias4, jnp.zeros((KAUG - HEAD_DIM - nb, width), BF16)], axis=0)

    def scores(c):
        return jnp.dot(ksel_ref[0, 0, pl.ds(pl.multiple_of(c * KC, KC), KC), :], q_aug, preferred_element_type=F32)

    def softmax_pv(c, s, m, acc):
        m_new = jnp.maximum(m, jnp.max(s, axis=0, keepdims=True))
        alpha = jnp.exp2(m - m_new)
        pr = jnp.exp2(s - m_new).astype(BF16)
        v_blk = vselt_ref[0, 0, :, pl.ds(pl.multiple_of(c * KC, KC), KC)]
        return m_new, alpha * acc + jnp.dot(v_blk, pr, preferred_element_type=F32)

    def sel_step(c, carry):
        s, m, acc = carry
        s_next = scores(c + 1)
        return (s_next,) + softmax_pv(c, s, m, acc)

    last = t0 // KC
    key = last * KC + lax.broadcasted_iota(jnp.int32, (KC, width), 0)

    def online_softmax():
        init = (scores(0), jnp.full((1, width), NEG_INF, F32), jnp.zeros((V_ROWS, width), F32))
        s_last, m_sel, acc = lax.fori_loop(0, last, sel_step, init)
        return softmax_pv(last, jnp.where(key <= tok, s_last, NEG_INF), m_sel, acc)[1]

    @pl.when(qt == 0)
    def _():
        def body(i, kmax):
            k = ksel_ref[0, 0, pl.ds(pl.multiple_of(i * KC, KC), KC), :].astype(F32)
            k = jnp.where(lax.broadcasted_iota(jnp.int32, k.shape, 1) < HEAD_DIM, k, 0.0)
            return jnp.maximum(kmax, jnp.max(jnp.sum(k * k, axis=1, keepdims=True), axis=0, keepdims=True))
        kmax_ref[...] = jnp.broadcast_to(lax.fori_loop(0, ksel_ref.shape[2] // KC, body, jnp.zeros((1, 1), F32)),
                                         kmax_ref.shape)

    q4f = q4b.astype(F32)
    bound = jnp.sqrt(jnp.sum(q4f * q4f, axis=0, keepdims=True) * kmax_ref[0:1, 0:1]) * BOUND_SLACK
    bounded = jnp.max(bound) <= MAX_SHIFT

    def bounded_pv(c, s, acc):
        v_blk = vselt_ref[0, 0, :, pl.ds(pl.multiple_of(c * KC, KC), KC)]
        return acc + jnp.dot(v_blk, jnp.exp2(s - bound).astype(BF16), preferred_element_type=F32)

    @pl.when(bounded)
    def _():
        acc = lax.fori_loop(0, last, lambda c, acc: bounded_pv(c, scores(c), acc), jnp.zeros((V_ROWS, width), F32))
        acc_ref[...] = bounded_pv(last, jnp.where(key <= tok, scores(last), NEG_INF), acc)

    @pl.when(jnp.logical_not(bounded))
    def _():
        acc_ref[...] = online_softmax()

    acc_sel = acc_ref[...]
    o_sel = acc_sel[0:HEAD_DIM] / acc_sel[HEAD_DIM:HEAD_DIM + 1]

    w0 = pl.multiple_of(jnp.maximum(t0 - WINDOW, 0), TQ)
    sw = jnp.dot(kwin_ref[0, 0, pl.ds(w0, WIN_KEYS), :], q4b, preferred_element_type=F32)
    dist = tok - (w0 + lax.broadcasted_iota(jnp.int32, (WIN_KEYS, width), 0))
    sw = jnp.where((dist >= 0) & (dist < WINDOW), sw, NEG_INF)
    pw = jnp.exp2(sw - jnp.max(sw, axis=0, keepdims=True)).astype(BF16)
    acc_win = jnp.dot(vwint_ref[0, 0, :, pl.ds(w0, WIN_KEYS)], pw, preferred_element_type=F32)
    o_win = acc_win[0:HEAD_DIM] / acc_win[HEAD_DIM:HEAD_DIM + 1]

    outs = []
    for r in range(Q_PER_KV):
        sl = slice(r * TQ, (r + 1) * TQ)
        g = [gate_ref[0, 0, j * Q_PER_KV + r:j * Q_PER_KV + r + 1, :] for j in range(3)]
        outs.append(g[0] * o_cmp[:, sl] + g[1] * o_sel[:, sl] + g[2] * o_win[:, sl])
    o_ref[0] = jnp.concatenate(outs, axis=0).T


def _prompt_attention(q3, k_c, v_ct, ksel_aug, vsel_t, kwin, vwin_t, gates_t):
    b, t, _ = q3.shape
    nb = k_c.shape[2]
    width = Q_PER_KV * HEAD_DIM
    per_bg = lambda i, g, j: (i, g, 0, 0)
    return pl.pallas_call(
        _prompt_attn_kernel,
        grid=(b, N_KV, t // TQ),
        in_specs=[
            pl.BlockSpec((1, TQ, width), lambda i, g, j: (i, j, g)),
            pl.BlockSpec((1, 1, nb, HEAD_DIM), per_bg),
            pl.BlockSpec((1, 1, HEAD_DIM, nb), per_bg),
            pl.BlockSpec((1, 1, t, KAUG), per_bg),
            pl.BlockSpec((1, 1, V_ROWS, t), per_bg),
            pl.BlockSpec((1, 1, t, HEAD_DIM), per_bg),
            pl.BlockSpec((1, 1, V_ROWS, t), per_bg),
            pl.BlockSpec((1, 1, 3 * Q_PER_KV, TQ), lambda i, g, j: (i, g, 0, j)),
        ],
        out_specs=pl.BlockSpec((1, TQ, width), lambda i, g, j: (i, j, g)),
        out_shape=jax.ShapeDtypeStruct((b, t, COL_Q), F32),
        scratch_shapes=[pltpu.VMEM((SUBLANES, LANES), F32), pltpu.VMEM((V_ROWS, Q_PER_KV * TQ), F32)],
        compiler_params=_cparams("arbitrary", "arbitrary", "arbitrary"),
        name="prompt_attention",
    )(q3, k_c, v_ct, ksel_aug, vsel_t, kwin, vwin_t, gates_t)


def _sample_cmp_kernel(q_ref, kc_ref, vc_ref, ocmp_ref, idx_ref, imp_ref, *, past_len, dec_seq):
    b = pl.program_id(0)
    nb = kc_ref.shape[2]
    rows = Q_PER_KV * SUBLANES
    t_row = lax.broadcasted_iota(jnp.int32, (rows, nb), 0) % SUBLANES
    blk = lax.broadcasted_iota(jnp.int32, (rows, nb), 1)
    valid = (blk + 1) * CMP_BLOCK - 1 <= past_len + t_row
    for g in range(N_KV):
        s = lax.dot_general(q_ref[0, g], kc_ref[0, g].astype(BF16), (((1,), (1,)), ((), ())),
                            preferred_element_type=F32) * ATT_SCALE
        s = jnp.where(valid, s, NEG_INF)
        e = jnp.where(valid, jnp.exp(s - jnp.max(s, axis=1, keepdims=True)), 0.0)
        den = jnp.sum(e, axis=1, keepdims=True)
        p = e / jnp.where(den > 0.0, den, 1.0)
        ocmp_ref[0, g] = jnp.dot(p.astype(BF16), vc_ref[0, g].astype(BF16), preferred_element_type=F32)
        imp = p[0:SUBLANES]
        for r in range(1, Q_PER_KV):
            imp = imp + p[r * SUBLANES:(r + 1) * SUBLANES]
        row0 = pl.multiple_of((b * N_KV + g) * SUBLANES, SUBLANES)
        imp_ref[pl.ds(row0, SUBLANES), :] = imp

    @pl.when(b == pl.num_programs(0) - 1)
    def _():
        n_rows = imp_ref.shape[0]
        lane = lax.broadcasted_iota(jnp.int32, (n_rows, nb), 1)
        own = (past_len + lax.broadcasted_iota(jnp.int32, (n_rows, nb), 0) % SUBLANES) // SEL_BLOCK
        score = jnp.where(lane < own, imp_ref[...], -1.0)
        col = lax.broadcasted_iota(jnp.int32, (n_rows, LANES), 1)
        picks = jnp.zeros((n_rows, LANES), jnp.int32)
        for i in range(N_SEL):
            mx = jnp.max(score, axis=1, keepdims=True)
            idx = jnp.min(jnp.where(score == mx, lane, nb), axis=1, keepdims=True)
            score = jnp.where(lane == idx, -2.0, score)
            picks = jnp.where(col == i, idx, picks)
        idx_ref[...] = picks


def _sample_cmp_select(q_rt, k_c, v_c, past_len, dec_seq):
    db, _, rows, _ = q_rt.shape
    nb = k_c.shape[2]
    spec4 = lambda r, c: pl.BlockSpec((1, N_KV, r, c), lambda i: (i, 0, 0, 0))
    n_rows = db * N_KV * SUBLANES
    return pl.pallas_call(
        functools.partial(_sample_cmp_kernel, past_len=past_len, dec_seq=dec_seq),
        grid=(db,),
        in_specs=[spec4(rows, HEAD_DIM), spec4(nb, HEAD_DIM), spec4(nb, HEAD_DIM)],
        out_specs=(spec4(rows, HEAD_DIM), pl.BlockSpec((n_rows, LANES), lambda i: (0, 0))),
        out_shape=(jax.ShapeDtypeStruct((db, N_KV, rows, HEAD_DIM), F32),
                   jax.ShapeDtypeStruct((n_rows, LANES), jnp.int32)),
        scratch_shapes=[pltpu.VMEM((n_rows, nb), F32)],
        compiler_params=_cparams("arbitrary"),
        name="sample_cmp_select",
    )(q_rt, k_c, v_c)


def _sample_attn_kernel(idx_ref, pt_ref, cache_ref, q_ref, snew_ref, wnew_ref, wstate_ref, ocmp_ref, gate_ref,
                        o_ref, kv_ref, sem_ref, *, dec_seq, n_pages):
    b = pl.program_id(0)
    n_b = pl.num_programs(0)
    kvw = 2 * HEAD_DIM
    blocks_per_page = PAGE_SIZE // SEL_BLOCK

    def block_id(bb, g, t, i):
        return idx_ref[((bb * N_KV + g) * dec_seq + t) * N_SEL + i]

    def slab_copy(bb, slot, g, t, i):
        page = pt_ref[bb * n_pages + block_id(bb, g, t, i) // blocks_per_page]
        return pltpu.make_async_copy(cache_ref.at[page, pl.ds(g * kvw, kvw), :], kv_ref.at[slot, g, t, i],
                                     sem_ref.at[slot])

    def for_all_slabs(fn):
        for g in range(N_KV):
            for t in range(dec_seq):
                for i in range(N_SEL):
                    fn(g, t, i)

    slot = b % 2

    @pl.when(b == 0)
    def _():
        for_all_slabs(lambda g, t, i: slab_copy(b, slot, g, t, i).start())

    @pl.when(b + 1 < n_b)
    def _():
        for_all_slabs(lambda g, t, i: slab_copy(b + 1, 1 - slot, g, t, i).start())

    rows = dec_seq * SUBLANES
    tok = lax.broadcasted_iota(jnp.int32, (rows, 1), 0) // SUBLANES
    n_state = wstate_ref.shape[2]

    def attend(q, k_t, v_t, valid):
        s = jnp.dot(q, k_t.astype(BF16), preferred_element_type=F32) * ATT_SCALE
        s = jnp.where(valid, s, NEG_INF)
        p = jnp.exp(s - jnp.max(s, axis=1, keepdims=True))
        den = jnp.sum(p, axis=1, keepdims=True)
        return lax.dot_general(p.astype(BF16), v_t.astype(BF16), (((1,), (1,)), ((), ())),
                               preferred_element_type=F32) / den

    o_win = []
    for g in range(N_KV):
        k_rows, v_rows = pl.ds(g * kvw, HEAD_DIM), pl.ds(g * kvw + HEAD_DIM, HEAD_DIM)
        k_t = jnp.concatenate([wstate_ref[0, k_rows, :], wnew_ref[0, k_rows, :]], axis=1)
        v_t = jnp.concatenate([wstate_ref[0, v_rows, :], wnew_ref[0, v_rows, :]], axis=1)
        lane = lax.broadcasted_iota(jnp.int32, (rows, n_state + LANES), 1)
        new_i = lane - n_state
        valid = ((lane < n_state) & (lane > tok)) | ((new_i >= 0) & (new_i <= tok) & (new_i < dec_seq))
        o_win.append(attend(q_ref[0, g], k_t, v_t, valid))

    for_all_slabs(lambda g, t, i: slab_copy(b, slot, g, t, i).wait())

    lane1 = lax.broadcasted_iota(jnp.int32, (1, PAGE_SIZE), 1)
    for g in range(N_KV):
        o_sel = []
        for t in range(dec_seq):
            k_parts = [kv_ref[slot, g, t, i, 0:HEAD_DIM, :] for i in range(N_SEL)]
            v_parts = [kv_ref[slot, g, t, i, HEAD_DIM:kvw, :] for i in range(N_SEL)]
            k_parts.append(snew_ref[0, g * kvw:g * kvw + HEAD_DIM, :])
            v_parts.append(snew_ref[0, g * kvw + HEAD_DIM:(g + 1) * kvw, :])
            halves = [lane1 // SEL_BLOCK == block_id(b, g, t, i) % blocks_per_page for i in range(N_SEL)]
            halves.append((lane1 <= t) & (lane1 < dec_seq))
            q = q_ref[0, g, t * SUBLANES:(t + 1) * SUBLANES, :]
            o_sel.append(attend(q, jnp.concatenate(k_parts, axis=1), jnp.concatenate(v_parts, axis=1),
                                jnp.concatenate(halves, axis=1)))
        o_sel = jnp.concatenate(o_sel, axis=0)
        o_ref[0, g] = (gate_ref[0, g, 0] * ocmp_ref[0, g] + gate_ref[0, g, 1] * o_sel
                       + gate_ref[0, g, 2] * o_win[g])


def _sample_attention(sel_idx, page_table, cache_t, q_tr, snew_t, wnew_t, wstate_t, ocmp_tr, gates_tr, dec_seq):
    db, n_pages = page_table.shape
    rows = dec_seq * SUBLANES
    n_state = wstate_t.shape[2]
    per_b4 = lambda i, *_: (i, 0, 0, 0)
    per_b3 = lambda i, *_: (i, 0, 0)
    grid_spec = pltpu.PrefetchScalarGridSpec(
        num_scalar_prefetch=2,
        grid=(db,),
        in_specs=[
            pl.BlockSpec(memory_space=pl.ANY),
            pl.BlockSpec((1, N_KV, rows, HEAD_DIM), per_b4),
            pl.BlockSpec((1, COL_KV, LANES), per_b3),
            pl.BlockSpec((1, COL_KV, LANES), per_b3),
            pl.BlockSpec((1, COL_KV, n_state), per_b3),
            pl.BlockSpec((1, N_KV, rows, HEAD_DIM), per_b4),
            pl.BlockSpec((1, N_KV, 3, rows, HEAD_DIM), lambda i, *_: (i, 0, 0, 0, 0)),
        ],
        out_specs=pl.BlockSpec((1, N_KV, rows, HEAD_DIM), per_b4),
        scratch_shapes=[pltpu.VMEM((2, N_KV, dec_seq, N_SEL, 2 * HEAD_DIM, PAGE_SIZE), F32),
                        pltpu.SemaphoreType.DMA((2,))],
    )
    return pl.pallas_call(
        functools.partial(_sample_attn_kernel, dec_seq=dec_seq, n_pages=n_pages),
        grid_spec=grid_spec,
        out_shape=jax.ShapeDtypeStruct((db, N_KV, rows, HEAD_DIM), F32),
        compiler_params=_cparams("arbitrary"),
        name="sample_attention",
    )(sel_idx, page_table.reshape(-1), cache_t, q_tr, snew_t, wnew_t, wstate_t, ocmp_tr, gates_tr)


ROUTE_E1, ROUTE_E2, ROUTE_W1, ROUTE_W2 = 0, 1, 2, 3


def _merge_kernel(conv_ref, att_ref, x_ref, gc_ref, ga_ref, wo_ref, ln2_ref, wr_ref, x1_ref, h3_ref, route_ref):
    mix = jnp.concatenate([_rms(conv_ref[...], gc_ref[...]), _rms(att_ref[...], ga_ref[...])], axis=1)
    x1 = x_ref[...] + jnp.dot(mix.astype(BF16), wo_ref[...], preferred_element_type=F32)
    x1_ref[...] = x1
    h = _rms(x1, ln2_ref[...])
    _rows_to_tiles(h3_ref, h)

    logits = jnp.dot(h.astype(BF16), wr_ref[...], preferred_element_type=F32)
    lane = lax.broadcasted_iota(jnp.int32, logits.shape, 1)
    is_g = lane < N_GROUPS
    lg = jnp.where(is_g, logits, NEG_INF)
    mg = jnp.max(lg, axis=1, keepdims=True)
    sg = jnp.sum(jnp.where(is_g, jnp.exp(lg - mg), 0.0), axis=1, keepdims=True)
    grp = jnp.min(jnp.where(lg == mg, lane, LANES), axis=1, keepdims=True)
    p_top = 1.0 / sg
    in_grp = ((lane + (EXPERTS_PER_GROUP - N_GROUPS)) // EXPERTS_PER_GROUP) == grp + 1
    le = jnp.where(in_grp, logits, NEG_INF)
    ee = jnp.where(in_grp, jnp.exp(le - jnp.max(le, axis=1, keepdims=True)), 0.0)
    pe = jnp.where(in_grp, ee / jnp.sum(ee, axis=1, keepdims=True), -1.0)
    p1 = jnp.max(pe, axis=1, keepdims=True)
    i1 = jnp.min(jnp.where(pe == p1, lane, LANES), axis=1, keepdims=True)
    pe2 = jnp.where(lane == i1, -1.0, pe)
    p2 = jnp.max(pe2, axis=1, keepdims=True)
    i2 = jnp.min(jnp.where(pe2 == p2, lane, LANES), axis=1, keepdims=True)
    den = p1 + p2
    rec = jnp.where(lane == ROUTE_E1, (i1 - N_GROUPS).astype(F32), 0.0)
    rec = jnp.where(lane == ROUTE_E2, (i2 - N_GROUPS).astype(F32), rec)
    rec = jnp.where(lane == ROUTE_W1, p1 / den * p_top, rec)
    rec = jnp.where(lane == ROUTE_W2, p2 / den * p_top, rec)
    route_ref[...] = rec


def _merge(conv2d, att2d, x2d, g_conv, g_att, w_out_b, ln2, w_route_b, tm):
    n = x2d.shape[0]
    row = lambda i: (i, 0)
    const = lambda i: (0, 0)
    x1, h_tiles, route = pl.pallas_call(
        _merge_kernel,
        grid=(n // tm,),
        in_specs=[
            pl.BlockSpec((tm, CONV_CH), row),
            pl.BlockSpec((tm, COL_Q), row),
            pl.BlockSpec((tm, D_MODEL), row),
            pl.BlockSpec((1, CONV_CH), const),
            pl.BlockSpec((1, COL_Q), const),
            pl.BlockSpec((CONV_CH + COL_Q, D_MODEL), const),
            pl.BlockSpec((1, D_MODEL), const),
            pl.BlockSpec((D_MODEL, LANES), const),
        ],
        out_specs=(
            pl.BlockSpec((tm, D_MODEL), row),
            pl.BlockSpec((tm * ROW_TILES, LANES), row),
            pl.BlockSpec((tm, LANES), row),
        ),
        out_shape=(
            jax.ShapeDtypeStruct((n, D_MODEL), F32),
            jax.ShapeDtypeStruct((n * ROW_TILES, LANES), F32),
            jax.ShapeDtypeStruct((n, LANES), F32),
        ),
        compiler_params=_cparams("arbitrary"),
        name="merge_route",
    )(conv2d, att2d, x2d, g_conv.reshape(1, -1), g_att.reshape(1, -1), w_out_b, ln2.reshape(1, -1), w_route_b)
    return x1, h_tiles.reshape(n, ROW_TILES, LANES), route


META_W = 8 * LANES
ROUTE_TILE = 1024


def _rank_kernel(route_ref, dest_ref, meta_ref, e_ref, rank_ref, count_ref):
    i = pl.program_id(0)
    rt = route_ref.shape[0]

    @pl.when(i == 0)
    def _():
        count_ref[...] = jnp.zeros(count_ref.shape, F32)

    route_t = route_ref[...].T
    expert_id = lax.broadcasted_iota(jnp.int32, (N_EXPERTS, rt), 0).astype(F32)
    before = (lax.broadcasted_iota(jnp.int32, (rt, rt), 0) < lax.broadcasted_iota(jnp.int32, (rt, rt), 1))
    before = before.astype(BF16)
    ones = jnp.ones((rt, LANES), BF16)
    e_rows, rank_rows = [], []
    for k in range(TOP_K):
        e_k = route_t[ROUTE_E1 + k:ROUTE_E1 + k + 1, :]
        onehot = (expert_id == e_k).astype(F32)
        earlier = jnp.dot(onehot.astype(BF16), before, preferred_element_type=F32)
        seen = count_ref[...]
        seen_w = jnp.concatenate([seen] * (rt // LANES), axis=1)
        rank_rows.append(jnp.sum(onehot * (earlier + seen_w), axis=0, keepdims=True))
        e_rows.append(e_k)
        count_ref[...] = seen + jnp.dot(onehot.astype(BF16), ones, preferred_element_type=F32)
    e_ref[i] = jnp.concatenate(e_rows, axis=0)
    rank_ref[i] = jnp.concatenate(rank_rows, axis=0)

    @pl.when(i == pl.num_programs(0) - 1)
    def _():
        counts = jnp.concatenate([count_ref[...]] * (META_W // LANES), axis=1)
        padded = jnp.floor((counts + (MOE_ROWS - 1)) * (1.0 / MOE_ROWS)) * MOE_ROWS
        lane = lax.broadcasted_iota(jnp.int32, (1, META_W), 1)
        chunk_start = lane.astype(F32) * MOE_ROWS
        chunk_e = jnp.zeros((1, META_W), F32)
        ends = jnp.zeros((1, META_W), F32)
        end = jnp.zeros((1, META_W), F32)
        starts = []
        for ex in range(N_EXPERTS):
            starts.append(end[:, 0:rt])
            end = end + padded[ex:ex + 1, :]
            chunk_e = chunk_e + (end <= chunk_start).astype(F32)
            ends = jnp.where(lane == ex, end, ends)
        n_used = end * (1.0 / MOE_ROWS)
        chunk_e = jnp.minimum(chunk_e, float(N_EXPERTS - 1))
        row = lax.broadcasted_iota(jnp.int32, (SUBLANES, META_W), 0)
        meta = jnp.where(row == 0, chunk_e, jnp.where(row == 1, n_used, jnp.where(row == 2, ends, 0.0)))
        meta_ref[...] = meta.astype(jnp.int32)

        def place(ti, carry):
            e_t = e_ref[ti]
            dest = rank_ref[ti]
            for ex in range(N_EXPERTS):
                dest = dest + jnp.where(e_t == float(ex), starts[ex], 0.0)
            dest_ref[ti] = dest.astype(jnp.int32)
            return carry

        lax.fori_loop(0, pl.num_programs(0), place, 0)


def _rank(route, rt):
    n = route.shape[0]
    tiles = n // rt
    assert n * TOP_K // MOE_ROWS + N_EXPERTS <= META_W and rt <= META_W, "chunk table is one row of META_W lanes"
    whole = lambda i: (0, 0, 0)
    return pl.pallas_call(
        _rank_kernel,
        grid=(tiles,),
        in_specs=[pl.BlockSpec((rt, LANES), lambda i: (i, 0))],
        out_specs=(pl.BlockSpec((tiles, TOP_K, rt), whole), pl.BlockSpec((SUBLANES, META_W), lambda i: (0, 0))),
        out_shape=(jax.ShapeDtypeStruct((tiles, TOP_K, rt), jnp.int32),
                   jax.ShapeDtypeStruct((SUBLANES, META_W), jnp.int32)),
        scratch_shapes=[pltpu.VMEM((tiles, TOP_K, rt), F32), pltpu.VMEM((tiles, TOP_K, rt), F32),
                        pltpu.VMEM((N_EXPERTS, LANES), F32)],
        compiler_params=_cparams("arbitrary"),
        name="route_rank",
    )(route)


def _row_copies_wait(src_ref, dst_ref, sem, n_rows):
    pltpu.make_async_copy(src_ref.at[pl.ds(0, n_rows)], dst_ref.at[pl.ds(0, n_rows)], sem).wait()


def _scatter_kernel(meta_ref, dest_ref, h3_ref, xs_ref, zero_ref, sem_ref):
    i = pl.program_id(0)
    rt = h3_ref.shape[0]
    n_chunks = xs_ref.shape[0] // MOE_ROWS

    @pl.when(i == 0)
    def _():
        zero_ref[...] = jnp.zeros(zero_ref.shape, F32)

        def zero_chunk(first_row):
            return pltpu.make_async_copy(zero_ref, xs_ref.at[pl.ds(first_row, MOE_ROWS)], sem_ref.at[1])

        def fills(act):
            prev = 0
            for ex in range(N_EXPERTS):
                end = meta_ref[2, ex]

                @pl.when(end > prev)
                def _():
                    act(zero_chunk(end - MOE_ROWS))
                prev = end
            for c in range(n_chunks - N_EXPERTS, n_chunks):
                @pl.when(c >= meta_ref[1, 0])
                def _():
                    act(zero_chunk(c * MOE_ROWS))

        fills(lambda copy: copy.start())
        fills(lambda copy: copy.wait())

    def issue(t, carry):
        for k in range(TOP_K):
            pltpu.make_async_copy(h3_ref.at[t], xs_ref.at[dest_ref[0, k, t]], sem_ref.at[0]).start(priority=k % 2)
        return carry

    lax.fori_loop(0, rt, issue, 0, unroll=8)
    _row_copies_wait(h3_ref, h3_ref, sem_ref.at[0], rt)
    _row_copies_wait(h3_ref, h3_ref, sem_ref.at[0], rt)


def _scatter_rows(h3, dest, meta, rt):
    n = h3.shape[0]
    n_slots = (n * TOP_K // MOE_ROWS + N_EXPERTS) * MOE_ROWS
    grid_spec = pltpu.PrefetchScalarGridSpec(
        num_scalar_prefetch=1,
        grid=(n // rt,),
        in_specs=[pl.BlockSpec((1, TOP_K, rt), lambda i, meta: (i, 0, 0), memory_space=pltpu.SMEM),
                  pl.BlockSpec((rt, ROW_TILES, LANES), lambda i, meta: (i, 0, 0))],
        out_specs=pl.BlockSpec(memory_space=pl.ANY),
        scratch_shapes=[pltpu.VMEM((MOE_ROWS, ROW_TILES, LANES), F32), pltpu.SemaphoreType.DMA((2,))],
    )
    return pl.pallas_call(
        _scatter_kernel,
        grid_spec=grid_spec,
        out_shape=jax.ShapeDtypeStruct((n_slots, ROW_TILES, LANES), F32),
        compiler_params=_cparams("arbitrary"),
        name="moe_scatter_rows",
    )(meta, dest, h3)


def _expert_kernel(meta_ref, xs_ref, wg_ref, wu_ref, wd_ref, ys_ref, wg_b, wu_b, wd_b):
    c = pl.program_id(0)
    in_use = c < meta_ref[1, 0]

    @pl.when(jnp.logical_not(in_use))
    def _():
        ys_ref[...] = jnp.zeros(ys_ref.shape, F32)

    @pl.when(in_use & ((c == 0) | (meta_ref[0, c] != meta_ref[0, jnp.maximum(c - 1, 0)])))
    def _():
        wg_b[...] = wg_ref[0].astype(BF16)
        wu_b[...] = wu_ref[0].astype(BF16)
        wd_b[...] = wd_ref[0].astype(BF16)

    @pl.when(in_use)
    def _():
        x = _tiles_to_rows(xs_ref).astype(BF16)
        gate = jnp.dot(x, wg_b[...], preferred_element_type=F32)
        up = jnp.dot(x, wu_b[...], preferred_element_type=F32)
        act = (gate * jax.nn.sigmoid(gate) * up).astype(BF16)
        _rows_to_tiles(ys_ref, jnp.dot(act, wd_b[...], preferred_element_type=F32))


def _expert_mlp(xs, meta, w_g, w_u, w_d):
    n_chunks = xs.shape[0] // MOE_ROWS
    rows = lambda c, meta: (c, 0)
    expert = lambda c, meta: (meta[0, jnp.minimum(c, meta[1, 0] - 1)], 0, 0)
    grid_spec = pltpu.PrefetchScalarGridSpec(
        num_scalar_prefetch=1,
        grid=(n_chunks,),
        in_specs=[
            pl.BlockSpec((MOE_ROWS * ROW_TILES, LANES), rows),
            pl.BlockSpec((1, D_MODEL, D_EXPERT), expert),
            pl.BlockSpec((1, D_MODEL, D_EXPERT), expert),
            pl.BlockSpec((1, D_EXPERT, D_MODEL), expert),
        ],
        out_specs=pl.BlockSpec((MOE_ROWS * ROW_TILES, LANES), rows),
        scratch_shapes=[pltpu.VMEM((D_MODEL, D_EXPERT), BF16), pltpu.VMEM((D_MODEL, D_EXPERT), BF16),
                        pltpu.VMEM((D_EXPERT, D_MODEL), BF16)],
    )
    return pl.pallas_call(
        _expert_kernel,
        grid_spec=grid_spec,
        out_shape=jax.ShapeDtypeStruct((xs.shape[0] * ROW_TILES, LANES), F32),
        compiler_params=_cparams("arbitrary"),
        name="expert_mlp",
    )(meta, xs.reshape(-1, LANES), w_g, w_u, w_d).reshape(xs.shape)


def _combine_kernel(dest_ref, nxt_ref, x1_ref, route_ref, ln_ref, ys_ref, o_ref, buf_ref, sem_ref):
    i = pl.program_id(0)
    n_tiles = pl.num_programs(0)
    rt = x1_ref.shape[0]

    def issue(table_ref, slot):
        def body(t, carry):
            for k in range(TOP_K):
                tile = pl.ds(pl.multiple_of(t * ROW_TILES, ROW_TILES), ROW_TILES)
                pltpu.make_async_copy(ys_ref.at[table_ref[0, k, t]], buf_ref.at[slot, k, tile],
                                      sem_ref.at[slot]).start(priority=k % 2)
            return carry
        lax.fori_loop(0, rt, body, 0, unroll=8)

    slot = i % 2

    @pl.when(i == 0)
    def _():
        issue(dest_ref, slot)

    @pl.when(i + 1 < n_tiles)
    def _():
        issue(nxt_ref, 1 - slot)

    for k in range(TOP_K):
        pltpu.make_async_copy(buf_ref.at[slot, k], buf_ref.at[slot, k], sem_ref.at[slot]).wait()
    route = route_ref[...]
    moe = None
    for k in range(TOP_K):
        y = _tiles_to_rows(buf_ref.at[slot, k])
        term = y * route[:, ROUTE_W1 + k:ROUTE_W1 + k + 1]
        moe = term if moe is None else moe + term
    o_ref[...] = _rms(x1_ref[...] + moe, ln_ref[...])


def _combine(x1, ys, dest, route, ln_final, rt):
    n = x1.shape[0]
    tiles = n // rt
    row = lambda i: (i, 0)
    table = lambda f: pl.BlockSpec((1, TOP_K, rt), f, memory_space=pltpu.SMEM)
    return pl.pallas_call(
        _combine_kernel,
        grid=(tiles,),
        in_specs=[
            table(lambda i: (i, 0, 0)),
            table(lambda i: (jnp.minimum(i + 1, tiles - 1), 0, 0)),
            pl.BlockSpec((rt, D_MODEL), row),
            pl.BlockSpec((rt, LANES), row),
            pl.BlockSpec((1, D_MODEL), lambda i: (0, 0)),
            pl.BlockSpec(memory_space=pl.ANY),
        ],
        out_specs=pl.BlockSpec((rt, D_MODEL), row),
        out_shape=jax.ShapeDtypeStruct((n, D_MODEL), F32),
        scratch_shapes=[pltpu.VMEM((2, TOP_K, rt * ROW_TILES, LANES), F32), pltpu.SemaphoreType.DMA((2,))],
        compiler_params=_cparams("arbitrary"),
        name="moe_combine_norm",
    )(dest, dest, x1, route, ln_final.reshape(1, -1), ys)


def _ffn(conv2d, att2d, x2d, g_conv, g_att, w_out_b, ln2, w_route_b, w_g, w_u, w_d, ln_final, tm):
    x1, h3, route = _merge(conv2d, att2d, x2d, g_conv, g_att, w_out_b, ln2, w_route_b, tm)
    rt = min(x2d.shape[0], ROUTE_TILE)
    dest, meta = _rank(route, rt)
    ys = _expert_mlp(_scatter_rows(h3, dest, meta, rt), meta, w_g, w_u, w_d)
    return _combine(x1, ys, dest, route, ln_final, rt)


def kernel(x_prompt, x_sample, cache_cmp_kv, cache_sel_kv, state_win_kv, state_conv, page_table, ln1, w_in, conv_dw_w, conv_dw_b, conv_ln_g, conv_ln_b, cmp_pos_emb, w_cmp_k1, w_cmp_k2, w_cmp_v1, w_cmp_v2, out_norm_conv, out_norm_att, w_out, ln2, w_router_group, w_router_expert, w_exp_gate, w_exp_up, w_exp_down, ln_final):
    depth = ln1.shape[0]
    assert depth == 1, "single-layer step"
    b, t, _ = x_prompt.shape
    db, ds, _ = x_sample.shape
    n_phys = cache_cmp_kv.shape[1]
    n_pages = page_table.shape[1]
    past = n_pages * PAGE_SIZE
    win_rows = state_win_kv.shape[2]
    assert ds < CMP_BLOCK and ds <= SUBLANES and past % SEL_BLOCK == 0 and past // SEL_BLOCK >= N_SEL
    assert win_rows == WINDOW and past >= WINDOW and t % KC == 0 and t >= WIN_KEYS

    w_in_b = jnp.pad(w_in[0], ((0, 0), (0, D_IN_PAD - D_IN))).astype(BF16)
    w_out_b = w_out[0].astype(BF16)
    w_route_b = jnp.pad(jnp.concatenate([w_router_group[0], w_router_expert[0]], axis=1),
                        ((0, 0), (0, LANES - N_GROUPS - N_EXPERTS))).astype(BF16)
    cmp_w = _cmp_weights(cmp_pos_emb[0], w_cmp_k1[0], w_cmp_k2[0], w_cmp_v1[0], w_cmp_v2[0])
    conv_w = (conv_dw_w[0], conv_dw_b[0], conv_ln_g[0], conv_ln_b[0])
    ffn_w = (out_norm_conv[0], out_norm_att[0], w_out_b, ln2[0], w_route_b, w_exp_gate[0], w_exp_up[0],
             w_exp_down[0], ln_final)

    xp2 = x_prompt.reshape(b * t, D_MODEL)
    c_p, s_p = _rope_tables(jnp.arange(t))
    (a_p, q_p, ckv_p, _, _, gate_p, ckv_pt, skv_pt, wkv_pt, ksel_aug, kwin_p, vsel_t, vwin_t) = _project(
        xp2, ln1[0], w_in_b, c_p, s_p, 512)
    a_p3 = a_p.reshape(b, t, CONV_CH)
    conv_p = _conv_module(a_p3, jnp.zeros((b, HIST_ROWS, CONV_CH), F32), *conv_w, 512)
    kc_p, vc_p = _compress_dense(ckv_p.reshape(b, t, COL_KV), cmp_w)
    gates_t = jnp.transpose(gate_p[:, :COL_GATE].reshape(b, t, N_KV, Q_PER_KV, 3), (0, 2, 4, 3, 1))
    att_p = _prompt_attention(q_p.reshape(b, t, COL_Q), kc_p, jnp.swapaxes(vc_p, 2, 3), ksel_aug,
                              vsel_t, kwin_p, vwin_t, gates_t.reshape(b, N_KV, 3 * Q_PER_KV, t))
    y_p = _ffn(conv_p.reshape(b * t, CONV_CH), att_p.reshape(b * t, COL_Q), xp2, *ffn_w, 512)

    n_s = db * ds
    xs2 = x_sample.reshape(n_s, D_MODEL)
    c_s, s_s = _rope_tables(jnp.tile(past + jnp.arange(ds), db))
    a_s, q_s, ckv_s, skv_s, wkv_s, gate_s = _project(xs2, ln1[0], w_in_b, c_s, s_s, n_s)[:6]
    a_s3 = a_s.reshape(db, ds, CONV_CH)
    hist_s = jnp.pad(state_conv[0], ((0, 0), (HIST_ROWS - (CONV_W - 1), 0), (0, 0)))
    conv_s = _conv_module(a_s3, hist_s, *conv_w, ds)
    rows_minor = lambda a, n, r: jnp.swapaxes(a.reshape(n, r, COL_KV), 1, 2)
    kc_s, vc_s = _compress_paged(rows_minor(cache_cmp_kv[0], n_phys, PAGE_SIZE), page_table, cmp_w)

    q5 = q_s.reshape(db, ds, N_KV, Q_PER_KV, HEAD_DIM)
    pad_tok = SUBLANES - ds
    pad_head = SUBLANES - Q_PER_KV
    q_rt = jnp.pad(jnp.transpose(q5, (0, 2, 3, 1, 4)), ((0, 0), (0, 0), (0, 0), (0, pad_tok), (0, 0)))
    q_rt = q_rt.reshape(db, N_KV, Q_PER_KV * SUBLANES, HEAD_DIM)
    q_tr = jnp.pad(jnp.transpose(q5, (0, 2, 1, 3, 4)), ((0, 0), (0, 0), (0, 0), (0, pad_head), (0, 0)))
    q_tr = q_tr.reshape(db, N_KV, ds * SUBLANES, HEAD_DIM)
    g5 = gate_s[:, :COL_GATE].reshape(db, ds, N_KV, Q_PER_KV, 3)
    g_tr = jnp.pad(jnp.transpose(g5, (0, 2, 4, 1, 3)), ((0, 0),) * 4 + ((0, pad_head),))
    g_tr = jnp.broadcast_to(g_tr.reshape(db, N_KV, 3, ds * SUBLANES)[..., None], (db, N_KV, 3, ds * SUBLANES, HEAD_DIM))
    ocmp, picks = _sample_cmp_select(q_rt, kc_s, vc_s, past, ds)
    sel_idx = picks.reshape(db, N_KV, SUBLANES, LANES)[:, :, :ds, :N_SEL].reshape(-1)
    ocmp_tr = jnp.transpose(ocmp.reshape(db, N_KV, Q_PER_KV, SUBLANES, HEAD_DIM)[:, :, :, :ds], (0, 1, 3, 2, 4))
    ocmp_tr = jnp.pad(ocmp_tr, ((0, 0), (0, 0), (0, 0), (0, pad_head), (0, 0)))
    ocmp_tr = ocmp_tr.reshape(db, N_KV, ds * SUBLANES, HEAD_DIM)
    new_t = lambda kv: jnp.pad(rows_minor(kv, db, ds), ((0, 0), (0, 0), (0, LANES - ds)))
    o_s = _sample_attention(sel_idx, page_table, rows_minor(cache_sel_kv[0], n_phys, PAGE_SIZE), q_tr,
                            new_t(skv_s), new_t(wkv_s), rows_minor(state_win_kv[0], db, win_rows),
                            ocmp_tr, g_tr, ds)
    att_s = o_s.reshape(db, N_KV, ds, SUBLANES, HEAD_DIM)[:, :, :, :Q_PER_KV]
    att_s = jnp.transpose(att_s, (0, 2, 1, 3, 4)).reshape(n_s, COL_Q)
    y_s = _ffn(conv_s.reshape(n_s, CONV_CH), att_s, xs2, *ffn_w, n_s)

    kv6 = lambda kv, bb, tt: kv.reshape(1, bb, tt, N_KV, 2, HEAD_DIM)
    kv6_t = lambda kv_t: jnp.swapaxes(kv_t, 1, 2).reshape(1, b, kv_t.shape[2], N_KV, 2, HEAD_DIM)
    new_win_s = jnp.concatenate([state_win_kv, kv6(wkv_s, db, ds)], axis=2)[:, :, ds:]
    new_conv_s = jnp.concatenate([state_conv[0], a_s3], axis=1)[None, :, ds:]
    return (y_p.reshape(b, t, D_MODEL), y_s.reshape(db, ds, D_MODEL),
            kv6_t(ckv_pt), kv6(ckv_s, db, ds), kv6_t(skv_pt), kv6(skv_s, db, ds),
            kv6_t(wkv_pt[:, :, t - min(WINDOW, t):]), new_win_s,
            a_p3[None, :, t - (CONV_W - 1):], new_conv_s)
```
